```python
import math
import jax, jax.numpy as jnp
from jax import lax
import numpy as np

D_MODEL = 1024
BATCH = 8
SEQ = 2048
DEPTH = 4
DEC_BATCH = 128
DEC_SEQ = 4
PAST_LEN = 16384
PAGE_SIZE = 128

N_MIXERS = 2
N_A_LAYERS = (DEPTH + 1) // 2
N_B_LAYERS = DEPTH // 2
N_DENSE = (DEPTH + 1) // 2
N_MOE = DEPTH // 2
A_CHUNK = 128
A_HALF = 3 * D_MODEL
A_GROUPS = 8
A_GROUP_DIM = A_HALF // A_GROUPS
HG_EXPAND = 128
HG_HEADS = D_MODEL // HG_EXPAND
HG_DK = HG_EXPAND
HG_DV = D_MODEL // HG_HEADS
HG_CHUNK = 64
FORGET_FLOOR = 1e-20
D_FF = 7 * D_MODEL // 2
N_EXPERTS = 8
TOP_K = 2
MOE_BLOCK = 128
NORM_EPS = 1e-6

kernel_name = 'hybrid_gmlp_hgrn2_moe_decode_step'


def _rmsnorm(x, g):
    xf = x.astype(jnp.float32)
    y = xf * lax.rsqrt(jnp.mean(xf * xf, axis=-1, keepdims=True) + NORM_EPS)
    return (y * g.astype(jnp.float32)).astype(x.dtype)


def _layernorm(x, g, b):
    xf = x.astype(jnp.float32)
    xc = xf - jnp.mean(xf, axis=-1, keepdims=True)
    y = xc * lax.rsqrt(jnp.mean(xc * xc, axis=-1, keepdims=True) + NORM_EPS)
    return (y * g.astype(jnp.float32) + b.astype(jnp.float32)).astype(x.dtype)


def _swiglu(x, w_gate, w_up, w_down):
    return (jax.nn.silu(x @ w_gate) * (x @ w_up)) @ w_down


def _gmlp_mixer(h, w_in, ln_g, ln_b, w_s, b_s, w_out):
    n, L, _ = h.shape
    z = jax.nn.gelu(h @ w_in, approximate=False)
    u = z[..., :A_HALF]
    v = _layernorm(z[..., A_HALF:], ln_g, ln_b)
    c = min(L, A_CHUNK)
    nc = L // c
    causal = jnp.tril(jnp.ones((c, c), dtype=bool))
    w = jnp.where(causal[None], w_s[:, :c, :c], jnp.zeros((), w_s.dtype))
    vc = v.reshape(n, nc, c, A_GROUPS, A_GROUP_DIM)
    s = jnp.einsum('gts,bcsgd->bctgd', w, vc) + jnp.transpose(b_s[:, :c])[None, None, :, :, None]
    out = (u * s.reshape(n, L, A_HALF)) @ w_out
    return out, v


def _hgrn_chunk_step(S, inp):
    q, k, v, logf = inp
    c = q.shape[2]
    b = jnp.cumsum(logf, axis=2)
    causal = jnp.tril(jnp.ones((c, c), dtype=bool))[None, None, :, :, None]
    rel = b[:, :, :, None, :] - b[:, :, None, :, :]
    decay = jnp.where(causal, jnp.exp(jnp.where(causal, rel, 0.0)), 0.0)
    a = jnp.einsum('nhtk,nhsk,nhtsk->nhts', q, k, decay)
    o = jnp.einsum('nhts,nhsv->nhtv', a, v) + jnp.einsum('nhtk,nhkv->nhtv', q * jnp.exp(b), S)
    b_last = b[:, :, -1:, :]
    S_new = (jnp.exp(b_last[:, :, 0, :])[..., None] * S
             + jnp.einsum('nhsk,nhsv->nhkv', k * jnp.exp(b_last - b), v))
    return S_new, o


def _hgrn2_mixer(h, w_in, lb, norm_g, w_out, S0):
    n, L, _ = h.shape
    proj = (h @ w_in).astype(jnp.float32)
    q, fpre, i, g = jnp.split(proj, 4, axis=-1)
    lb = lb.astype(jnp.float32)
    q = jax.nn.silu(q)
    f = lb + (1.0 - lb) * jax.nn.sigmoid(fpre)
    logf = jnp.log(jnp.maximum(f, FORGET_FLOOR))
    k = (1.0 - lb) * jax.nn.sigmoid(-fpre)
    c = L if L <= HG_CHUNK else math.gcd(L, HG_CHUNK)
    nc = L // c

    def to_blocks(t):
        return t.reshape(n, nc, c, HG_HEADS, -1).transpose(1, 0, 3, 2, 4)

    S_fin, o = lax.scan(_hgrn_chunk_step, S0.astype(jnp.float32),
                        (to_blocks(q), to_blocks(k), to_blocks(i), to_blocks(logf)))
    o = o.transpose(1, 0, 3, 2, 4).reshape(n, L, HG_HEADS, HG_DV)
    o = _rmsnorm(o, norm_g.reshape(HG_HEADS, HG_DV)) * jax.nn.sigmoid(g).reshape(n, L, HG_HEADS, HG_DV)
    out = o.reshape(n, L, D_MODEL).astype(h.dtype) @ w_out
    return out, S_fin.astype(S0.dtype)


def _moe_ffn(h, router, w_gate, w_up, w_down):
    n, L, d = h.shape
    x = h.reshape(-1, d)
    T = x.shape[0]
    logits = (x @ router).astype(jnp.float32)
    top_val, top_idx = lax.top_k(logits, TOP_K)
    gates = jax.nn.softmax(top_val, axis=-1)
    n_assign = T * TOP_K
    e_flat = top_idx.reshape(-1).astype(jnp.int32)
    tok_flat = jnp.repeat(jnp.arange(T, dtype=jnp.int32), TOP_K)
    g_flat = gates.reshape(-1)
    order = jnp.argsort(e_flat)
    e_sorted = e_flat[order]
    counts = jnp.zeros((N_EXPERTS,), jnp.int32).at[e_flat].add(1)
    padded = (counts + MOE_BLOCK - 1) // MOE_BLOCK * MOE_BLOCK
    start = jnp.cumsum(counts) - counts
    pstart = jnp.cumsum(padded) - padded
    pend = pstart + padded
    rank = jnp.arange(n_assign, dtype=jnp.int32) - start[e_sorted]
    dest = pstart[e_sorted] + rank
    n_blocks = -(-n_assign // MOE_BLOCK) + N_EXPERTS
    slot_tok = jnp.full((n_blocks * MOE_BLOCK,), T, jnp.int32).at[dest].set(tok_flat[order])
    slot_gate = jnp.zeros((n_blocks * MOE_BLOCK,), jnp.float32).at[dest].set(g_flat[order])
    block_start = jnp.arange(n_blocks, dtype=jnp.int32) * MOE_BLOCK
    block_expert = jnp.minimum(jnp.sum(pend[None, :] <= block_start[:, None], axis=1),
                               N_EXPERTS - 1).astype(jnp.int32)
    x_pad = jnp.concatenate([x, jnp.zeros((1, d), x.dtype)], axis=0)

    def expert_block(args):
        idx, e = args
        xb = x_pad[idx]
        return _swiglu(xb, w_gate[e], w_up[e], w_down[e])

    yb = lax.map(expert_block, (slot_tok.reshape(n_blocks, MOE_BLOCK), block_expert))
    y = yb.reshape(-1, d) * slot_gate[:, None].astype(yb.dtype)
    y = jax.ops.segment_sum(y, slot_tok, num_segments=T + 1)[:T]
    return y.reshape(n, L, d)


def _trunk(x, hg_init, lbs, norm_mix_g, norm_ffn_g, final_norm_g,
           a_w_in, a_ln_g, a_ln_b, a_w_s, a_b_s, a_w_out,
           b_w_in, b_norm_g, b_w_out,
           ffn_w_gate, ffn_w_up, ffn_w_down,
           moe_router, moe_w_gate, moe_w_up, moe_w_down):
    hg_states, gmlp_v = [], []
    for layer in range(DEPTH):
        j = layer // N_MIXERS
        h = _rmsnorm(x, norm_mix_g[layer])
        if layer % N_MIXERS == 0:
            y, v = _gmlp_mixer(h, a_w_in[j], a_ln_g[j], a_ln_b[j], a_w_s[j], a_b_s[j], a_w_out[j])
            gmlp_v.append(v)
        else:
            y, S = _hgrn2_mixer(h, b_w_in[j], lbs[j], b_norm_g[j], b_w_out[j], hg_init[j])
            hg_states.append(S)
        x = x + y
        h = _rmsnorm(x, norm_ffn_g[layer])
        m = layer // 2
        if layer % 2 == 0:
            x = x + _swiglu(h, ffn_w_gate[m], ffn_w_up[m], ffn_w_down[m])
        else:
            x = x + _moe_ffn(h, moe_router[m], moe_w_gate[m], moe_w_up[m], moe_w_down[m])
    return _rmsnorm(x, final_norm_g), hg_states, gmlp_v


def setup_inputs(seed: int = 0) -> dict:
    key = jax.random.key(seed)
    ks = jax.random.split(key, 23)
    f32 = jnp.float32
    d = D_MODEL

    def nrm(k, shape, scale):
        return jax.random.normal(k, shape, f32) * scale

    return {
        'x_prompt': nrm(ks[0], (BATCH, SEQ, d), 1.0),
        'x_sample': nrm(ks[1], (DEC_BATCH, DEC_SEQ, d), 1.0),
        'state_hgrn': nrm(ks[2], (N_B_LAYERS, DEC_BATCH, HG_HEADS, HG_DK, HG_DV), 0.5),
        'norm_mix_g': 1.0 + nrm(ks[3], (DEPTH, d), 0.05),
        'norm_ffn_g': 1.0 + nrm(ks[4], (DEPTH, d), 0.05),
        'final_norm_g': 1.0 + nrm(ks[5], (d,), 0.05),
        'a_w_in': nrm(ks[6], (N_A_LAYERS, d, 2 * A_HALF), d ** -0.5),
        'a_ln_g': 1.0 + nrm(ks[7], (N_A_LAYERS, A_HALF), 0.05),
        'a_ln_b': nrm(ks[8], (N_A_LAYERS, A_HALF), 0.02),
        'a_w_s': nrm(ks[9], (N_A_LAYERS, A_GROUPS, A_CHUNK, A_CHUNK), 0.5 * A_CHUNK ** -0.5),
        'a_b_s': 1.0 + nrm(ks[10], (N_A_LAYERS, A_GROUPS, A_CHUNK), 0.05),
        'a_w_out': nrm(ks[11], (N_A_LAYERS, A_HALF, d), A_HALF ** -0.5),
        'b_w_in': nrm(ks[12], (N_B_LAYERS, d, 4 * d), d ** -0.5),
        'b_lb_logits': nrm(ks[13], (N_B_LAYERS, d), 0.5),
        'b_norm_g': 1.0 + nrm(ks[14], (N_B_LAYERS, d), 0.05),
        'b_w_out': nrm(ks[15], (N_B_LAYERS, d, d), d ** -0.5),
        'ffn_w_gate': nrm(ks[16], (N_DENSE, d, D_FF), d ** -0.5),
        'ffn_w_up': nrm(ks[17], (N_DENSE, d, D_FF), d ** -0.5),
        'ffn_w_down': nrm(ks[18], (N_DENSE, D_FF, d), D_FF ** -0.5),
        'moe_router': nrm(ks[19], (N_MOE, d, N_EXPERTS), d ** -0.5),
        'moe_w_gate': nrm(ks[20], (N_MOE, N_EXPERTS, d, D_FF), d ** -0.5),
        'moe_w_up': nrm(ks[21], (N_MOE, N_EXPERTS, d, D_FF), d ** -0.5),
        'moe_w_down': nrm(ks[22], (N_MOE, N_EXPERTS, D_FF, d), D_FF ** -0.5),
    }


def reference(x_prompt, x_sample, state_hgrn, norm_mix_g, norm_ffn_g, final_norm_g,
              a_w_in, a_ln_g, a_ln_b, a_w_s, a_b_s, a_w_out,
              b_w_in, b_lb_logits, b_norm_g, b_w_out,
              ffn_w_gate, ffn_w_up, ffn_w_down,
              moe_router, moe_w_gate, moe_w_up, moe_w_down):
    p = jax.nn.softmax(b_lb_logits.astype(jnp.float32), axis=0)
    lbs = jnp.cumsum(p, axis=0) - p[0:1]
    hg_zero = jnp.zeros((N_B_LAYERS, x_prompt.shape[0], HG_HEADS, HG_DK, HG_DV), state_hgrn.dtype)
    y_prompt, hg_prompt, _ = _trunk(
        x_prompt, hg_zero, lbs, norm_mix_g, norm_ffn_g, final_norm_g,
        a_w_in, a_ln_g, a_ln_b, a_w_s, a_b_s, a_w_out, b_w_in, b_norm_g, b_w_out,
        ffn_w_gate, ffn_w_up, ffn_w_down, moe_router, moe_w_gate, moe_w_up, moe_w_down)
    y_sample, hg_sample, v_sample = _trunk(
        x_sample, state_hgrn, lbs, norm_mix_g, norm_ffn_g, final_norm_g,
        a_w_in, a_ln_g, a_ln_b, a_w_s, a_b_s, a_w_out, b_w_in, b_norm_g, b_w_out,
        ffn_w_gate, ffn_w_up, ffn_w_down, moe_router, moe_w_gate, moe_w_up, moe_w_down)
    new_state_hgrn_prompt = jnp.stack(hg_prompt)
    new_state_hgrn_sample = jnp.stack(hg_sample)
    new_state_gmlp_v_sample = jnp.stack(v_sample)
    return (y_prompt, y_sample, new_state_hgrn_prompt, new_state_hgrn_sample, new_state_gmlp_v_sample)
```

```python
import functools
import math

import jax
import jax.numpy as jnp
from jax import lax
from jax.experimental import pallas as pl
from jax.experimental.pallas import tpu as pltpu

F32 = jnp.float32
BF16 = jnp.bfloat16

D_MODEL = 1024
A_CHUNK = 128
A_HALF = 3 * D_MODEL
A_GROUPS = 8
A_GROUP_DIM = A_HALF // A_GROUPS
HG_HEADS = 8
HG_DK = 128
HG_CHUNK = 64
HG_SUB = 16
FORGET_FLOOR = 1e-20
D_FF = 7 * D_MODEL // 2
N_EXPERTS = 8
NORM_EPS = 1e-6
LANES = 128
MASKED_LOG = -1e30

VMEM_LIMIT = 56 * 1024 * 1024


def _params(*sem):
    return pltpu.CompilerParams(dimension_semantics=sem, vmem_limit_bytes=VMEM_LIMIT)


def _rms(x, g):
    ms = jnp.mean(x * x, axis=-1, keepdims=True)
    return x * lax.rsqrt(ms + NORM_EPS) * g


def _gelu(y):
    return 0.5 * y * (1.0 + lax.erf(y * math.sqrt(0.5)))


def _identity(y):
    return y


def _norm_matmul_kernel(x_ref, g_ref, w_ref, o_ref, h_ref, *, act):
    @pl.when(pl.program_id(1) == 0)
    def _():
        h_ref[...] = _rms(x_ref[...], g_ref[...]).astype(BF16)

    y = jnp.dot(h_ref[...], w_ref[...].astype(BF16), preferred_element_type=F32)
    o_ref[...] = act(y).astype(o_ref.dtype)


def _norm_matmul(x, g_all, layer, w_all, w_layer, act, out_dtype, tm, tn):
    T = x.shape[0]
    N = w_all.shape[-1]
    return pl.pallas_call(
        functools.partial(_norm_matmul_kernel, act=act),
        grid=(T // tm, N // tn),
        in_specs=[
            pl.BlockSpec((tm, D_MODEL), lambda i, j: (i, 0)),
            pl.BlockSpec((None, 1, D_MODEL), lambda i, j: (layer, 0, 0)),
            pl.BlockSpec((None, D_MODEL, tn), lambda i, j: (w_layer, 0, j)),
        ],
        out_specs=pl.BlockSpec((tm, tn), lambda i, j: (i, j)),
        out_shape=jax.ShapeDtypeStruct((T, N), out_dtype),
        scratch_shapes=[pltpu.VMEM((tm, D_MODEL), BF16)],
        compiler_params=_params("parallel", "arbitrary"),
        name="norm_matmul",
    )(x, g_all, w_all)


def _gmlp_gate_kernel(z_ref, lng_ref, lnb_ref, wmix_ref, bias_ref, wout_ref, x_ref,
                      o_ref, v_ref, acc_ref, mu_ref, rstd_ref):
    g = pl.program_id(1)
    tm = z_ref.shape[0]

    @pl.when(g == 0)
    def _():
        zv = z_ref[:, A_HALF:].astype(F32)
        mu = jnp.mean(zv, axis=-1, keepdims=True)
        xc = zv - mu
        mu_ref[...] = mu
        rstd_ref[...] = lax.rsqrt(jnp.mean(xc * xc, axis=-1, keepdims=True) + NORM_EPS)

    off = pl.multiple_of(g * A_GROUP_DIM, LANES)
    u = z_ref[:, pl.ds(off, A_GROUP_DIM)]
    zv = z_ref[:, pl.ds(A_HALF + off, A_GROUP_DIM)].astype(F32)
    v = (zv - mu_ref[...]) * rstd_ref[...] * lng_ref[:, pl.ds(off, A_GROUP_DIM)] \
        + lnb_ref[:, pl.ds(off, A_GROUP_DIM)]
    v_ref[...] = v
    vb = v.astype(BF16)
    wm = wmix_ref[...].astype(BF16)
    bb = bias_ref[...]
    bias = jnp.concatenate([bb] * (A_GROUP_DIM // LANES), axis=1)
    rows = []
    for c in range(tm // A_CHUNK):
        sl = slice(c * A_CHUNK, (c + 1) * A_CHUNK)
        s = jnp.dot(wm, vb[sl], preferred_element_type=F32) + bias
        rows.append((u[sl].astype(F32) * s).astype(BF16))
    gated = jnp.concatenate(rows, axis=0)
    part = jnp.dot(gated, wout_ref[...].astype(BF16), preferred_element_type=F32)

    @pl.when(g == 0)
    def _():
        acc_ref[...] = part

    @pl.when(g > 0)
    def _():
        acc_ref[...] += part

    @pl.when(g == A_GROUPS - 1)
    def _():
        o_ref[...] = x_ref[...] + acc_ref[...]


def _gmlp_gate(z, x, ln_g, ln_b, wmix, bias, w_out, j, n_prompt_blocks, tm):
    T = x.shape[0]
    nb = T // tm
    assert nb == n_prompt_blocks + 1, "the sample rows must be exactly the last row block"

    def kind(i):
        return jnp.where(i >= n_prompt_blocks, 1, 0)

    return pl.pallas_call(
        _gmlp_gate_kernel,
        grid=(nb, A_GROUPS),
        in_specs=[
            pl.BlockSpec((tm, 2 * A_HALF), lambda i, g: (i, 0)),
            pl.BlockSpec((None, 1, A_HALF), lambda i, g: (j, 0, 0)),
            pl.BlockSpec((None, 1, A_HALF), lambda i, g: (j, 0, 0)),
            pl.BlockSpec((None, None, A_CHUNK, A_CHUNK), lambda i, g: (kind(i), g, 0, 0)),
            pl.BlockSpec((None, None, A_CHUNK, LANES), lambda i, g: (kind(i), g, 0, 0)),
            pl.BlockSpec((None, A_GROUP_DIM, D_MODEL), lambda i, g: (j, g, 0)),
            pl.BlockSpec((tm, D_MODEL), lambda i, g: (i, 0)),
        ],
        out_specs=[
            pl.BlockSpec((tm, D_MODEL), lambda i, g: (i, 0)),
            pl.BlockSpec((tm, A_GROUP_DIM), lambda i, g: (0, jnp.where(i == nb - 1, g, 0))),
        ],
        out_shape=[
            jax.ShapeDtypeStruct((T, D_MODEL), F32),
            jax.ShapeDtypeStruct((tm, A_HALF), F32),
        ],
        scratch_shapes=[
            pltpu.VMEM((tm, D_MODEL), F32),
            pltpu.VMEM((tm, 1), F32),
            pltpu.VMEM((tm, 1), F32),
        ],
        compiler_params=_params("arbitrary", "arbitrary"),
        name="gmlp_gate",
    )(z, ln_g, ln_b, wmix, bias, w_out, x)


def _swiglu_part(h, wg, wu, wd):
    a = jnp.dot(h, wg.astype(BF16), preferred_element_type=F32)
    b = jnp.dot(h, wu.astype(BF16), preferred_element_type=F32)
    m = (a * jax.nn.sigmoid(a) * b).astype(BF16)
    return jnp.dot(m, wd.astype(BF16), preferred_element_type=F32)


def _ffn_kernel(x_ref, g_ref, wg_ref, wu_ref, wd_ref, o_ref, h_ref, acc_ref):
    f = pl.program_id(1)

    @pl.when(f == 0)
    def _():
        h_ref[...] = _rms(x_ref[...], g_ref[...]).astype(BF16)

    part = _swiglu_part(h_ref[...], wg_ref[...], wu_ref[...], wd_ref[...])

    @pl.when(f == 0)
    def _():
        acc_ref[...] = part

    @pl.when(f > 0)
    def _():
        acc_ref[...] += part

    @pl.when(f == pl.num_programs(1) - 1)
    def _():
        o_ref[...] = x_ref[...] + acc_ref[...]


def _ffn_dense(x, g_all, layer, wg, wu, wd, m, tm, tf):
    T = x.shape[0]
    return pl.pallas_call(
        _ffn_kernel,
        grid=(T // tm, D_FF // tf),
        in_specs=[
            pl.BlockSpec((tm, D_MODEL), lambda i, f: (i, 0)),
            pl.BlockSpec((None, 1, D_MODEL), lambda i, f: (layer, 0, 0)),
            pl.BlockSpec((None, D_MODEL, tf), lambda i, f: (m, 0, f)),
            pl.BlockSpec((None, D_MODEL, tf), lambda i, f: (m, 0, f)),
            pl.BlockSpec((None, tf, D_MODEL), lambda i, f: (m, f, 0)),
        ],
        out_specs=pl.BlockSpec((tm, D_MODEL), lambda i, f: (i, 0)),
        out_shape=jax.ShapeDtypeStruct((T, D_MODEL), F32),
        scratch_shapes=[pltpu.VMEM((tm, D_MODEL), BF16), pltpu.VMEM((tm, D_MODEL), F32)],
        compiler_params=_params("parallel", "arbitrary"),
        name="ffn_dense",
    )(x, g_all, wg, wu, wd)


def _matmul_res_kernel(a_ref, w_ref, x_ref, o_ref, wb_ref):
    @pl.when(pl.program_id(0) == 0)
    def _():
        wb_ref[...] = w_ref[...].astype(BF16)

    o_ref[...] = x_ref[...] + jnp.dot(a_ref[...], wb_ref[...], preferred_element_type=F32)


def _matmul_res(a, w_all, j, x, tm):
    T = x.shape[0]
    return pl.pallas_call(
        _matmul_res_kernel,
        grid=(T // tm,),
        in_specs=[
            pl.BlockSpec((tm, D_MODEL), lambda i: (i, 0)),
            pl.BlockSpec((None, D_MODEL, D_MODEL), lambda i: (j, 0, 0)),
            pl.BlockSpec((tm, D_MODEL), lambda i: (i, 0)),
        ],
        out_specs=pl.BlockSpec((tm, D_MODEL), lambda i: (i, 0)),
        out_shape=jax.ShapeDtypeStruct((T, D_MODEL), F32),
        scratch_shapes=[pltpu.VMEM((D_MODEL, D_MODEL), BF16)],
        compiler_params=_params("arbitrary"),
        name="matmul_res",
    )(a, w_all, x)


_NT = (((1,), (1,)), ((), ()))
_TN = (((0,), (0,)), ((), ()))


def _hgrn_gates(qpre, fpre, gpre, lb):
    q = qpre * jax.nn.sigmoid(qpre)
    f = lb + (1.0 - lb) * jax.nn.sigmoid(fpre)
    logf = jnp.log(jnp.maximum(f, FORGET_FLOOR))
    k = (1.0 - lb) * jax.nn.sigmoid(-fpre)
    return q, k, logf, jax.nn.sigmoid(gpre)


def _cumsum_rows(x):
    C = x.shape[0]
    r = lax.broadcasted_iota(jnp.int32, (C, C), 0)
    c = lax.broadcasted_iota(jnp.int32, (C, C), 1)
    if C >= HG_SUB:
        tri = jnp.where(r >= c, 1.0, 0.0).astype(F32)
        return jnp.dot(tri, x, preferred_element_type=F32, precision=lax.Precision.HIGHEST)
    row = lax.broadcasted_iota(jnp.int32, (C, 1), 0)
    out = jnp.zeros_like(x)
    for s in range(C):
        out = out + jnp.where(row >= s, x[s:s + 1], 0.0)
    return out


def _hgrn_chunk(q, k, v, logf, st, sub):
    C = q.shape[0]
    b = _cumsum_rows(logf)
    b_last = b[C - 1:C]
    qe = (q * jnp.exp(b)).astype(BF16)
    o = lax.dot_general(qe, st.astype(BF16), _NT, preferred_element_type=F32)
    kd = (k * jnp.exp(b_last - b)).astype(BF16)
    st_new = st * jnp.exp(b_last) + lax.dot_general(v.astype(BF16), kd, _TN, preferred_element_type=F32)

    row = lax.broadcasted_iota(jnp.int32, (sub, 1), 0)
    parts = []
    for blk in range(C // sub):
        lo = blk * sub
        b_i, q_i, k_i, v_i = b[lo:lo + sub], q[lo:lo + sub], k[lo:lo + sub], v[lo:lo + sub]
        if blk == 0:
            o_i = jnp.zeros((sub, HG_DK), F32)
        else:
            ref = b[lo - 1:lo]
            qs = (q_i * jnp.exp(b_i - ref)).astype(BF16)
            ks = (k[:lo] * jnp.exp(ref - b[:lo])).astype(BF16)
            a = lax.dot_general(qs, ks, _NT, preferred_element_type=F32)
            o_i = jnp.dot(a.astype(BF16), v[:lo].astype(BF16), preferred_element_type=F32)
        for s in range(sub):
            rel = jnp.where(row >= s, b_i - b_i[s:s + 1], MASKED_LOG)
            a_col = jnp.sum(q_i * k_i[s:s + 1] * jnp.exp(rel), axis=-1, keepdims=True)
            o_i = o_i + a_col * v_i[s:s + 1]
        parts.append(o_i)
    o = o + (parts[0] if len(parts) == 1 else jnp.concatenate(parts, axis=0))
    return o, st_new


def _head_norm_gate(o, ng, gate):
    return (o * lax.rsqrt(jnp.mean(o * o, axis=-1, keepdims=True) + NORM_EPS) * ng * gate)


def _hgrn_prompt_kernel(q_ref, f_ref, i_ref, g_ref, lb_ref, ng_ref, o_ref, sfin_ref, st_ref):
    tb = pl.program_id(2)

    @pl.when(tb == 0)
    def _():
        st_ref[...] = jnp.zeros_like(st_ref)

    lb = lb_ref[...]
    ng = ng_ref[...]

    def body(ci, carry):
        rows = pl.ds(pl.multiple_of(ci * HG_CHUNK, HG_CHUNK), HG_CHUNK)
        q, k, logf, gate = _hgrn_gates(q_ref[rows, :], f_ref[rows, :], g_ref[rows, :], lb)
        o, st_new = _hgrn_chunk(q, k, i_ref[rows, :], logf, st_ref[...], HG_SUB)
        st_ref[...] = st_new
        o_ref[rows, :] = _head_norm_gate(o, ng, gate).astype(o_ref.dtype)
        return carry

    lax.fori_loop(0, q_ref.shape[0] // HG_CHUNK, body, 0)

    @pl.when(tb == pl.num_programs(2) - 1)
    def _():
        sfin_ref[...] = st_ref[...].T


def _hgrn_prompt(proj, lbs, ng_all, j, n_seq, L, lb_rows):
    nb = L // lb_rows
    H = HG_HEADS

    def field(k):
        return pl.BlockSpec((lb_rows, HG_DK), lambda n, h, t: (n * nb + t, k * H + h))

    vec = pl.BlockSpec((None, 1, HG_DK), lambda n, h, t: (j, 0, h))
    return pl.pallas_call(
        _hgrn_prompt_kernel,
        grid=(n_seq, H, nb),
        in_specs=[field(0), field(1), field(2), field(3), vec, vec],
        out_specs=[
            pl.BlockSpec((lb_rows, HG_DK), lambda n, h, t: (n * nb + t, h)),
            pl.BlockSpec((None, None, HG_DK, HG_DK), lambda n, h, t: (n, h, 0, 0)),
        ],
        out_shape=[
            jax.ShapeDtypeStruct((n_seq * L, D_MODEL), BF16),
            jax.ShapeDtypeStruct((n_seq, H, HG_DK, HG_DK), F32),
        ],
        scratch_shapes=[pltpu.VMEM((HG_DK, HG_DK), F32)],
        compiler_params=_params("parallel", "parallel", "arbitrary"),
        name="hgrn_prompt",
    )(proj, proj, proj, proj, lbs, ng_all)


SAMPLE_LEN = 4
SUBLANES = 8


def _hgrn_sample_kernel(q_ref, f_ref, i_ref, g_ref, lb_ref, ng_ref, s0_ref, o_ref, sfin_ref):
    lb = lb_ref[...]
    ng = ng_ref[...]
    n_seq = s0_ref.shape[0]
    row = lax.broadcasted_iota(jnp.int32, (SUBLANES, 1), 0)
    for p in range(n_seq // 2):
        rows = slice(p * SUBLANES, (p + 1) * SUBLANES)
        q, k, logf, gate = _hgrn_gates(q_ref[rows, :], f_ref[rows, :], g_ref[rows, :], lb)
        v = i_ref[rows, :]
        o_tile = jnp.zeros((SUBLANES, HG_DK), F32)
        for half in range(2):
            mine = (row >= half * SAMPLE_LEN) & (row < (half + 1) * SAMPLE_LEN)
            seq = 2 * p + half
            o, st_new = _hgrn_chunk(jnp.where(mine, q, 0.0), jnp.where(mine, k, 0.0),
                                    jnp.where(mine, v, 0.0), jnp.where(mine, logf, 0.0),
                                    s0_ref[seq].T, SUBLANES)
            sfin_ref[seq] = st_new.T
            o_tile = jnp.where(mine, o, o_tile)
        o_ref[rows, :] = _head_norm_gate(o_tile, ng, gate).astype(o_ref.dtype)


def _hgrn_sample(proj, row0, state_all, lbs, ng_all, j, n_seq, sb):
    H = HG_HEADS
    rb = sb * SAMPLE_LEN
    assert row0 % rb == 0 and sb % 2 == 0
    r0 = row0 // rb

    def field(k):
        return pl.BlockSpec((rb, HG_DK), lambda s, h: (r0 + s, k * H + h))

    vec = pl.BlockSpec((None, 1, HG_DK), lambda s, h: (j, 0, h))
    return pl.pallas_call(
        _hgrn_sample_kernel,
        grid=(n_seq // sb, H),
        in_specs=[field(0), field(1), field(2), field(3), vec, vec,
                  pl.BlockSpec((None, sb, None, HG_DK, HG_DK), lambda s, h: (j, s, h, 0, 0))],
        out_specs=[
            pl.BlockSpec((rb, HG_DK), lambda s, h: (s, h)),
            pl.BlockSpec((sb, None, HG_DK, HG_DK), lambda s, h: (s, h, 0, 0)),
        ],
        out_shape=[
            jax.ShapeDtypeStruct((n_seq * SAMPLE_LEN, D_MODEL), BF16),
            jax.ShapeDtypeStruct((n_seq, H, HG_DK, HG_DK), F32),
        ],
        compiler_params=_params("parallel", "parallel"),
        name="hgrn_sample",
    )(proj, proj, proj, proj, lbs, ng_all, state_all)


META_E1, META_E2, META_G1, META_G2, META_R1, META_R2 = range(6)


def _router_kernel(x_ref, g_ref, rw_ref, h_ref, meta_ref, cnt_ref, carry_ref):
    i = pl.program_id(0)
    tm = x_ref.shape[0]

    @pl.when(i == 0)
    def _():
        carry_ref[...] = jnp.zeros_like(carry_ref)

    h = _rms(x_ref[...], g_ref[...])
    h_ref[...] = h.astype(h_ref.dtype)
    logits = jnp.dot(h, rw_ref[...], preferred_element_type=F32, precision=lax.Precision.HIGHEST)
    lane = lax.broadcasted_iota(jnp.int32, (tm, LANES), 1)
    neg = -jnp.inf
    lg = jnp.where(lane < N_EXPERTS, logits, neg)
    m1 = jnp.max(lg, axis=-1, keepdims=True)
    e1 = jnp.min(jnp.where(lg == m1, lane, LANES), axis=-1, keepdims=True)
    lg2 = jnp.where(lane == e1, neg, lg)
    m2 = jnp.max(lg2, axis=-1, keepdims=True)
    e2 = jnp.min(jnp.where(lg2 == m2, lane, LANES), axis=-1, keepdims=True)
    ex = jnp.exp(m2 - m1)
    g1 = 1.0 / (1.0 + ex)
    g2 = ex / (1.0 + ex)

    onehot = jnp.where((lane == e1) | (lane == e2), 1.0, 0.0)
    r = lax.broadcasted_iota(jnp.int32, (tm, tm), 0)
    c = lax.broadcasted_iota(jnp.int32, (tm, tm), 1)
    before = jnp.where(r > c, 1.0, 0.0).astype(BF16)
    seen = jnp.dot(before, onehot.astype(BF16), preferred_element_type=F32) + carry_ref[0:1, :]
    r1 = jnp.sum(jnp.where(lane == e1, seen, 0.0), axis=-1, keepdims=True)
    r2 = jnp.sum(jnp.where(lane == e2, seen, 0.0), axis=-1, keepdims=True)
    total = carry_ref[0:1, :] + jnp.sum(onehot, axis=0, keepdims=True)
    carry_ref[...] = jnp.broadcast_to(total, carry_ref.shape)
    cnt_ref[...] = jnp.broadcast_to(total, cnt_ref.shape)

    meta = jnp.zeros((tm, LANES), F32)
    for idx, val in ((META_E1, e1.astype(F32)), (META_E2, e2.astype(F32)), (META_G1, g1),
                     (META_G2, g2), (META_R1, r1), (META_R2, r2)):
        meta = jnp.where(lane == idx, val, meta)
    meta_ref[...] = meta


def _router(x, g_all, layer, rw_pad, m, tm):
    T = x.shape[0]
    return pl.pallas_call(
        _router_kernel,
        grid=(T // tm,),
        in_specs=[
            pl.BlockSpec((tm, D_MODEL), lambda i: (i, 0)),
            pl.BlockSpec((None, 1, D_MODEL), lambda i: (layer, 0, 0)),
            pl.BlockSpec((None, D_MODEL, LANES), lambda i: (m, 0, 0)),
        ],
        out_specs=[
            pl.BlockSpec((tm, D_MODEL), lambda i: (i, 0)),
            pl.BlockSpec((tm, LANES), lambda i: (i, 0)),
            pl.BlockSpec((SUBLANES, LANES), lambda i: (0, 0)),
        ],
        out_shape=[
            jax.ShapeDtypeStruct((T, D_MODEL), BF16),
            jax.ShapeDtypeStruct((T, LANES), F32),
            jax.ShapeDtypeStruct((SUBLANES, LANES), F32),
        ],
        scratch_shapes=[pltpu.VMEM((SUBLANES, LANES), F32)],
        compiler_params=_params("arbitrary"),
        name="moe_router",
    )(x, g_all, rw_pad)


def _experts_kernel(be_ref, nu_ref, xs_ref, wg_ref, wu_ref, wd_ref, o_ref):
    b = pl.program_id(0)
    f = pl.program_id(1)
    used = b < nu_ref[0]

    @pl.when(used)
    def _():
        part = _swiglu_part(xs_ref[...], wg_ref[...], wu_ref[...], wd_ref[...])

        @pl.when(f == 0)
        def _():
            o_ref[...] = part

        @pl.when(f > 0)
        def _():
            o_ref[...] += part

    @pl.when(jnp.logical_not(used) & (f == 0))
    def _():
        o_ref[...] = jnp.zeros_like(o_ref)


def _experts(xs, block_expert, n_used, wg, wu, wd, m, bm, tf):
    n_blocks = xs.shape[0] // bm
    nf = D_FF // tf

    def fidx(b, f, nu):
        return jnp.where(b < nu[0], f, nf - 1)

    grid_spec = pltpu.PrefetchScalarGridSpec(
        num_scalar_prefetch=2,
        grid=(n_blocks, nf),
        in_specs=[
            pl.BlockSpec((bm, D_MODEL), lambda b, f, be, nu: (b, 0)),
            pl.BlockSpec((None, None, D_MODEL, tf), lambda b, f, be, nu: (m, be[b], 0, fidx(b, f, nu))),
            pl.BlockSpec((None, None, D_MODEL, tf), lambda b, f, be, nu: (m, be[b], 0, fidx(b, f, nu))),
            pl.BlockSpec((None, None, tf, D_MODEL), lambda b, f, be, nu: (m, be[b], fidx(b, f, nu), 0)),
        ],
        out_specs=pl.BlockSpec((bm, D_MODEL), lambda b, f, be, nu: (b, 0)),
    )
    return pl.pallas_call(
        _experts_kernel,
        grid_spec=grid_spec,
        out_shape=jax.ShapeDtypeStruct((n_blocks * bm, D_MODEL), F32),
        compiler_params=_params("arbitrary", "arbitrary"),
        name="moe_experts",
    )(block_expert, n_used, xs, wg, wu, wd)


def _final_norm_kernel(x_ref, g_ref, o_ref):
    o_ref[...] = _rms(x_ref[...], g_ref[...])


def _final_norm(x, g, row0, n_rows, tm):
    r0 = row0 // tm
    return pl.pallas_call(
        _final_norm_kernel,
        grid=(n_rows // tm,),
        in_specs=[pl.BlockSpec((tm, D_MODEL), lambda i: (r0 + i, 0)),
                  pl.BlockSpec((1, D_MODEL), lambda i: (0, 0))],
        out_specs=pl.BlockSpec((tm, D_MODEL), lambda i: (i, 0)),
        out_shape=jax.ShapeDtypeStruct((n_rows, D_MODEL), F32),
        compiler_params=_params("parallel"),
        name="final_norm",
    )(x, g)


def _moe_layer(x, norm_ffn_g, layer, rw_pad, wg, wu, wd, m, tm, bm, tf):
    T = x.shape[0]
    h, meta, cnt = _router(x, norm_ffn_g, layer, rw_pad, m, tm)
    e = meta[:, META_E1:META_E2 + 1].astype(jnp.int32)
    gates = meta[:, META_G1:META_G2 + 1]
    rank = meta[:, META_R1:META_R2 + 1].astype(jnp.int32)
    counts = cnt[0, :N_EXPERTS].astype(jnp.int32)
    padded = (counts + bm - 1) // bm * bm
    pend = jnp.cumsum(padded)
    pstart = pend - padded
    dest = pstart[e] + rank
    n_blocks = (T * 2) // bm + N_EXPERTS
    block_start = jnp.arange(n_blocks, dtype=jnp.int32) * bm
    block_expert = jnp.minimum(jnp.sum(pend[None, :] <= block_start[:, None], axis=1),
                               N_EXPERTS - 1).astype(jnp.int32)
    n_used = (pend[-1] // bm).astype(jnp.int32).reshape(1)
    tok = jnp.broadcast_to(jnp.arange(T, dtype=jnp.int32)[:, None], (T, 2))
    slot_tok = jnp.zeros((n_blocks * bm,), jnp.int32).at[dest.reshape(-1)].set(tok.reshape(-1))
    xs = jnp.take(h, slot_tok, axis=0)
    ys = _experts(xs, block_expert, n_used, wg, wu, wd, m, bm, tf)
    y = jnp.take(ys, dest.reshape(-1), axis=0).reshape(T, 2, D_MODEL)
    return x + y[:, 0] * gates[:, 0:1] + y[:, 1] * gates[:, 1:2]


def _tril(w):
    n = w.shape[-1]
    return jnp.where(jnp.tril(jnp.ones((n, n), dtype=bool)), w, jnp.zeros((), w.dtype))


def _mix_tables(w_s, b_s, sample_len):
    reps = A_CHUNK // sample_len
    eye = jnp.eye(reps, dtype=w_s.dtype)
    w_prompt = _tril(w_s)
    w_small = _tril(w_s[:, :sample_len, :sample_len])
    w_sample = jax.vmap(lambda w: jnp.kron(eye, w))(w_small)
    b_prompt = b_s
    b_sample = jnp.tile(b_s[:, :sample_len], (1, reps))
    wmix = jnp.stack([w_prompt, w_sample])
    bias = jnp.stack([b_prompt, b_sample])[..., None]
    return wmix, jnp.broadcast_to(bias, bias.shape[:-1] + (LANES,))


def kernel(x_prompt, x_sample, state_hgrn, norm_mix_g, norm_ffn_g, final_norm_g, a_w_in, a_ln_g, a_ln_b, a_w_s, a_b_s, a_w_out, b_w_in, b_lb_logits, b_norm_g, b_w_out, ffn_w_gate, ffn_w_up, ffn_w_down, moe_router, moe_w_gate, moe_w_up, moe_w_down):
    n_p, L, d = x_prompt.shape
    n_s, l_s, _ = x_sample.shape
    assert d == D_MODEL and l_s == SAMPLE_LEN and L % A_CHUNK == 0
    T_p, T_s = n_p * L, n_s * l_s
    T = T_p + T_s
    depth = norm_mix_g.shape[0]

    tm_gate = T_s
    assert T_p % tm_gate == 0 and tm_gate % A_CHUNK == 0
    tm = next(t for t in (768, 512, 256, 128) if T % t == 0)
    bm = 512
    tf = 512
    lb_rows = min(L, 512)

    x = jnp.concatenate([x_prompt.reshape(T_p, d), x_sample.reshape(T_s, d)], axis=0)

    p = jax.nn.softmax(b_lb_logits.astype(F32), axis=0)
    lbs = (jnp.cumsum(p, axis=0) - p[0:1])[:, None, :]
    mix_g = norm_mix_g[:, None, :]
    ffn_g = norm_ffn_g[:, None, :]
    b_ng = b_norm_g[:, None, :]
    ln_g = a_ln_g[:, None, :]
    ln_b = a_ln_b[:, None, :]
    rw_pad = jnp.pad(moe_router, ((0, 0), (0, 0), (0, LANES - N_EXPERTS)))
    moe_wg = moe_w_gate.astype(BF16)
    moe_wu = moe_w_up.astype(BF16)
    moe_wd = moe_w_down.astype(BF16)

    hg_prompt, hg_sample, v_sample = [], [], []
    for layer in range(depth):
        j = layer // 2
        if layer % 2 == 0:
            z = _norm_matmul(x, mix_g, layer, a_w_in, j, _gelu, BF16, tm, 512)
            wmix, bias = _mix_tables(a_w_s[j], a_b_s[j], l_s)
            x, v = _gmlp_gate(z, x, ln_g, ln_b, wmix, bias, a_w_out, j, T_p // tm_gate, tm_gate)
            v_sample.append(v.reshape(n_s, l_s, A_HALF))
            x = _ffn_dense(x, ffn_g, layer, ffn_w_gate, ffn_w_up, ffn_w_down, j, tm, tf)
        else:
            proj = _norm_matmul(x, mix_g, layer, b_w_in, j, _identity, F32, tm, 512)
            o_p, s_p = _hgrn_prompt(proj, lbs, b_ng, j, n_p, L, lb_rows)
            o_s, s_s = _hgrn_sample(proj, T_p, state_hgrn, lbs, b_ng, j, n_s, 8)
            hg_prompt.append(s_p)
            hg_sample.append(s_s)
            o = jnp.concatenate([o_p, o_s], axis=0)
            x = _matmul_res(o, b_w_out, j, x, tm)
            x = _moe_layer(x, ffn_g, layer, rw_pad, moe_wg, moe_wu, moe_wd, j, 512, bm, tf)

    fg = final_norm_g[None, :]
    y_prompt = _final_norm(x, fg, 0, T_p, 512).reshape(n_p, L, d)
    y_sample = _final_norm(x, fg, T_p, T_s, T_s).reshape(n_s, l_s, d)
    return (y_prompt, y_sample, jnp.stack(hg_prompt), jnp.stack(hg_sample), jnp.stack(v_sample))
```

```python
import functools
import math

import jax
import jax.numpy as jnp
from jax import lax
from jax.experimental import pallas as pl
from jax.experimental.pallas import tpu as pltpu

F32 = jnp.float32
BF16 = jnp.bfloat16

D_MODEL = 1024
A_CHUNK = 128
A_HALF = 3 * D_MODEL
A_GROUPS = 8
A_GROUP_DIM = A_HALF // A_GROUPS
HG_HEADS = 8
HG_DK = 128
HG_CHUNK = 64
HG_SUB = 16
FORGET_FLOOR = 1e-20
D_FF = 7 * D_MODEL // 2
N_EXPERTS = 8
NORM_EPS = 1e-6
LANES = 128
SUBLANES = 8
SAMPLE_LEN = 4
MASKED_LOG = -1e30

VMEM_LIMIT = 56 * 1024 * 1024


def _params(*sem):
    return pltpu.CompilerParams(dimension_semantics=sem, vmem_limit_bytes=VMEM_LIMIT)


def _rms(x, g):
    ms = jnp.mean(x * x, axis=-1, keepdims=True)
    return x * lax.rsqrt(ms + NORM_EPS) * g


def _gelu(y):
    return 0.5 * y * (1.0 + lax.erf(y * math.sqrt(0.5)))


def _identity(y):
    return y


def _norm_matmul_kernel(x_ref, g_ref, w_ref, o_ref, h_ref, *, act):
    @pl.when(pl.program_id(1) == 0)
    def _():
        h_ref[...] = _rms(x_ref[...], g_ref[...]).astype(BF16)

    y = jnp.dot(h_ref[...], w_ref[...].astype(BF16), preferred_element_type=F32)
    o_ref[...] = act(y).astype(o_ref.dtype)


def _norm_matmul(x, g_all, layer, w_all, w_layer, act, out_dtype, tm, tn):
    T = x.shape[0]
    N = w_all.shape[-1]
    return pl.pallas_call(
        functools.partial(_norm_matmul_kernel, act=act),
        grid=(T // tm, N // tn),
        in_specs=[
            pl.BlockSpec((tm, D_MODEL), lambda i, j: (i, 0)),
            pl.BlockSpec((None, 1, D_MODEL), lambda i, j: (layer, 0, 0)),
            pl.BlockSpec((None, D_MODEL, tn), lambda i, j: (w_layer, 0, j)),
        ],
        out_specs=pl.BlockSpec((tm, tn), lambda i, j: (i, j)),
        out_shape=jax.ShapeDtypeStruct((T, N), out_dtype),
        scratch_shapes=[pltpu.VMEM((tm, D_MODEL), BF16)],
        compiler_params=_params("parallel", "arbitrary"),
        name="norm_matmul",
    )(x, g_all, w_all)


def _gmlp_gate_kernel(z_ref, lng_ref, lnb_ref, wmix_ref, bias_ref, wout_ref, x_ref,
                      o_ref, v_ref, acc_ref, mu_ref, rstd_ref):
    g = pl.program_id(1)
    tm = z_ref.shape[0]

    @pl.when(g == 0)
    def _():
        zv = z_ref[:, A_HALF:].astype(F32)
        mu = jnp.mean(zv, axis=-1, keepdims=True)
        xc = zv - mu
        mu_ref[...] = mu
        rstd_ref[...] = lax.rsqrt(jnp.mean(xc * xc, axis=-1, keepdims=True) + NORM_EPS)

    off = pl.multiple_of(g * A_GROUP_DIM, LANES)
    u = z_ref[:, pl.ds(off, A_GROUP_DIM)]
    zv = z_ref[:, pl.ds(A_HALF + off, A_GROUP_DIM)].astype(F32)
    v = (zv - mu_ref[...]) * rstd_ref[...] * lng_ref[:, pl.ds(off, A_GROUP_DIM)] \
        + lnb_ref[:, pl.ds(off, A_GROUP_DIM)]
    v_ref[...] = v
    vb = v.astype(BF16)
    wm = wmix_ref[...].astype(BF16)
    bb = bias_ref[...]
    bias = jnp.concatenate([bb] * (A_GROUP_DIM // LANES), axis=1)
    rows = []
    for c in range(tm // A_CHUNK):
        sl = slice(c * A_CHUNK, (c + 1) * A_CHUNK)
        s = jnp.dot(wm, vb[sl], preferred_element_type=F32) + bias
        rows.append((u[sl].astype(F32) * s).astype(BF16))
    gated = jnp.concatenate(rows, axis=0)
    part = jnp.dot(gated, wout_ref[...].astype(BF16), preferred_element_type=F32)

    @pl.when(g == 0)
    def _():
        acc_ref[...] = part

    @pl.when(g > 0)
    def _():
        acc_ref[...] += part

    @pl.when(g == A_GROUPS - 1)
    def _():
        o_ref[...] = x_ref[...] + acc_ref[...]


def _gmlp_gate(z, x, ln_g, ln_b, wmix, bias, w_out, j, n_prompt_blocks, tm):
    T = x.shape[0]
    nb = T // tm
    assert nb == n_prompt_blocks + 1, "the sample rows must be exactly the last row block"

    def kind(i):
        return jnp.where(i >= n_prompt_blocks, 1, 0)

    return pl.pallas_call(
        _gmlp_gate_kernel,
        grid=(nb, A_GROUPS),
        in_specs=[
            pl.BlockSpec((tm, 2 * A_HALF), lambda i, g: (i, 0)),
            pl.BlockSpec((None, 1, A_HALF), lambda i, g: (j, 0, 0)),
            pl.BlockSpec((None, 1, A_HALF), lambda i, g: (j, 0, 0)),
            pl.BlockSpec((None, None, A_CHUNK, A_CHUNK), lambda i, g: (kind(i), g, 0, 0)),
            pl.BlockSpec((None, None, A_CHUNK, LANES), lambda i, g: (kind(i), g, 0, 0)),
            pl.BlockSpec((None, A_GROUP_DIM, D_MODEL), lambda i, g: (j, g, 0)),
            pl.BlockSpec((tm, D_MODEL), lambda i, g: (i, 0)),
        ],
        out_specs=[
            pl.BlockSpec((tm, D_MODEL), lambda i, g: (i, 0)),
            pl.BlockSpec((tm, A_GROUP_DIM), lambda i, g: (0, jnp.where(i == nb - 1, g, 0))),
        ],
        out_shape=[
            jax.ShapeDtypeStruct((T, D_MODEL), F32),
            jax.ShapeDtypeStruct((tm, A_HALF), F32),
        ],
        scratch_shapes=[
            pltpu.VMEM((tm, D_MODEL), F32),
            pltpu.VMEM((tm, 1), F32),
            pltpu.VMEM((tm, 1), F32),
        ],
        compiler_params=_params("arbitrary", "arbitrary"),
        name="gmlp_gate",
    )(z, ln_g, ln_b, wmix, bias, w_out, x)


def _swiglu_part(h, wg, wu, wd):
    a = jnp.dot(h, wg.astype(BF16), preferred_element_type=F32)
    b = jnp.dot(h, wu.astype(BF16), preferred_element_type=F32)
    m = (a * jax.nn.sigmoid(a) * b).astype(BF16)
    return jnp.dot(m, wd.astype(BF16), preferred_element_type=F32)


def _ffn_kernel(x_ref, g_ref, wg_ref, wu_ref, wd_ref, o_ref, h_ref, acc_ref):
    f = pl.program_id(1)

    @pl.when(f == 0)
    def _():
        h_ref[...] = _rms(x_ref[...], g_ref[...]).astype(BF16)

    part = _swiglu_part(h_ref[...], wg_ref[...], wu_ref[...], wd_ref[...])

    @pl.when(f == 0)
    def _():
        acc_ref[...] = part

    @pl.when(f > 0)
    def _():
        acc_ref[...] += part

    @pl.when(f == pl.num_programs(1) - 1)
    def _():
        o_ref[...] = x_ref[...] + acc_ref[...]


def _ffn_dense(x, g_all, layer, wg, wu, wd, m, tm, tf):
    T = x.shape[0]
    return pl.pallas_call(
        _ffn_kernel,
        grid=(T // tm, D_FF // tf),
        in_specs=[
            pl.BlockSpec((tm, D_MODEL), lambda i, f: (i, 0)),
            pl.BlockSpec((None, 1, D_MODEL), lambda i, f: (layer, 0, 0)),
            pl.BlockSpec((None, D_MODEL, tf), lambda i, f: (m, 0, f)),
            pl.BlockSpec((None, D_MODEL, tf), lambda i, f: (m, 0, f)),
            pl.BlockSpec((None, tf, D_MODEL), lambda i, f: (m, f, 0)),
        ],
        out_specs=pl.BlockSpec((tm, D_MODEL), lambda i, f: (i, 0)),
        out_shape=jax.ShapeDtypeStruct((T, D_MODEL), F32),
        scratch_shapes=[pltpu.VMEM((tm, D_MODEL), BF16), pltpu.VMEM((tm, D_MODEL), F32)],
        compiler_params=_params("parallel", "arbitrary"),
        name="ffn_dense",
    )(x, g_all, wg, wu, wd)


def _matmul_res_kernel(a_ref, w_ref, x_ref, o_ref, wb_ref):
    @pl.when(pl.program_id(0) == 0)
    def _():
        wb_ref[...] = w_ref[...].astype(BF16)

    o_ref[...] = x_ref[...] + jnp.dot(a_ref[...], wb_ref[...], preferred_element_type=F32)


def _matmul_res(a, w_all, j, x, tm):
    T = x.shape[0]
    return pl.pallas_call(
        _matmul_res_kernel,
        grid=(T // tm,),
        in_specs=[
            pl.BlockSpec((tm, D_MODEL), lambda i: (i, 0)),
            pl.BlockSpec((None, D_MODEL, D_MODEL), lambda i: (j, 0, 0)),
            pl.BlockSpec((tm, D_MODEL), lambda i: (i, 0)),
        ],
        out_specs=pl.BlockSpec((tm, D_MODEL), lambda i: (i, 0)),
        out_shape=jax.ShapeDtypeStruct((T, D_MODEL), F32),
        scratch_shapes=[pltpu.VMEM((D_MODEL, D_MODEL), BF16)],
        compiler_params=_params("arbitrary"),
        name="matmul_res",
    )(a, w_all, x)


_NT = (((1,), (1,)), ((), ()))
_TN = (((0,), (0,)), ((), ()))


HG_GROUP = 4
MAX_BLOCK_DECAY = 75.0


def _hgrn_gates(qpre, fpre, lb):
    q = qpre * jax.nn.sigmoid(qpre)
    sig = jax.nn.sigmoid(fpre)
    f = lb + (1.0 - lb) * sig
    logf = jnp.log(jnp.maximum(f, FORGET_FLOOR))
    k = (1.0 - lb) * (1.0 - sig)
    return q, k, logf


def _cumsum_rows(x):
    C = x.shape[0]
    if C >= HG_SUB:
        r = lax.broadcasted_iota(jnp.int32, (C, C), 0)
        c = lax.broadcasted_iota(jnp.int32, (C, C), 1)
        tri = jnp.where(r >= c, 1.0, 0.0).astype(F32)
        return jnp.dot(tri, x, preferred_element_type=F32, precision=lax.Precision.HIGHEST)
    row = lax.broadcasted_iota(jnp.int32, (C, 1), 0)
    out = jnp.zeros_like(x)
    for s in range(C):
        out = out + jnp.where(row >= s, x[s:s + 1], 0.0)
    return out


def _chunk_prep(q, k, logf):
    C = q.shape[0]
    b = _cumsum_rows(logf)
    b_last = b[C - 1:C]
    qe = (q * jnp.exp(b)).astype(BF16)
    kd = (k * jnp.exp(b_last - b)).astype(BF16)
    return b, qe, kd, jnp.exp(b_last)


def _intra_exact(q, k, v, b, sub):
    C = q.shape[0]
    row = lax.broadcasted_iota(jnp.int32, (sub, 1), 0)
    parts = []
    for blk in range(C // sub):
        lo = blk * sub
        b_i, q_i, k_i, v_i = b[lo:lo + sub], q[lo:lo + sub], k[lo:lo + sub], v[lo:lo + sub]
        if blk == 0:
            o_i = jnp.zeros((sub, HG_DK), F32)
        else:
            ref = b[lo - 1:lo]
            qs = (q_i * jnp.exp(b_i - ref)).astype(BF16)
            ks = (k[:lo] * jnp.exp(ref - b[:lo])).astype(BF16)
            a = lax.dot_general(qs, ks, _NT, preferred_element_type=F32)
            o_i = jnp.dot(a.astype(BF16), v[:lo].astype(BF16), preferred_element_type=F32)
        for s in range(sub):
            rel = jnp.where(row >= s, b_i - b_i[s:s + 1], MASKED_LOG)
            a_col = jnp.sum(q_i * k_i[s:s + 1] * jnp.exp(rel), axis=-1, keepdims=True)
            o_i = o_i + a_col * v_i[s:s + 1]
        parts.append(o_i)
    return parts[0] if len(parts) == 1 else jnp.concatenate(parts, axis=0)


def _intra_factored(q, k, v, b, sub):
    C = q.shape[0]
    vb = v.astype(BF16)
    heads = [slice(h * HG_DK, (h + 1) * HG_DK) for h in range(q.shape[1] // HG_DK)]
    scores = []
    for blk in range(C // sub):
        lo, hi = blk * sub, (blk + 1) * sub
        if blk == 0:
            qs = q[:hi] * jnp.exp(b[:hi])
            ks = k[:hi] * jnp.exp(-b[:hi])
        else:
            ref = b[lo - 1:lo]
            qs = q[lo:hi] * jnp.exp(b[lo:hi] - ref)
            ks = k[:hi] * jnp.exp(ref - b[:hi])
        qs, ks = qs.astype(BF16), ks.astype(BF16)
        r = lax.broadcasted_iota(jnp.int32, (sub, hi), 0)
        c = lax.broadcasted_iota(jnp.int32, (sub, hi), 1)
        row = []
        for cols in heads:
            a = lax.dot_general(qs[:, cols], ks[:, cols], _NT, preferred_element_type=F32)
            row.append(jnp.where(c <= r + lo, a, 0.0).astype(BF16))
        scores.append(row)
    parts = []
    for blk, row in enumerate(scores):
        hi = (blk + 1) * sub
        outs = [jnp.dot(a, vb[:hi, cols], preferred_element_type=F32) for a, cols in zip(row, heads)]
        parts.append(outs[0] if len(outs) == 1 else jnp.concatenate(outs, axis=1))
    return jnp.concatenate(parts, axis=0)


def _state_step(o_intra, qe, kd, eb_last, v, st):
    o = o_intra + lax.dot_general(qe, st.astype(BF16), _NT, preferred_element_type=F32)
    st_new = st * eb_last + lax.dot_general(v.astype(BF16), kd, _TN, preferred_element_type=F32)
    return o, st_new


def _hgrn_chunk(q, k, v, logf, st, sub):
    b, qe, kd, eb_last = _chunk_prep(q, k, logf)
    return _state_step(_intra_exact(q, k, v, b, sub), qe, kd, eb_last, v, st)


def _head_norm_gate(o, ng, gate):
    return (o * lax.rsqrt(jnp.mean(o * o, axis=-1, keepdims=True) + NORM_EPS) * ng * gate)


def _hgrn_prompt_kernel(q_ref, f_ref, i_ref, g_ref, lb_ref, ng_ref, o_ref, sfin_ref,
                        st_ref, qs_ref, ks_ref, lf_ref, qe_ref, kd_ref, eb_ref, oi_ref):
    tb = pl.program_id(2)
    rows_total = q_ref.shape[0]
    n_chunks = rows_total // HG_CHUNK

    @pl.when(tb == 0)
    def _():
        st_ref[...] = jnp.zeros_like(st_ref)

    q, k, logf = _hgrn_gates(q_ref[...], f_ref[...], lb_ref[...])
    qs_ref[...] = q
    ks_ref[...] = k
    lf_ref[...] = logf
    block_decay = jnp.sum(logf.reshape(rows_total // HG_SUB, HG_SUB, logf.shape[-1]), axis=1)
    mild = jnp.min(block_decay) >= -MAX_BLOCK_DECAY

    heads = [slice(h * HG_DK, (h + 1) * HG_DK) for h in range(HG_GROUP)]

    def exact_group(qg, kg, vg, b, sub):
        return jnp.concatenate([_intra_exact(qg[:, c], kg[:, c], vg[:, c], b[:, c], sub) for c in heads], axis=1)

    def intra_pass(intra):
        def body(ci, carry):
            rows = pl.ds(pl.multiple_of(ci * HG_CHUNK, HG_CHUNK), HG_CHUNK)
            qg, kg, vg = qs_ref[rows, :], ks_ref[rows, :], i_ref[rows, :]
            b, qe, kd, eb_last = _chunk_prep(qg, kg, lf_ref[rows, :])
            qe_ref[rows, :] = qe
            kd_ref[rows, :] = kd
            eb_ref[pl.ds(pl.multiple_of(ci * SUBLANES, SUBLANES), SUBLANES), :] = \
                jnp.broadcast_to(eb_last, (SUBLANES, eb_last.shape[1]))
            oi_ref[rows, :] = intra(qg, kg, vg, b, HG_SUB)
            return carry
        lax.fori_loop(0, n_chunks, body, 0)

    @pl.when(mild)
    def _():
        intra_pass(_intra_factored)

    @pl.when(jnp.logical_not(mild))
    def _():
        intra_pass(exact_group)

    ng = ng_ref[...]

    def state_body(ci, carry):
        rows = pl.ds(pl.multiple_of(ci * HG_CHUNK, HG_CHUNK), HG_CHUNK)
        st = st_ref[...]
        stb = st.astype(BF16)
        qe, kd, oi = qe_ref[rows, :], kd_ref[rows, :], oi_ref[rows, :]
        vb = i_ref[rows, :].astype(BF16)
        eb_last = eb_ref[pl.ds(pl.multiple_of(ci * SUBLANES, SUBLANES), 1), :]
        outs, adds = [], []
        for c in heads:
            o = oi[:, c] + lax.dot_general(qe[:, c], stb[:, c], _NT, preferred_element_type=F32)
            outs.append(o * lax.rsqrt(jnp.mean(o * o, axis=-1, keepdims=True) + NORM_EPS))
            adds.append(lax.dot_general(vb[:, c], kd[:, c], _TN, preferred_element_type=F32))
        st_ref[...] = st * eb_last + jnp.concatenate(adds, axis=1)
        gate = jax.nn.sigmoid(g_ref[rows, :])
        o_ref[rows, :] = (jnp.concatenate(outs, axis=1) * ng * gate).astype(o_ref.dtype)
        return carry

    lax.fori_loop(0, n_chunks, state_body, 0)

    @pl.when(tb == pl.num_programs(2) - 1)
    def _():
        for h, c in enumerate(heads):
            sfin_ref[h] = st_ref[:, c].T


def _hgrn_prompt(proj, lbs, ng_all, j, n_seq, L, lb_rows):
    nb = L // lb_rows
    ng_groups = HG_HEADS // HG_GROUP
    gw = HG_GROUP * HG_DK

    def field(k):
        return pl.BlockSpec((lb_rows, gw), lambda n, h, t: (n * nb + t, k * ng_groups + h))

    vec = pl.BlockSpec((None, 1, gw), lambda n, h, t: (j, 0, h))
    return pl.pallas_call(
        _hgrn_prompt_kernel,
        grid=(n_seq, ng_groups, nb),
        in_specs=[field(0), field(1), field(2), field(3), vec, vec],
        out_specs=[
            pl.BlockSpec((lb_rows, gw), lambda n, h, t: (n * nb + t, h)),
            pl.BlockSpec((None, HG_GROUP, HG_DK, HG_DK), lambda n, h, t: (n, h, 0, 0)),
        ],
        out_shape=[
            jax.ShapeDtypeStruct((n_seq * L, D_MODEL), BF16),
            jax.ShapeDtypeStruct((n_seq, HG_HEADS, HG_DK, HG_DK), F32),
        ],
        scratch_shapes=[
            pltpu.VMEM((HG_DK, gw), F32),
            pltpu.VMEM((lb_rows, gw), F32),
            pltpu.VMEM((lb_rows, gw), F32),
            pltpu.VMEM((lb_rows, gw), F32),
            pltpu.VMEM((lb_rows, gw), BF16),
            pltpu.VMEM((lb_rows, gw), BF16),
            pltpu.VMEM((lb_rows // HG_CHUNK * SUBLANES, gw), F32),
            pltpu.VMEM((lb_rows, gw), F32),
        ],
        compiler_params=_params("parallel", "parallel", "arbitrary"),
        name="hgrn_prompt",
    )(proj, proj, proj, proj, lbs, ng_all)


def _hgrn_sample_kernel(q_ref, f_ref, i_ref, g_ref, lb_ref, ng_ref, s0_ref, o_ref, sfin_ref):
    lb = lb_ref[...]
    ng = ng_ref[...]
    n_seq = s0_ref.shape[0]
    row = lax.broadcasted_iota(jnp.int32, (SUBLANES, 1), 0)
    for p in range(n_seq // 2):
        rows = slice(p * SUBLANES, (p + 1) * SUBLANES)
        q, k, logf = _hgrn_gates(q_ref[rows, :], f_ref[rows, :], lb)
        gate = jax.nn.sigmoid(g_ref[rows, :])
        v = i_ref[rows, :]
        o_tile = jnp.zeros((SUBLANES, HG_DK), F32)
        for half in range(2):
            mine = (row >= half * SAMPLE_LEN) & (row < (half + 1) * SAMPLE_LEN)
            seq = 2 * p + half
            o, st_new = _hgrn_chunk(jnp.where(mine, q, 0.0), jnp.where(mine, k, 0.0),
                                    jnp.where(mine, v, 0.0), jnp.where(mine, logf, 0.0),
                                    s0_ref[seq].T, SUBLANES)
            sfin_ref[seq] = st_new.T
            o_tile = jnp.where(mine, o, o_tile)
        o_ref[rows, :] = _head_norm_gate(o_tile, ng, gate).astype(o_ref.dtype)


def _hgrn_sample(proj, row0, state_all, lbs, ng_all, j, n_seq, sb):
    H = HG_HEADS
    rb = sb * SAMPLE_LEN
    assert row0 % rb == 0 and sb % 2 == 0
    r0 = row0 // rb

    def field(k):
        return pl.BlockSpec((rb, HG_DK), lambda s, h: (r0 + s, k * H + h))

    vec = pl.BlockSpec((None, 1, HG_DK), lambda s, h: (j, 0, h))
    return pl.pallas_call(
        _hgrn_sample_kernel,
        grid=(n_seq // sb, H),
        in_specs=[field(0), field(1), field(2), field(3), vec, vec,
                  pl.BlockSpec((None, sb, None, HG_DK, HG_DK), lambda s, h: (j, s, h, 0, 0))],
        out_specs=[
            pl.BlockSpec((rb, HG_DK), lambda s, h: (s, h)),
            pl.BlockSpec((sb, None, HG_DK, HG_DK), lambda s, h: (s, h, 0, 0)),
        ],
        out_shape=[
            jax.ShapeDtypeStruct((n_seq * SAMPLE_LEN, D_MODEL), BF16),
            jax.ShapeDtypeStruct((n_seq, H, HG_DK, HG_DK), F32),
        ],
        compiler_params=_params("parallel", "parallel"),
        name="hgrn_sample",
    )(proj, proj, proj, proj, lbs, ng_all, state_all)


META_E1, META_E2, META_G1, META_G2, META_R1, META_R2 = range(6)
TOKEN_BLOCK = 128


def _router_kernel(x_ref, g_ref, rw_ref, h_ref, meta_ref, cnt_ref, cb_ref, carry_ref):
    i = pl.program_id(0)
    tm = x_ref.shape[0]

    @pl.when(i == 0)
    def _():
        carry_ref[...] = jnp.zeros_like(carry_ref)

    h = _rms(x_ref[...], g_ref[...])
    h_ref[...] = h.astype(h_ref.dtype)
    logits = jnp.dot(h, rw_ref[...], preferred_element_type=F32, precision=lax.Precision.HIGHEST)
    lane = lax.broadcasted_iota(jnp.int32, (tm, LANES), 1)
    neg = -jnp.inf
    lg = jnp.where(lane < N_EXPERTS, logits, neg)
    m1 = jnp.max(lg, axis=-1, keepdims=True)
    e1 = jnp.min(jnp.where(lg == m1, lane, LANES), axis=-1, keepdims=True)
    lg2 = jnp.where(lane == e1, neg, lg)
    m2 = jnp.max(lg2, axis=-1, keepdims=True)
    e2 = jnp.min(jnp.where(lg2 == m2, lane, LANES), axis=-1, keepdims=True)
    ex = jnp.exp(m2 - m1)
    g1 = 1.0 / (1.0 + ex)
    g2 = ex / (1.0 + ex)

    onehot = jnp.where((lane == e1) | (lane == e2), 1.0, 0.0)
    r = lax.broadcasted_iota(jnp.int32, (tm, tm), 0)
    c = lax.broadcasted_iota(jnp.int32, (tm, tm), 1)
    before = jnp.where(r > c, 1.0, 0.0).astype(BF16)
    seen = jnp.dot(before, onehot.astype(BF16), preferred_element_type=F32) + carry_ref[0:1, :]
    r1 = jnp.sum(jnp.where(lane == e1, seen, 0.0), axis=-1, keepdims=True)
    r2 = jnp.sum(jnp.where(lane == e2, seen, 0.0), axis=-1, keepdims=True)
    for blk in range(tm // TOKEN_BLOCK):
        cb_ref[blk] = jnp.broadcast_to(seen[blk * TOKEN_BLOCK:blk * TOKEN_BLOCK + 1], (SUBLANES, LANES))
    total = carry_ref[0:1, :] + jnp.sum(onehot, axis=0, keepdims=True)
    carry_ref[...] = jnp.broadcast_to(total, carry_ref.shape)
    cnt_ref[...] = jnp.broadcast_to(total, cnt_ref.shape)

    meta = jnp.zeros((tm, LANES), F32)
    for idx, val in ((META_E1, e1.astype(F32)), (META_E2, e2.astype(F32)), (META_G1, g1),
                     (META_G2, g2), (META_R1, r1), (META_R2, r2)):
        meta = jnp.where(lane == idx, val, meta)
    meta_ref[...] = meta


def _router(x, g_all, layer, rw_pad, m, tm):
    T = x.shape[0]
    return pl.pallas_call(
        _router_kernel,
        grid=(T // tm,),
        in_specs=[
            pl.BlockSpec((tm, D_MODEL), lambda i: (i, 0)),
            pl.BlockSpec((None, 1, D_MODEL), lambda i: (layer, 0, 0)),
            pl.BlockSpec((None, D_MODEL, LANES), lambda i: (m, 0, 0)),
        ],
        out_specs=[
            pl.BlockSpec((tm, D_MODEL), lambda i: (i, 0)),
            pl.BlockSpec((tm, LANES), lambda i: (i, 0)),
            pl.BlockSpec((SUBLANES, LANES), lambda i: (0, 0)),
            pl.BlockSpec((tm // TOKEN_BLOCK, SUBLANES, LANES), lambda i: (i, 0, 0)),
        ],
        out_shape=[
            jax.ShapeDtypeStruct((T, D_MODEL), F32),
            jax.ShapeDtypeStruct((T, LANES), F32),
            jax.ShapeDtypeStruct((SUBLANES, LANES), F32),
            jax.ShapeDtypeStruct((T // TOKEN_BLOCK, SUBLANES, LANES), F32),
        ],
        scratch_shapes=[pltpu.VMEM((SUBLANES, LANES), F32)],
        compiler_params=_params("arbitrary"),
        name="moe_router",
    )(x, g_all, rw_pad)


def _experts_kernel(be_ref, nu_ref, xs_ref, wg_ref, wu_ref, wd_ref, o_ref, acc_ref):
    b = pl.program_id(0)
    f = pl.program_id(1)
    used = b < nu_ref[0]

    @pl.when(used)
    def _():
        part = _swiglu_part(xs_ref[...].astype(BF16), wg_ref[...], wu_ref[...], wd_ref[...])

        @pl.when(f == 0)
        def _():
            acc_ref[...] = part

        @pl.when(f > 0)
        def _():
            acc_ref[...] += part

        @pl.when(f == pl.num_programs(1) - 1)
        def _():
            o_ref[...] = acc_ref[...].astype(o_ref.dtype)

    @pl.when(jnp.logical_not(used) & (f == 0))
    def _():
        o_ref[...] = jnp.zeros_like(o_ref)


def _experts(xs, block_expert, n_used, wg, wu, wd, m, bm, tf):
    n_blocks = xs.shape[0] // bm + 1
    nf = D_FF // tf
    last = n_blocks - 2

    def fidx(b, f, nu):
        return jnp.where(b < nu[0], f, nf - 1)

    grid_spec = pltpu.PrefetchScalarGridSpec(
        num_scalar_prefetch=2,
        grid=(n_blocks, nf),
        in_specs=[
            pl.BlockSpec((bm, D_MODEL), lambda b, f, be, nu: (jnp.minimum(b, last), 0)),
            pl.BlockSpec((None, None, D_MODEL, tf), lambda b, f, be, nu: (m, be[b], 0, fidx(b, f, nu))),
            pl.BlockSpec((None, None, D_MODEL, tf), lambda b, f, be, nu: (m, be[b], 0, fidx(b, f, nu))),
            pl.BlockSpec((None, None, tf, D_MODEL), lambda b, f, be, nu: (m, be[b], fidx(b, f, nu), 0)),
        ],
        out_specs=pl.BlockSpec((bm, D_MODEL), lambda b, f, be, nu: (b, 0)),
        scratch_shapes=[pltpu.VMEM((bm, D_MODEL), F32)],
    )
    return pl.pallas_call(
        _experts_kernel,
        grid_spec=grid_spec,
        out_shape=jax.ShapeDtypeStruct((n_blocks * bm, D_MODEL), BF16),
        compiler_params=_params("arbitrary", "arbitrary"),
        name="moe_experts",
    )(block_expert, n_used, xs, wg, wu, wd)


N_WINDOWS = 2 * N_EXPERTS


def _combine_kernel(win_ref, ps_ref, x_ref, meta_ref, *refs):
    ys_refs, o_ref = refs[:N_WINDOWS], refs[N_WINDOWS]
    blk = pl.program_id(0)
    meta = meta_ref[...]
    e1, e2 = meta[:, META_E1:META_E1 + 1], meta[:, META_E2:META_E2 + 1]
    g1, g2 = meta[:, META_G1:META_G1 + 1], meta[:, META_G2:META_G2 + 1]
    d1, d2 = meta[:, META_R1:META_R1 + 1], meta[:, META_R2:META_R2 + 1]
    for e in range(N_EXPERTS):
        start = ps_ref[e].astype(F32)
        d1 = d1 + jnp.where(e1 == e, start, 0.0)
        d2 = d2 + jnp.where(e2 == e, start, 0.0)
    lane = lax.broadcasted_iota(jnp.int32, (1, TOKEN_BLOCK), 1)
    acc = x_ref[...]
    for w in range(N_WINDOWS):
        e = w // 2
        slot = (win_ref[blk * N_WINDOWS + w] * TOKEN_BLOCK + lane).astype(F32)
        sel = jnp.where((d1 == slot) & (e1 == e), g1, 0.0) + jnp.where((d2 == slot) & (e2 == e), g2, 0.0)
        acc = acc + jnp.dot(sel.astype(BF16), ys_refs[w][...], preferred_element_type=F32)
    o_ref[...] = acc


def _combine(x, meta, ys, windows, pstart):
    T = x.shape[0]
    nb = T // TOKEN_BLOCK

    def window(w):
        return pl.BlockSpec((TOKEN_BLOCK, D_MODEL), lambda i, win, ps: (win[i * N_WINDOWS + w], 0))

    grid_spec = pltpu.PrefetchScalarGridSpec(
        num_scalar_prefetch=2,
        grid=(nb,),
        in_specs=[pl.BlockSpec((TOKEN_BLOCK, D_MODEL), lambda i, win, ps: (i, 0)),
                  pl.BlockSpec((TOKEN_BLOCK, LANES), lambda i, win, ps: (i, 0))]
                 + [window(w) for w in range(N_WINDOWS)],
        out_specs=pl.BlockSpec((TOKEN_BLOCK, D_MODEL), lambda i, win, ps: (i, 0)),
    )
    return pl.pallas_call(
        _combine_kernel,
        grid_spec=grid_spec,
        out_shape=jax.ShapeDtypeStruct((T, D_MODEL), F32),
        compiler_params=_params("arbitrary"),
        name="moe_combine",
    )(windows, pstart, x, meta, *([ys] * N_WINDOWS))


def _final_norm_kernel(x_ref, g_ref, o_ref):
    o_ref[...] = _rms(x_ref[...], g_ref[...])


def _final_norm(x, g, row0, n_rows, tm):
    r0 = row0 // tm
    return pl.pallas_call(
        _final_norm_kernel,
        grid=(n_rows // tm,),
        in_specs=[pl.BlockSpec((tm, D_MODEL), lambda i: (r0 + i, 0)),
                  pl.BlockSpec((1, D_MODEL), lambda i: (0, 0))],
        out_specs=pl.BlockSpec((tm, D_MODEL), lambda i: (i, 0)),
        out_shape=jax.ShapeDtypeStruct((n_rows, D_MODEL), F32),
        compiler_params=_params("parallel"),
        name="final_norm",
    )(x, g)


def _moe_layer(x, norm_ffn_g, layer, rw_pad, wg, wu, wd, m, tm, bm, tf):
    T = x.shape[0]
    h, meta, cnt, cblk = _router(x, norm_ffn_g, layer, rw_pad, m, tm)
    e = meta[:, META_E1:META_E2 + 1].astype(jnp.int32)
    rank = meta[:, META_R1:META_R2 + 1].astype(jnp.int32)
    counts = cnt[0, :N_EXPERTS].astype(jnp.int32)
    padded = (counts + bm - 1) // bm * bm
    pend = jnp.cumsum(padded)
    pstart = pend - padded
    dest = pstart[e] + rank
    n_blocks = (T * 2) // bm + N_EXPERTS
    block_start = jnp.arange(n_blocks + 1, dtype=jnp.int32) * bm
    block_expert = jnp.minimum(jnp.sum(pend[None, :] <= block_start[:, None], axis=1),
                               N_EXPERTS - 1).astype(jnp.int32)
    n_used = (pend[-1] // bm).astype(jnp.int32).reshape(1)
    tok = jnp.broadcast_to(jnp.arange(T, dtype=jnp.int32)[:, None], (T, 2))
    slot_tok = jnp.zeros((n_blocks * bm,), jnp.int32).at[dest.reshape(-1)].set(tok.reshape(-1))
    xs = h.at[slot_tok].get(mode="promise_in_bounds")
    ys = _experts(xs, block_expert, n_used, wg, wu, wd, m, bm, tf)
    run_start = pstart[None, :] + cblk[:, 0, :N_EXPERTS].astype(jnp.int32)
    first = run_start // TOKEN_BLOCK
    windows = jnp.stack([first, first + 1], axis=-1).reshape(-1).astype(jnp.int32)
    return _combine(x, meta, ys, windows, pstart.astype(jnp.int32))


def _tril(w):
    n = w.shape[-1]
    return jnp.where(jnp.tril(jnp.ones((n, n), dtype=bool)), w, jnp.zeros((), w.dtype))


def _mix_tables(w_s, b_s, sample_len):
    reps = A_CHUNK // sample_len
    eye = jnp.eye(reps, dtype=w_s.dtype)
    w_prompt = _tril(w_s)
    w_small = _tril(w_s[:, :sample_len, :sample_len])
    w_sample = jax.vmap(lambda w: jnp.kron(eye, w))(w_small)
    b_prompt = b_s
    b_sample = jnp.tile(b_s[:, :sample_len], (1, reps))
    wmix = jnp.stack([w_prompt, w_sample])
    bias = jnp.stack([b_prompt, b_sample])[..., None]
    return wmix, jnp.broadcast_to(bias, bias.shape[:-1] + (LANES,))


def kernel(x_prompt, x_sample, state_hgrn, norm_mix_g, norm_ffn_g, final_norm_g, a_w_in, a_ln_g, a_ln_b, a_w_s, a_b_s, a_w_out, b_w_in, b_lb_logits, b_norm_g, b_w_out, ffn_w_gate, ffn_w_up, ffn_w_down, moe_router, moe_w_gate, moe_w_up, moe_w_down):
    n_p, L, d = x_prompt.shape
    n_s, l_s, _ = x_sample.shape
    assert d == D_MODEL and l_s == SAMPLE_LEN and L % A_CHUNK == 0
    T_p, T_s = n_p * L, n_s * l_s
    T = T_p + T_s
    depth = norm_mix_g.shape[0]

    tm_gate = T_s
    assert T_p % tm_gate == 0 and tm_gate % A_CHUNK == 0
    tm = next(t for t in (768, 512, 256, 128) if T % t == 0)
    bm = 512
    tf = 512
    lb_rows = min(L, 512)

    x = jnp.concatenate([x_prompt.reshape(T_p, d), x_sample.reshape(T_s, d)], axis=0)

    p = jax.nn.softmax(b_lb_logits.astype(F32), axis=0)
    lbs = (jnp.cumsum(p, axis=0) - p[0:1])[:, None, :]
    mix_g = norm_mix_g[:, None, :]
    ffn_g = norm_ffn_g[:, None, :]
    b_ng = b_norm_g[:, None, :]
    ln_g = a_ln_g[:, None, :]
    ln_b = a_ln_b[:, None, :]
    rw_pad = jnp.pad(moe_router, ((0, 0), (0, 0), (0, LANES - N_EXPERTS)))
    moe_wg = moe_w_gate.astype(BF16)
    moe_wu = moe_w_up.astype(BF16)
    moe_wd = moe_w_down.astype(BF16)

    hg_prompt, hg_sample, v_sample = [], [], []
    for layer in range(depth):
        j = layer // 2
        if layer % 2 == 0:
            z = _norm_matmul(x, mix_g, layer, a_w_in, j, _gelu, BF16, tm, 512)
            wmix, bias = _mix_tables(a_w_s[j], a_b_s[j], l_s)
            x, v = _gmlp_gate(z, x, ln_g, ln_b, wmix, bias, a_w_out, j, T_p // tm_gate, tm_gate)
            v_sample.append(v.reshape(n_s, l_s, A_HALF))
            x = _ffn_dense(x, ffn_g, layer, ffn_w_gate, ffn_w_up, ffn_w_down, j, tm, tf)
        else:
            proj = _norm_matmul(x, mix_g, layer, b_w_in, j, _identity, F32, tm, 512)
            o_p, s_p = _hgrn_prompt(proj, lbs, b_ng, j, n_p, L, lb_rows)
            o_s, s_s = _hgrn_sample(proj, T_p, state_hgrn, lbs, b_ng, j, n_s, 8)
            hg_prompt.append(s_p)
            hg_sample.append(s_s)
            o = jnp.concatenate([o_p, o_s], axis=0)
            x = _matmul_res(o, b_w_out, j, x, tm)
            x = _moe_layer(x, ffn_g, layer, rw_pad, moe_wg, moe_wu, moe_wd, j, 512, bm, tf)

    fg = final_norm_g[None, :]
    y_prompt = _final_norm(x, fg, 0, T_p, 512).reshape(n_p, L, d)
    y_sample = _final_norm(x, fg, T_p, T_s, T_s).reshape(n_s, l_s, d)
    return (y_prompt, y_sample, jnp.stack(hg_prompt), jnp.stack(hg_sample), jnp.stack(v_sample))
```

```python
import functools
import math

import jax
import jax.numpy as jnp
from jax import lax
from jax.experimental import pallas as pl
from jax.experimental.pallas import tpu as pltpu

F32 = jnp.float32
BF16 = jnp.bfloat16

D_MODEL = 1024
A_CHUNK = 128
A_HALF = 3 * D_MODEL
A_GROUPS = 8
A_GROUP_DIM = A_HALF // A_GROUPS
HG_HEADS = 8
HG_DK = 128
HG_CHUNK = 64
HG_SUB = 16
FORGET_FLOOR = 1e-20
D_FF = 7 * D_MODEL // 2
N_EXPERTS = 8
NORM_EPS = 1e-6
LANES = 128
SUBLANES = 8
SAMPLE_LEN = 4
MASKED_LOG = -1e30

VMEM_LIMIT = 56 * 1024 * 1024


def _params(*sem):
    return pltpu.CompilerParams(dimension_semantics=sem, vmem_limit_bytes=VMEM_LIMIT)


def _rms(x, g):
    ms = jnp.mean(x * x, axis=-1, keepdims=True)
    return x * lax.rsqrt(ms + NORM_EPS) * g


def _gelu(y):
    return 0.5 * y * (1.0 + lax.erf(y * math.sqrt(0.5)))


def _identity(y):
    return y


def _norm_matmul_kernel(x_ref, g_ref, w_ref, o_ref, h_ref, *, act):
    @pl.when(pl.program_id(1) == 0)
    def _():
        h_ref[...] = _rms(x_ref[...], g_ref[...]).astype(BF16)

    y = jnp.dot(h_ref[...], w_ref[...].astype(BF16), preferred_element_type=F32)
    o_ref[...] = act(y).astype(o_ref.dtype)


def _norm_matmul(x, g_all, layer, w_all, w_layer, act, out_dtype, tm, tn):
    T = x.shape[0]
    N = w_all.shape[-1]
    return pl.pallas_call(
        functools.partial(_norm_matmul_kernel, act=act),
        grid=(T // tm, N // tn),
        in_specs=[
            pl.BlockSpec((tm, D_MODEL), lambda i, j: (i, 0)),
            pl.BlockSpec((None, 1, D_MODEL), lambda i, j: (layer, 0, 0)),
            pl.BlockSpec((None, D_MODEL, tn), lambda i, j: (w_layer, 0, j)),
        ],
        out_specs=pl.BlockSpec((tm, tn), lambda i, j: (i, j)),
        out_shape=jax.ShapeDtypeStruct((T, N), out_dtype),
        scratch_shapes=[pltpu.VMEM((tm, D_MODEL), BF16)],
        compiler_params=_params("parallel", "arbitrary"),
        name="norm_matmul",
    )(x, g_all, w_all)


def _gmlp_gate_kernel(z_ref, lng_ref, lnb_ref, wmix_ref, bias_ref, wout_ref, x_ref, o_ref, v_ref):
    tm = z_ref.shape[0]
    zv = z_ref[:, A_HALF:].astype(F32)
    mu = jnp.mean(zv, axis=-1, keepdims=True)
    xc = zv - mu
    rstd = lax.rsqrt(jnp.mean(xc * xc, axis=-1, keepdims=True) + NORM_EPS)
    v = xc * rstd * lng_ref[...] + lnb_ref[...]
    v_ref[...] = v
    vb = v.astype(BF16)
    groups = []
    for g in range(A_GROUPS):
        cols = slice(g * A_GROUP_DIM, (g + 1) * A_GROUP_DIM)
        wm = wmix_ref[g].astype(BF16)
        bias = jnp.concatenate([bias_ref[g]] * (A_GROUP_DIM // LANES), axis=1)
        rows = []
        for c in range(tm // A_CHUNK):
            sl = slice(c * A_CHUNK, (c + 1) * A_CHUNK)
            s = jnp.dot(wm, vb[sl, cols], preferred_element_type=F32) + bias
            rows.append((z_ref[sl, cols].astype(F32) * s).astype(BF16))
        groups.append(jnp.concatenate(rows, axis=0))
    gated = jnp.concatenate(groups, axis=1)
    o_ref[...] = x_ref[...] + jnp.dot(gated, wout_ref[...], preferred_element_type=F32)


def _gmlp_gate(z, x, ln_g, ln_b, wmix, bias, w_out, j, n_prompt_blocks, tm):
    T = x.shape[0]
    nb = T // tm

    def kind(i):
        return jnp.where(i >= n_prompt_blocks, 1, 0)

    return pl.pallas_call(
        _gmlp_gate_kernel,
        grid=(nb,),
        in_specs=[
            pl.BlockSpec((tm, 2 * A_HALF), lambda i: (i, 0)),
            pl.BlockSpec((None, 1, A_HALF), lambda i: (j, 0, 0)),
            pl.BlockSpec((None, 1, A_HALF), lambda i: (j, 0, 0)),
            pl.BlockSpec((None, A_GROUPS, A_CHUNK, A_CHUNK), lambda i: (kind(i), 0, 0, 0)),
            pl.BlockSpec((None, A_GROUPS, A_CHUNK, LANES), lambda i: (kind(i), 0, 0, 0)),
            pl.BlockSpec((None, A_HALF, D_MODEL), lambda i: (j, 0, 0)),
            pl.BlockSpec((tm, D_MODEL), lambda i: (i, 0)),
        ],
        out_specs=[
            pl.BlockSpec((tm, D_MODEL), lambda i: (i, 0)),
            pl.BlockSpec((tm, A_HALF), lambda i: (jnp.maximum(i - n_prompt_blocks, 0), 0)),
        ],
        out_shape=[
            jax.ShapeDtypeStruct((T, D_MODEL), F32),
            jax.ShapeDtypeStruct(((nb - n_prompt_blocks) * tm, A_HALF), F32),
        ],
        compiler_params=_params("arbitrary"),
        name="gmlp_gate",
    )(z, ln_g, ln_b, wmix, bias, w_out, x)


def _swiglu_part(h, wg, wu, wd):
    a = jnp.dot(h, wg.astype(BF16), preferred_element_type=F32)
    b = jnp.dot(h, wu.astype(BF16), preferred_element_type=F32)
    m = (a * jax.nn.sigmoid(a) * b).astype(BF16)
    return jnp.dot(m, wd.astype(BF16), preferred_element_type=F32)


def _ffn_kernel(x_ref, g_ref, wg_ref, wu_ref, wd_ref, o_ref, h_ref):
    f = pl.program_id(1)

    @pl.when(f == 0)
    def _():
        h_ref[...] = _rms(x_ref[...], g_ref[...]).astype(BF16)

    part = _swiglu_part(h_ref[...], wg_ref[...], wu_ref[...], wd_ref[...])

    @pl.when(f == 0)
    def _():
        o_ref[...] = x_ref[...] + part

    @pl.when(f > 0)
    def _():
        o_ref[...] += part


def _ffn_dense(x, g_all, layer, wg, wu, wd, m, tm, tf):
    T = x.shape[0]
    return pl.pallas_call(
        _ffn_kernel,
        grid=(T // tm, D_FF // tf),
        in_specs=[
            pl.BlockSpec((tm, D_MODEL), lambda i, f: (i, 0)),
            pl.BlockSpec((None, 1, D_MODEL), lambda i, f: (layer, 0, 0)),
            pl.BlockSpec((None, D_MODEL, tf), lambda i, f: (m, 0, f)),
            pl.BlockSpec((None, D_MODEL, tf), lambda i, f: (m, 0, f)),
            pl.BlockSpec((None, tf, D_MODEL), lambda i, f: (m, f, 0)),
        ],
        out_specs=pl.BlockSpec((tm, D_MODEL), lambda i, f: (i, 0)),
        out_shape=jax.ShapeDtypeStruct((T, D_MODEL), F32),
        scratch_shapes=[pltpu.VMEM((tm, D_MODEL), BF16)],
        compiler_params=_params("parallel", "arbitrary"),
        name="ffn_dense",
    )(x, g_all, wg, wu, wd)


def _matmul_res_kernel(a_ref, w_ref, x_ref, o_ref):
    o_ref[...] = x_ref[...] + jnp.dot(a_ref[...], w_ref[...], preferred_element_type=F32)


def _matmul_res(a, w_all, j, x, tm):
    T = x.shape[0]
    return pl.pallas_call(
        _matmul_res_kernel,
        grid=(T // tm,),
        in_specs=[
            pl.BlockSpec((tm, D_MODEL), lambda i: (i, 0)),
            pl.BlockSpec((None, D_MODEL, D_MODEL), lambda i: (j, 0, 0)),
            pl.BlockSpec((tm, D_MODEL), lambda i: (i, 0)),
        ],
        out_specs=pl.BlockSpec((tm, D_MODEL), lambda i: (i, 0)),
        out_shape=jax.ShapeDtypeStruct((T, D_MODEL), F32),
        compiler_params=_params("parallel"),
        name="matmul_res",
    )(a, w_all, x)


_NT = (((1,), (1,)), ((), ()))
_TN = (((0,), (0,)), ((), ()))


HG_GROUP = 4
MAX_BLOCK_DECAY = 75.0


def _hgrn_gates(qpre, fpre, lb):
    q = qpre * jax.nn.sigmoid(qpre)
    sig = jax.nn.sigmoid(fpre)
    f = lb + (1.0 - lb) * sig
    logf = jnp.log(jnp.maximum(f, FORGET_FLOOR))
    k = (1.0 - lb) * (1.0 - sig)
    return q, k, logf


def _cumsum_rows(x):
    C = x.shape[0]
    if C >= HG_SUB:
        r = lax.broadcasted_iota(jnp.int32, (C, C), 0)
        c = lax.broadcasted_iota(jnp.int32, (C, C), 1)
        tri = jnp.where(r >= c, 1.0, 0.0).astype(F32)
        return jnp.dot(tri, x, preferred_element_type=F32, precision=lax.Precision.HIGHEST)
    row = lax.broadcasted_iota(jnp.int32, (C, 1), 0)
    out = jnp.zeros_like(x)
    for s in range(C):
        out = out + jnp.where(row >= s, x[s:s + 1], 0.0)
    return out


def _chunk_prep(q, k, logf):
    C = q.shape[0]
    b = _cumsum_rows(logf)
    b_last = b[C - 1:C]
    qe = (q * jnp.exp(b)).astype(BF16)
    kd = (k * jnp.exp(b_last - b)).astype(BF16)
    return b, qe, kd, jnp.exp(b_last)


def _intra_exact(q, k, v, b, sub):
    C = q.shape[0]
    row = lax.broadcasted_iota(jnp.int32, (sub, 1), 0)
    parts = []
    for blk in range(C // sub):
        lo = blk * sub
        b_i, q_i, k_i, v_i = b[lo:lo + sub], q[lo:lo + sub], k[lo:lo + sub], v[lo:lo + sub]
        if blk == 0:
            o_i = jnp.zeros((sub, HG_DK), F32)
        else:
            ref = b[lo - 1:lo]
            qs = (q_i * jnp.exp(b_i - ref)).astype(BF16)
            ks = (k[:lo] * jnp.exp(ref - b[:lo])).astype(BF16)
            a = lax.dot_general(qs, ks, _NT, preferred_element_type=F32)
            o_i = jnp.dot(a.astype(BF16), v[:lo].astype(BF16), preferred_element_type=F32)
        for s in range(sub):
            rel = jnp.where(row >= s, b_i - b_i[s:s + 1], MASKED_LOG)
            a_col = jnp.sum(q_i * k_i[s:s + 1] * jnp.exp(rel), axis=-1, keepdims=True)
            o_i = o_i + a_col * v_i[s:s + 1]
        parts.append(o_i)
    return parts[0] if len(parts) == 1 else jnp.concatenate(parts, axis=0)


def _intra_factored(q, k, v, b, sub):
    C = q.shape[0]
    vb = v.astype(BF16)
    heads = [slice(h * HG_DK, (h + 1) * HG_DK) for h in range(q.shape[1] // HG_DK)]
    scores = []
    for blk in range(C // sub):
        lo, hi = blk * sub, (blk + 1) * sub
        if blk == 0:
            qs = q[:hi] * jnp.exp(b[:hi])
            ks = k[:hi] * jnp.exp(-b[:hi])
        else:
            ref = b[lo - 1:lo]
            qs = q[lo:hi] * jnp.exp(b[lo:hi] - ref)
            ks = k[:hi] * jnp.exp(ref - b[:hi])
        qs, ks = qs.astype(BF16), ks.astype(BF16)
        r = lax.broadcasted_iota(jnp.int32, (sub, hi), 0)
        c = lax.broadcasted_iota(jnp.int32, (sub, hi), 1)
        row = []
        for cols in heads:
            a = lax.dot_general(qs[:, cols], ks[:, cols], _NT, preferred_element_type=F32)
            row.append(jnp.where(c <= r + lo, a, 0.0).astype(BF16))
        scores.append(row)
    parts = []
    for blk, row in enumerate(scores):
        hi = (blk + 1) * sub
        outs = [jnp.dot(a, vb[:hi, cols], preferred_element_type=F32) for a, cols in zip(row, heads)]
        parts.append(outs[0] if len(outs) == 1 else jnp.concatenate(outs, axis=1))
    return jnp.concatenate(parts, axis=0)


def _state_step(o_intra, qe, kd, eb_last, v, st):
    o = o_intra + lax.dot_general(qe, st.astype(BF16), _NT, preferred_element_type=F32)
    st_new = st * eb_last + lax.dot_general(v.astype(BF16), kd, _TN, preferred_element_type=F32)
    return o, st_new


def _hgrn_chunk(q, k, v, logf, st, sub):
    b, qe, kd, eb_last = _chunk_prep(q, k, logf)
    return _state_step(_intra_exact(q, k, v, b, sub), qe, kd, eb_last, v, st)


def _head_norm_gate(o, ng, gate):
    return (o * lax.rsqrt(jnp.mean(o * o, axis=-1, keepdims=True) + NORM_EPS) * ng * gate)


def _hgrn_prompt_kernel(q_ref, f_ref, i_ref, g_ref, lb_ref, ng_ref, o_ref, sfin_ref,
                        st_ref, qs_ref, ks_ref, lf_ref, qe_ref, kd_ref, eb_ref, oi_ref):
    tb = pl.program_id(2)
    rows_total = q_ref.shape[0]
    n_chunks = rows_total // HG_CHUNK

    @pl.when(tb == 0)
    def _():
        st_ref[...] = jnp.zeros_like(st_ref)

    q, k, logf = _hgrn_gates(q_ref[...], f_ref[...], lb_ref[...])
    qs_ref[...] = q
    ks_ref[...] = k
    lf_ref[...] = logf
    block_decay = jnp.sum(logf.reshape(rows_total // HG_SUB, HG_SUB, logf.shape[-1]), axis=1)
    mild = jnp.min(block_decay) >= -MAX_BLOCK_DECAY

    heads = [slice(h * HG_DK, (h + 1) * HG_DK) for h in range(HG_GROUP)]

    def exact_group(qg, kg, vg, b, sub):
        return jnp.concatenate([_intra_exact(qg[:, c], kg[:, c], vg[:, c], b[:, c], sub) for c in heads], axis=1)

    def intra_pass(intra):
        def body(ci, carry):
            rows = pl.ds(pl.multiple_of(ci * HG_CHUNK, HG_CHUNK), HG_CHUNK)
            qg, kg, vg = qs_ref[rows, :], ks_ref[rows, :], i_ref[rows, :]
            b, qe, kd, eb_last = _chunk_prep(qg, kg, lf_ref[rows, :])
            qe_ref[rows, :] = qe
            kd_ref[rows, :] = kd
            eb_ref[pl.ds(pl.multiple_of(ci * SUBLANES, SUBLANES), SUBLANES), :] = \
                jnp.broadcast_to(eb_last, (SUBLANES, eb_last.shape[1]))
            oi_ref[rows, :] = intra(qg, kg, vg, b, HG_SUB)
            return carry
        lax.fori_loop(0, n_chunks, body, 0)

    @pl.when(mild)
    def _():
        intra_pass(_intra_factored)

    @pl.when(jnp.logical_not(mild))
    def _():
        intra_pass(exact_group)

    ng = ng_ref[...]

    def state_body(ci, carry):
        rows = pl.ds(pl.multiple_of(ci * HG_CHUNK, HG_CHUNK), HG_CHUNK)
        st = st_ref[...]
        stb = st.astype(BF16)
        qe, kd, oi = qe_ref[rows, :], kd_ref[rows, :], oi_ref[rows, :]
        vb = i_ref[rows, :].astype(BF16)
        eb_last = eb_ref[pl.ds(pl.multiple_of(ci * SUBLANES, SUBLANES), 1), :]
        outs, adds = [], []
        for c in heads:
            o = oi[:, c] + lax.dot_general(qe[:, c], stb[:, c], _NT, preferred_element_type=F32)
            outs.append(o * lax.rsqrt(jnp.mean(o * o, axis=-1, keepdims=True) + NORM_EPS))
            adds.append(lax.dot_general(vb[:, c], kd[:, c], _TN, preferred_element_type=F32))
        st_ref[...] = st * eb_last + jnp.concatenate(adds, axis=1)
        gate = jax.nn.sigmoid(g_ref[rows, :])
        o_ref[rows, :] = (jnp.concatenate(outs, axis=1) * ng * gate).astype(o_ref.dtype)
        return carry

    lax.fori_loop(0, n_chunks, state_body, 0)

    @pl.when(tb == pl.num_programs(2) - 1)
    def _():
        for h, c in enumerate(heads):
            sfin_ref[h] = st_ref[:, c].T


def _hgrn_prompt(proj, lbs, ng_all, j, n_seq, L, lb_rows):
    nb = L // lb_rows
    ng_groups = HG_HEADS // HG_GROUP
    gw = HG_GROUP * HG_DK

    def field(k):
        return pl.BlockSpec((lb_rows, gw), lambda n, h, t: (n * nb + t, k * ng_groups + h))

    vec = pl.BlockSpec((None, 1, gw), lambda n, h, t: (j, 0, h))
    return pl.pallas_call(
        _hgrn_prompt_kernel,
        grid=(n_seq, ng_groups, nb),
        in_specs=[field(0), field(1), field(2), field(3), vec, vec],
        out_specs=[
            pl.BlockSpec((lb_rows, gw), lambda n, h, t: (n * nb + t, h)),
            pl.BlockSpec((None, HG_GROUP, HG_DK, HG_DK), lambda n, h, t: (n, h, 0, 0)),
        ],
        out_shape=[
            jax.ShapeDtypeStruct((n_seq * L, D_MODEL), BF16),
            jax.ShapeDtypeStruct((n_seq, HG_HEADS, HG_DK, HG_DK), F32),
        ],
        scratch_shapes=[
            pltpu.VMEM((HG_DK, gw), F32),
            pltpu.VMEM((lb_rows, gw), F32),
            pltpu.VMEM((lb_rows, gw), F32),
            pltpu.VMEM((lb_rows, gw), F32),
            pltpu.VMEM((lb_rows, gw), BF16),
            pltpu.VMEM((lb_rows, gw), BF16),
            pltpu.VMEM((lb_rows // HG_CHUNK * SUBLANES, gw), F32),
            pltpu.VMEM((lb_rows, gw), F32),
        ],
        compiler_params=_params("parallel", "parallel", "arbitrary"),
        name="hgrn_prompt",
    )(proj, proj, proj, proj, lbs, ng_all)


def _hgrn_sample_kernel(q_ref, f_ref, i_ref, g_ref, lb_ref, ng_ref, s0_ref, o_ref, sfin_ref):
    lb = lb_ref[...]
    ng = ng_ref[...]
    n_seq = s0_ref.shape[0]
    row = lax.broadcasted_iota(jnp.int32, (SUBLANES, 1), 0)
    for p in range(n_seq // 2):
        rows = slice(p * SUBLANES, (p + 1) * SUBLANES)
        q, k, logf = _hgrn_gates(q_ref[rows, :], f_ref[rows, :], lb)
        gate = jax.nn.sigmoid(g_ref[rows, :])
        v = i_ref[rows, :]
        o_tile = jnp.zeros((SUBLANES, HG_DK), F32)
        for half in range(2):
            mine = (row >= half * SAMPLE_LEN) & (row < (half + 1) * SAMPLE_LEN)
            seq = 2 * p + half
            o, st_new = _hgrn_chunk(jnp.where(mine, q, 0.0), jnp.where(mine, k, 0.0),
                                    jnp.where(mine, v, 0.0), jnp.where(mine, logf, 0.0),
                                    s0_ref[seq].T, SUBLANES)
            sfin_ref[seq] = st_new.T
            o_tile = jnp.where(mine, o, o_tile)
        o_ref[rows, :] = _head_norm_gate(o_tile, ng, gate).astype(o_ref.dtype)


def _hgrn_sample(proj, row0, state_all, lbs, ng_all, j, n_seq, sb):
    H = HG_HEADS
    rb = sb * SAMPLE_LEN
    assert row0 % rb == 0 and sb % 2 == 0
    r0 = row0 // rb

    def field(k):
        return pl.BlockSpec((rb, HG_DK), lambda s, h: (r0 + s, k * H + h))

    vec = pl.BlockSpec((None, 1, HG_DK), lambda s, h: (j, 0, h))
    return pl.pallas_call(
        _hgrn_sample_kernel,
        grid=(n_seq // sb, H),
        in_specs=[field(0), field(1), field(2), field(3), vec, vec,
                  pl.BlockSpec((None, sb, None, HG_DK, HG_DK), lambda s, h: (j, s, h, 0, 0))],
        out_specs=[
            pl.BlockSpec((rb, HG_DK), lambda s, h: (s, h)),
            pl.BlockSpec((None, sb, None, HG_DK, HG_DK), lambda s, h: (j, s, h, 0, 0)),
        ],
        out_shape=[
            jax.ShapeDtypeStruct((n_seq * SAMPLE_LEN, D_MODEL), BF16),
            jax.ShapeDtypeStruct(state_all.shape, state_all.dtype),
        ],
        input_output_aliases={6: 1},
        compiler_params=_params("parallel", "parallel"),
        name="hgrn_sample",
    )(proj, proj, proj, proj, lbs, ng_all, state_all)


META_E1, META_E2, META_G1, META_G2, META_R1, META_R2 = range(6)
TOKEN_BLOCK = 128


def _router_kernel(x_ref, g_ref, rw_ref, h_ref, meta_ref, cnt_ref, cb_ref, carry_ref):
    i = pl.program_id(0)
    tm = x_ref.shape[0]

    @pl.when(i == 0)
    def _():
        carry_ref[...] = jnp.zeros_like(carry_ref)

    h = _rms(x_ref[...], g_ref[...])
    h_ref[...] = h.astype(h_ref.dtype)
    logits = jnp.dot(h.astype(BF16), rw_ref[...].astype(BF16), preferred_element_type=F32)
    lane = lax.broadcasted_iota(jnp.int32, (tm, LANES), 1)
    neg = -jnp.inf
    lg = jnp.where(lane < N_EXPERTS, logits, neg)
    m1 = jnp.max(lg, axis=-1, keepdims=True)
    e1 = jnp.min(jnp.where(lg == m1, lane, LANES), axis=-1, keepdims=True)
    lg2 = jnp.where(lane == e1, neg, lg)
    m2 = jnp.max(lg2, axis=-1, keepdims=True)
    e2 = jnp.min(jnp.where(lg2 == m2, lane, LANES), axis=-1, keepdims=True)
    ex = jnp.exp(m2 - m1)
    g1 = 1.0 / (1.0 + ex)
    g2 = ex / (1.0 + ex)

    onehot = jnp.where((lane == e1) | (lane == e2), 1.0, 0.0)
    r = lax.broadcasted_iota(jnp.int32, (tm, tm), 0)
    c = lax.broadcasted_iota(jnp.int32, (tm, tm), 1)
    before = jnp.where(r > c, 1.0, 0.0).astype(BF16)
    seen = jnp.dot(before, onehot.astype(BF16), preferred_element_type=F32) + carry_ref[0:1, :]
    r1 = jnp.sum(jnp.where(lane == e1, seen, 0.0), axis=-1, keepdims=True)
    r2 = jnp.sum(jnp.where(lane == e2, seen, 0.0), axis=-1, keepdims=True)
    for blk in range(tm // TOKEN_BLOCK):
        cb_ref[blk] = jnp.broadcast_to(seen[blk * TOKEN_BLOCK:blk * TOKEN_BLOCK + 1], (SUBLANES, LANES))
    total = carry_ref[0:1, :] + jnp.sum(onehot, axis=0, keepdims=True)
    carry_ref[...] = jnp.broadcast_to(total, carry_ref.shape)
    cnt_ref[...] = jnp.broadcast_to(total, cnt_ref.shape)

    meta = jnp.zeros((tm, LANES), F32)
    for idx, val in ((META_E1, e1.astype(F32)), (META_E2, e2.astype(F32)), (META_G1, g1),
                     (META_G2, g2), (META_R1, r1), (META_R2, r2)):
        meta = jnp.where(lane == idx, val, meta)
    meta_ref[...] = meta


def _router(x, g_all, layer, rw_pad, m, tm):
    T = x.shape[0]
    return pl.pallas_call(
        _router_kernel,
        grid=(T // tm,),
        in_specs=[
            pl.BlockSpec((tm, D_MODEL), lambda i: (i, 0)),
            pl.BlockSpec((None, 1, D_MODEL), lambda i: (layer, 0, 0)),
            pl.BlockSpec((None, D_MODEL, LANES), lambda i: (m, 0, 0)),
        ],
        out_specs=[
            pl.BlockSpec((tm, D_MODEL), lambda i: (i, 0)),
            pl.BlockSpec((tm, LANES), lambda i: (i, 0)),
            pl.BlockSpec((SUBLANES, LANES), lambda i: (0, 0)),
            pl.BlockSpec((tm // TOKEN_BLOCK, SUBLANES, LANES), lambda i: (i, 0, 0)),
        ],
        out_shape=[
            jax.ShapeDtypeStruct((T, D_MODEL), F32),
            jax.ShapeDtypeStruct((T, LANES), F32),
            jax.ShapeDtypeStruct((SUBLANES, LANES), F32),
            jax.ShapeDtypeStruct((T // TOKEN_BLOCK, SUBLANES, LANES), F32),
        ],
        scratch_shapes=[pltpu.VMEM((SUBLANES, LANES), F32)],
        compiler_params=_params("arbitrary"),
        name="moe_router",
    )(x, g_all, rw_pad)


def _experts_kernel(be_ref, nu_ref, xs_ref, wg_ref, wu_ref, wd_ref, o_ref, acc_ref, xb_ref):
    b = pl.program_id(0)
    f = pl.program_id(1)
    used = b < nu_ref[0]

    @pl.when(used & (f == 0))
    def _():
        xb_ref[...] = xs_ref[...].astype(BF16)

    @pl.when(used)
    def _():
        part = _swiglu_part(xb_ref[...], wg_ref[...], wu_ref[...], wd_ref[...])

        @pl.when(f == 0)
        def _():
            acc_ref[...] = part

        @pl.when(f > 0)
        def _():
            acc_ref[...] += part

        @pl.when(f == pl.num_programs(1) - 1)
        def _():
            o_ref[...] = acc_ref[...].astype(o_ref.dtype)

    @pl.when(jnp.logical_not(used) & (f == 0))
    def _():
        o_ref[...] = jnp.zeros_like(o_ref)


def _experts(xs, block_expert, n_used, wg, wu, wd, m, bm, tf):
    n_blocks = xs.shape[0] // bm + 1
    nf = D_FF // tf
    last = n_blocks - 2

    def fidx(b, f, nu):
        return jnp.where(b < nu[0], f, nf - 1)

    grid_spec = pltpu.PrefetchScalarGridSpec(
        num_scalar_prefetch=2,
        grid=(n_blocks, nf),
        in_specs=[
            pl.BlockSpec((bm, D_MODEL), lambda b, f, be, nu: (jnp.minimum(b, last), 0)),
            pl.BlockSpec((None, None, D_MODEL, tf), lambda b, f, be, nu: (m, be[b], 0, fidx(b, f, nu))),
            pl.BlockSpec((None, None, D_MODEL, tf), lambda b, f, be, nu: (m, be[b], 0, fidx(b, f, nu))),
            pl.BlockSpec((None, None, tf, D_MODEL), lambda b, f, be, nu: (m, be[b], fidx(b, f, nu), 0)),
        ],
        out_specs=pl.BlockSpec((bm, D_MODEL), lambda b, f, be, nu: (b, 0)),
        scratch_shapes=[pltpu.VMEM((bm, D_MODEL), F32), pltpu.VMEM((bm, D_MODEL), BF16)],
    )
    return pl.pallas_call(
        _experts_kernel,
        grid_spec=grid_spec,
        out_shape=jax.ShapeDtypeStruct((n_blocks * bm, D_MODEL), BF16),
        compiler_params=_params("arbitrary", "arbitrary"),
        name="moe_experts",
    )(block_expert, n_used, xs, wg, wu, wd)


N_WINDOWS = 2 * N_EXPERTS


def _combine_kernel(win_ref, ps_ref, x_ref, meta_ref, *refs):
    ys_refs, o_ref = refs[:N_WINDOWS], refs[N_WINDOWS]
    blk = pl.program_id(0)
    meta = meta_ref[...]
    e1, e2 = meta[:, META_E1:META_E1 + 1], meta[:, META_E2:META_E2 + 1]
    g1, g2 = meta[:, META_G1:META_G1 + 1], meta[:, META_G2:META_G2 + 1]
    d1, d2 = meta[:, META_R1:META_R1 + 1], meta[:, META_R2:META_R2 + 1]
    for e in range(N_EXPERTS):
        start = ps_ref[e].astype(F32)
        d1 = d1 + jnp.where(e1 == e, start, 0.0)
        d2 = d2 + jnp.where(e2 == e, start, 0.0)
    lane = lax.broadcasted_iota(jnp.int32, (1, TOKEN_BLOCK), 1)
    acc = x_ref[...]
    for w in range(N_WINDOWS):
        e = w // 2
        slot = (win_ref[blk * N_WINDOWS + w] * TOKEN_BLOCK + lane).astype(F32)
        sel = jnp.where((d1 == slot) & (e1 == e), g1, 0.0) + jnp.where((d2 == slot) & (e2 == e), g2, 0.0)
        acc = acc + jnp.dot(sel.astype(BF16), ys_refs[w][...], preferred_element_type=F32)
    o_ref[...] = acc


def _combine(x, meta, ys, windows, pstart):
    T = x.shape[0]
    nb = T // TOKEN_BLOCK

    def window(w):
        return pl.BlockSpec((TOKEN_BLOCK, D_MODEL), lambda i, win, ps: (win[i * N_WINDOWS + w], 0))

    grid_spec = pltpu.PrefetchScalarGridSpec(
        num_scalar_prefetch=2,
        grid=(nb,),
        in_specs=[pl.BlockSpec((TOKEN_BLOCK, D_MODEL), lambda i, win, ps: (i, 0)),
                  pl.BlockSpec((TOKEN_BLOCK, LANES), lambda i, win, ps: (i, 0))]
                 + [window(w) for w in range(N_WINDOWS)],
        out_specs=pl.BlockSpec((TOKEN_BLOCK, D_MODEL), lambda i, win, ps: (i, 0)),
    )
    return pl.pallas_call(
        _combine_kernel,
        grid_spec=grid_spec,
        out_shape=jax.ShapeDtypeStruct((T, D_MODEL), F32),
        compiler_params=_params("arbitrary"),
        name="moe_combine",
    )(windows, pstart, x, meta, *([ys] * N_WINDOWS))


def _final_norm_kernel(x_ref, g_ref, o_ref):
    o_ref[...] = _rms(x_ref[...], g_ref[...])


def _final_norm(x, g, row0, n_rows, tm):
    r0 = row0 // tm
    return pl.pallas_call(
        _final_norm_kernel,
        grid=(n_rows // tm,),
        in_specs=[pl.BlockSpec((tm, D_MODEL), lambda i: (r0 + i, 0)),
                  pl.BlockSpec((1, D_MODEL), lambda i: (0, 0))],
        out_specs=pl.BlockSpec((tm, D_MODEL), lambda i: (i, 0)),
        out_shape=jax.ShapeDtypeStruct((n_rows, D_MODEL), F32),
        compiler_params=_params("parallel"),
        name="final_norm",
    )(x, g)


def _moe_layer(x, norm_ffn_g, layer, rw_pad, wg, wu, wd, m, tm, bm, tf):
    T = x.shape[0]
    h, meta, cnt, cblk = _router(x, norm_ffn_g, layer, rw_pad, m, tm)
    e = meta[:, META_E1:META_E2 + 1].astype(jnp.int32)
    rank = meta[:, META_R1:META_R2 + 1].astype(jnp.int32)
    counts = cnt[0, :N_EXPERTS].astype(jnp.int32)
    padded = (counts + bm - 1) // bm * bm
    pend = jnp.cumsum(padded)
    pstart = pend - padded
    dest = pstart[e] + rank
    n_blocks = (T * 2) // bm + N_EXPERTS
    block_start = jnp.arange(n_blocks + 1, dtype=jnp.int32) * bm
    block_expert = jnp.minimum(jnp.sum(pend[None, :] <= block_start[:, None], axis=1),
                               N_EXPERTS - 1).astype(jnp.int32)
    n_used = (pend[-1] // bm).astype(jnp.int32).reshape(1)
    tok = jnp.broadcast_to(jnp.arange(T, dtype=jnp.int32)[:, None], (T, 2))
    slot_tok = jnp.zeros((n_blocks * bm,), jnp.int32).at[dest.reshape(-1)].set(tok.reshape(-1))
    xs = h.at[slot_tok].get(mode="promise_in_bounds")
    ys = _experts(xs, block_expert, n_used, wg, wu, wd, m, bm, tf)
    run_start = pstart[None, :] + cblk[:, 0, :N_EXPERTS].astype(jnp.int32)
    first = run_start // TOKEN_BLOCK
    windows = jnp.stack([first, first + 1], axis=-1).reshape(-1).astype(jnp.int32)
    return _combine(x, meta, ys, windows, pstart.astype(jnp.int32))


def _tril(w):
    n = w.shape[-1]
    return jnp.where(jnp.tril(jnp.ones((n, n), dtype=bool)), w, jnp.zeros((), w.dtype))


def _mix_tables(w_s, b_s, sample_len):
    reps = A_CHUNK // sample_len
    eye = jnp.eye(reps, dtype=w_s.dtype)
    w_prompt = _tril(w_s)
    w_small = _tril(w_s[:, :sample_len, :sample_len])
    w_sample = jax.vmap(lambda w: jnp.kron(eye, w))(w_small)
    b_prompt = b_s
    b_sample = jnp.tile(b_s[:, :sample_len], (1, reps))
    wmix = jnp.stack([w_prompt, w_sample])
    bias = jnp.stack([b_prompt, b_sample])[..., None]
    return wmix, jnp.broadcast_to(bias, bias.shape[:-1] + (LANES,))


def kernel(x_prompt, x_sample, state_hgrn, norm_mix_g, norm_ffn_g, final_norm_g, a_w_in, a_ln_g, a_ln_b, a_w_s, a_b_s, a_w_out, b_w_in, b_lb_logits, b_norm_g, b_w_out, ffn_w_gate, ffn_w_up, ffn_w_down, moe_router, moe_w_gate, moe_w_up, moe_w_down):
    n_p, L, d = x_prompt.shape
    n_s, l_s, _ = x_sample.shape
    assert d == D_MODEL and l_s == SAMPLE_LEN and L % A_CHUNK == 0
    T_p, T_s = n_p * L, n_s * l_s
    T = T_p + T_s
    depth = norm_mix_g.shape[0]

    tm_gate = 2 * A_CHUNK
    assert T_p % tm_gate == 0 and T_s % tm_gate == 0
    tm = next(t for t in (768, 512, 256, 128) if T % t == 0)
    tm_ffn = next(t for t in (1536, 768, 512, 256, 128) if T % t == 0)
    bm = 1024
    tf = 512
    lb_rows = min(L, 512)

    x = jnp.concatenate([x_prompt.reshape(T_p, d), x_sample.reshape(T_s, d)], axis=0)

    p = jax.nn.softmax(b_lb_logits.astype(F32), axis=0)
    lbs = (jnp.cumsum(p, axis=0) - p[0:1])[:, None, :]
    mix_g = norm_mix_g[:, None, :]
    ffn_g = norm_ffn_g[:, None, :]
    b_ng = b_norm_g[:, None, :]
    ln_g = a_ln_g[:, None, :]
    ln_b = a_ln_b[:, None, :]
    rw_pad = jnp.pad(moe_router, ((0, 0), (0, 0), (0, LANES - N_EXPERTS)))
    a_w_in, a_w_out, b_w_in, b_w_out, ffn_w_gate, ffn_w_up, ffn_w_down = (
        w.astype(BF16) for w in (a_w_in, a_w_out, b_w_in, b_w_out, ffn_w_gate, ffn_w_up, ffn_w_down))

    hg_prompt, v_sample = [], []
    hg_sample = state_hgrn
    for layer in range(depth):
        j = layer // 2
        if layer % 2 == 0:
            z = _norm_matmul(x, mix_g, layer, a_w_in, j, _gelu, BF16, tm, 1024)
            wmix, bias = _mix_tables(a_w_s[j], a_b_s[j], l_s)
            x, v = _gmlp_gate(z, x, ln_g, ln_b, wmix, bias, a_w_out, j, T_p // tm_gate, tm_gate)
            v_sample.append(v.reshape(n_s, l_s, A_HALF))
            x = _ffn_dense(x, ffn_g, layer, ffn_w_gate, ffn_w_up, ffn_w_down, j, tm_ffn, tf)
        else:
            proj = _norm_matmul(x, mix_g, layer, b_w_in, j, _identity, F32, tm, 1024)
            o_p, s_p = _hgrn_prompt(proj, lbs, b_ng, j, n_p, L, lb_rows)
            o_s, hg_sample = _hgrn_sample(proj, T_p, hg_sample, lbs, b_ng, j, n_s, 8)
            hg_prompt.append(s_p)
            o = jnp.concatenate([o_p, o_s], axis=0)
            x = _matmul_res(o, b_w_out, j, x, tm)
            x = _moe_layer(x, ffn_g, layer, rw_pad, moe_w_gate, moe_w_up, moe_w_down, j, 512, bm, tf)

    fg = final_norm_g[None, :]
    y_prompt = _final_norm(x, fg, 0, T_p, 512).reshape(n_p, L, d)
    y_sample = _final_norm(x, fg, T_p, T_s, T_s).reshape(n_s, l_s, d)
    return (y_prompt, y_sample, jnp.stack(hg_prompt), hg_sample, jnp.stack(v_sample))
```

```python
import functools
import math

import jax
import jax.numpy as jnp
from jax import lax
from jax.experimental import pallas as pl
from jax.experimental.pallas import tpu as pltpu
from jax.experimental.pallas import tpu_sc as plsc

F32 = jnp.float32
BF16 = jnp.bfloat16

D_MODEL = 1024
A_CHUNK = 128
A_HALF = 3 * D_MODEL
A_GROUPS = 8
A_GROUP_DIM = A_HALF // A_GROUPS
HG_HEADS = 8
HG_DK = 128
HG_CHUNK = 64
HG_SUB = 16
FORGET_FLOOR = 1e-20
D_FF = 7 * D_MODEL // 2
N_EXPERTS = 8
NORM_EPS = 1e-6
LANES = 128
SUBLANES = 8
SAMPLE_LEN = 4
MASKED_LOG = -1e30

VMEM_LIMIT = 56 * 1024 * 1024


def _params(*sem):
    return pltpu.CompilerParams(dimension_semantics=sem, vmem_limit_bytes=VMEM_LIMIT)


def _rms(x, g):
    ms = jnp.mean(x * x, axis=-1, keepdims=True)
    return x * lax.rsqrt(ms + NORM_EPS) * g


def _gelu(y):
    return 0.5 * y * (1.0 + lax.erf(y * math.sqrt(0.5)))


def _identity(y):
    return y


def _norm_matmul_kernel(x_ref, g_ref, w_ref, o_ref, h_ref, *, act):
    @pl.when(pl.program_id(1) == 0)
    def _():
        h_ref[...] = _rms(x_ref[...], g_ref[...]).astype(BF16)

    y = jnp.dot(h_ref[...], w_ref[...].astype(BF16), preferred_element_type=F32)
    o_ref[...] = act(y).astype(o_ref.dtype)


def _norm_matmul(x, g_all, layer, w_all, w_layer, act, out_dtype, tm, tn):
    T = x.shape[0]
    N = w_all.shape[-1]
    return pl.pallas_call(
        functools.partial(_norm_matmul_kernel, act=act),
        grid=(T // tm, N // tn),
        in_specs=[
            pl.BlockSpec((tm, D_MODEL), lambda i, j: (i, 0)),
            pl.BlockSpec((None, 1, D_MODEL), lambda i, j: (layer, 0, 0)),
            pl.BlockSpec((None, D_MODEL, tn), lambda i, j: (w_layer, 0, j)),
        ],
        out_specs=pl.BlockSpec((tm, tn), lambda i, j: (i, j)),
        out_shape=jax.ShapeDtypeStruct((T, N), out_dtype),
        scratch_shapes=[pltpu.VMEM((tm, D_MODEL), BF16)],
        compiler_params=_params("parallel", "arbitrary"),
        name="norm_matmul",
    )(x, g_all, w_all)


def _gmlp_gate_kernel(z_ref, lng_ref, lnb_ref, wmix_ref, bias_ref, wout_ref, x_ref, o_ref, v_ref):
    tm = z_ref.shape[0]
    zv = z_ref[:, A_HALF:].astype(F32)
    mu = jnp.mean(zv, axis=-1, keepdims=True)
    xc = zv - mu
    rstd = lax.rsqrt(jnp.mean(xc * xc, axis=-1, keepdims=True) + NORM_EPS)
    v = xc * rstd * lng_ref[...] + lnb_ref[...]
    v_ref[...] = v
    vb = v.astype(BF16)
    groups = []
    for g in range(A_GROUPS):
        cols = slice(g * A_GROUP_DIM, (g + 1) * A_GROUP_DIM)
        wm = wmix_ref[g].astype(BF16)
        bias = jnp.concatenate([bias_ref[g]] * (A_GROUP_DIM // LANES), axis=1)
        rows = []
        for c in range(tm // A_CHUNK):
            sl = slice(c * A_CHUNK, (c + 1) * A_CHUNK)
            s = jnp.dot(wm, vb[sl, cols], preferred_element_type=F32) + bias
            rows.append((z_ref[sl, cols].astype(F32) * s).astype(BF16))
        groups.append(jnp.concatenate(rows, axis=0))
    gated = jnp.concatenate(groups, axis=1)
    o_ref[...] = x_ref[...] + jnp.dot(gated, wout_ref[...], preferred_element_type=F32)


def _gmlp_gate(z, x, ln_g, ln_b, wmix, bias, w_out, j, n_prompt_blocks, tm):
    T = x.shape[0]
    nb = T // tm

    def kind(i):
        return jnp.where(i >= n_prompt_blocks, 1, 0)

    return pl.pallas_call(
        _gmlp_gate_kernel,
        grid=(nb,),
        in_specs=[
            pl.BlockSpec((tm, 2 * A_HALF), lambda i: (i, 0)),
            pl.BlockSpec((None, 1, A_HALF), lambda i: (j, 0, 0)),
            pl.BlockSpec((None, 1, A_HALF), lambda i: (j, 0, 0)),
            pl.BlockSpec((None, A_GROUPS, A_CHUNK, A_CHUNK), lambda i: (kind(i), 0, 0, 0)),
            pl.BlockSpec((None, A_GROUPS, A_CHUNK, LANES), lambda i: (kind(i), 0, 0, 0)),
            pl.BlockSpec((None, A_HALF, D_MODEL), lambda i: (j, 0, 0)),
            pl.BlockSpec((tm, D_MODEL), lambda i: (i, 0)),
        ],
        out_specs=[
            pl.BlockSpec((tm, D_MODEL), lambda i: (i, 0)),
            pl.BlockSpec((tm, A_HALF), lambda i: (jnp.maximum(i - n_prompt_blocks, 0), 0)),
        ],
        out_shape=[
            jax.ShapeDtypeStruct((T, D_MODEL), F32),
            jax.ShapeDtypeStruct(((nb - n_prompt_blocks) * tm, A_HALF), F32),
        ],
        compiler_params=_params("arbitrary"),
        name="gmlp_gate",
    )(z, ln_g, ln_b, wmix, bias, w_out, x)


def _swiglu_part(h, wg, wu, wd):
    a = jnp.dot(h, wg.astype(BF16), preferred_element_type=F32)
    b = jnp.dot(h, wu.astype(BF16), preferred_element_type=F32)
    m = (a * jax.nn.sigmoid(a) * b).astype(BF16)
    return jnp.dot(m, wd.astype(BF16), preferred_element_type=F32)


def _ffn_kernel(x_ref, g_ref, wg_ref, wu_ref, wd_ref, o_ref, h_ref):
    f = pl.program_id(1)

    @pl.when(f == 0)
    def _():
        h_ref[...] = _rms(x_ref[...], g_ref[...]).astype(BF16)

    part = _swiglu_part(h_ref[...], wg_ref[...], wu_ref[...], wd_ref[...])

    @pl.when(f == 0)
    def _():
        o_ref[...] = x_ref[...] + part

    @pl.when(f > 0)
    def _():
        o_ref[...] += part


def _ffn_dense(x, g_all, layer, wg, wu, wd, m, tm, tf):
    T = x.shape[0]
    return pl.pallas_call(
        _ffn_kernel,
        grid=(T // tm, D_FF // tf),
        in_specs=[
            pl.BlockSpec((tm, D_MODEL), lambda i, f: (i, 0)),
            pl.BlockSpec((None, 1, D_MODEL), lambda i, f: (layer, 0, 0)),
            pl.BlockSpec((None, D_MODEL, tf), lambda i, f: (m, 0, f)),
            pl.BlockSpec((None, D_MODEL, tf), lambda i, f: (m, 0, f)),
            pl.BlockSpec((None, tf, D_MODEL), lambda i, f: (m, f, 0)),
        ],
        out_specs=pl.BlockSpec((tm, D_MODEL), lambda i, f: (i, 0)),
        out_shape=jax.ShapeDtypeStruct((T, D_MODEL), F32),
        scratch_shapes=[pltpu.VMEM((tm, D_MODEL), BF16)],
        compiler_params=_params("parallel", "arbitrary"),
        name="ffn_dense",
    )(x, g_all, wg, wu, wd)


def _matmul_res_kernel(a_ref, w_ref, x_ref, o_ref):
    o_ref[...] = x_ref[...] + jnp.dot(a_ref[...], w_ref[...], preferred_element_type=F32)


def _matmul_res(a, w_all, j, x, tm):
    T = x.shape[0]
    return pl.pallas_call(
        _matmul_res_kernel,
        grid=(T // tm,),
        in_specs=[
            pl.BlockSpec((tm, D_MODEL), lambda i: (i, 0)),
            pl.BlockSpec((None, D_MODEL, D_MODEL), lambda i: (j, 0, 0)),
            pl.BlockSpec((tm, D_MODEL), lambda i: (i, 0)),
        ],
        out_specs=pl.BlockSpec((tm, D_MODEL), lambda i: (i, 0)),
        out_shape=jax.ShapeDtypeStruct((T, D_MODEL), F32),
        compiler_params=_params("parallel"),
        name="matmul_res",
    )(a, w_all, x)


_NT = (((1,), (1,)), ((), ()))
_TN = (((0,), (0,)), ((), ()))


HG_GROUP = 4
MAX_BLOCK_DECAY = 75.0


def _hgrn_gates(qpre, fpre, lb):
    q = qpre * jax.nn.sigmoid(qpre)
    sig = jax.nn.sigmoid(fpre)
    f = lb + (1.0 - lb) * sig
    logf = jnp.log(jnp.maximum(f, FORGET_FLOOR))
    k = (1.0 - lb) * (1.0 - sig)
    return q, k, logf


def _cumsum_rows(x):
    C = x.shape[0]
    if C >= HG_SUB:
        r = lax.broadcasted_iota(jnp.int32, (C, C), 0)
        c = lax.broadcasted_iota(jnp.int32, (C, C), 1)
        tri = jnp.where(r >= c, 1.0, 0.0).astype(F32)
        return jnp.dot(tri, x, preferred_element_type=F32, precision=lax.Precision.HIGHEST)
    row = lax.broadcasted_iota(jnp.int32, (C, 1), 0)
    out = jnp.zeros_like(x)
    for s in range(C):
        out = out + jnp.where(row >= s, x[s:s + 1], 0.0)
    return out


def _chunk_prep(q, k, logf):
    C = q.shape[0]
    b = _cumsum_rows(logf)
    b_last = b[C - 1:C]
    qe = (q * jnp.exp(b)).astype(BF16)
    kd = (k * jnp.exp(b_last - b)).astype(BF16)
    return b, qe, kd, jnp.exp(b_last)


def _intra_exact(q, k, v, b, sub):
    C = q.shape[0]
    row = lax.broadcasted_iota(jnp.int32, (sub, 1), 0)
    parts = []
    for blk in range(C // sub):
        lo = blk * sub
        b_i, q_i, k_i, v_i = b[lo:lo + sub], q[lo:lo + sub], k[lo:lo + sub], v[lo:lo + sub]
        if blk == 0:
            o_i = jnp.zeros((sub, HG_DK), F32)
        else:
            ref = b[lo - 1:lo]
            qs = (q_i * jnp.exp(b_i - ref)).astype(BF16)
            ks = (k[:lo] * jnp.exp(ref - b[:lo])).astype(BF16)
            a = lax.dot_general(qs, ks, _NT, preferred_element_type=F32)
            o_i = jnp.dot(a.astype(BF16), v[:lo].astype(BF16), preferred_element_type=F32)
        for s in range(sub):
            rel = jnp.where(row >= s, b_i - b_i[s:s + 1], MASKED_LOG)
            a_col = jnp.sum(q_i * k_i[s:s + 1] * jnp.exp(rel), axis=-1, keepdims=True)
            o_i = o_i + a_col * v_i[s:s + 1]
        parts.append(o_i)
    return parts[0] if len(parts) == 1 else jnp.concatenate(parts, axis=0)


def _intra_factored(q, k, v, b, sub):
    C = q.shape[0]
    vb = v.astype(BF16)
    heads = [slice(h * HG_DK, (h + 1) * HG_DK) for h in range(q.shape[1] // HG_DK)]
    scores = []
    for blk in range(C // sub):
        lo, hi = blk * sub, (blk + 1) * sub
        if blk == 0:
            qs = q[:hi] * jnp.exp(b[:hi])
            ks = k[:hi] * jnp.exp(-b[:hi])
        else:
            ref = b[lo - 1:lo]
            qs = q[lo:hi] * jnp.exp(b[lo:hi] - ref)
            ks = k[:hi] * jnp.exp(ref - b[:hi])
        qs, ks = qs.astype(BF16), ks.astype(BF16)
        r = lax.broadcasted_iota(jnp.int32, (sub, hi), 0)
        c = lax.broadcasted_iota(jnp.int32, (sub, hi), 1)
        row = []
        for cols in heads:
            a = lax.dot_general(qs[:, cols], ks[:, cols], _NT, preferred_element_type=F32)
            row.append(jnp.where(c <= r + lo, a, 0.0).astype(BF16))
        scores.append(row)
    parts = []
    for blk, row in enumerate(scores):
        hi = (blk + 1) * sub
        outs = [jnp.dot(a, vb[:hi, cols], preferred_element_type=F32) for a, cols in zip(row, heads)]
        parts.append(outs[0] if len(outs) == 1 else jnp.concatenate(outs, axis=1))
    return jnp.concatenate(parts, axis=0)


def _state_step(o_intra, qe, kd, eb_last, v, st):
    o = o_intra + lax.dot_general(qe, st.astype(BF16), _NT, preferred_element_type=F32)
    st_new = st * eb_last + lax.dot_general(v.astype(BF16), kd, _TN, preferred_element_type=F32)
    return o, st_new


def _hgrn_chunk(q, k, v, logf, st, sub):
    b, qe, kd, eb_last = _chunk_prep(q, k, logf)
    return _state_step(_intra_exact(q, k, v, b, sub), qe, kd, eb_last, v, st)


def _head_norm_gate(o, ng, gate):
    return (o * lax.rsqrt(jnp.mean(o * o, axis=-1, keepdims=True) + NORM_EPS) * ng * gate)


def _hgrn_prompt_kernel(q_ref, f_ref, i_ref, g_ref, lb_ref, ng_ref, o_ref, sfin_ref,
                        st_ref, qs_ref, ks_ref, lf_ref, qe_ref, kd_ref, eb_ref, oi_ref):
    tb = pl.program_id(2)
    rows_total = q_ref.shape[0]
    n_chunks = rows_total // HG_CHUNK

    @pl.when(tb == 0)
    def _():
        st_ref[...] = jnp.zeros_like(st_ref)

    q, k, logf = _hgrn_gates(q_ref[...], f_ref[...], lb_ref[...])
    qs_ref[...] = q
    ks_ref[...] = k
    lf_ref[...] = logf
    block_decay = jnp.sum(logf.reshape(rows_total // HG_SUB, HG_SUB, logf.shape[-1]), axis=1)
    mild = jnp.min(block_decay) >= -MAX_BLOCK_DECAY

    heads = [slice(h * HG_DK, (h + 1) * HG_DK) for h in range(HG_GROUP)]

    def exact_group(qg, kg, vg, b, sub):
        return jnp.concatenate([_intra_exact(qg[:, c], kg[:, c], vg[:, c], b[:, c], sub) for c in heads], axis=1)

    def intra_pass(intra):
        def body(ci, carry):
            rows = pl.ds(pl.multiple_of(ci * HG_CHUNK, HG_CHUNK), HG_CHUNK)
            qg, kg, vg = qs_ref[rows, :], ks_ref[rows, :], i_ref[rows, :]
            b, qe, kd, eb_last = _chunk_prep(qg, kg, lf_ref[rows, :])
            qe_ref[rows, :] = qe
            kd_ref[rows, :] = kd
            eb_ref[pl.ds(pl.multiple_of(ci * SUBLANES, SUBLANES), SUBLANES), :] = \
                jnp.broadcast_to(eb_last, (SUBLANES, eb_last.shape[1]))
            oi_ref[rows, :] = intra(qg, kg, vg, b, HG_SUB)
            return carry
        lax.fori_loop(0, n_chunks, body, 0)

    @pl.when(mild)
    def _():
        intra_pass(_intra_factored)

    @pl.when(jnp.logical_not(mild))
    def _():
        intra_pass(exact_group)

    ng = ng_ref[...]

    def state_body(ci, carry):
        rows = pl.ds(pl.multiple_of(ci * HG_CHUNK, HG_CHUNK), HG_CHUNK)
        st = st_ref[...]
        stb = st.astype(BF16)
        qe, kd, oi = qe_ref[rows, :], kd_ref[rows, :], oi_ref[rows, :]
        vb = i_ref[rows, :].astype(BF16)
        eb_last = eb_ref[pl.ds(pl.multiple_of(ci * SUBLANES, SUBLANES), 1), :]
        outs, adds = [], []
        for c in heads:
            o = oi[:, c] + lax.dot_general(qe[:, c], stb[:, c], _NT, preferred_element_type=F32)
            outs.append(o * lax.rsqrt(jnp.mean(o * o, axis=-1, keepdims=True) + NORM_EPS))
            adds.append(lax.dot_general(vb[:, c], kd[:, c], _TN, preferred_element_type=F32))
        st_ref[...] = st * eb_last + jnp.concatenate(adds, axis=1)
        gate = jax.nn.sigmoid(g_ref[rows, :])
        o_ref[rows, :] = (jnp.concatenate(outs, axis=1) * ng * gate).astype(o_ref.dtype)
        return carry

    lax.fori_loop(0, n_chunks, state_body, 0)

    @pl.when(tb == pl.num_programs(2) - 1)
    def _():
        for h, c in enumerate(heads):
            sfin_ref[h] = st_ref[:, c].T


def _hgrn_prompt(proj, lbs, ng_all, j, n_seq, L, lb_rows):
    nb = L // lb_rows
    ng_groups = HG_HEADS // HG_GROUP
    gw = HG_GROUP * HG_DK

    def field(k):
        return pl.BlockSpec((lb_rows, gw), lambda n, h, t: (n * nb + t, k * ng_groups + h))

    vec = pl.BlockSpec((None, 1, gw), lambda n, h, t: (j, 0, h))
    return pl.pallas_call(
        _hgrn_prompt_kernel,
        grid=(n_seq, ng_groups, nb),
        in_specs=[field(0), field(1), field(2), field(3), vec, vec],
        out_specs=[
            pl.BlockSpec((lb_rows, gw), lambda n, h, t: (n * nb + t, h)),
            pl.BlockSpec((None, HG_GROUP, HG_DK, HG_DK), lambda n, h, t: (n, h, 0, 0)),
        ],
        out_shape=[
            jax.ShapeDtypeStruct((n_seq * L, D_MODEL), BF16),
            jax.ShapeDtypeStruct((n_seq, HG_HEADS, HG_DK, HG_DK), F32),
        ],
        scratch_shapes=[
            pltpu.VMEM((HG_DK, gw), F32),
            pltpu.VMEM((lb_rows, gw), F32),
            pltpu.VMEM((lb_rows, gw), F32),
            pltpu.VMEM((lb_rows, gw), F32),
            pltpu.VMEM((lb_rows, gw), BF16),
            pltpu.VMEM((lb_rows, gw), BF16),
            pltpu.VMEM((lb_rows // HG_CHUNK * SUBLANES, gw), F32),
            pltpu.VMEM((lb_rows, gw), F32),
        ],
        compiler_params=_params("parallel", "parallel", "arbitrary"),
        name="hgrn_prompt",
    )(proj, proj, proj, proj, lbs, ng_all)


def _hgrn_sample_kernel(q_ref, f_ref, i_ref, g_ref, lb_ref, ng_ref, s0_ref, o_ref, sfin_ref):
    lb = lb_ref[...]
    ng = ng_ref[...]
    n_seq = s0_ref.shape[0]
    row = lax.broadcasted_iota(jnp.int32, (SUBLANES, 1), 0)
    for p in range(n_seq // 2):
        rows = slice(p * SUBLANES, (p + 1) * SUBLANES)
        q, k, logf = _hgrn_gates(q_ref[rows, :], f_ref[rows, :], lb)
        gate = jax.nn.sigmoid(g_ref[rows, :])
        v = i_ref[rows, :]
        o_tile = jnp.zeros((SUBLANES, HG_DK), F32)
        for half in range(2):
            mine = (row >= half * SAMPLE_LEN) & (row < (half + 1) * SAMPLE_LEN)
            seq = 2 * p + half
            o, st_new = _hgrn_chunk(jnp.where(mine, q, 0.0), jnp.where(mine, k, 0.0),
                                    jnp.where(mine, v, 0.0), jnp.where(mine, logf, 0.0),
                                    s0_ref[seq].T, SUBLANES)
            sfin_ref[seq] = st_new.T
            o_tile = jnp.where(mine, o, o_tile)
        o_ref[rows, :] = _head_norm_gate(o_tile, ng, gate).astype(o_ref.dtype)


def _hgrn_sample(proj, row0, state_all, lbs, ng_all, j, n_seq, sb):
    H = HG_HEADS
    rb = sb * SAMPLE_LEN
    assert row0 % rb == 0 and sb % 2 == 0
    r0 = row0 // rb

    def field(k):
        return pl.BlockSpec((rb, HG_DK), lambda s, h: (r0 + s, k * H + h))

    vec = pl.BlockSpec((None, 1, HG_DK), lambda s, h: (j, 0, h))
    return pl.pallas_call(
        _hgrn_sample_kernel,
        grid=(n_seq // sb, H),
        in_specs=[field(0), field(1), field(2), field(3), vec, vec,
                  pl.BlockSpec((None, sb, None, HG_DK, HG_DK), lambda s, h: (j, s, h, 0, 0))],
        out_specs=[
            pl.BlockSpec((rb, HG_DK), lambda s, h: (s, h)),
            pl.BlockSpec((None, sb, None, HG_DK, HG_DK), lambda s, h: (j, s, h, 0, 0)),
        ],
        out_shape=[
            jax.ShapeDtypeStruct((n_seq * SAMPLE_LEN, D_MODEL), BF16),
            jax.ShapeDtypeStruct(state_all.shape, state_all.dtype),
        ],
        input_output_aliases={6: 1},
        compiler_params=_params("parallel", "parallel"),
        name="hgrn_sample",
    )(proj, proj, proj, proj, lbs, ng_all, state_all)


META_E1, META_E2, META_G1, META_G2, META_R1, META_R2 = range(6)
TOKEN_BLOCK = 128


def _router_kernel(x_ref, g_ref, rw_ref, h_ref, meta_ref, cnt_ref, cb_ref, carry_ref):
    i = pl.program_id(0)
    tm = x_ref.shape[0]

    @pl.when(i == 0)
    def _():
        carry_ref[...] = jnp.zeros_like(carry_ref)

    h = _rms(x_ref[...], g_ref[...])
    h_ref[...] = h.astype(h_ref.dtype)
    logits = jnp.dot(h.astype(BF16), rw_ref[...].astype(BF16), preferred_element_type=F32)
    lane = lax.broadcasted_iota(jnp.int32, (tm, LANES), 1)
    neg = -jnp.inf
    lg = jnp.where(lane < N_EXPERTS, logits, neg)
    m1 = jnp.max(lg, axis=-1, keepdims=True)
    e1 = jnp.min(jnp.where(lg == m1, lane, LANES), axis=-1, keepdims=True)
    lg2 = jnp.where(lane == e1, neg, lg)
    m2 = jnp.max(lg2, axis=-1, keepdims=True)
    e2 = jnp.min(jnp.where(lg2 == m2, lane, LANES), axis=-1, keepdims=True)
    ex = jnp.exp(m2 - m1)
    g1 = 1.0 / (1.0 + ex)
    g2 = ex / (1.0 + ex)

    onehot = jnp.where((lane == e1) | (lane == e2), 1.0, 0.0)
    r = lax.broadcasted_iota(jnp.int32, (tm, tm), 0)
    c = lax.broadcasted_iota(jnp.int32, (tm, tm), 1)
    before = jnp.where(r > c, 1.0, 0.0).astype(BF16)
    seen = jnp.dot(before, onehot.astype(BF16), preferred_element_type=F32) + carry_ref[0:1, :]
    r1 = jnp.sum(jnp.where(lane == e1, seen, 0.0), axis=-1, keepdims=True)
    r2 = jnp.sum(jnp.where(lane == e2, seen, 0.0), axis=-1, keepdims=True)
    for blk in range(tm // TOKEN_BLOCK):
        cb_ref[blk] = jnp.broadcast_to(seen[blk * TOKEN_BLOCK:blk * TOKEN_BLOCK + 1], (SUBLANES, LANES))
    total = carry_ref[0:1, :] + jnp.sum(onehot, axis=0, keepdims=True)
    carry_ref[...] = jnp.broadcast_to(total, carry_ref.shape)
    cnt_ref[...] = jnp.broadcast_to(total, cnt_ref.shape)

    meta = jnp.zeros((tm, LANES), F32)
    for idx, val in ((META_E1, e1.astype(F32)), (META_E2, e2.astype(F32)), (META_G1, g1),
                     (META_G2, g2), (META_R1, r1), (META_R2, r2)):
        meta = jnp.where(lane == idx, val, meta)
    meta_ref[...] = meta


def _router(x, g_all, layer, rw_pad, m, tm):
    T = x.shape[0]
    return pl.pallas_call(
        _router_kernel,
        grid=(T // tm,),
        in_specs=[
            pl.BlockSpec((tm, D_MODEL), lambda i: (i, 0)),
            pl.BlockSpec((None, 1, D_MODEL), lambda i: (layer, 0, 0)),
            pl.BlockSpec((None, D_MODEL, LANES), lambda i: (m, 0, 0)),
        ],
        out_specs=[
            pl.BlockSpec((tm, D_MODEL), lambda i: (i, 0)),
            pl.BlockSpec((tm, LANES), lambda i: (i, 0)),
            pl.BlockSpec((SUBLANES, LANES), lambda i: (0, 0)),
            pl.BlockSpec((tm // TOKEN_BLOCK, SUBLANES, LANES), lambda i: (i, 0, 0)),
        ],
        out_shape=[
            jax.ShapeDtypeStruct((T, D_MODEL), F32),
            jax.ShapeDtypeStruct((T, LANES), F32),
            jax.ShapeDtypeStruct((SUBLANES, LANES), F32),
            jax.ShapeDtypeStruct((T // TOKEN_BLOCK, SUBLANES, LANES), F32),
        ],
        scratch_shapes=[pltpu.VMEM((SUBLANES, LANES), F32)],
        compiler_params=_params("arbitrary"),
        name="moe_router",
    )(x, g_all, rw_pad)


SC_CORES = 2
SC_SUBCORES = 16
SC_CHUNK = 32


def _dispatch(h, slot_tok):
    n_workers = SC_CORES * SC_SUBCORES
    n_slots = slot_tok.shape[0]
    per_worker = n_slots // n_workers
    assert n_slots % (n_workers * SC_CHUNK) == 0
    mesh = plsc.VectorSubcoreMesh(core_axis_name="core", subcore_axis_name="subcore")

    @functools.partial(
        pl.kernel, mesh=mesh,
        out_type=jax.ShapeDtypeStruct((n_slots, D_MODEL), h.dtype),
        scratch_types=[pltpu.VMEM((SC_CHUNK,), jnp.int32),
                       pltpu.VMEM((SC_CHUNK, D_MODEL), h.dtype),
                       pltpu.SemaphoreType.DMA],
        name="moe_dispatch")
    def gather(h_hbm, idx_hbm, xs_hbm, idx_v, rows_v, sem):
        worker = lax.axis_index("subcore") * SC_CORES + lax.axis_index("core")
        base = worker * per_worker

        @pl.loop(0, per_worker // SC_CHUNK)
        def _(c):
            rows = pl.ds(base + c * SC_CHUNK, SC_CHUNK)
            pltpu.sync_copy(idx_hbm.at[rows], idx_v)
            pltpu.async_copy(h_hbm.at[idx_v], rows_v, sem).wait()
            pltpu.sync_copy(rows_v, xs_hbm.at[rows])

    return gather(h, slot_tok)


def _experts_kernel(be_ref, nu_ref, xs_ref, wg_ref, wu_ref, wd_ref, o_ref, acc_ref, xb_ref):
    b = pl.program_id(0)
    f = pl.program_id(1)
    used = b < nu_ref[0]

    @pl.when(used & (f == 0))
    def _():
        xb_ref[...] = xs_ref[...].astype(BF16)

    @pl.when(used)
    def _():
        part = _swiglu_part(xb_ref[...], wg_ref[...], wu_ref[...], wd_ref[...])

        @pl.when(f == 0)
        def _():
            acc_ref[...] = part

        @pl.when(f > 0)
        def _():
            acc_ref[...] += part

        @pl.when(f == pl.num_programs(1) - 1)
        def _():
            o_ref[...] = acc_ref[...].astype(o_ref.dtype)

    @pl.when(jnp.logical_not(used) & (f == 0))
    def _():
        o_ref[...] = jnp.zeros_like(o_ref)


def _experts(xs, block_expert, n_used, wg, wu, wd, m, bm, tf):
    n_blocks = xs.shape[0] // bm + 1
    nf = D_FF // tf
    last = n_blocks - 2

    def fidx(b, f, nu):
        return jnp.where(b < nu[0], f, nf - 1)

    grid_spec = pltpu.PrefetchScalarGridSpec(
        num_scalar_prefetch=2,
        grid=(n_blocks, nf),
        in_specs=[
            pl.BlockSpec((bm, D_MODEL), lambda b, f, be, nu: (jnp.minimum(b, last), 0)),
            pl.BlockSpec((None, None, D_MODEL, tf), lambda b, f, be, nu: (m, be[b], 0, fidx(b, f, nu))),
            pl.BlockSpec((None, None, D_MODEL, tf), lambda b, f, be, nu: (m, be[b], 0, fidx(b, f, nu))),
            pl.BlockSpec((None, None, tf, D_MODEL), lambda b, f, be, nu: (m, be[b], fidx(b, f, nu), 0)),
        ],
        out_specs=pl.BlockSpec((bm, D_MODEL), lambda b, f, be, nu: (b, 0)),
        scratch_shapes=[pltpu.VMEM((bm, D_MODEL), F32), pltpu.VMEM((bm, D_MODEL), BF16)],
    )
    return pl.pallas_call(
        _experts_kernel,
        grid_spec=grid_spec,
        out_shape=jax.ShapeDtypeStruct((n_blocks * bm, D_MODEL), BF16),
        compiler_params=_params("arbitrary", "arbitrary"),
        name="moe_experts",
    )(block_expert, n_used, xs, wg, wu, wd)


N_WINDOWS = 2 * N_EXPERTS


def _combine_kernel(win_ref, ps_ref, x_ref, meta_ref, *refs):
    ys_refs, o_ref = refs[:N_WINDOWS], refs[N_WINDOWS]
    blk = pl.program_id(0)
    meta = meta_ref[...]
    e1, e2 = meta[:, META_E1:META_E1 + 1], meta[:, META_E2:META_E2 + 1]
    g1, g2 = meta[:, META_G1:META_G1 + 1], meta[:, META_G2:META_G2 + 1]
    d1, d2 = meta[:, META_R1:META_R1 + 1], meta[:, META_R2:META_R2 + 1]
    for e in range(N_EXPERTS):
        start = ps_ref[e].astype(F32)
        d1 = d1 + jnp.where(e1 == e, start, 0.0)
        d2 = d2 + jnp.where(e2 == e, start, 0.0)
    lane = lax.broadcasted_iota(jnp.int32, (1, TOKEN_BLOCK), 1)
    acc = x_ref[...]
    for w in range(N_WINDOWS):
        e = w // 2
        slot = (win_ref[blk * N_WINDOWS + w] * TOKEN_BLOCK + lane).astype(F32)
        sel = jnp.where((d1 == slot) & (e1 == e), g1, 0.0) + jnp.where((d2 == slot) & (e2 == e), g2, 0.0)
        acc = acc + jnp.dot(sel.astype(BF16), ys_refs[w][...], preferred_element_type=F32)
    o_ref[...] = acc


def _combine(x, meta, ys, windows, pstart):
    T = x.shape[0]
    nb = T // TOKEN_BLOCK

    def window(w):
        return pl.BlockSpec((TOKEN_BLOCK, D_MODEL), lambda i, win, ps: (win[i * N_WINDOWS + w], 0))

    grid_spec = pltpu.PrefetchScalarGridSpec(
        num_scalar_prefetch=2,
        grid=(nb,),
        in_specs=[pl.BlockSpec((TOKEN_BLOCK, D_MODEL), lambda i, win, ps: (i, 0)),
                  pl.BlockSpec((TOKEN_BLOCK, LANES), lambda i, win, ps: (i, 0))]
                 + [window(w) for w in range(N_WINDOWS)],
        out_specs=pl.BlockSpec((TOKEN_BLOCK, D_MODEL), lambda i, win, ps: (i, 0)),
    )
    return pl.pallas_call(
        _combine_kernel,
        grid_spec=grid_spec,
        out_shape=jax.ShapeDtypeStruct((T, D_MODEL), F32),
        compiler_params=_params("arbitrary"),
        name="moe_combine",
    )(windows, pstart, x, meta, *([ys] * N_WINDOWS))


def _final_norm_kernel(x_ref, g_ref, o_ref):
    o_ref[...] = _rms(x_ref[...], g_ref[...])


def _final_norm(x, g, row0, n_rows, tm):
    r0 = row0 // tm
    return pl.pallas_call(
        _final_norm_kernel,
        grid=(n_rows // tm,),
        in_specs=[pl.BlockSpec((tm, D_MODEL), lambda i: (r0 + i, 0)),
                  pl.BlockSpec((1, D_MODEL), lambda i: (0, 0))],
        out_specs=pl.BlockSpec((tm, D_MODEL), lambda i: (i, 0)),
        out_shape=jax.ShapeDtypeStruct((n_rows, D_MODEL), F32),
        compiler_params=_params("parallel"),
        name="final_norm",
    )(x, g)


def _moe_layer(x, norm_ffn_g, layer, rw_pad, wg, wu, wd, m, tm, bm, tf):
    T = x.shape[0]
    h, meta, cnt, cblk = _router(x, norm_ffn_g, layer, rw_pad, m, tm)
    e = meta[:, META_E1:META_E2 + 1].astype(jnp.int32)
    rank = meta[:, META_R1:META_R2 + 1].astype(jnp.int32)
    counts = cnt[0, :N_EXPERTS].astype(jnp.int32)
    padded = (counts + bm - 1) // bm * bm
    pend = jnp.cumsum(padded)
    pstart = pend - padded
    dest = pstart[e] + rank
    n_blocks = (T * 2) // bm + N_EXPERTS
    block_start = jnp.arange(n_blocks + 1, dtype=jnp.int32) * bm
    block_expert = jnp.minimum(jnp.sum(pend[None, :] <= block_start[:, None], axis=1),
                               N_EXPERTS - 1).astype(jnp.int32)
    n_used = (pend[-1] // bm).astype(jnp.int32).reshape(1)
    tok = jnp.broadcast_to(jnp.arange(T, dtype=jnp.int32)[:, None], (T, 2))
    slot_tok = jnp.zeros((n_blocks * bm,), jnp.int32).at[dest.reshape(-1)].set(tok.reshape(-1))
    xs = _dispatch(h, slot_tok)
    ys = _experts(xs, block_expert, n_used, wg, wu, wd, m, bm, tf)
    run_start = pstart[None, :] + cblk[:, 0, :N_EXPERTS].astype(jnp.int32)
    first = run_start // TOKEN_BLOCK
    windows = jnp.stack([first, first + 1], axis=-1).reshape(-1).astype(jnp.int32)
    return _combine(x, meta, ys, windows, pstart.astype(jnp.int32))


def _tril(w):
    n = w.shape[-1]
    return jnp.where(jnp.tril(jnp.ones((n, n), dtype=bool)), w, jnp.zeros((), w.dtype))


def _mix_tables(w_s, b_s, sample_len):
    reps = A_CHUNK // sample_len
    eye = jnp.eye(reps, dtype=w_s.dtype)
    w_prompt = _tril(w_s)
    w_small = _tril(w_s[:, :sample_len, :sample_len])
    w_sample = jax.vmap(lambda w: jnp.kron(eye, w))(w_small)
    b_prompt = b_s
    b_sample = jnp.tile(b_s[:, :sample_len], (1, reps))
    wmix = jnp.stack([w_prompt, w_sample])
    bias = jnp.stack([b_prompt, b_sample])[..., None]
    return wmix, jnp.broadcast_to(bias, bias.shape[:-1] + (LANES,))


def kernel(x_prompt, x_sample, state_hgrn, norm_mix_g, norm_ffn_g, final_norm_g, a_w_in, a_ln_g, a_ln_b, a_w_s, a_b_s, a_w_out, b_w_in, b_lb_logits, b_norm_g, b_w_out, ffn_w_gate, ffn_w_up, ffn_w_down, moe_router, moe_w_gate, moe_w_up, moe_w_down):
    n_p, L, d = x_prompt.shape
    n_s, l_s, _ = x_sample.shape
    assert d == D_MODEL and l_s == SAMPLE_LEN and L % A_CHUNK == 0
    T_p, T_s = n_p * L, n_s * l_s
    T = T_p + T_s
    depth = norm_mix_g.shape[0]

    tm_gate = 2 * A_CHUNK
    assert T_p % tm_gate == 0 and T_s % tm_gate == 0
    tm = next(t for t in (768, 512, 256, 128) if T % t == 0)
    tm_ffn = next(t for t in (1536, 768, 512, 256, 128) if T % t == 0)
    bm = 1024
    tf = 512
    lb_rows = min(L, 512)

    x = jnp.concatenate([x_prompt.reshape(T_p, d), x_sample.reshape(T_s, d)], axis=0)

    p = jax.nn.softmax(b_lb_logits.astype(F32), axis=0)
    lbs = (jnp.cumsum(p, axis=0) - p[0:1])[:, None, :]
    mix_g = norm_mix_g[:, None, :]
    ffn_g = norm_ffn_g[:, None, :]
    b_ng = b_norm_g[:, None, :]
    ln_g = a_ln_g[:, None, :]
    ln_b = a_ln_b[:, None, :]
    rw_pad = jnp.pad(moe_router, ((0, 0), (0, 0), (0, LANES - N_EXPERTS)))
    a_w_in, a_w_out, b_w_in, b_w_out, ffn_w_gate, ffn_w_up, ffn_w_down = (
        w.astype(BF16) for w in (a_w_in, a_w_out, b_w_in, b_w_out, ffn_w_gate, ffn_w_up, ffn_w_down))

    hg_prompt, v_sample = [], []
    hg_sample = state_hgrn
    for layer in range(depth):
        j = layer // 2
        if layer % 2 == 0:
            z = _norm_matmul(x, mix_g, layer, a_w_in, j, _gelu, BF16, tm, 1024)
            wmix, bias = _mix_tables(a_w_s[j], a_b_s[j], l_s)
            x, v = _gmlp_gate(z, x, ln_g, ln_b, wmix, bias, a_w_out, j, T_p // tm_gate, tm_gate)
            v_sample.append(v.reshape(n_s, l_s, A_HALF))
            x = _ffn_dense(x, ffn_g, layer, ffn_w_gate, ffn_w_up, ffn_w_down, j, tm_ffn, tf)
        else:
            proj = _norm_matmul(x, mix_g, layer, b_w_in, j, _identity, F32, tm, 1024)
            o_p, s_p = _hgrn_prompt(proj, lbs, b_ng, j, n_p, L, lb_rows)
            o_s, hg_sample = _hgrn_sample(proj, T_p, hg_sample, lbs, b_ng, j, n_s, 8)
            hg_prompt.append(s_p)
            o = jnp.concatenate([o_p, o_s], axis=0)
            x = _matmul_res(o, b_w_out, j, x, tm)
            x = _moe_layer(x, ffn_g, layer, rw_pad, moe_w_gate, moe_w_up, moe_w_down, j, 512, bm, tf)

    fg = final_norm_g[None, :]
    y_prompt = _final_norm(x, fg, 0, T_p, 512).reshape(n_p, L, d)
    y_sample = _final_norm(x, fg, T_p, T_s, T_s).reshape(n_s, l_s, d)
    return (y_prompt, y_sample, jnp.stack(hg_prompt), hg_sample, jnp.stack(v_sample))
```

```python
import functools
import math

import jax
import jax.numpy as jnp
from jax import lax
from jax.experimental import pallas as pl
from jax.experimental.pallas import tpu as pltpu
from jax.experimental.pallas import tpu_sc as plsc

F32 = jnp.float32
BF16 = jnp.bfloat16

D_MODEL = 1024
A_CHUNK = 128
A_HALF = 3 * D_MODEL
A_GROUPS = 8
A_GROUP_DIM = A_HALF // A_GROUPS
HG_HEADS = 8
HG_DK = 128
HG_CHUNK = 64
HG_SUB = 16
FORGET_FLOOR = 1e-20
D_FF = 7 * D_MODEL // 2
N_EXPERTS = 8
NORM_EPS = 1e-6
LANES = 128
SUBLANES = 8
SAMPLE_LEN = 4
MASKED_LOG = -1e30

VMEM_LIMIT = 56 * 1024 * 1024


def _params(*sem):
    return pltpu.CompilerParams(dimension_semantics=sem, vmem_limit_bytes=VMEM_LIMIT)


def _rms(x, g):
    ms = jnp.mean(x * x, axis=-1, keepdims=True)
    return x * lax.rsqrt(ms + NORM_EPS) * g


def _gelu(y):
    return 0.5 * y * (1.0 + lax.erf(y * math.sqrt(0.5)))


def _identity(y):
    return y


def _norm_matmul_kernel(x_ref, g_ref, w_ref, o_ref, h_ref, *, act):
    @pl.when(pl.program_id(1) == 0)
    def _():
        h_ref[...] = _rms(x_ref[...], g_ref[...]).astype(BF16)

    y = jnp.dot(h_ref[...], w_ref[...].astype(BF16), preferred_element_type=F32)
    o_ref[...] = act(y).astype(o_ref.dtype)


def _norm_matmul(x, g_all, layer, w_all, w_layer, act, out_dtype, tm, tn):
    T = x.shape[0]
    N = w_all.shape[-1]
    return pl.pallas_call(
        functools.partial(_norm_matmul_kernel, act=act),
        grid=(T // tm, N // tn),
        in_specs=[
            pl.BlockSpec((tm, D_MODEL), lambda i, j: (i, 0)),
            pl.BlockSpec((None, 1, D_MODEL), lambda i, j: (layer, 0, 0)),
            pl.BlockSpec((None, D_MODEL, tn), lambda i, j: (w_layer, 0, j)),
        ],
        out_specs=pl.BlockSpec((tm, tn), lambda i, j: (i, j)),
        out_shape=jax.ShapeDtypeStruct((T, N), out_dtype),
        scratch_shapes=[pltpu.VMEM((tm, D_MODEL), BF16)],
        compiler_params=_params("parallel", "arbitrary"),
        name="norm_matmul",
    )(x, g_all, w_all)


def _gmlp_gate_kernel(z_ref, lng_ref, lnb_ref, wmix_ref, bias_ref, wout_ref, x_ref, o_ref, v_ref):
    tm = z_ref.shape[0]
    zv = z_ref[:, A_HALF:].astype(F32)
    mu = jnp.mean(zv, axis=-1, keepdims=True)
    xc = zv - mu
    rstd = lax.rsqrt(jnp.mean(xc * xc, axis=-1, keepdims=True) + NORM_EPS)
    v = xc * rstd * lng_ref[...] + lnb_ref[...]
    v_ref[...] = v
    vb = v.astype(BF16)
    groups = []
    for g in range(A_GROUPS):
        cols = slice(g * A_GROUP_DIM, (g + 1) * A_GROUP_DIM)
        wm = wmix_ref[g].astype(BF16)
        bias = jnp.concatenate([bias_ref[g]] * (A_GROUP_DIM // LANES), axis=1)
        rows = []
        for c in range(tm // A_CHUNK):
            sl = slice(c * A_CHUNK, (c + 1) * A_CHUNK)
            s = jnp.dot(wm, vb[sl, cols], preferred_element_type=F32) + bias
            rows.append((z_ref[sl, cols].astype(F32) * s).astype(BF16))
        groups.append(jnp.concatenate(rows, axis=0))
    gated = jnp.concatenate(groups, axis=1)
    o_ref[...] = x_ref[...] + jnp.dot(gated, wout_ref[...], preferred_element_type=F32)


def _gmlp_gate(z, x, ln_g, ln_b, wmix, bias, w_out, j, n_prompt_blocks, tm):
    T = x.shape[0]
    nb = T // tm

    def kind(i):
        return jnp.where(i >= n_prompt_blocks, 1, 0)

    return pl.pallas_call(
        _gmlp_gate_kernel,
        grid=(nb,),
        in_specs=[
            pl.BlockSpec((tm, 2 * A_HALF), lambda i: (i, 0)),
            pl.BlockSpec((None, 1, A_HALF), lambda i: (j, 0, 0)),
            pl.BlockSpec((None, 1, A_HALF), lambda i: (j, 0, 0)),
            pl.BlockSpec((None, A_GROUPS, A_CHUNK, A_CHUNK), lambda i: (kind(i), 0, 0, 0)),
            pl.BlockSpec((None, A_GROUPS, A_CHUNK, LANES), lambda i: (kind(i), 0, 0, 0)),
            pl.BlockSpec((None, A_HALF, D_MODEL), lambda i: (j, 0, 0)),
            pl.BlockSpec((tm, D_MODEL), lambda i: (i, 0)),
        ],
        out_specs=[
            pl.BlockSpec((tm, D_MODEL), lambda i: (i, 0)),
            pl.BlockSpec((tm, A_HALF), lambda i: (jnp.maximum(i - n_prompt_blocks, 0), 0)),
        ],
        out_shape=[
            jax.ShapeDtypeStruct((T, D_MODEL), F32),
            jax.ShapeDtypeStruct(((nb - n_prompt_blocks) * tm, A_HALF), F32),
        ],
        compiler_params=_params("arbitrary"),
        name="gmlp_gate",
    )(z, ln_g, ln_b, wmix, bias, w_out, x)


FF_SPLIT = 2


def _swiglu_part(h, wg, wu, wd):
    sub = wg.shape[1] // FF_SPLIT
    out = None
    for c in range(FF_SPLIT):
        cols = slice(c * sub, (c + 1) * sub)
        a = jnp.dot(h, wg[:, cols].astype(BF16), preferred_element_type=F32)
        b = jnp.dot(h, wu[:, cols].astype(BF16), preferred_element_type=F32)
        m = (a * jax.nn.sigmoid(a) * b).astype(BF16)
        part = jnp.dot(m, wd[cols, :].astype(BF16), preferred_element_type=F32)
        out = part if out is None else out + part
    return out


def _ffn_kernel(x_ref, g_ref, wg_ref, wu_ref, wd_ref, o_ref, h_ref):
    f = pl.program_id(1)

    @pl.when(f == 0)
    def _():
        h_ref[...] = _rms(x_ref[...], g_ref[...]).astype(BF16)

    part = _swiglu_part(h_ref[...], wg_ref[...], wu_ref[...], wd_ref[...])

    @pl.when(f == 0)
    def _():
        o_ref[...] = x_ref[...] + part

    @pl.when(f > 0)
    def _():
        o_ref[...] += part


def _ffn_dense(x, g_all, layer, wg, wu, wd, m, tm, tf):
    T = x.shape[0]
    return pl.pallas_call(
        _ffn_kernel,
        grid=(T // tm, D_FF // tf),
        in_specs=[
            pl.BlockSpec((tm, D_MODEL), lambda i, f: (i, 0)),
            pl.BlockSpec((None, 1, D_MODEL), lambda i, f: (layer, 0, 0)),
            pl.BlockSpec((None, D_MODEL, tf), lambda i, f: (m, 0, f)),
            pl.BlockSpec((None, D_MODEL, tf), lambda i, f: (m, 0, f)),
            pl.BlockSpec((None, tf, D_MODEL), lambda i, f: (m, f, 0)),
        ],
        out_specs=pl.BlockSpec((tm, D_MODEL), lambda i, f: (i, 0)),
        out_shape=jax.ShapeDtypeStruct((T, D_MODEL), F32),
        scratch_shapes=[pltpu.VMEM((tm, D_MODEL), BF16)],
        compiler_params=_params("parallel", "arbitrary"),
        name="ffn_dense",
    )(x, g_all, wg, wu, wd)


def _matmul_res_kernel(a_ref, w_ref, x_ref, o_ref):
    o_ref[...] = x_ref[...] + jnp.dot(a_ref[...], w_ref[...], preferred_element_type=F32)


def _matmul_res(a, w_all, j, x, tm):
    T = x.shape[0]
    return pl.pallas_call(
        _matmul_res_kernel,
        grid=(T // tm,),
        in_specs=[
            pl.BlockSpec((tm, D_MODEL), lambda i: (i, 0)),
            pl.BlockSpec((None, D_MODEL, D_MODEL), lambda i: (j, 0, 0)),
            pl.BlockSpec((tm, D_MODEL), lambda i: (i, 0)),
        ],
        out_specs=pl.BlockSpec((tm, D_MODEL), lambda i: (i, 0)),
        out_shape=jax.ShapeDtypeStruct((T, D_MODEL), F32),
        compiler_params=_params("parallel"),
        name="matmul_res",
    )(a, w_all, x)


_NT = (((1,), (1,)), ((), ()))
_TN = (((0,), (0,)), ((), ()))


HG_GROUP = 4
MAX_BLOCK_DECAY = 75.0


def _hgrn_gates(qpre, fpre, lb):
    q = qpre * jax.nn.sigmoid(qpre)
    sig = jax.nn.sigmoid(fpre)
    f = lb + (1.0 - lb) * sig
    logf = jnp.log(jnp.maximum(f, FORGET_FLOOR))
    k = (1.0 - lb) * (1.0 - sig)
    return q, k, logf


def _cumsum_rows(x):
    C = x.shape[0]
    if C >= HG_SUB:
        r = lax.broadcasted_iota(jnp.int32, (C, C), 0)
        c = lax.broadcasted_iota(jnp.int32, (C, C), 1)
        tri = jnp.where(r >= c, 1.0, 0.0).astype(F32)
        return jnp.dot(tri, x, preferred_element_type=F32, precision=lax.Precision.HIGHEST)
    row = lax.broadcasted_iota(jnp.int32, (C, 1), 0)
    out = jnp.zeros_like(x)
    for s in range(C):
        out = out + jnp.where(row >= s, x[s:s + 1], 0.0)
    return out


def _chunk_prep(q, k, logf):
    C = q.shape[0]
    b = _cumsum_rows(logf)
    b_last = b[C - 1:C]
    qe = (q * jnp.exp(b)).astype(BF16)
    kd = (k * jnp.exp(b_last - b)).astype(BF16)
    return b, qe, kd, jnp.exp(b_last)


def _intra_exact(q, k, v, b, sub):
    C = q.shape[0]
    row = lax.broadcasted_iota(jnp.int32, (sub, 1), 0)
    parts = []
    for blk in range(C // sub):
        lo = blk * sub
        b_i, q_i, k_i, v_i = b[lo:lo + sub], q[lo:lo + sub], k[lo:lo + sub], v[lo:lo + sub]
        if blk == 0:
            o_i = jnp.zeros((sub, HG_DK), F32)
        else:
            ref = b[lo - 1:lo]
            qs = (q_i * jnp.exp(b_i - ref)).astype(BF16)
            ks = (k[:lo] * jnp.exp(ref - b[:lo])).astype(BF16)
            a = lax.dot_general(qs, ks, _NT, preferred_element_type=F32)
            o_i = jnp.dot(a.astype(BF16), v[:lo].astype(BF16), preferred_element_type=F32)
        for s in range(sub):
            rel = jnp.where(row >= s, b_i - b_i[s:s + 1], MASKED_LOG)
            a_col = jnp.sum(q_i * k_i[s:s + 1] * jnp.exp(rel), axis=-1, keepdims=True)
            o_i = o_i + a_col * v_i[s:s + 1]
        parts.append(o_i)
    return parts[0] if len(parts) == 1 else jnp.concatenate(parts, axis=0)


def _intra_factored(q, k, v, b, sub):
    C = q.shape[0]
    vb = v.astype(BF16)
    heads = [slice(h * HG_DK, (h + 1) * HG_DK) for h in range(q.shape[1] // HG_DK)]
    scores = []
    for blk in range(C // sub):
        lo, hi = blk * sub, (blk + 1) * sub
        if blk == 0:
            qs = q[:hi] * jnp.exp(b[:hi])
            ks = k[:hi] * jnp.exp(-b[:hi])
        else:
            ref = b[lo - 1:lo]
            qs = q[lo:hi] * jnp.exp(b[lo:hi] - ref)
            ks = k[:hi] * jnp.exp(ref - b[:hi])
        qs, ks = qs.astype(BF16), ks.astype(BF16)
        r = lax.broadcasted_iota(jnp.int32, (sub, hi), 0)
        c = lax.broadcasted_iota(jnp.int32, (sub, hi), 1)
        row = []
        for cols in heads:
            a = lax.dot_general(qs[:, cols], ks[:, cols], _NT, preferred_element_type=F32)
            row.append(jnp.where(c <= r + lo, a, 0.0).astype(BF16))
        scores.append(row)
    parts = []
    for blk, row in enumerate(scores):
        hi = (blk + 1) * sub
        outs = [jnp.dot(a, vb[:hi, cols], preferred_element_type=F32) for a, cols in zip(row, heads)]
        parts.append(outs[0] if len(outs) == 1 else jnp.concatenate(outs, axis=1))
    return jnp.concatenate(parts, axis=0)


def _state_step(o_intra, qe, kd, eb_last, v, st):
    o = o_intra + lax.dot_general(qe, st.astype(BF16), _NT, preferred_element_type=F32)
    st_new = st * eb_last + lax.dot_general(v.astype(BF16), kd, _TN, preferred_element_type=F32)
    return o, st_new


def _hgrn_chunk(q, k, v, logf, st, sub):
    b, qe, kd, eb_last = _chunk_prep(q, k, logf)
    return _state_step(_intra_exact(q, k, v, b, sub), qe, kd, eb_last, v, st)


def _head_norm_gate(o, ng, gate):
    return (o * lax.rsqrt(jnp.mean(o * o, axis=-1, keepdims=True) + NORM_EPS) * ng * gate)


def _hgrn_prompt_kernel(q_ref, f_ref, i_ref, g_ref, lb_ref, ng_ref, o_ref, sfin_ref,
                        st_ref, qs_ref, ks_ref, lf_ref, qe_ref, kd_ref, eb_ref, oi_ref):
    tb = pl.program_id(2)
    rows_total = q_ref.shape[0]
    n_chunks = rows_total // HG_CHUNK

    @pl.when(tb == 0)
    def _():
        st_ref[...] = jnp.zeros_like(st_ref)

    q, k, logf = _hgrn_gates(q_ref[...], f_ref[...], lb_ref[...])
    qs_ref[...] = q
    ks_ref[...] = k
    lf_ref[...] = logf
    block_decay = jnp.sum(logf.reshape(rows_total // HG_SUB, HG_SUB, logf.shape[-1]), axis=1)
    mild = jnp.min(block_decay) >= -MAX_BLOCK_DECAY

    heads = [slice(h * HG_DK, (h + 1) * HG_DK) for h in range(HG_GROUP)]

    def exact_group(qg, kg, vg, b, sub):
        return jnp.concatenate([_intra_exact(qg[:, c], kg[:, c], vg[:, c], b[:, c], sub) for c in heads], axis=1)

    def intra_pass(intra):
        def body(ci, carry):
            rows = pl.ds(pl.multiple_of(ci * HG_CHUNK, HG_CHUNK), HG_CHUNK)
            qg, kg, vg = qs_ref[rows, :], ks_ref[rows, :], i_ref[rows, :]
            b, qe, kd, eb_last = _chunk_prep(qg, kg, lf_ref[rows, :])
            qe_ref[rows, :] = qe
            kd_ref[rows, :] = kd
            eb_ref[pl.ds(pl.multiple_of(ci * SUBLANES, SUBLANES), SUBLANES), :] = \
                jnp.broadcast_to(eb_last, (SUBLANES, eb_last.shape[1]))
            oi_ref[rows, :] = intra(qg, kg, vg, b, HG_SUB)
            return carry
        lax.fori_loop(0, n_chunks, body, 0)

    @pl.when(mild)
    def _():
        intra_pass(_intra_factored)

    @pl.when(jnp.logical_not(mild))
    def _():
        intra_pass(exact_group)

    ng = ng_ref[...]

    def state_body(ci, carry):
        rows = pl.ds(pl.multiple_of(ci * HG_CHUNK, HG_CHUNK), HG_CHUNK)
        st = st_ref[...]
        stb = st.astype(BF16)
        qe, kd, oi = qe_ref[rows, :], kd_ref[rows, :], oi_ref[rows, :]
        vb = i_ref[rows, :].astype(BF16)
        eb_last = eb_ref[pl.ds(pl.multiple_of(ci * SUBLANES, SUBLANES), 1), :]
        outs, adds = [], []
        for c in heads:
            o = oi[:, c] + lax.dot_general(qe[:, c], stb[:, c], _NT, preferred_element_type=F32)
            outs.append(o * lax.rsqrt(jnp.mean(o * o, axis=-1, keepdims=True) + NORM_EPS))
            adds.append(lax.dot_general(vb[:, c], kd[:, c], _TN, preferred_element_type=F32))
        st_ref[...] = st * eb_last + jnp.concatenate(adds, axis=1)
        gate = jax.nn.sigmoid(g_ref[rows, :])
        o_ref[rows, :] = (jnp.concatenate(outs, axis=1) * ng * gate).astype(o_ref.dtype)
        return carry

    lax.fori_loop(0, n_chunks, state_body, 0)

    @pl.when(tb == pl.num_programs(2) - 1)
    def _():
        for h, c in enumerate(heads):
            sfin_ref[h] = st_ref[:, c].T


def _hgrn_prompt(proj, lbs, ng_all, j, n_seq, L, lb_rows):
    nb = L // lb_rows
    ng_groups = HG_HEADS // HG_GROUP
    gw = HG_GROUP * HG_DK

    def field(k):
        return pl.BlockSpec((lb_rows, gw), lambda n, h, t: (n * nb + t, k * ng_groups + h))

    vec = pl.BlockSpec((None, 1, gw), lambda n, h, t: (j, 0, h))
    return pl.pallas_call(
        _hgrn_prompt_kernel,
        grid=(n_seq, ng_groups, nb),
        in_specs=[field(0), field(1), field(2), field(3), vec, vec],
        out_specs=[
            pl.BlockSpec((lb_rows, gw), lambda n, h, t: (n * nb + t, h)),
            pl.BlockSpec((None, HG_GROUP, HG_DK, HG_DK), lambda n, h, t: (n, h, 0, 0)),
        ],
        out_shape=[
            jax.ShapeDtypeStruct((n_seq * L, D_MODEL), BF16),
            jax.ShapeDtypeStruct((n_seq, HG_HEADS, HG_DK, HG_DK), F32),
        ],
        scratch_shapes=[
            pltpu.VMEM((HG_DK, gw), F32),
            pltpu.VMEM((lb_rows, gw), F32),
            pltpu.VMEM((lb_rows, gw), F32),
            pltpu.VMEM((lb_rows, gw), F32),
            pltpu.VMEM((lb_rows, gw), BF16),
            pltpu.VMEM((lb_rows, gw), BF16),
            pltpu.VMEM((lb_rows // HG_CHUNK * SUBLANES, gw), F32),
            pltpu.VMEM((lb_rows, gw), F32),
        ],
        compiler_params=_params("parallel", "parallel", "arbitrary"),
        name="hgrn_prompt",
    )(proj, proj, proj, proj, lbs, ng_all)


def _hgrn_sample_kernel(q_ref, f_ref, i_ref, g_ref, lb_ref, ng_ref, s0_ref, o_ref, sfin_ref):
    lb = lb_ref[...]
    ng = ng_ref[...]
    n_seq = s0_ref.shape[0]
    row = lax.broadcasted_iota(jnp.int32, (SUBLANES, 1), 0)
    for p in range(n_seq // 2):
        rows = slice(p * SUBLANES, (p + 1) * SUBLANES)
        q, k, logf = _hgrn_gates(q_ref[rows, :], f_ref[rows, :], lb)
        gate = jax.nn.sigmoid(g_ref[rows, :])
        v = i_ref[rows, :]
        o_tile = jnp.zeros((SUBLANES, HG_DK), F32)
        for half in range(2):
            mine = (row >= half * SAMPLE_LEN) & (row < (half + 1) * SAMPLE_LEN)
            seq = 2 * p + half
            o, st_new = _hgrn_chunk(jnp.where(mine, q, 0.0), jnp.where(mine, k, 0.0),
                                    jnp.where(mine, v, 0.0), jnp.where(mine, logf, 0.0),
                                    s0_ref[seq].T, SUBLANES)
            sfin_ref[seq] = st_new.T
            o_tile = jnp.where(mine, o, o_tile)
        o_ref[rows, :] = _head_norm_gate(o_tile, ng, gate).astype(o_ref.dtype)


def _hgrn_sample(proj, row0, state_all, lbs, ng_all, j, n_seq, sb):
    H = HG_HEADS
    rb = sb * SAMPLE_LEN
    assert row0 % rb == 0 and sb % 2 == 0
    r0 = row0 // rb

    def field(k):
        return pl.BlockSpec((rb, HG_DK), lambda s, h: (r0 + s, k * H + h))

    vec = pl.BlockSpec((None, 1, HG_DK), lambda s, h: (j, 0, h))
    return pl.pallas_call(
        _hgrn_sample_kernel,
        grid=(n_seq // sb, H),
        in_specs=[field(0), field(1), field(2), field(3), vec, vec,
                  pl.BlockSpec((None, sb, None, HG_DK, HG_DK), lambda s, h: (j, s, h, 0, 0))],
        out_specs=[
            pl.BlockSpec((rb, HG_DK), lambda s, h: (s, h)),
            pl.BlockSpec((None, sb, None, HG_DK, HG_DK), lambda s, h: (j, s, h, 0, 0)),
        ],
        out_shape=[
            jax.ShapeDtypeStruct((n_seq * SAMPLE_LEN, D_MODEL), BF16),
            jax.ShapeDtypeStruct(state_all.shape, state_all.dtype),
        ],
        input_output_aliases={6: 1},
        compiler_params=_params("parallel", "parallel"),
        name="hgrn_sample",
    )(proj, proj, proj, proj, lbs, ng_all, state_all)


META_E1, META_E2, META_G1, META_G2, META_R1, META_R2 = range(6)
TOKEN_BLOCK = 128
HIGH_HALF = 0xFFFF0000


def _pack_bf16_pairs(h):
    half = h.shape[1] // 2
    bits = lax.bitcast_convert_type(h.astype(BF16).astype(F32), jnp.uint32)
    packed = (bits[:, :half] >> 16) | (bits[:, half:] & jnp.uint32(HIGH_HALF))
    return lax.bitcast_convert_type(packed, jnp.int32)


def _unpack_bf16_pairs(p):
    w = lax.bitcast_convert_type(p, jnp.uint32)
    lo = lax.bitcast_convert_type(w << 16, F32)
    hi = lax.bitcast_convert_type(w & jnp.uint32(HIGH_HALF), F32)
    return jnp.concatenate([lo, hi], axis=1).astype(BF16)


def _router_kernel(x_ref, g_ref, rw_ref, h_ref, meta_ref, cnt_ref, cb_ref, carry_ref):
    i = pl.program_id(0)
    tm = x_ref.shape[0]

    @pl.when(i == 0)
    def _():
        carry_ref[...] = jnp.zeros_like(carry_ref)

    h = _rms(x_ref[...], g_ref[...])
    h_ref[...] = _pack_bf16_pairs(h)
    logits = jnp.dot(h.astype(BF16), rw_ref[...].astype(BF16), preferred_element_type=F32)
    lane = lax.broadcasted_iota(jnp.int32, (tm, LANES), 1)
    neg = -jnp.inf
    lg = jnp.where(lane < N_EXPERTS, logits, neg)
    m1 = jnp.max(lg, axis=-1, keepdims=True)
    e1 = jnp.min(jnp.where(lg == m1, lane, LANES), axis=-1, keepdims=True)
    lg2 = jnp.where(lane == e1, neg, lg)
    m2 = jnp.max(lg2, axis=-1, keepdims=True)
    e2 = jnp.min(jnp.where(lg2 == m2, lane, LANES), axis=-1, keepdims=True)
    ex = jnp.exp(m2 - m1)
    g1 = 1.0 / (1.0 + ex)
    g2 = ex / (1.0 + ex)

    onehot = jnp.where((lane == e1) | (lane == e2), 1.0, 0.0)
    r = lax.broadcasted_iota(jnp.int32, (tm, tm), 0)
    c = lax.broadcasted_iota(jnp.int32, (tm, tm), 1)
    before = jnp.where(r > c, 1.0, 0.0).astype(BF16)
    seen = jnp.dot(before, onehot.astype(BF16), preferred_element_type=F32) + carry_ref[0:1, :]
    r1 = jnp.sum(jnp.where(lane == e1, seen, 0.0), axis=-1, keepdims=True)
    r2 = jnp.sum(jnp.where(lane == e2, seen, 0.0), axis=-1, keepdims=True)
    for blk in range(tm // TOKEN_BLOCK):
        cb_ref[blk] = jnp.broadcast_to(seen[blk * TOKEN_BLOCK:blk * TOKEN_BLOCK + 1], (SUBLANES, LANES))
    total = carry_ref[0:1, :] + jnp.sum(onehot, axis=0, keepdims=True)
    carry_ref[...] = jnp.broadcast_to(total, carry_ref.shape)
    cnt_ref[...] = jnp.broadcast_to(total, cnt_ref.shape)

    meta = jnp.zeros((tm, LANES), F32)
    for idx, val in ((META_E1, e1.astype(F32)), (META_E2, e2.astype(F32)), (META_G1, g1),
                     (META_G2, g2), (META_R1, r1), (META_R2, r2)):
        meta = jnp.where(lane == idx, val, meta)
    meta_ref[...] = meta


def _router(x, g_all, layer, rw_pad, m, tm):
    T = x.shape[0]
    return pl.pallas_call(
        _router_kernel,
        grid=(T // tm,),
        in_specs=[
            pl.BlockSpec((tm, D_MODEL), lambda i: (i, 0)),
            pl.BlockSpec((None, 1, D_MODEL), lambda i: (layer, 0, 0)),
            pl.BlockSpec((None, D_MODEL, LANES), lambda i: (m, 0, 0)),
        ],
        out_specs=[
            pl.BlockSpec((tm, D_MODEL // 2), lambda i: (i, 0)),
            pl.BlockSpec((tm, LANES), lambda i: (i, 0)),
            pl.BlockSpec((SUBLANES, LANES), lambda i: (0, 0)),
            pl.BlockSpec((tm // TOKEN_BLOCK, SUBLANES, LANES), lambda i: (i, 0, 0)),
        ],
        out_shape=[
            jax.ShapeDtypeStruct((T, D_MODEL // 2), jnp.int32),
            jax.ShapeDtypeStruct((T, LANES), F32),
            jax.ShapeDtypeStruct((SUBLANES, LANES), F32),
            jax.ShapeDtypeStruct((T // TOKEN_BLOCK, SUBLANES, LANES), F32),
        ],
        scratch_shapes=[pltpu.VMEM((SUBLANES, LANES), F32)],
        compiler_params=_params("arbitrary"),
        name="moe_router",
    )(x, g_all, rw_pad)


SC_CORES = 2
SC_SUBCORES = 16
SC_CHUNK = 32


def _dispatch(h, slot_tok):
    n_workers = SC_CORES * SC_SUBCORES
    n_slots, width = slot_tok.shape[0], h.shape[1]
    per_worker = n_slots // n_workers
    n = per_worker // SC_CHUNK
    assert n_slots % (n_workers * SC_CHUNK) == 0 and n >= 2
    mesh = plsc.VectorSubcoreMesh(core_axis_name="core", subcore_axis_name="subcore")

    @functools.partial(
        pl.kernel, mesh=mesh,
        out_type=jax.ShapeDtypeStruct((n_slots, width), h.dtype),
        scratch_types=[pltpu.VMEM((n, SC_CHUNK), jnp.int32),
                       pltpu.VMEM((SC_CHUNK, width), h.dtype),
                       pltpu.VMEM((SC_CHUNK, width), h.dtype),
                       pltpu.SemaphoreType.DMA((2,)),
                       pltpu.SemaphoreType.DMA((2,))],
        name="moe_dispatch")
    def gather(h_hbm, idx_hbm, xs_hbm, idx_v, buf0, buf1, gsem, wsem):
        worker = lax.axis_index("subcore") * SC_CORES + lax.axis_index("core")
        base = worker * per_worker
        bufs = (buf0, buf1)
        pltpu.sync_copy(idx_hbm.at[worker], idx_v)

        def fetch(c, slot):
            return pltpu.make_async_copy(h_hbm.at[idx_v.at[c]], bufs[slot], gsem.at[slot])

        def store(c, slot):
            return pltpu.make_async_copy(bufs[slot], xs_hbm.at[pl.ds(base + c * SC_CHUNK, SC_CHUNK)],
                                         wsem.at[slot])

        fetch(0, 0).start()

        @pl.loop(0, n // 2)
        def _(p):
            c0, c1 = 2 * p, 2 * p + 1

            @pl.when(p > 0)
            def _():
                store(c0 - 1, 1).wait()

            fetch(c1, 1).start()
            fetch(c0, 0).wait()
            store(c0, 0).start()

            @pl.when(c1 + 1 < n)
            def _():
                store(c0, 0).wait()
                fetch(c1 + 1, 0).start()

            fetch(c1, 1).wait()
            store(c1, 1).start()

        if n % 2:
            fetch(n - 1, 0).wait()
            store(n - 1, 0).start()
            store(n - 2, 1).wait()
            store(n - 1, 0).wait()
        else:
            store(n - 2, 0).wait()
            store(n - 1, 1).wait()

    return gather(h, slot_tok.reshape(n_workers, n, SC_CHUNK))


def _experts_kernel(be_ref, nu_ref, xs_ref, wg_ref, wu_ref, wd_ref, o_ref, acc_ref, xb_ref):
    b = pl.program_id(0)
    f = pl.program_id(1)
    used = b < nu_ref[0]

    @pl.when(used & (f == 0))
    def _():
        xb_ref[...] = _unpack_bf16_pairs(xs_ref[...])

    @pl.when(used)
    def _():
        part = _swiglu_part(xb_ref[...], wg_ref[...], wu_ref[...], wd_ref[...])

        @pl.when(f == 0)
        def _():
            acc_ref[...] = part

        @pl.when(f > 0)
        def _():
            acc_ref[...] += part

        @pl.when(f == pl.num_programs(1) - 1)
        def _():
            o_ref[...] = acc_ref[...].astype(o_ref.dtype)

    @pl.when(jnp.logical_not(used) & (f == 0))
    def _():
        o_ref[...] = jnp.zeros_like(o_ref)


def _experts(xs, block_expert, n_used, wg, wu, wd, m, bm, tf):
    n_blocks = xs.shape[0] // bm + 1
    nf = D_FF // tf
    last = n_blocks - 2

    def fidx(b, f, nu):
        return jnp.where(b < nu[0], f, nf - 1)

    grid_spec = pltpu.PrefetchScalarGridSpec(
        num_scalar_prefetch=2,
        grid=(n_blocks, nf),
        in_specs=[
            pl.BlockSpec((bm, D_MODEL // 2), lambda b, f, be, nu: (jnp.minimum(b, last), 0)),
            pl.BlockSpec((None, None, D_MODEL, tf), lambda b, f, be, nu: (m, be[b], 0, fidx(b, f, nu))),
            pl.BlockSpec((None, None, D_MODEL, tf), lambda b, f, be, nu: (m, be[b], 0, fidx(b, f, nu))),
            pl.BlockSpec((None, None, tf, D_MODEL), lambda b, f, be, nu: (m, be[b], fidx(b, f, nu), 0)),
        ],
        out_specs=pl.BlockSpec((bm, D_MODEL), lambda b, f, be, nu: (b, 0)),
        scratch_shapes=[pltpu.VMEM((bm, D_MODEL), F32), pltpu.VMEM((bm, D_MODEL), BF16)],
    )
    return pl.pallas_call(
        _experts_kernel,
        grid_spec=grid_spec,
        out_shape=jax.ShapeDtypeStruct((n_blocks * bm, D_MODEL), BF16),
        compiler_params=_params("arbitrary", "arbitrary"),
        name="moe_experts",
    )(block_expert, n_used, xs, wg, wu, wd)


N_WINDOWS = 2 * N_EXPERTS


def _combine_kernel(win_ref, ps_ref, x_ref, meta_ref, *refs):
    ys_refs, o_ref = refs[:N_WINDOWS], refs[N_WINDOWS]
    blk = pl.program_id(0)
    meta = meta_ref[...]
    e1, e2 = meta[:, META_E1:META_E1 + 1], meta[:, META_E2:META_E2 + 1]
    g1, g2 = meta[:, META_G1:META_G1 + 1], meta[:, META_G2:META_G2 + 1]
    d1, d2 = meta[:, META_R1:META_R1 + 1], meta[:, META_R2:META_R2 + 1]
    for e in range(N_EXPERTS):
        start = ps_ref[e].astype(F32)
        d1 = d1 + jnp.where(e1 == e, start, 0.0)
        d2 = d2 + jnp.where(e2 == e, start, 0.0)
    lane = lax.broadcasted_iota(jnp.int32, (1, TOKEN_BLOCK), 1)
    acc = x_ref[...]
    for w in range(N_WINDOWS):
        e = w // 2
        slot = (win_ref[blk * N_WINDOWS + w] * TOKEN_BLOCK + lane).astype(F32)
        sel = jnp.where((d1 == slot) & (e1 == e), g1, 0.0) + jnp.where((d2 == slot) & (e2 == e), g2, 0.0)
        acc = acc + jnp.dot(sel.astype(BF16), ys_refs[w][...], preferred_element_type=F32)
    o_ref[...] = acc


def _combine(x, meta, ys, windows, pstart):
    T = x.shape[0]
    nb = T // TOKEN_BLOCK

    def window(w):
        return pl.BlockSpec((TOKEN_BLOCK, D_MODEL), lambda i, win, ps: (win[i * N_WINDOWS + w], 0))

    grid_spec = pltpu.PrefetchScalarGridSpec(
        num_scalar_prefetch=2,
        grid=(nb,),
        in_specs=[pl.BlockSpec((TOKEN_BLOCK, D_MODEL), lambda i, win, ps: (i, 0)),
                  pl.BlockSpec((TOKEN_BLOCK, LANES), lambda i, win, ps: (i, 0))]
                 + [window(w) for w in range(N_WINDOWS)],
        out_specs=pl.BlockSpec((TOKEN_BLOCK, D_MODEL), lambda i, win, ps: (i, 0)),
    )
    return pl.pallas_call(
        _combine_kernel,
        grid_spec=grid_spec,
        out_shape=jax.ShapeDtypeStruct((T, D_MODEL), F32),
        compiler_params=_params("arbitrary"),
        name="moe_combine",
    )(windows, pstart, x, meta, *([ys] * N_WINDOWS))


def _final_norm_kernel(x_ref, g_ref, o_ref):
    o_ref[...] = _rms(x_ref[...], g_ref[...])


def _final_norm(x, g, row0, n_rows, tm):
    r0 = row0 // tm
    return pl.pallas_call(
        _final_norm_kernel,
        grid=(n_rows // tm,),
        in_specs=[pl.BlockSpec((tm, D_MODEL), lambda i: (r0 + i, 0)),
                  pl.BlockSpec((1, D_MODEL), lambda i: (0, 0))],
        out_specs=pl.BlockSpec((tm, D_MODEL), lambda i: (i, 0)),
        out_shape=jax.ShapeDtypeStruct((n_rows, D_MODEL), F32),
        compiler_params=_params("parallel"),
        name="final_norm",
    )(x, g)


def _moe_layer(x, norm_ffn_g, layer, rw_pad, wg, wu, wd, m, tm, bm, tf):
    T = x.shape[0]
    h, meta, cnt, cblk = _router(x, norm_ffn_g, layer, rw_pad, m, tm)
    e = meta[:, META_E1:META_E2 + 1].astype(jnp.int32)
    rank = meta[:, META_R1:META_R2 + 1].astype(jnp.int32)
    counts = cnt[0, :N_EXPERTS].astype(jnp.int32)
    padded = (counts + bm - 1) // bm * bm
    pend = jnp.cumsum(padded)
    pstart = pend - padded
    dest = pstart[e] + rank
    n_blocks = (T * 2) // bm + N_EXPERTS
    block_start = jnp.arange(n_blocks + 1, dtype=jnp.int32) * bm
    block_expert = jnp.minimum(jnp.sum(pend[None, :] <= block_start[:, None], axis=1),
                               N_EXPERTS - 1).astype(jnp.int32)
    n_used = (pend[-1] // bm).astype(jnp.int32).reshape(1)
    tok = jnp.broadcast_to(jnp.arange(T, dtype=jnp.int32)[:, None], (T, 2))
    slot_tok = jnp.zeros((n_blocks * bm,), jnp.int32).at[dest.reshape(-1)].set(tok.reshape(-1))
    xs = _dispatch(h, slot_tok)
    ys = _experts(xs, block_expert, n_used, wg, wu, wd, m, bm, tf)
    run_start = pstart[None, :] + cblk[:, 0, :N_EXPERTS].astype(jnp.int32)
    first = run_start // TOKEN_BLOCK
    windows = jnp.stack([first, first + 1], axis=-1).reshape(-1).astype(jnp.int32)
    return _combine(x, meta, ys, windows, pstart.astype(jnp.int32))


def _tril(w):
    n = w.shape[-1]
    return jnp.where(jnp.tril(jnp.ones((n, n), dtype=bool)), w, jnp.zeros((), w.dtype))


def _mix_tables(w_s, b_s, sample_len):
    reps = A_CHUNK // sample_len
    eye = jnp.eye(reps, dtype=w_s.dtype)
    w_prompt = _tril(w_s)
    w_small = _tril(w_s[:, :sample_len, :sample_len])
    w_sample = jax.vmap(lambda w: jnp.kron(eye, w))(w_small)
    b_prompt = b_s
    b_sample = jnp.tile(b_s[:, :sample_len], (1, reps))
    wmix = jnp.stack([w_prompt, w_sample])
    bias = jnp.stack([b_prompt, b_sample])[..., None]
    return wmix, jnp.broadcast_to(bias, bias.shape[:-1] + (LANES,))


def kernel(x_prompt, x_sample, state_hgrn, norm_mix_g, norm_ffn_g, final_norm_g, a_w_in, a_ln_g, a_ln_b, a_w_s, a_b_s, a_w_out, b_w_in, b_lb_logits, b_norm_g, b_w_out, ffn_w_gate, ffn_w_up, ffn_w_down, moe_router, moe_w_gate, moe_w_up, moe_w_down):
    n_p, L, d = x_prompt.shape
    n_s, l_s, _ = x_sample.shape
    assert d == D_MODEL and l_s == SAMPLE_LEN and L % A_CHUNK == 0
    T_p, T_s = n_p * L, n_s * l_s
    T = T_p + T_s
    depth = norm_mix_g.shape[0]

    tm_gate = 2 * A_CHUNK
    assert T_p % tm_gate == 0 and T_s % tm_gate == 0
    tm = next(t for t in (768, 512, 256, 128) if T % t == 0)
    tm_ffn = next(t for t in (1536, 768, 512, 256, 128) if T % t == 0)
    bm = 1024
    tf = 512
    lb_rows = min(L, 512)

    x = jnp.concatenate([x_prompt.reshape(T_p, d), x_sample.reshape(T_s, d)], axis=0)

    p = jax.nn.softmax(b_lb_logits.astype(F32), axis=0)
    lbs = (jnp.cumsum(p, axis=0) - p[0:1])[:, None, :]
    mix_g = norm_mix_g[:, None, :]
    ffn_g = norm_ffn_g[:, None, :]
    b_ng = b_norm_g[:, None, :]
    ln_g = a_ln_g[:, None, :]
    ln_b = a_ln_b[:, None, :]
    rw_pad = jnp.pad(moe_router, ((0, 0), (0, 0), (0, LANES - N_EXPERTS)))
    a_w_in, a_w_out, b_w_in, b_w_out, ffn_w_gate, ffn_w_up, ffn_w_down = (
        w.astype(BF16) for w in (a_w_in, a_w_out, b_w_in, b_w_out, ffn_w_gate, ffn_w_up, ffn_w_down))

    hg_prompt, v_sample = [], []
    hg_sample = state_hgrn
    for layer in range(depth):
        j = layer // 2
        if layer % 2 == 0:
            z = _norm_matmul(x, mix_g, layer, a_w_in, j, _gelu, BF16, tm, 1024)
            wmix, bias = _mix_tables(a_w_s[j], a_b_s[j], l_s)
            x, v = _gmlp_gate(z, x, ln_g, ln_b, wmix, bias, a_w_out, j, T_p // tm_gate, tm_gate)
            v_sample.append(v.reshape(n_s, l_s, A_HALF))
            x = _ffn_dense(x, ffn_g, layer, ffn_w_gate, ffn_w_up, ffn_w_down, j, tm_ffn, tf)
        else:
            proj = _norm_matmul(x, mix_g, layer, b_w_in, j, _identity, F32, tm, 1024)
            o_p, s_p = _hgrn_prompt(proj, lbs, b_ng, j, n_p, L, lb_rows)
            o_s, hg_sample = _hgrn_sample(proj, T_p, hg_sample, lbs, b_ng, j, n_s, 8)
            hg_prompt.append(s_p)
            o = jnp.concatenate([o_p, o_s], axis=0)
            x = _matmul_res(o, b_w_out, j, x, tm)
            x = _moe_layer(x, ffn_g, layer, rw_pad, moe_w_gate, moe_w_up, moe_w_down, j, 512, bm, tf)

    fg = final_norm_g[None, :]
    y_prompt = _final_norm(x, fg, 0, T_p, 512).reshape(n_p, L, d)
    y_sample = _final_norm(x, fg, T_p, T_s, T_s).reshape(n_s, l_s, d)
    return (y_prompt, y_sample, jnp.stack(hg_prompt), hg_sample, jnp.stack(v_sample))
```

```python
import functools
import math

import jax
import jax.numpy as jnp
from jax import lax
from jax.experimental import pallas as pl
from jax.experimental.pallas import tpu as pltpu
from jax.experimental.pallas import tpu_sc as plsc

F32 = jnp.float32
BF16 = jnp.bfloat16

D_MODEL = 1024
A_CHUNK = 128
A_HALF = 3 * D_MODEL
A_GROUPS = 8
A_GROUP_DIM = A_HALF // A_GROUPS
HG_HEADS = 8
HG_DK = 128
HG_CHUNK = 64
HG_SUB = 16
FORGET_FLOOR = 1e-20
D_FF = 7 * D_MODEL // 2
N_EXPERTS = 8
NORM_EPS = 1e-6
LANES = 128
SUBLANES = 8
SAMPLE_LEN = 4
MASKED_LOG = -1e30

VMEM_LIMIT = 56 * 1024 * 1024


def _params(*sem):
    return pltpu.CompilerParams(dimension_semantics=sem, vmem_limit_bytes=VMEM_LIMIT)


def _rms(x, g):
    ms = jnp.mean(x * x, axis=-1, keepdims=True)
    return x * lax.rsqrt(ms + NORM_EPS) * g


def _gelu(y):
    return 0.5 * y * (1.0 + lax.erf(y * math.sqrt(0.5)))


def _identity(y):
    return y


def _norm_matmul_kernel(x_ref, g_ref, w_ref, o_ref, h_ref, *, act):
    @pl.when(pl.program_id(1) == 0)
    def _():
        h_ref[...] = _rms(x_ref[...], g_ref[...]).astype(BF16)

    y = jnp.dot(h_ref[...], w_ref[...].astype(BF16), preferred_element_type=F32)
    o_ref[...] = act(y).astype(o_ref.dtype)


def _norm_matmul(x, g_all, layer, w_all, w_layer, act, out_dtype, tm, tn):
    T = x.shape[0]
    N = w_all.shape[-1]
    return pl.pallas_call(
        functools.partial(_norm_matmul_kernel, act=act),
        grid=(T // tm, N // tn),
        in_specs=[
            pl.BlockSpec((tm, D_MODEL), lambda i, j: (i, 0)),
            pl.BlockSpec((None, 1, D_MODEL), lambda i, j: (layer, 0, 0)),
            pl.BlockSpec((None, D_MODEL, tn), lambda i, j: (w_layer, 0, j)),
        ],
        out_specs=pl.BlockSpec((tm, tn), lambda i, j: (i, j)),
        out_shape=jax.ShapeDtypeStruct((T, N), out_dtype),
        scratch_shapes=[pltpu.VMEM((tm, D_MODEL), BF16)],
        compiler_params=_params("parallel", "arbitrary"),
        name="norm_matmul",
    )(x, g_all, w_all)


def _gmlp_gate_kernel(z_ref, lng_ref, lnb_ref, wmix_ref, bias_ref, wout_ref, x_ref, o_ref, v_ref):
    tm = z_ref.shape[0]
    zv = z_ref[:, A_HALF:].astype(F32)
    mu = jnp.mean(zv, axis=-1, keepdims=True)
    xc = zv - mu
    rstd = lax.rsqrt(jnp.mean(xc * xc, axis=-1, keepdims=True) + NORM_EPS)
    v = xc * rstd * lng_ref[...] + lnb_ref[...]
    v_ref[...] = v
    vb = v.astype(BF16)
    groups = []
    for g in range(A_GROUPS):
        cols = slice(g * A_GROUP_DIM, (g + 1) * A_GROUP_DIM)
        wm = wmix_ref[g].astype(BF16)
        bias = jnp.concatenate([bias_ref[g]] * (A_GROUP_DIM // LANES), axis=1)
        rows = []
        for c in range(tm // A_CHUNK):
            sl = slice(c * A_CHUNK, (c + 1) * A_CHUNK)
            s = jnp.dot(wm, vb[sl, cols], preferred_element_type=F32) + bias
            rows.append((z_ref[sl, cols].astype(F32) * s).astype(BF16))
        groups.append(jnp.concatenate(rows, axis=0))
    gated = jnp.concatenate(groups, axis=1)
    o_ref[...] = x_ref[...] + jnp.dot(gated, wout_ref[...], preferred_element_type=F32)


def _gmlp_gate(z, x, ln_g, ln_b, wmix, bias, w_out, j, n_prompt_blocks, tm):
    T = x.shape[0]
    nb = T // tm

    def kind(i):
        return jnp.where(i >= n_prompt_blocks, 1, 0)

    return pl.pallas_call(
        _gmlp_gate_kernel,
        grid=(nb,),
        in_specs=[
            pl.BlockSpec((tm, 2 * A_HALF), lambda i: (i, 0)),
            pl.BlockSpec((None, 1, A_HALF), lambda i: (j, 0, 0)),
            pl.BlockSpec((None, 1, A_HALF), lambda i: (j, 0, 0)),
            pl.BlockSpec((None, A_GROUPS, A_CHUNK, A_CHUNK), lambda i: (kind(i), 0, 0, 0)),
            pl.BlockSpec((None, A_GROUPS, A_CHUNK, LANES), lambda i: (kind(i), 0, 0, 0)),
            pl.BlockSpec((None, A_HALF, D_MODEL), lambda i: (j, 0, 0)),
            pl.BlockSpec((tm, D_MODEL), lambda i: (i, 0)),
        ],
        out_specs=[
            pl.BlockSpec((tm, D_MODEL), lambda i: (i, 0)),
            pl.BlockSpec((tm, A_HALF), lambda i: (jnp.maximum(i - n_prompt_blocks, 0), 0)),
        ],
        out_shape=[
            jax.ShapeDtypeStruct((T, D_MODEL), F32),
            jax.ShapeDtypeStruct(((nb - n_prompt_blocks) * tm, A_HALF), F32),
        ],
        compiler_params=_params("arbitrary"),
        name="gmlp_gate",
    )(z, ln_g, ln_b, wmix, bias, w_out, x)


FF_SPLIT = 2


def _swiglu_part(h, wg, wu, wd):
    sub = wg.shape[1] // FF_SPLIT
    out = None
    for c in range(FF_SPLIT):
        cols = slice(c * sub, (c + 1) * sub)
        a = jnp.dot(h, wg[:, cols].astype(BF16), preferred_element_type=F32)
        b = jnp.dot(h, wu[:, cols].astype(BF16), preferred_element_type=F32)
        m = (a * jax.nn.sigmoid(a) * b).astype(BF16)
        part = jnp.dot(m, wd[cols, :].astype(BF16), preferred_element_type=F32)
        out = part if out is None else out + part
    return out


def _ffn_kernel(x_ref, g_ref, wg_ref, wu_ref, wd_ref, o_ref, h_ref):
    f = pl.program_id(1)

    @pl.when(f == 0)
    def _():
        h_ref[...] = _rms(x_ref[...], g_ref[...]).astype(BF16)

    part = _swiglu_part(h_ref[...], wg_ref[...], wu_ref[...], wd_ref[...])

    @pl.when(f == 0)
    def _():
        o_ref[...] = x_ref[...] + part

    @pl.when(f > 0)
    def _():
        o_ref[...] += part


def _ffn_dense(x, g_all, layer, wg, wu, wd, m, tm, tf):
    T = x.shape[0]
    return pl.pallas_call(
        _ffn_kernel,
        grid=(T // tm, D_FF // tf),
        in_specs=[
            pl.BlockSpec((tm, D_MODEL), lambda i, f: (i, 0)),
            pl.BlockSpec((None, 1, D_MODEL), lambda i, f: (layer, 0, 0)),
            pl.BlockSpec((None, D_MODEL, tf), lambda i, f: (m, 0, f)),
            pl.BlockSpec((None, D_MODEL, tf), lambda i, f: (m, 0, f)),
            pl.BlockSpec((None, tf, D_MODEL), lambda i, f: (m, f, 0)),
        ],
        out_specs=pl.BlockSpec((tm, D_MODEL), lambda i, f: (i, 0)),
        out_shape=jax.ShapeDtypeStruct((T, D_MODEL), F32),
        scratch_shapes=[pltpu.VMEM((tm, D_MODEL), BF16)],
        compiler_params=_params("parallel", "arbitrary"),
        name="ffn_dense",
    )(x, g_all, wg, wu, wd)


def _matmul_res_kernel(a_ref, w_ref, x_ref, o_ref):
    o_ref[...] = x_ref[...] + jnp.dot(a_ref[...], w_ref[...], preferred_element_type=F32)


def _matmul_res(a, w_all, j, x, tm):
    T = x.shape[0]
    return pl.pallas_call(
        _matmul_res_kernel,
        grid=(T // tm,),
        in_specs=[
            pl.BlockSpec((tm, D_MODEL), lambda i: (i, 0)),
            pl.BlockSpec((None, D_MODEL, D_MODEL), lambda i: (j, 0, 0)),
            pl.BlockSpec((tm, D_MODEL), lambda i: (i, 0)),
        ],
        out_specs=pl.BlockSpec((tm, D_MODEL), lambda i: (i, 0)),
        out_shape=jax.ShapeDtypeStruct((T, D_MODEL), F32),
        compiler_params=_params("parallel"),
        name="matmul_res",
    )(a, w_all, x)


_NT = (((1,), (1,)), ((), ()))
_TN = (((0,), (0,)), ((), ()))


HG_GROUP = 8
MAX_BLOCK_DECAY = 75.0


def _hgrn_gates(qpre, fpre, lb):
    q = qpre * jax.nn.sigmoid(qpre)
    sig = jax.nn.sigmoid(fpre)
    f = lb + (1.0 - lb) * sig
    logf = jnp.log(jnp.maximum(f, FORGET_FLOOR))
    k = (1.0 - lb) * (1.0 - sig)
    return q, k, logf


def _cumsum_rows(x):
    C = x.shape[0]
    if C >= HG_SUB:
        r = lax.broadcasted_iota(jnp.int32, (C, C), 0)
        c = lax.broadcasted_iota(jnp.int32, (C, C), 1)
        tri = jnp.where(r >= c, 1.0, 0.0).astype(F32)
        return jnp.dot(tri, x, preferred_element_type=F32, precision=lax.Precision.HIGHEST)
    row = lax.broadcasted_iota(jnp.int32, (C, 1), 0)
    out = jnp.zeros_like(x)
    for s in range(C):
        out = out + jnp.where(row >= s, x[s:s + 1], 0.0)
    return out


def _chunk_prep(q, k, logf):
    C = q.shape[0]
    b = _cumsum_rows(logf)
    b_last = b[C - 1:C]
    qe = (q * jnp.exp(b)).astype(BF16)
    kd = (k * jnp.exp(b_last - b)).astype(BF16)
    return b, qe, kd, jnp.exp(b_last)


def _intra_exact(q, k, v, b, sub):
    C = q.shape[0]
    row = lax.broadcasted_iota(jnp.int32, (sub, 1), 0)
    parts = []
    for blk in range(C // sub):
        lo = blk * sub
        b_i, q_i, k_i, v_i = b[lo:lo + sub], q[lo:lo + sub], k[lo:lo + sub], v[lo:lo + sub]
        if blk == 0:
            o_i = jnp.zeros((sub, HG_DK), F32)
        else:
            ref = b[lo - 1:lo]
            qs = (q_i * jnp.exp(b_i - ref)).astype(BF16)
            ks = (k[:lo] * jnp.exp(ref - b[:lo])).astype(BF16)
            a = lax.dot_general(qs, ks, _NT, preferred_element_type=F32)
            o_i = jnp.dot(a.astype(BF16), v[:lo].astype(BF16), preferred_element_type=F32)
        for s in range(sub):
            rel = jnp.where(row >= s, b_i - b_i[s:s + 1], MASKED_LOG)
            a_col = jnp.sum(q_i * k_i[s:s + 1] * jnp.exp(rel), axis=-1, keepdims=True)
            o_i = o_i + a_col * v_i[s:s + 1]
        parts.append(o_i)
    return parts[0] if len(parts) == 1 else jnp.concatenate(parts, axis=0)


def _intra_factored(q, k, v, b, sub):
    C = q.shape[0]
    vb = v.astype(BF16)
    heads = [slice(h * HG_DK, (h + 1) * HG_DK) for h in range(q.shape[1] // HG_DK)]
    scores = []
    for blk in range(C // sub):
        lo, hi = blk * sub, (blk + 1) * sub
        if blk == 0:
            qs = q[:hi] * jnp.exp(b[:hi])
            ks = k[:hi] * jnp.exp(-b[:hi])
        else:
            ref = b[lo - 1:lo]
            qs = q[lo:hi] * jnp.exp(b[lo:hi] - ref)
            ks = k[:hi] * jnp.exp(ref - b[:hi])
        qs, ks = qs.astype(BF16), ks.astype(BF16)
        r = lax.broadcasted_iota(jnp.int32, (sub, hi), 0)
        c = lax.broadcasted_iota(jnp.int32, (sub, hi), 1)
        row = []
        for cols in heads:
            a = lax.dot_general(qs[:, cols], ks[:, cols], _NT, preferred_element_type=F32)
            row.append(jnp.where(c <= r + lo, a, 0.0).astype(BF16))
        scores.append(row)
    parts = []
    for blk, row in enumerate(scores):
        hi = (blk + 1) * sub
        outs = [jnp.dot(a, vb[:hi, cols], preferred_element_type=F32) for a, cols in zip(row, heads)]
        parts.append(outs[0] if len(outs) == 1 else jnp.concatenate(outs, axis=1))
    return jnp.concatenate(parts, axis=0)


def _state_step(o_intra, qe, kd, eb_last, v, st):
    o = o_intra + lax.dot_general(qe, st.astype(BF16), _NT, preferred_element_type=F32)
    st_new = st * eb_last + lax.dot_general(v.astype(BF16), kd, _TN, preferred_element_type=F32)
    return o, st_new


def _hgrn_chunk(q, k, v, logf, st, sub):
    b, qe, kd, eb_last = _chunk_prep(q, k, logf)
    return _state_step(_intra_exact(q, k, v, b, sub), qe, kd, eb_last, v, st)


def _head_norm_gate(o, ng, gate):
    return (o * lax.rsqrt(jnp.mean(o * o, axis=-1, keepdims=True) + NORM_EPS) * ng * gate)


def _hgrn_prompt_kernel(q_ref, f_ref, i_ref, g_ref, lb_ref, ng_ref, o_ref, sfin_ref,
                        st_ref, qs_ref, ks_ref, lf_ref, qe_ref, kd_ref, eb_ref, oi_ref):
    tb = pl.program_id(2)
    rows_total = q_ref.shape[0]
    n_chunks = rows_total // HG_CHUNK

    @pl.when(tb == 0)
    def _():
        st_ref[...] = jnp.zeros_like(st_ref)

    q, k, logf = _hgrn_gates(q_ref[...], f_ref[...], lb_ref[...])
    qs_ref[...] = q
    ks_ref[...] = k
    lf_ref[...] = logf
    block_decay = jnp.sum(logf.reshape(rows_total // HG_SUB, HG_SUB, logf.shape[-1]), axis=1)
    mild = jnp.min(block_decay) >= -MAX_BLOCK_DECAY

    heads = [slice(h * HG_DK, (h + 1) * HG_DK) for h in range(HG_GROUP)]

    def exact_group(qg, kg, vg, b, sub):
        return jnp.concatenate([_intra_exact(qg[:, c], kg[:, c], vg[:, c], b[:, c], sub) for c in heads], axis=1)

    def intra_pass(intra):
        def body(ci, carry):
            rows = pl.ds(pl.multiple_of(ci * HG_CHUNK, HG_CHUNK), HG_CHUNK)
            qg, kg, vg = qs_ref[rows, :], ks_ref[rows, :], i_ref[rows, :]
            b, qe, kd, eb_last = _chunk_prep(qg, kg, lf_ref[rows, :])
            qe_ref[rows, :] = qe
            kd_ref[rows, :] = kd
            eb_ref[pl.ds(pl.multiple_of(ci * SUBLANES, SUBLANES), SUBLANES), :] = \
                jnp.broadcast_to(eb_last, (SUBLANES, eb_last.shape[1]))
            oi_ref[rows, :] = intra(qg, kg, vg, b, HG_SUB)
            return carry
        lax.fori_loop(0, n_chunks, body, 0)

    @pl.when(mild)
    def _():
        intra_pass(_intra_factored)

    @pl.when(jnp.logical_not(mild))
    def _():
        intra_pass(exact_group)

    ng = ng_ref[...]

    def state_body(ci, carry):
        rows = pl.ds(pl.multiple_of(ci * HG_CHUNK, HG_CHUNK), HG_CHUNK)
        st = st_ref[...]
        stb = st.astype(BF16)
        qe, kd, oi = qe_ref[rows, :], kd_ref[rows, :], oi_ref[rows, :]
        vb = i_ref[rows, :].astype(BF16)
        eb_last = eb_ref[pl.ds(pl.multiple_of(ci * SUBLANES, SUBLANES), 1), :]
        outs, adds = [], []
        for c in heads:
            o = oi[:, c] + lax.dot_general(qe[:, c], stb[:, c], _NT, preferred_element_type=F32)
            outs.append(o * lax.rsqrt(jnp.mean(o * o, axis=-1, keepdims=True) + NORM_EPS))
            adds.append(lax.dot_general(vb[:, c], kd[:, c], _TN, preferred_element_type=F32))
        st_ref[...] = st * eb_last + jnp.concatenate(adds, axis=1)
        gate = jax.nn.sigmoid(g_ref[rows, :])
        o_ref[rows, :] = (jnp.concatenate(outs, axis=1) * ng * gate).astype(o_ref.dtype)
        return carry

    lax.fori_loop(0, n_chunks, state_body, 0)

    @pl.when(tb == pl.num_programs(2) - 1)
    def _():
        for h, c in enumerate(heads):
            sfin_ref[h] = st_ref[:, c].T


def _hgrn_prompt(proj, lbs, ng_all, j, n_seq, L, lb_rows):
    nb = L // lb_rows
    ng_groups = HG_HEADS // HG_GROUP
    gw = HG_GROUP * HG_DK

    def field(k):
        return pl.BlockSpec((lb_rows, gw), lambda n, h, t: (n * nb + t, k * ng_groups + h))

    vec = pl.BlockSpec((None, 1, gw), lambda n, h, t: (j, 0, h))
    return pl.pallas_call(
        _hgrn_prompt_kernel,
        grid=(n_seq, ng_groups, nb),
        in_specs=[field(0), field(1), field(2), field(3), vec, vec],
        out_specs=[
            pl.BlockSpec((lb_rows, gw), lambda n, h, t: (n * nb + t, h)),
            pl.BlockSpec((None, HG_GROUP, HG_DK, HG_DK), lambda n, h, t: (n, h, 0, 0)),
        ],
        out_shape=[
            jax.ShapeDtypeStruct((n_seq * L, D_MODEL), BF16),
            jax.ShapeDtypeStruct((n_seq, HG_HEADS, HG_DK, HG_DK), F32),
        ],
        scratch_shapes=[
            pltpu.VMEM((HG_DK, gw), F32),
            pltpu.VMEM((lb_rows, gw), F32),
            pltpu.VMEM((lb_rows, gw), F32),
            pltpu.VMEM((lb_rows, gw), F32),
            pltpu.VMEM((lb_rows, gw), BF16),
            pltpu.VMEM((lb_rows, gw), BF16),
            pltpu.VMEM((lb_rows // HG_CHUNK * SUBLANES, gw), F32),
            pltpu.VMEM((lb_rows, gw), F32),
        ],
        compiler_params=_params("parallel", "parallel", "arbitrary"),
        name="hgrn_prompt",
    )(proj, proj, proj, proj, lbs, ng_all)


def _hgrn_sample_kernel(q_ref, f_ref, i_ref, g_ref, lb_ref, ng_ref, s0_ref, o_ref, sfin_ref):
    lb = lb_ref[...]
    ng = ng_ref[...]
    n_seq = s0_ref.shape[0]
    row = lax.broadcasted_iota(jnp.int32, (SUBLANES, 1), 0)
    for p in range(n_seq // 2):
        rows = slice(p * SUBLANES, (p + 1) * SUBLANES)
        q, k, logf = _hgrn_gates(q_ref[rows, :], f_ref[rows, :], lb)
        gate = jax.nn.sigmoid(g_ref[rows, :])
        v = i_ref[rows, :]
        o_tile = jnp.zeros((SUBLANES, HG_DK), F32)
        for half in range(2):
            mine = (row >= half * SAMPLE_LEN) & (row < (half + 1) * SAMPLE_LEN)
            seq = 2 * p + half
            o, st_new = _hgrn_chunk(jnp.where(mine, q, 0.0), jnp.where(mine, k, 0.0),
                                    jnp.where(mine, v, 0.0), jnp.where(mine, logf, 0.0),
                                    s0_ref[seq].T, SUBLANES)
            sfin_ref[seq] = st_new.T
            o_tile = jnp.where(mine, o, o_tile)
        o_ref[rows, :] = _head_norm_gate(o_tile, ng, gate).astype(o_ref.dtype)


def _hgrn_sample(proj, row0, state_all, lbs, ng_all, j, n_seq, sb):
    H = HG_HEADS
    rb = sb * SAMPLE_LEN
    assert row0 % rb == 0 and sb % 2 == 0
    r0 = row0 // rb

    def field(k):
        return pl.BlockSpec((rb, HG_DK), lambda s, h: (r0 + s, k * H + h))

    vec = pl.BlockSpec((None, 1, HG_DK), lambda s, h: (j, 0, h))
    return pl.pallas_call(
        _hgrn_sample_kernel,
        grid=(n_seq // sb, H),
        in_specs=[field(0), field(1), field(2), field(3), vec, vec,
                  pl.BlockSpec((None, sb, None, HG_DK, HG_DK), lambda s, h: (j, s, h, 0, 0))],
        out_specs=[
            pl.BlockSpec((rb, HG_DK), lambda s, h: (s, h)),
            pl.BlockSpec((None, sb, None, HG_DK, HG_DK), lambda s, h: (j, s, h, 0, 0)),
        ],
        out_shape=[
            jax.ShapeDtypeStruct((n_seq * SAMPLE_LEN, D_MODEL), BF16),
            jax.ShapeDtypeStruct(state_all.shape, state_all.dtype),
        ],
        input_output_aliases={6: 1},
        compiler_params=_params("parallel", "parallel"),
        name="hgrn_sample",
    )(proj, proj, proj, proj, lbs, ng_all, state_all)


META_E1, META_E2, META_G1, META_G2, META_R1, META_R2 = range(6)
TOKEN_BLOCK = 128
HIGH_HALF = 0xFFFF0000


def _pack_bf16_pairs(h):
    half = h.shape[1] // 2
    bits = lax.bitcast_convert_type(h.astype(BF16).astype(F32), jnp.uint32)
    packed = (bits[:, :half] >> 16) | (bits[:, half:] & jnp.uint32(HIGH_HALF))
    return lax.bitcast_convert_type(packed, jnp.int32)


def _unpack_bf16_pairs(p):
    w = lax.bitcast_convert_type(p, jnp.uint32)
    lo = lax.bitcast_convert_type(w << 16, F32)
    hi = lax.bitcast_convert_type(w & jnp.uint32(HIGH_HALF), F32)
    return jnp.concatenate([lo, hi], axis=1).astype(BF16)


def _router_kernel(x_ref, g_ref, rw_ref, h_ref, meta_ref, cnt_ref, cb_ref, carry_ref):
    i = pl.program_id(0)
    tm = x_ref.shape[0]

    @pl.when(i == 0)
    def _():
        carry_ref[...] = jnp.zeros_like(carry_ref)

    h = _rms(x_ref[...], g_ref[...])
    h_ref[...] = _pack_bf16_pairs(h)
    logits = jnp.dot(h.astype(BF16), rw_ref[...].astype(BF16), preferred_element_type=F32)
    lane = lax.broadcasted_iota(jnp.int32, (tm, LANES), 1)
    neg = -jnp.inf
    lg = jnp.where(lane < N_EXPERTS, logits, neg)
    m1 = jnp.max(lg, axis=-1, keepdims=True)
    e1 = jnp.min(jnp.where(lg == m1, lane, LANES), axis=-1, keepdims=True)
    lg2 = jnp.where(lane == e1, neg, lg)
    m2 = jnp.max(lg2, axis=-1, keepdims=True)
    e2 = jnp.min(jnp.where(lg2 == m2, lane, LANES), axis=-1, keepdims=True)
    ex = jnp.exp(m2 - m1)
    g1 = 1.0 / (1.0 + ex)
    g2 = ex / (1.0 + ex)

    onehot = jnp.where((lane == e1) | (lane == e2), 1.0, 0.0)
    r = lax.broadcasted_iota(jnp.int32, (tm, tm), 0)
    c = lax.broadcasted_iota(jnp.int32, (tm, tm), 1)
    before = jnp.where(r > c, 1.0, 0.0).astype(BF16)
    seen = jnp.dot(before, onehot.astype(BF16), preferred_element_type=F32) + carry_ref[0:1, :]
    r1 = jnp.sum(jnp.where(lane == e1, seen, 0.0), axis=-1, keepdims=True)
    r2 = jnp.sum(jnp.where(lane == e2, seen, 0.0), axis=-1, keepdims=True)
    for blk in range(tm // TOKEN_BLOCK):
        cb_ref[blk] = jnp.broadcast_to(seen[blk * TOKEN_BLOCK:blk * TOKEN_BLOCK + 1], (SUBLANES, LANES))
    total = carry_ref[0:1, :] + jnp.sum(onehot, axis=0, keepdims=True)
    carry_ref[...] = jnp.broadcast_to(total, carry_ref.shape)
    cnt_ref[...] = jnp.broadcast_to(total, cnt_ref.shape)

    meta = jnp.zeros((tm, LANES), F32)
    for idx, val in ((META_E1, e1.astype(F32)), (META_E2, e2.astype(F32)), (META_G1, g1),
                     (META_G2, g2), (META_R1, r1), (META_R2, r2)):
        meta = jnp.where(lane == idx, val, meta)
    meta_ref[...] = meta


def _router(x, g_all, layer, rw_pad, m, tm):
    T = x.shape[0]
    return pl.pallas_call(
        _router_kernel,
        grid=(T // tm,),
        in_specs=[
            pl.BlockSpec((tm, D_MODEL), lambda i: (i, 0)),
            pl.BlockSpec((None, 1, D_MODEL), lambda i: (layer, 0, 0)),
            pl.BlockSpec((None, D_MODEL, LANES), lambda i: (m, 0, 0)),
        ],
        out_specs=[
            pl.BlockSpec((tm, D_MODEL // 2), lambda i: (i, 0)),
            pl.BlockSpec((tm, LANES), lambda i: (i, 0)),
            pl.BlockSpec((SUBLANES, LANES), lambda i: (0, 0)),
            pl.BlockSpec((tm // TOKEN_BLOCK, SUBLANES, LANES), lambda i: (i, 0, 0)),
        ],
        out_shape=[
            jax.ShapeDtypeStruct((T, D_MODEL // 2), jnp.int32),
            jax.ShapeDtypeStruct((T, LANES), F32),
            jax.ShapeDtypeStruct((SUBLANES, LANES), F32),
            jax.ShapeDtypeStruct((T // TOKEN_BLOCK, SUBLANES, LANES), F32),
        ],
        scratch_shapes=[pltpu.VMEM((SUBLANES, LANES), F32)],
        compiler_params=_params("arbitrary"),
        name="moe_router",
    )(x, g_all, rw_pad)


SC_CORES = 2
SC_SUBCORES = 16
SC_CHUNK = 32
MOE_SEGMENTS = 3


def _dispatch(h, slot_tok):
    n_workers = SC_CORES * SC_SUBCORES
    n_slots, width = slot_tok.shape[0], h.shape[1]
    per_worker = n_slots // n_workers
    n = per_worker // SC_CHUNK
    assert n_slots % (n_workers * SC_CHUNK) == 0 and n >= 2
    mesh = plsc.VectorSubcoreMesh(core_axis_name="core", subcore_axis_name="subcore")

    @functools.partial(
        pl.kernel, mesh=mesh,
        out_type=jax.ShapeDtypeStruct((n_slots, width), h.dtype),
        scratch_types=[pltpu.VMEM((n, SC_CHUNK), jnp.int32),
                       pltpu.VMEM((SC_CHUNK, width), h.dtype),
                       pltpu.VMEM((SC_CHUNK, width), h.dtype),
                       pltpu.SemaphoreType.DMA((2,)),
                       pltpu.SemaphoreType.DMA((2,))],
        name="moe_dispatch")
    def gather(h_hbm, idx_hbm, xs_hbm, idx_v, buf0, buf1, gsem, wsem):
        worker = lax.axis_index("subcore") * SC_CORES + lax.axis_index("core")
        base = worker * per_worker
        bufs = (buf0, buf1)
        pltpu.sync_copy(idx_hbm.at[worker], idx_v)

        def fetch(c, slot):
            return pltpu.make_async_copy(h_hbm.at[idx_v.at[c]], bufs[slot], gsem.at[slot])

        def store(c, slot):
            return pltpu.make_async_copy(bufs[slot], xs_hbm.at[pl.ds(base + c * SC_CHUNK, SC_CHUNK)],
                                         wsem.at[slot])

        fetch(0, 0).start()

        @pl.loop(0, n // 2)
        def _(p):
            c0, c1 = 2 * p, 2 * p + 1

            @pl.when(p > 0)
            def _():
                store(c0 - 1, 1).wait()

            fetch(c1, 1).start()
            fetch(c0, 0).wait()
            store(c0, 0).start()

            @pl.when(c1 + 1 < n)
            def _():
                store(c0, 0).wait()
                fetch(c1 + 1, 0).start()

            fetch(c1, 1).wait()
            store(c1, 1).start()

        if n % 2:
            fetch(n - 1, 0).wait()
            store(n - 1, 0).start()
            store(n - 2, 1).wait()
            store(n - 1, 0).wait()
        else:
            store(n - 2, 0).wait()
            store(n - 1, 1).wait()

    return gather(h, slot_tok.reshape(n_workers, n, SC_CHUNK))


def _experts_kernel(be_ref, nu_ref, xs_ref, wg_ref, wu_ref, wd_ref, *refs, first_block, has_prev):
    o_ref, acc_ref, xb_ref = refs[1:] if has_prev else refs
    b = pl.program_id(0)
    f = pl.program_id(1)
    used = first_block + b < nu_ref[0]

    @pl.when(used & (f == 0))
    def _():
        xb_ref[...] = _unpack_bf16_pairs(xs_ref[...])

    @pl.when(used)
    def _():
        part = _swiglu_part(xb_ref[...], wg_ref[...], wu_ref[...], wd_ref[...])

        @pl.when(f == 0)
        def _():
            acc_ref[...] = part

        @pl.when(f > 0)
        def _():
            acc_ref[...] += part

        @pl.when(f == pl.num_programs(1) - 1)
        def _():
            o_ref[...] = acc_ref[...].astype(o_ref.dtype)

    @pl.when(jnp.logical_not(used) & (f == 0))
    def _():
        o_ref[...] = jnp.zeros_like(o_ref)


def _experts(xs, ys, block_expert, n_used, wg, wu, wd, m, bm, tf, first_block, total_blocks):
    n_blocks = xs.shape[0] // bm
    nf = D_FF // tf

    def used(b, nu):
        return first_block + b < nu[0]

    def expert(b, be):
        return be[first_block + b]

    def fidx(b, f, nu):
        return jnp.where(used(b, nu), f, nf - 1)

    in_specs = [
        pl.BlockSpec((bm, D_MODEL // 2), lambda b, f, be, nu: (b, 0)),
        pl.BlockSpec((None, None, D_MODEL, tf), lambda b, f, be, nu: (m, expert(b, be), 0, fidx(b, f, nu))),
        pl.BlockSpec((None, None, D_MODEL, tf), lambda b, f, be, nu: (m, expert(b, be), 0, fidx(b, f, nu))),
        pl.BlockSpec((None, None, tf, D_MODEL), lambda b, f, be, nu: (m, expert(b, be), fidx(b, f, nu), 0)),
    ]
    operands = [block_expert, n_used, xs, wg, wu, wd]
    aliases = {}
    if ys is not None:
        in_specs.append(pl.BlockSpec(memory_space=pl.ANY))
        aliases = {len(operands): 0}
        operands.append(ys)
    grid_spec = pltpu.PrefetchScalarGridSpec(
        num_scalar_prefetch=2,
        grid=(n_blocks, nf),
        in_specs=in_specs,
        out_specs=pl.BlockSpec((bm, D_MODEL), lambda b, f, be, nu: (first_block + b, 0)),
        scratch_shapes=[pltpu.VMEM((bm, D_MODEL), F32), pltpu.VMEM((bm, D_MODEL), BF16)],
    )
    return pl.pallas_call(
        functools.partial(_experts_kernel, first_block=first_block, has_prev=ys is not None),
        grid_spec=grid_spec,
        out_shape=jax.ShapeDtypeStruct((total_blocks * bm, D_MODEL), BF16),
        input_output_aliases=aliases,
        compiler_params=_params("arbitrary", "arbitrary"),
        name="moe_experts",
    )(*operands)


BF16_ROWS = 16
WINDOW = TOKEN_BLOCK + BF16_ROWS


def _combine_kernel(win_ref, ps_ref, x_ref, meta_ref, *refs):
    ys_refs, o_ref = refs[:N_EXPERTS], refs[N_EXPERTS]
    blk = pl.program_id(0)
    meta = meta_ref[...]
    e1, e2 = meta[:, META_E1:META_E1 + 1], meta[:, META_E2:META_E2 + 1]
    g1, g2 = meta[:, META_G1:META_G1 + 1], meta[:, META_G2:META_G2 + 1]
    d1, d2 = meta[:, META_R1:META_R1 + 1], meta[:, META_R2:META_R2 + 1]
    for e in range(N_EXPERTS):
        start = ps_ref[e].astype(F32)
        d1 = d1 + jnp.where(e1 == e, start, 0.0)
        d2 = d2 + jnp.where(e2 == e, start, 0.0)
    lane = lax.broadcasted_iota(jnp.int32, (1, WINDOW), 1)
    acc = x_ref[...]
    for e in range(N_EXPERTS):
        slot = (win_ref[blk * N_EXPERTS + e] + lane).astype(F32)
        sel = jnp.where((d1 == slot) & (e1 == e), g1, 0.0) + jnp.where((d2 == slot) & (e2 == e), g2, 0.0)
        acc = acc + jnp.dot(sel.astype(BF16), ys_refs[e][...], preferred_element_type=F32)
    o_ref[...] = acc


def _combine(x, meta, ys, windows, pstart):
    T = x.shape[0]
    nb = T // TOKEN_BLOCK

    def window(e):
        return pl.BlockSpec((pl.Element(WINDOW), pl.Element(D_MODEL)),
                            lambda i, win, ps: (pl.multiple_of(win[i * N_EXPERTS + e], BF16_ROWS), 0))

    grid_spec = pltpu.PrefetchScalarGridSpec(
        num_scalar_prefetch=2,
        grid=(nb,),
        in_specs=[pl.BlockSpec((TOKEN_BLOCK, D_MODEL), lambda i, win, ps: (i, 0)),
                  pl.BlockSpec((TOKEN_BLOCK, LANES), lambda i, win, ps: (i, 0))]
                 + [window(e) for e in range(N_EXPERTS)],
        out_specs=pl.BlockSpec((TOKEN_BLOCK, D_MODEL), lambda i, win, ps: (i, 0)),
    )
    return pl.pallas_call(
        _combine_kernel,
        grid_spec=grid_spec,
        out_shape=jax.ShapeDtypeStruct((T, D_MODEL), F32),
        compiler_params=_params("arbitrary"),
        name="moe_combine",
    )(windows, pstart, x, meta, *([ys] * N_EXPERTS))


def _final_norm_kernel(x_ref, g_ref, o_ref):
    o_ref[...] = _rms(x_ref[...], g_ref[...])


def _final_norm(x, g, row0, n_rows, tm):
    r0 = row0 // tm
    return pl.pallas_call(
        _final_norm_kernel,
        grid=(n_rows // tm,),
        in_specs=[pl.BlockSpec((tm, D_MODEL), lambda i: (r0 + i, 0)),
                  pl.BlockSpec((1, D_MODEL), lambda i: (0, 0))],
        out_specs=pl.BlockSpec((tm, D_MODEL), lambda i: (i, 0)),
        out_shape=jax.ShapeDtypeStruct((n_rows, D_MODEL), F32),
        compiler_params=_params("parallel"),
        name="final_norm",
    )(x, g)


def _moe_layer(x, norm_ffn_g, layer, rw_pad, wg, wu, wd, m, tm, bm, tf):
    T = x.shape[0]
    h, meta, cnt, cblk = _router(x, norm_ffn_g, layer, rw_pad, m, tm)
    e = meta[:, META_E1:META_E2 + 1].astype(jnp.int32)
    rank = meta[:, META_R1:META_R2 + 1].astype(jnp.int32)
    counts = cnt[0, :N_EXPERTS].astype(jnp.int32)
    padded = (counts + bm - 1) // bm * bm
    pend = jnp.cumsum(padded)
    pstart = pend - padded
    dest = pstart[e] + rank
    n_blocks = -(-((T * 2) // bm + N_EXPERTS + 1) // MOE_SEGMENTS) * MOE_SEGMENTS
    seg_blocks = n_blocks // MOE_SEGMENTS
    block_start = jnp.arange(n_blocks, dtype=jnp.int32) * bm
    block_expert = jnp.minimum(jnp.sum(pend[None, :] <= block_start[:, None], axis=1),
                               N_EXPERTS - 1).astype(jnp.int32)
    n_used = (pend[-1] // bm).astype(jnp.int32).reshape(1)
    tok = jnp.broadcast_to(jnp.arange(T, dtype=jnp.int32)[:, None], (T, 2))
    slot_tok = jnp.zeros((n_blocks * bm,), jnp.int32).at[dest.reshape(-1)].set(tok.reshape(-1))
    ys = None
    for s in range(MOE_SEGMENTS):
        rows = slice(s * seg_blocks * bm, (s + 1) * seg_blocks * bm)
        xs = _dispatch(h, slot_tok[rows])
        ys = _experts(xs, ys, block_expert, n_used, wg, wu, wd, m, bm, tf, s * seg_blocks, n_blocks)
    run_start = pstart[None, :] + cblk[:, 0, :N_EXPERTS].astype(jnp.int32)
    windows = (run_start // BF16_ROWS * BF16_ROWS).reshape(-1).astype(jnp.int32)
    return _combine(x, meta, ys, windows, pstart.astype(jnp.int32))


def _tril(w):
    n = w.shape[-1]
    return jnp.where(jnp.tril(jnp.ones((n, n), dtype=bool)), w, jnp.zeros((), w.dtype))


def _mix_tables(w_s, b_s, sample_len):
    reps = A_CHUNK // sample_len
    eye = jnp.eye(reps, dtype=w_s.dtype)
    w_prompt = _tril(w_s)
    w_small = _tril(w_s[:, :sample_len, :sample_len])
    w_sample = jax.vmap(lambda w: jnp.kron(eye, w))(w_small)
    b_prompt = b_s
    b_sample = jnp.tile(b_s[:, :sample_len], (1, reps))
    wmix = jnp.stack([w_prompt, w_sample])
    bias = jnp.stack([b_prompt, b_sample])[..., None]
    return wmix, jnp.broadcast_to(bias, bias.shape[:-1] + (LANES,))


def kernel(x_prompt, x_sample, state_hgrn, norm_mix_g, norm_ffn_g, final_norm_g, a_w_in, a_ln_g, a_ln_b, a_w_s, a_b_s, a_w_out, b_w_in, b_lb_logits, b_norm_g, b_w_out, ffn_w_gate, ffn_w_up, ffn_w_down, moe_router, moe_w_gate, moe_w_up, moe_w_down):
    n_p, L, d = x_prompt.shape
    n_s, l_s, _ = x_sample.shape
    assert d == D_MODEL and l_s == SAMPLE_LEN and L % A_CHUNK == 0
    T_p, T_s = n_p * L, n_s * l_s
    T = T_p + T_s
    depth = norm_mix_g.shape[0]

    tm_gate = 2 * A_CHUNK
    assert T_p % tm_gate == 0 and T_s % tm_gate == 0
    tm = next(t for t in (768, 512, 256, 128) if T % t == 0)
    tm_ffn = next(t for t in (1536, 768, 512, 256, 128) if T % t == 0)
    bm = 1024
    tf = 512
    lb_rows = min(L, 512)

    x = jnp.concatenate([x_prompt.reshape(T_p, d), x_sample.reshape(T_s, d)], axis=0)

    p = jax.nn.softmax(b_lb_logits.astype(F32), axis=0)
    lbs = (jnp.cumsum(p, axis=0) - p[0:1])[:, None, :]
    mix_g = norm_mix_g[:, None, :]
    ffn_g = norm_ffn_g[:, None, :]
    b_ng = b_norm_g[:, None, :]
    ln_g = a_ln_g[:, None, :]
    ln_b = a_ln_b[:, None, :]
    rw_pad = jnp.pad(moe_router, ((0, 0), (0, 0), (0, LANES - N_EXPERTS)))
    a_w_in, a_w_out, b_w_in, b_w_out, ffn_w_gate, ffn_w_up, ffn_w_down = (
        w.astype(BF16) for w in (a_w_in, a_w_out, b_w_in, b_w_out, ffn_w_gate, ffn_w_up, ffn_w_down))

    hg_prompt, v_sample = [], []
    hg_sample = state_hgrn
    for layer in range(depth):
        j = layer // 2
        if layer % 2 == 0:
            z = _norm_matmul(x, mix_g, layer, a_w_in, j, _gelu, BF16, tm, 1024)
            wmix, bias = _mix_tables(a_w_s[j], a_b_s[j], l_s)
            x, v = _gmlp_gate(z, x, ln_g, ln_b, wmix, bias, a_w_out, j, T_p // tm_gate, tm_gate)
            v_sample.append(v.reshape(n_s, l_s, A_HALF))
            x = _ffn_dense(x, ffn_g, layer, ffn_w_gate, ffn_w_up, ffn_w_down, j, tm_ffn, tf)
        else:
            proj = _norm_matmul(x, mix_g, layer, b_w_in, j, _identity, F32, tm, 1024)
            o_p, s_p = _hgrn_prompt(proj, lbs, b_ng, j, n_p, L, lb_rows)
            o_s, hg_sample = _hgrn_sample(proj, T_p, hg_sample, lbs, b_ng, j, n_s, 8)
            hg_prompt.append(s_p)
            o = jnp.concatenate([o_p, o_s], axis=0)
            x = _matmul_res(o, b_w_out, j, x, tm)
            x = _moe_layer(x, ffn_g, layer, rw_pad, moe_w_gate, moe_w_up, moe_w_down, j, 512, bm, tf)

    fg = final_norm_g[None, :]
    y_prompt = _final_norm(x, fg, 0, T_p, 512).reshape(n_p, L, d)
    y_sample = _final_norm(x, fg, T_p, T_s, T_s).reshape(n_s, l_s, d)
    return (y_prompt, y_sample, jnp.stack(hg_prompt), hg_sample, jnp.stack(v_sample))
```

```python
import functools
import math

import jax
import jax.numpy as jnp
from jax import lax
from jax.experimental import pallas as pl
from jax.experimental.pallas import tpu as pltpu
from jax.experimental.pallas import tpu_sc as plsc

F32 = jnp.float32
BF16 = jnp.bfloat16

D_MODEL = 1024
A_CHUNK = 128
A_HALF = 3 * D_MODEL
A_GROUPS = 8
A_GROUP_DIM = A_HALF // A_GROUPS
HG_HEADS = 8
HG_DK = 128
HG_CHUNK = 64
HG_SUB = 16
FORGET_FLOOR = 1e-20
D_FF = 7 * D_MODEL // 2
N_EXPERTS = 8
NORM_EPS = 1e-6
LANES = 128
SUBLANES = 8
SAMPLE_LEN = 4
MASKED_LOG = -1e30

VMEM_LIMIT = 56 * 1024 * 1024


def _params(*sem):
    return pltpu.CompilerParams(dimension_semantics=sem, vmem_limit_bytes=VMEM_LIMIT)


def _rms(x, g):
    ms = jnp.mean(x * x, axis=-1, keepdims=True)
    return x * lax.rsqrt(ms + NORM_EPS) * g


def _gelu(y):
    return 0.5 * y * (1.0 + lax.erf(y * math.sqrt(0.5)))


def _identity(y):
    return y


def _norm_matmul_kernel(x_ref, g_ref, w_ref, o_ref, h_ref, *, act):
    @pl.when(pl.program_id(1) == 0)
    def _():
        h_ref[...] = _rms(x_ref[...], g_ref[...]).astype(BF16)

    y = jnp.dot(h_ref[...], w_ref[...].astype(BF16), preferred_element_type=F32)
    o_ref[...] = act(y).astype(o_ref.dtype)


def _norm_matmul(x, g_all, layer, w_all, w_layer, act, out_dtype, tm, tn):
    T = x.shape[0]
    N = w_all.shape[-1]
    return pl.pallas_call(
        functools.partial(_norm_matmul_kernel, act=act),
        grid=(T // tm, N // tn),
        in_specs=[
            pl.BlockSpec((tm, D_MODEL), lambda i, j: (i, 0)),
            pl.BlockSpec((None, 1, D_MODEL), lambda i, j: (layer, 0, 0)),
            pl.BlockSpec((None, D_MODEL, tn), lambda i, j: (w_layer, 0, j)),
        ],
        out_specs=pl.BlockSpec((tm, tn), lambda i, j: (i, j)),
        out_shape=jax.ShapeDtypeStruct((T, N), out_dtype),
        scratch_shapes=[pltpu.VMEM((tm, D_MODEL), BF16)],
        compiler_params=_params("parallel", "arbitrary"),
        name="norm_matmul",
    )(x, g_all, w_all)


def _gmlp_gate_kernel(z_ref, lng_ref, lnb_ref, wmix_ref, bias_ref, wout_ref, x_ref, o_ref, v_ref):
    tm = z_ref.shape[0]
    zv = z_ref[:, A_HALF:].astype(F32)
    mu = jnp.mean(zv, axis=-1, keepdims=True)
    xc = zv - mu
    rstd = lax.rsqrt(jnp.mean(xc * xc, axis=-1, keepdims=True) + NORM_EPS)
    v = xc * rstd * lng_ref[...] + lnb_ref[...]
    v_ref[...] = v
    vb = v.astype(BF16)
    groups = []
    for g in range(A_GROUPS):
        cols = slice(g * A_GROUP_DIM, (g + 1) * A_GROUP_DIM)
        wm = wmix_ref[g].astype(BF16)
        bias = jnp.concatenate([bias_ref[g]] * (A_GROUP_DIM // LANES), axis=1)
        rows = []
        for c in range(tm // A_CHUNK):
            sl = slice(c * A_CHUNK, (c + 1) * A_CHUNK)
            s = jnp.dot(wm, vb[sl, cols], preferred_element_type=F32) + bias
            rows.append((z_ref[sl, cols].astype(F32) * s).astype(BF16))
        groups.append(jnp.concatenate(rows, axis=0))
    gated = jnp.concatenate(groups, axis=1)
    o_ref[...] = x_ref[...] + jnp.dot(gated, wout_ref[...], preferred_element_type=F32)


def _gmlp_gate(z, x, ln_g, ln_b, wmix, bias, w_out, j, n_prompt_blocks, tm):
    T = x.shape[0]
    nb = T // tm

    def kind(i):
        return jnp.where(i >= n_prompt_blocks, 1, 0)

    return pl.pallas_call(
        _gmlp_gate_kernel,
        grid=(nb,),
        in_specs=[
            pl.BlockSpec((tm, 2 * A_HALF), lambda i: (i, 0)),
            pl.BlockSpec((None, 1, A_HALF), lambda i: (j, 0, 0)),
            pl.BlockSpec((None, 1, A_HALF), lambda i: (j, 0, 0)),
            pl.BlockSpec((None, A_GROUPS, A_CHUNK, A_CHUNK), lambda i: (kind(i), 0, 0, 0)),
            pl.BlockSpec((None, A_GROUPS, A_CHUNK, LANES), lambda i: (kind(i), 0, 0, 0)),
            pl.BlockSpec((None, A_HALF, D_MODEL), lambda i: (j, 0, 0)),
            pl.BlockSpec((tm, D_MODEL), lambda i: (i, 0)),
        ],
        out_specs=[
            pl.BlockSpec((tm, D_MODEL), lambda i: (i, 0)),
            pl.BlockSpec((tm, A_HALF), lambda i: (jnp.maximum(i - n_prompt_blocks, 0), 0)),
        ],
        out_shape=[
            jax.ShapeDtypeStruct((T, D_MODEL), F32),
            jax.ShapeDtypeStruct(((nb - n_prompt_blocks) * tm, A_HALF), F32),
        ],
        compiler_params=_params("arbitrary"),
        name="gmlp_gate",
    )(z, ln_g, ln_b, wmix, bias, w_out, x)


FF_SPLIT = 2


def _swiglu_part(h, wg, wu, wd):
    sub = wg.shape[1] // FF_SPLIT
    out = None
    for c in range(FF_SPLIT):
        cols = slice(c * sub, (c + 1) * sub)
        a = jnp.dot(h, wg[:, cols].astype(BF16), preferred_element_type=F32)
        b = jnp.dot(h, wu[:, cols].astype(BF16), preferred_element_type=F32)
        m = (a * jax.nn.sigmoid(a) * b).astype(BF16)
        part = jnp.dot(m, wd[cols, :].astype(BF16), preferred_element_type=F32)
        out = part if out is None else out + part
    return out


def _ffn_kernel(x_ref, g_ref, wg_ref, wu_ref, wd_ref, o_ref, h_ref):
    f = pl.program_id(1)

    @pl.when(f == 0)
    def _():
        h_ref[...] = _rms(x_ref[...], g_ref[...]).astype(BF16)

    part = _swiglu_part(h_ref[...], wg_ref[...], wu_ref[...], wd_ref[...])

    @pl.when(f == 0)
    def _():
        o_ref[...] = x_ref[...] + part

    @pl.when(f > 0)
    def _():
        o_ref[...] += part


def _ffn_dense(x, g_all, layer, wg, wu, wd, m, tm, tf):
    T = x.shape[0]
    return pl.pallas_call(
        _ffn_kernel,
        grid=(T // tm, D_FF // tf),
        in_specs=[
            pl.BlockSpec((tm, D_MODEL), lambda i, f: (i, 0)),
            pl.BlockSpec((None, 1, D_MODEL), lambda i, f: (layer, 0, 0)),
            pl.BlockSpec((None, D_MODEL, tf), lambda i, f: (m, 0, f)),
            pl.BlockSpec((None, D_MODEL, tf), lambda i, f: (m, 0, f)),
            pl.BlockSpec((None, tf, D_MODEL), lambda i, f: (m, f, 0)),
        ],
        out_specs=pl.BlockSpec((tm, D_MODEL), lambda i, f: (i, 0)),
        out_shape=jax.ShapeDtypeStruct((T, D_MODEL), F32),
        scratch_shapes=[pltpu.VMEM((tm, D_MODEL), BF16)],
        compiler_params=_params("parallel", "arbitrary"),
        name="ffn_dense",
    )(x, g_all, wg, wu, wd)


def _matmul_res_kernel(a_ref, w_ref, x_ref, o_ref):
    o_ref[...] = x_ref[...] + jnp.dot(a_ref[...], w_ref[...], preferred_element_type=F32)


def _matmul_res(a, w_all, j, x, tm):
    T = x.shape[0]
    return pl.pallas_call(
        _matmul_res_kernel,
        grid=(T // tm,),
        in_specs=[
            pl.BlockSpec((tm, D_MODEL), lambda i: (i, 0)),
            pl.BlockSpec((None, D_MODEL, D_MODEL), lambda i: (j, 0, 0)),
            pl.BlockSpec((tm, D_MODEL), lambda i: (i, 0)),
        ],
        out_specs=pl.BlockSpec((tm, D_MODEL), lambda i: (i, 0)),
        out_shape=jax.ShapeDtypeStruct((T, D_MODEL), F32),
        compiler_params=_params("parallel"),
        name="matmul_res",
    )(a, w_all, x)


_NT = (((1,), (1,)), ((), ()))
_TN = (((0,), (0,)), ((), ()))


HG_GROUP = 8
MAX_BLOCK_DECAY = 75.0


def _hgrn_gates(qpre, fpre, lb):
    q = qpre * jax.nn.sigmoid(qpre)
    sig = jax.nn.sigmoid(fpre)
    f = lb + (1.0 - lb) * sig
    logf = jnp.log(jnp.maximum(f, FORGET_FLOOR))
    k = (1.0 - lb) * (1.0 - sig)
    return q, k, logf


def _cumsum_rows(x):
    C = x.shape[0]
    if C >= HG_SUB:
        r = lax.broadcasted_iota(jnp.int32, (C, C), 0)
        c = lax.broadcasted_iota(jnp.int32, (C, C), 1)
        tri = jnp.where(r >= c, 1.0, 0.0).astype(F32)
        return jnp.dot(tri, x, preferred_element_type=F32, precision=lax.Precision.HIGHEST)
    row = lax.broadcasted_iota(jnp.int32, (C, 1), 0)
    out = jnp.zeros_like(x)
    for s in range(C):
        out = out + jnp.where(row >= s, x[s:s + 1], 0.0)
    return out


def _chunk_prep(q, k, logf):
    C = q.shape[0]
    b = _cumsum_rows(logf)
    b_last = b[C - 1:C]
    qe = (q * jnp.exp(b)).astype(BF16)
    kd = (k * jnp.exp(b_last - b)).astype(BF16)
    return b, qe, kd, jnp.exp(b_last)


def _intra_exact(q, k, v, b, sub):
    C = q.shape[0]
    row = lax.broadcasted_iota(jnp.int32, (sub, 1), 0)
    parts = []
    for blk in range(C // sub):
        lo = blk * sub
        b_i, q_i, k_i, v_i = b[lo:lo + sub], q[lo:lo + sub], k[lo:lo + sub], v[lo:lo + sub]
        if blk == 0:
            o_i = jnp.zeros((sub, HG_DK), F32)
        else:
            ref = b[lo - 1:lo]
            qs = (q_i * jnp.exp(b_i - ref)).astype(BF16)
            ks = (k[:lo] * jnp.exp(ref - b[:lo])).astype(BF16)
            a = lax.dot_general(qs, ks, _NT, preferred_element_type=F32)
            o_i = jnp.dot(a.astype(BF16), v[:lo].astype(BF16), preferred_element_type=F32)
        for s in range(sub):
            rel = jnp.where(row >= s, b_i - b_i[s:s + 1], MASKED_LOG)
            a_col = jnp.sum(q_i * k_i[s:s + 1] * jnp.exp(rel), axis=-1, keepdims=True)
            o_i = o_i + a_col * v_i[s:s + 1]
        parts.append(o_i)
    return parts[0] if len(parts) == 1 else jnp.concatenate(parts, axis=0)


def _intra_factored(q, k, v, b, sub):
    C = q.shape[0]
    vb = v.astype(BF16)
    heads = [slice(h * HG_DK, (h + 1) * HG_DK) for h in range(q.shape[1] // HG_DK)]
    scores = []
    for blk in range(C // sub):
        lo, hi = blk * sub, (blk + 1) * sub
        if blk == 0:
            qs = q[:hi] * jnp.exp(b[:hi])
            ks = k[:hi] * jnp.exp(-b[:hi])
        else:
            ref = b[lo - 1:lo]
            qs = q[lo:hi] * jnp.exp(b[lo:hi] - ref)
            ks = k[:hi] * jnp.exp(ref - b[:hi])
        qs, ks = qs.astype(BF16), ks.astype(BF16)
        r = lax.broadcasted_iota(jnp.int32, (sub, hi), 0)
        c = lax.broadcasted_iota(jnp.int32, (sub, hi), 1)
        row = []
        for cols in heads:
            a = lax.dot_general(qs[:, cols], ks[:, cols], _NT, preferred_element_type=F32)
            row.append(jnp.where(c <= r + lo, a, 0.0).astype(BF16))
        scores.append(row)
    parts = []
    for blk, row in enumerate(scores):
        hi = (blk + 1) * sub
        outs = [jnp.dot(a, vb[:hi, cols], preferred_element_type=F32) for a, cols in zip(row, heads)]
        parts.append(outs[0] if len(outs) == 1 else jnp.concatenate(outs, axis=1))
    return jnp.concatenate(parts, axis=0)


def _state_step(o_intra, qe, kd, eb_last, v, st):
    o = o_intra + lax.dot_general(qe, st.astype(BF16), _NT, preferred_element_type=F32)
    st_new = st * eb_last + lax.dot_general(v.astype(BF16), kd, _TN, preferred_element_type=F32)
    return o, st_new


def _hgrn_chunk(q, k, v, logf, st, sub):
    b, qe, kd, eb_last = _chunk_prep(q, k, logf)
    return _state_step(_intra_exact(q, k, v, b, sub), qe, kd, eb_last, v, st)


def _head_norm_gate(o, ng, gate):
    return (o * lax.rsqrt(jnp.mean(o * o, axis=-1, keepdims=True) + NORM_EPS) * ng * gate)


def _hgrn_prompt_kernel(q_ref, f_ref, i_ref, g_ref, lb_ref, ng_ref, o_ref, sfin_ref,
                        st_ref, qs_ref, ks_ref, lf_ref, qe_ref, kd_ref, eb_ref, oi_ref):
    tb = pl.program_id(2)
    rows_total = q_ref.shape[0]
    n_chunks = rows_total // HG_CHUNK

    @pl.when(tb == 0)
    def _():
        st_ref[...] = jnp.zeros_like(st_ref)

    q, k, logf = _hgrn_gates(q_ref[...], f_ref[...], lb_ref[...])
    qs_ref[...] = q
    ks_ref[...] = k
    lf_ref[...] = logf
    block_decay = jnp.sum(logf.reshape(rows_total // HG_SUB, HG_SUB, logf.shape[-1]), axis=1)
    mild = jnp.min(block_decay) >= -MAX_BLOCK_DECAY

    heads = [slice(h * HG_DK, (h + 1) * HG_DK) for h in range(HG_GROUP)]

    def exact_group(qg, kg, vg, b, sub):
        return jnp.concatenate([_intra_exact(qg[:, c], kg[:, c], vg[:, c], b[:, c], sub) for c in heads], axis=1)

    def intra_pass(intra):
        def body(ci, carry):
            rows = pl.ds(pl.multiple_of(ci * HG_CHUNK, HG_CHUNK), HG_CHUNK)
            qg, kg, vg = qs_ref[rows, :], ks_ref[rows, :], i_ref[rows, :]
            b, qe, kd, eb_last = _chunk_prep(qg, kg, lf_ref[rows, :])
            qe_ref[rows, :] = qe
            kd_ref[rows, :] = kd
            eb_ref[pl.ds(pl.multiple_of(ci * SUBLANES, SUBLANES), SUBLANES), :] = \
                jnp.broadcast_to(eb_last, (SUBLANES, eb_last.shape[1]))
            oi_ref[rows, :] = intra(qg, kg, vg, b, HG_SUB)
            return carry
        lax.fori_loop(0, n_chunks, body, 0)

    @pl.when(mild)
    def _():
        intra_pass(_intra_factored)

    @pl.when(jnp.logical_not(mild))
    def _():
        intra_pass(exact_group)

    ng = ng_ref[...]

    def state_body(ci, carry):
        rows = pl.ds(pl.multiple_of(ci * HG_CHUNK, HG_CHUNK), HG_CHUNK)
        st = st_ref[...]
        stb = st.astype(BF16)
        qe, kd, oi = qe_ref[rows, :], kd_ref[rows, :], oi_ref[rows, :]
        vb = i_ref[rows, :].astype(BF16)
        eb_last = eb_ref[pl.ds(pl.multiple_of(ci * SUBLANES, SUBLANES), 1), :]
        outs, adds = [], []
        for c in heads:
            o = oi[:, c] + lax.dot_general(qe[:, c], stb[:, c], _NT, preferred_element_type=F32)
            outs.append(o * lax.rsqrt(jnp.mean(o * o, axis=-1, keepdims=True) + NORM_EPS))
            adds.append(lax.dot_general(vb[:, c], kd[:, c], _TN, preferred_element_type=F32))
        st_ref[...] = st * eb_last + jnp.concatenate(adds, axis=1)
        gate = jax.nn.sigmoid(g_ref[rows, :])
        o_ref[rows, :] = (jnp.concatenate(outs, axis=1) * ng * gate).astype(o_ref.dtype)
        return carry

    lax.fori_loop(0, n_chunks, state_body, 0)

    @pl.when(tb == pl.num_programs(2) - 1)
    def _():
        for h, c in enumerate(heads):
            sfin_ref[h] = st_ref[:, c].T


def _hgrn_prompt(proj, lbs, ng_all, j, n_seq, L, lb_rows):
    nb = L // lb_rows
    ng_groups = HG_HEADS // HG_GROUP
    gw = HG_GROUP * HG_DK

    def field(k):
        return pl.BlockSpec((lb_rows, gw), lambda n, h, t: (n * nb + t, k * ng_groups + h))

    vec = pl.BlockSpec((None, 1, gw), lambda n, h, t: (j, 0, h))
    return pl.pallas_call(
        _hgrn_prompt_kernel,
        grid=(n_seq, ng_groups, nb),
        in_specs=[field(0), field(1), field(2), field(3), vec, vec],
        out_specs=[
            pl.BlockSpec((lb_rows, gw), lambda n, h, t: (n * nb + t, h)),
            pl.BlockSpec((None, HG_GROUP, HG_DK, HG_DK), lambda n, h, t: (n, h, 0, 0)),
        ],
        out_shape=[
            jax.ShapeDtypeStruct((n_seq * L, D_MODEL), BF16),
            jax.ShapeDtypeStruct((n_seq, HG_HEADS, HG_DK, HG_DK), F32),
        ],
        scratch_shapes=[
            pltpu.VMEM((HG_DK, gw), F32),
            pltpu.VMEM((lb_rows, gw), F32),
            pltpu.VMEM((lb_rows, gw), F32),
            pltpu.VMEM((lb_rows, gw), F32),
            pltpu.VMEM((lb_rows, gw), BF16),
            pltpu.VMEM((lb_rows, gw), BF16),
            pltpu.VMEM((lb_rows // HG_CHUNK * SUBLANES, gw), F32),
            pltpu.VMEM((lb_rows, gw), F32),
        ],
        compiler_params=_params("parallel", "parallel", "arbitrary"),
        name="hgrn_prompt",
    )(proj, proj, proj, proj, lbs, ng_all)


def _hgrn_sample_kernel(q_ref, f_ref, i_ref, g_ref, lb_ref, ng_ref, s0_ref, o_ref, sfin_ref):
    lb = lb_ref[...]
    ng = ng_ref[...]
    n_seq = s0_ref.shape[0]
    row = lax.broadcasted_iota(jnp.int32, (SUBLANES, 1), 0)
    for p in range(n_seq // 2):
        rows = slice(p * SUBLANES, (p + 1) * SUBLANES)
        q, k, logf = _hgrn_gates(q_ref[rows, :], f_ref[rows, :], lb)
        gate = jax.nn.sigmoid(g_ref[rows, :])
        v = i_ref[rows, :]
        o_tile = jnp.zeros((SUBLANES, HG_DK), F32)
        for half in range(2):
            mine = (row >= half * SAMPLE_LEN) & (row < (half + 1) * SAMPLE_LEN)
            seq = 2 * p + half
            o, st_new = _hgrn_chunk(jnp.where(mine, q, 0.0), jnp.where(mine, k, 0.0),
                                    jnp.where(mine, v, 0.0), jnp.where(mine, logf, 0.0),
                                    s0_ref[seq].T, SUBLANES)
            sfin_ref[seq] = st_new.T
            o_tile = jnp.where(mine, o, o_tile)
        o_ref[rows, :] = _head_norm_gate(o_tile, ng, gate).astype(o_ref.dtype)


def _hgrn_sample(proj, row0, state_all, lbs, ng_all, j, n_seq, sb):
    H = HG_HEADS
    rb = sb * SAMPLE_LEN
    assert row0 % rb == 0 and sb % 2 == 0
    r0 = row0 // rb

    def field(k):
        return pl.BlockSpec((rb, HG_DK), lambda s, h: (r0 + s, k * H + h))

    vec = pl.BlockSpec((None, 1, HG_DK), lambda s, h: (j, 0, h))
    return pl.pallas_call(
        _hgrn_sample_kernel,
        grid=(n_seq // sb, H),
        in_specs=[field(0), field(1), field(2), field(3), vec, vec,
                  pl.BlockSpec((None, sb, None, HG_DK, HG_DK), lambda s, h: (j, s, h, 0, 0))],
        out_specs=[
            pl.BlockSpec((rb, HG_DK), lambda s, h: (s, h)),
            pl.BlockSpec((None, sb, None, HG_DK, HG_DK), lambda s, h: (j, s, h, 0, 0)),
        ],
        out_shape=[
            jax.ShapeDtypeStruct((n_seq * SAMPLE_LEN, D_MODEL), BF16),
            jax.ShapeDtypeStruct(state_all.shape, state_all.dtype),
        ],
        input_output_aliases={6: 1},
        compiler_params=_params("parallel", "parallel"),
        name="hgrn_sample",
    )(proj, proj, proj, proj, lbs, ng_all, state_all)


META_E1, META_E2, META_G1, META_G2, META_R1, META_R2 = range(6)
TOKEN_BLOCK = 128
HIGH_HALF = 0xFFFF0000


def _pack_bf16_pairs(h):
    half = h.shape[1] // 2
    bits = lax.bitcast_convert_type(h.astype(BF16).astype(F32), jnp.uint32)
    packed = (bits[:, :half] >> 16) | (bits[:, half:] & jnp.uint32(HIGH_HALF))
    return lax.bitcast_convert_type(packed, jnp.int32)


def _unpack_bf16_pairs(p):
    w = lax.bitcast_convert_type(p, jnp.uint32)
    lo = lax.bitcast_convert_type(w << 16, F32)
    hi = lax.bitcast_convert_type(w & jnp.uint32(HIGH_HALF), F32)
    return jnp.concatenate([lo, hi], axis=1).astype(BF16)


def _router_kernel(x_ref, g_ref, rw_ref, h_ref, meta_ref, cnt_ref, cb_ref, carry_ref):
    i = pl.program_id(0)
    tm = x_ref.shape[0]

    @pl.when(i == 0)
    def _():
        carry_ref[...] = jnp.zeros_like(carry_ref)

    h = _rms(x_ref[...], g_ref[...])
    h_ref[...] = _pack_bf16_pairs(h)
    logits = jnp.dot(h.astype(BF16), rw_ref[...].astype(BF16), preferred_element_type=F32)
    lane = lax.broadcasted_iota(jnp.int32, (tm, LANES), 1)
    neg = -jnp.inf
    lg = jnp.where(lane < N_EXPERTS, logits, neg)
    m1 = jnp.max(lg, axis=-1, keepdims=True)
    e1 = jnp.min(jnp.where(lg == m1, lane, LANES), axis=-1, keepdims=True)
    lg2 = jnp.where(lane == e1, neg, lg)
    m2 = jnp.max(lg2, axis=-1, keepdims=True)
    e2 = jnp.min(jnp.where(lg2 == m2, lane, LANES), axis=-1, keepdims=True)
    ex = jnp.exp(m2 - m1)
    g1 = 1.0 / (1.0 + ex)
    g2 = ex / (1.0 + ex)

    onehot = jnp.where((lane == e1) | (lane == e2), 1.0, 0.0)
    r = lax.broadcasted_iota(jnp.int32, (tm, tm), 0)
    c = lax.broadcasted_iota(jnp.int32, (tm, tm), 1)
    before = jnp.where(r > c, 1.0, 0.0).astype(BF16)
    seen = jnp.dot(before, onehot.astype(BF16), preferred_element_type=F32) + carry_ref[0:1, :]
    r1 = jnp.sum(jnp.where(lane == e1, seen, 0.0), axis=-1, keepdims=True)
    r2 = jnp.sum(jnp.where(lane == e2, seen, 0.0), axis=-1, keepdims=True)
    for blk in range(tm // TOKEN_BLOCK):
        cb_ref[blk] = jnp.broadcast_to(seen[blk * TOKEN_BLOCK:blk * TOKEN_BLOCK + 1], (SUBLANES, LANES))
    total = carry_ref[0:1, :] + jnp.sum(onehot, axis=0, keepdims=True)
    carry_ref[...] = jnp.broadcast_to(total, carry_ref.shape)
    cnt_ref[...] = jnp.broadcast_to(total, cnt_ref.shape)

    meta = jnp.zeros((tm, LANES), F32)
    for idx, val in ((META_E1, e1.astype(F32)), (META_E2, e2.astype(F32)), (META_G1, g1),
                     (META_G2, g2), (META_R1, r1), (META_R2, r2)):
        meta = jnp.where(lane == idx, val, meta)
    meta_ref[...] = meta


def _router(x, g_all, layer, rw_pad, m, tm):
    T = x.shape[0]
    return pl.pallas_call(
        _router_kernel,
        grid=(T // tm,),
        in_specs=[
            pl.BlockSpec((tm, D_MODEL), lambda i: (i, 0)),
            pl.BlockSpec((None, 1, D_MODEL), lambda i: (layer, 0, 0)),
            pl.BlockSpec((None, D_MODEL, LANES), lambda i: (m, 0, 0)),
        ],
        out_specs=[
            pl.BlockSpec((tm, D_MODEL // 2), lambda i: (i, 0)),
            pl.BlockSpec((tm, LANES), lambda i: (i, 0)),
            pl.BlockSpec((SUBLANES, LANES), lambda i: (0, 0)),
            pl.BlockSpec((tm // TOKEN_BLOCK, SUBLANES, LANES), lambda i: (i, 0, 0)),
        ],
        out_shape=[
            jax.ShapeDtypeStruct((T, D_MODEL // 2), jnp.int32),
            jax.ShapeDtypeStruct((T, LANES), F32),
            jax.ShapeDtypeStruct((SUBLANES, LANES), F32),
            jax.ShapeDtypeStruct((T // TOKEN_BLOCK, SUBLANES, LANES), F32),
        ],
        scratch_shapes=[pltpu.VMEM((SUBLANES, LANES), F32)],
        compiler_params=_params("arbitrary"),
        name="moe_router",
    )(x, g_all, rw_pad)


SC_CORES = 2
SC_SUBCORES = 16
SC_CHUNK = 32
MOE_SEGMENTS = 3


def _dispatch(h, slot_tok):
    n_workers = SC_CORES * SC_SUBCORES
    width = h.shape[1]
    grain = n_workers * SC_CHUNK
    slot_tok = jnp.pad(slot_tok, (0, -slot_tok.shape[0] % grain))
    n_slots = slot_tok.shape[0]
    per_worker = n_slots // n_workers
    n = per_worker // SC_CHUNK
    assert n >= 2
    mesh = plsc.VectorSubcoreMesh(core_axis_name="core", subcore_axis_name="subcore")

    @functools.partial(
        pl.kernel, mesh=mesh,
        out_type=jax.ShapeDtypeStruct((n_slots, width), h.dtype),
        scratch_types=[pltpu.VMEM((n, SC_CHUNK), jnp.int32),
                       pltpu.VMEM((SC_CHUNK, width), h.dtype),
                       pltpu.VMEM((SC_CHUNK, width), h.dtype),
                       pltpu.SemaphoreType.DMA((2,)),
                       pltpu.SemaphoreType.DMA((2,))],
        name="moe_dispatch")
    def gather(h_hbm, idx_hbm, xs_hbm, idx_v, buf0, buf1, gsem, wsem):
        worker = lax.axis_index("subcore") * SC_CORES + lax.axis_index("core")
        base = worker * per_worker
        bufs = (buf0, buf1)
        pltpu.sync_copy(idx_hbm.at[worker], idx_v)

        def fetch(c, slot):
            return pltpu.make_async_copy(h_hbm.at[idx_v.at[c]], bufs[slot], gsem.at[slot])

        def store(c, slot):
            return pltpu.make_async_copy(bufs[slot], xs_hbm.at[pl.ds(base + c * SC_CHUNK, SC_CHUNK)],
                                         wsem.at[slot])

        fetch(0, 0).start()

        @pl.loop(0, n // 2)
        def _(p):
            c0, c1 = 2 * p, 2 * p + 1

            @pl.when(p > 0)
            def _():
                store(c0 - 1, 1).wait()

            fetch(c1, 1).start()
            fetch(c0, 0).wait()
            store(c0, 0).start()

            @pl.when(c1 + 1 < n)
            def _():
                store(c0, 0).wait()
                fetch(c1 + 1, 0).start()

            fetch(c1, 1).wait()
            store(c1, 1).start()

        if n % 2:
            fetch(n - 1, 0).wait()
            store(n - 1, 0).start()
            store(n - 2, 1).wait()
            store(n - 1, 0).wait()
        else:
            store(n - 2, 0).wait()
            store(n - 1, 1).wait()

    return gather(h, slot_tok.reshape(n_workers, n, SC_CHUNK))


def _experts_kernel(be_ref, nu_ref, xs_ref, wg_ref, wu_ref, wd_ref, *refs, first_block, has_prev):
    o_ref, acc_ref, xb_ref = refs[1:] if has_prev else refs
    b = pl.program_id(0)
    f = pl.program_id(1)
    used = first_block + b < nu_ref[0]

    @pl.when(used & (f == 0))
    def _():
        xb_ref[...] = _unpack_bf16_pairs(xs_ref[...])

    @pl.when(used)
    def _():
        part = _swiglu_part(xb_ref[...], wg_ref[...], wu_ref[...], wd_ref[...])

        @pl.when(f == 0)
        def _():
            acc_ref[...] = part

        @pl.when(f > 0)
        def _():
            acc_ref[...] += part

        @pl.when(f == pl.num_programs(1) - 1)
        def _():
            o_ref[...] = acc_ref[...].astype(o_ref.dtype)

    @pl.when(jnp.logical_not(used) & (f == 0))
    def _():
        o_ref[...] = jnp.zeros_like(o_ref)


def _experts(xs, ys, block_expert, n_used, wg, wu, wd, m, bm, tf, first_block, total_blocks):
    n_blocks = xs.shape[0] // bm
    nf = D_FF // tf

    def used(b, nu):
        return first_block + b < nu[0]

    def expert(b, be):
        return be[first_block + b]

    def fidx(b, f, nu):
        return jnp.where(used(b, nu), f, nf - 1)

    in_specs = [
        pl.BlockSpec((bm, D_MODEL // 2), lambda b, f, be, nu: (b, 0)),
        pl.BlockSpec((None, None, D_MODEL, tf), lambda b, f, be, nu: (m, expert(b, be), 0, fidx(b, f, nu))),
        pl.BlockSpec((None, None, D_MODEL, tf), lambda b, f, be, nu: (m, expert(b, be), 0, fidx(b, f, nu))),
        pl.BlockSpec((None, None, tf, D_MODEL), lambda b, f, be, nu: (m, expert(b, be), fidx(b, f, nu), 0)),
    ]
    operands = [block_expert, n_used, xs, wg, wu, wd]
    aliases = {}
    if ys is not None:
        in_specs.append(pl.BlockSpec(memory_space=pl.ANY))
        aliases = {len(operands): 0}
        operands.append(ys)
    grid_spec = pltpu.PrefetchScalarGridSpec(
        num_scalar_prefetch=2,
        grid=(n_blocks, nf),
        in_specs=in_specs,
        out_specs=pl.BlockSpec((bm, D_MODEL), lambda b, f, be, nu: (first_block + b, 0)),
        scratch_shapes=[pltpu.VMEM((bm, D_MODEL), F32), pltpu.VMEM((bm, D_MODEL), BF16)],
    )
    return pl.pallas_call(
        functools.partial(_experts_kernel, first_block=first_block, has_prev=ys is not None),
        grid_spec=grid_spec,
        out_shape=jax.ShapeDtypeStruct((total_blocks * bm, D_MODEL), BF16),
        input_output_aliases=aliases,
        compiler_params=_params("arbitrary", "arbitrary"),
        name="moe_experts",
    )(*operands)


BF16_ROWS = 16
WINDOW = TOKEN_BLOCK + BF16_ROWS


def _combine_kernel(win_ref, ps_ref, x_ref, meta_ref, *refs):
    ys_refs, o_ref = refs[:N_EXPERTS], refs[N_EXPERTS]
    blk = pl.program_id(0)
    meta = meta_ref[...]
    e1, e2 = meta[:, META_E1:META_E1 + 1], meta[:, META_E2:META_E2 + 1]
    g1, g2 = meta[:, META_G1:META_G1 + 1], meta[:, META_G2:META_G2 + 1]
    d1, d2 = meta[:, META_R1:META_R1 + 1], meta[:, META_R2:META_R2 + 1]
    for e in range(N_EXPERTS):
        start = ps_ref[e].astype(F32)
        d1 = d1 + jnp.where(e1 == e, start, 0.0)
        d2 = d2 + jnp.where(e2 == e, start, 0.0)
    lane = lax.broadcasted_iota(jnp.int32, (1, WINDOW), 1)
    acc = x_ref[...]
    for e in range(N_EXPERTS):
        slot = (win_ref[blk * N_EXPERTS + e] + lane).astype(F32)
        sel = jnp.where((d1 == slot) & (e1 == e), g1, 0.0) + jnp.where((d2 == slot) & (e2 == e), g2, 0.0)
        acc = acc + jnp.dot(sel.astype(BF16), ys_refs[e][...], preferred_element_type=F32)
    o_ref[...] = acc


def _combine(x, meta, ys, windows, pstart):
    T = x.shape[0]
    nb = T // TOKEN_BLOCK

    def window(e):
        return pl.BlockSpec((pl.Element(WINDOW), pl.Element(D_MODEL)),
                            lambda i, win, ps: (pl.multiple_of(win[i * N_EXPERTS + e], BF16_ROWS), 0))

    grid_spec = pltpu.PrefetchScalarGridSpec(
        num_scalar_prefetch=2,
        grid=(nb,),
        in_specs=[pl.BlockSpec((TOKEN_BLOCK, D_MODEL), lambda i, win, ps: (i, 0)),
                  pl.BlockSpec((TOKEN_BLOCK, LANES), lambda i, win, ps: (i, 0))]
                 + [window(e) for e in range(N_EXPERTS)],
        out_specs=pl.BlockSpec((TOKEN_BLOCK, D_MODEL), lambda i, win, ps: (i, 0)),
    )
    return pl.pallas_call(
        _combine_kernel,
        grid_spec=grid_spec,
        out_shape=jax.ShapeDtypeStruct((T, D_MODEL), F32),
        compiler_params=_params("arbitrary"),
        name="moe_combine",
    )(windows, pstart, x, meta, *([ys] * N_EXPERTS))


def _final_norm_kernel(x_ref, g_ref, o_ref):
    o_ref[...] = _rms(x_ref[...], g_ref[...])


def _final_norm(x, g, row0, n_rows, tm):
    r0 = row0 // tm
    return pl.pallas_call(
        _final_norm_kernel,
        grid=(n_rows // tm,),
        in_specs=[pl.BlockSpec((tm, D_MODEL), lambda i: (r0 + i, 0)),
                  pl.BlockSpec((1, D_MODEL), lambda i: (0, 0))],
        out_specs=pl.BlockSpec((tm, D_MODEL), lambda i: (i, 0)),
        out_shape=jax.ShapeDtypeStruct((n_rows, D_MODEL), F32),
        compiler_params=_params("parallel"),
        name="final_norm",
    )(x, g)


def _moe_layer(x, norm_ffn_g, layer, rw_pad, wg, wu, wd, m, tm, bm, tf):
    T = x.shape[0]
    h, meta, cnt, cblk = _router(x, norm_ffn_g, layer, rw_pad, m, tm)
    e = meta[:, META_E1:META_E2 + 1].astype(jnp.int32)
    rank = meta[:, META_R1:META_R2 + 1].astype(jnp.int32)
    counts = cnt[0, :N_EXPERTS].astype(jnp.int32)
    padded = (counts + bm - 1) // bm * bm
    pend = jnp.cumsum(padded)
    pstart = pend - padded
    dest = pstart[e] + rank
    n_blocks = -(-((T * 2) // bm + N_EXPERTS + 1) // MOE_SEGMENTS) * MOE_SEGMENTS
    seg_blocks = n_blocks // MOE_SEGMENTS
    block_start = jnp.arange(n_blocks, dtype=jnp.int32) * bm
    block_expert = jnp.minimum(jnp.sum(pend[None, :] <= block_start[:, None], axis=1),
                               N_EXPERTS - 1).astype(jnp.int32)
    n_used = (pend[-1] // bm).astype(jnp.int32).reshape(1)
    tok = jnp.broadcast_to(jnp.arange(T, dtype=jnp.int32)[:, None], (T, 2))
    slot_tok = jnp.zeros((n_blocks * bm,), jnp.int32).at[dest.reshape(-1)].set(tok.reshape(-1))
    ys = None
    for s in range(MOE_SEGMENTS):
        rows = slice(s * seg_blocks * bm, (s + 1) * seg_blocks * bm)
        xs = _dispatch(h, slot_tok[rows])
        ys = _experts(xs, ys, block_expert, n_used, wg, wu, wd, m, bm, tf, s * seg_blocks, n_blocks)
    run_start = pstart[None, :] + cblk[:, 0, :N_EXPERTS].astype(jnp.int32)
    windows = (run_start // BF16_ROWS * BF16_ROWS).reshape(-1).astype(jnp.int32)
    return _combine(x, meta, ys, windows, pstart.astype(jnp.int32))


def _tril(w):
    n = w.shape[-1]
    return jnp.where(jnp.tril(jnp.ones((n, n), dtype=bool)), w, jnp.zeros((), w.dtype))


def _mix_tables(w_s, b_s, sample_len):
    reps = A_CHUNK // sample_len
    eye = jnp.eye(reps, dtype=w_s.dtype)
    w_prompt = _tril(w_s)
    w_small = _tril(w_s[:, :sample_len, :sample_len])
    w_sample = jax.vmap(lambda w: jnp.kron(eye, w))(w_small)
    b_prompt = b_s
    b_sample = jnp.tile(b_s[:, :sample_len], (1, reps))
    wmix = jnp.stack([w_prompt, w_sample])
    bias = jnp.stack([b_prompt, b_sample])[..., None]
    return wmix, jnp.broadcast_to(bias, bias.shape[:-1] + (LANES,))


def kernel(x_prompt, x_sample, state_hgrn, norm_mix_g, norm_ffn_g, final_norm_g, a_w_in, a_ln_g, a_ln_b, a_w_s, a_b_s, a_w_out, b_w_in, b_lb_logits, b_norm_g, b_w_out, ffn_w_gate, ffn_w_up, ffn_w_down, moe_router, moe_w_gate, moe_w_up, moe_w_down):
    n_p, L, d = x_prompt.shape
    n_s, l_s, _ = x_sample.shape
    assert d == D_MODEL and l_s == SAMPLE_LEN and L % A_CHUNK == 0
    T_p, T_s = n_p * L, n_s * l_s
    T = T_p + T_s
    depth = norm_mix_g.shape[0]

    tm_gate = 2 * A_CHUNK
    assert T_p % tm_gate == 0 and T_s % tm_gate == 0
    tm = next(t for t in (768, 512, 256, 128) if T % t == 0)
    tm_ffn = next(t for t in (1536, 768, 512, 256, 128) if T % t == 0)
    bm = -(-(T * 2 // N_EXPERTS + 128) // (4 * BF16_ROWS)) * BF16_ROWS
    tf = 512
    lb_rows = min(L, 512)

    x = jnp.concatenate([x_prompt.reshape(T_p, d), x_sample.reshape(T_s, d)], axis=0)

    p = jax.nn.softmax(b_lb_logits.astype(F32), axis=0)
    lbs = (jnp.cumsum(p, axis=0) - p[0:1])[:, None, :]
    mix_g = norm_mix_g[:, None, :]
    ffn_g = norm_ffn_g[:, None, :]
    b_ng = b_norm_g[:, None, :]
    ln_g = a_ln_g[:, None, :]
    ln_b = a_ln_b[:, None, :]
    rw_pad = jnp.pad(moe_router, ((0, 0), (0, 0), (0, LANES - N_EXPERTS)))
    a_w_in, a_w_out, b_w_in, b_w_out, ffn_w_gate, ffn_w_up, ffn_w_down = (
        w.astype(BF16) for w in (a_w_in, a_w_out, b_w_in, b_w_out, ffn_w_gate, ffn_w_up, ffn_w_down))

    hg_prompt, v_sample = [], []
    hg_sample = state_hgrn
    for layer in range(depth):
        j = layer // 2
        if layer % 2 == 0:
            z = _norm_matmul(x, mix_g, layer, a_w_in, j, _gelu, BF16, tm, 1024)
            wmix, bias = _mix_tables(a_w_s[j], a_b_s[j], l_s)
            x, v = _gmlp_gate(z, x, ln_g, ln_b, wmix, bias, a_w_out, j, T_p // tm_gate, tm_gate)
            v_sample.append(v.reshape(n_s, l_s, A_HALF))
            x = _ffn_dense(x, ffn_g, layer, ffn_w_gate, ffn_w_up, ffn_w_down, j, tm_ffn, tf)
        else:
            proj = _norm_matmul(x, mix_g, layer, b_w_in, j, _identity, F32, tm, 1024)
            o_p, s_p = _hgrn_prompt(proj, lbs, b_ng, j, n_p, L, lb_rows)
            o_s, hg_sample = _hgrn_sample(proj, T_p, hg_sample, lbs, b_ng, j, n_s, 8)
            hg_prompt.append(s_p)
            o = jnp.concatenate([o_p, o_s], axis=0)
            x = _matmul_res(o, b_w_out, j, x, tm)
            x = _moe_layer(x, ffn_g, layer, rw_pad, moe_w_gate, moe_w_up, moe_w_down, j, 512, bm, tf)

    fg = final_norm_g[None, :]
    y_prompt = _final_norm(x, fg, 0, T_p, 512).reshape(n_p, L, d)
    y_sample = _final_norm(x, fg, T_p, T_s, T_s).reshape(n_s, l_s, d)
    return (y_prompt, y_sample, jnp.stack(hg_prompt), hg_sample, jnp.stack(v_sample))
```

```python
import functools
import math

import jax
import jax.numpy as jnp
from jax import lax
from jax.experimental import pallas as pl
from jax.experimental.pallas import tpu as pltpu

F32 = jnp.float32
BF16 = jnp.bfloat16

D_MODEL = 1024
A_CHUNK = 128
A_HALF = 3 * D_MODEL
A_GROUPS = 8
A_GROUP_DIM = A_HALF // A_GROUPS
HG_HEADS = 8
HG_DK = 128
HG_CHUNK = 64
HG_SUB = 16
FORGET_FLOOR = 1e-20
D_FF = 7 * D_MODEL // 2
N_EXPERTS = 8
NORM_EPS = 1e-6
LANES = 128
SUBLANES = 8
SAMPLE_LEN = 4
MASKED_LOG = -1e30

VMEM_LIMIT = 56 * 1024 * 1024


def _params(*sem):
    return pltpu.CompilerParams(dimension_semantics=sem, vmem_limit_bytes=VMEM_LIMIT)


def _rms(x, g):
    ms = jnp.mean(x * x, axis=-1, keepdims=True)
    return x * lax.rsqrt(ms + NORM_EPS) * g


def _gelu(y):
    return 0.5 * y * (1.0 + lax.erf(y * math.sqrt(0.5)))


def _identity(y):
    return y


def _norm_matmul_kernel(x_ref, g_ref, w_ref, o_ref, h_ref, *, act):
    @pl.when(pl.program_id(1) == 0)
    def _():
        h_ref[...] = _rms(x_ref[...], g_ref[...]).astype(BF16)

    y = jnp.dot(h_ref[...], w_ref[...].astype(BF16), preferred_element_type=F32)
    o_ref[...] = act(y).astype(o_ref.dtype)


def _norm_matmul(x, g_all, layer, w_all, w_layer, act, out_dtype, tm, tn):
    T = x.shape[0]
    N = w_all.shape[-1]
    return pl.pallas_call(
        functools.partial(_norm_matmul_kernel, act=act),
        grid=(T // tm, N // tn),
        in_specs=[
            pl.BlockSpec((tm, D_MODEL), lambda i, j: (i, 0)),
            pl.BlockSpec((None, 1, D_MODEL), lambda i, j: (layer, 0, 0)),
            pl.BlockSpec((None, D_MODEL, tn), lambda i, j: (w_layer, 0, j)),
        ],
        out_specs=pl.BlockSpec((tm, tn), lambda i, j: (i, j)),
        out_shape=jax.ShapeDtypeStruct((T, N), out_dtype),
        scratch_shapes=[pltpu.VMEM((tm, D_MODEL), BF16)],
        compiler_params=_params("parallel", "arbitrary"),
        name="norm_matmul",
    )(x, g_all, w_all)


def _gmlp_gate_kernel(z_ref, lng_ref, lnb_ref, wmix_ref, bias_ref, wout_ref, x_ref, o_ref, v_ref):
    tm = z_ref.shape[0]
    zv = z_ref[:, A_HALF:].astype(F32)
    mu = jnp.mean(zv, axis=-1, keepdims=True)
    xc = zv - mu
    rstd = lax.rsqrt(jnp.mean(xc * xc, axis=-1, keepdims=True) + NORM_EPS)
    v = xc * rstd * lng_ref[...] + lnb_ref[...]
    v_ref[...] = v
    vb = v.astype(BF16)
    groups = []
    for g in range(A_GROUPS):
        cols = slice(g * A_GROUP_DIM, (g + 1) * A_GROUP_DIM)
        wm = wmix_ref[g].astype(BF16)
        bias = jnp.concatenate([bias_ref[g]] * (A_GROUP_DIM // LANES), axis=1)
        rows = []
        for c in range(tm // A_CHUNK):
            sl = slice(c * A_CHUNK, (c + 1) * A_CHUNK)
            s = jnp.dot(wm, vb[sl, cols], preferred_element_type=F32) + bias
            rows.append((z_ref[sl, cols].astype(F32) * s).astype(BF16))
        groups.append(jnp.concatenate(rows, axis=0))
    gated = jnp.concatenate(groups, axis=1)
    o_ref[...] = x_ref[...] + jnp.dot(gated, wout_ref[...], preferred_element_type=F32)


def _gmlp_gate(z, x, ln_g, ln_b, wmix, bias, w_out, j, n_prompt_blocks, tm):
    T = x.shape[0]
    nb = T // tm

    def kind(i):
        return jnp.where(i >= n_prompt_blocks, 1, 0)

    return pl.pallas_call(
        _gmlp_gate_kernel,
        grid=(nb,),
        in_specs=[
            pl.BlockSpec((tm, 2 * A_HALF), lambda i: (i, 0)),
            pl.BlockSpec((None, 1, A_HALF), lambda i: (j, 0, 0)),
            pl.BlockSpec((None, 1, A_HALF), lambda i: (j, 0, 0)),
            pl.BlockSpec((None, A_GROUPS, A_CHUNK, A_CHUNK), lambda i: (kind(i), 0, 0, 0)),
            pl.BlockSpec((None, A_GROUPS, A_CHUNK, LANES), lambda i: (kind(i), 0, 0, 0)),
            pl.BlockSpec((None, A_HALF, D_MODEL), lambda i: (j, 0, 0)),
            pl.BlockSpec((tm, D_MODEL), lambda i: (i, 0)),
        ],
        out_specs=[
            pl.BlockSpec((tm, D_MODEL), lambda i: (i, 0)),
            pl.BlockSpec((tm, A_HALF), lambda i: (jnp.maximum(i - n_prompt_blocks, 0), 0)),
        ],
        out_shape=[
            jax.ShapeDtypeStruct((T, D_MODEL), F32),
            jax.ShapeDtypeStruct(((nb - n_prompt_blocks) * tm, A_HALF), F32),
        ],
        compiler_params=_params("arbitrary"),
        name="gmlp_gate",
    )(z, ln_g, ln_b, wmix, bias, w_out, x)


FF_SPLIT = 2


def _swiglu_part(h, wg, wu, wd):
    sub = wg.shape[1] // FF_SPLIT
    out = None
    for c in range(FF_SPLIT):
        cols = slice(c * sub, (c + 1) * sub)
        a = jnp.dot(h, wg[:, cols].astype(BF16), preferred_element_type=F32)
        b = jnp.dot(h, wu[:, cols].astype(BF16), preferred_element_type=F32)
        m = (a * jax.nn.sigmoid(a) * b).astype(BF16)
        part = jnp.dot(m, wd[cols, :].astype(BF16), preferred_element_type=F32)
        out = part if out is None else out + part
    return out


def _ffn_kernel(x_ref, g_ref, wg_ref, wu_ref, wd_ref, o_ref, h_ref):
    f = pl.program_id(1)

    @pl.when(f == 0)
    def _():
        h_ref[...] = _rms(x_ref[...], g_ref[...]).astype(BF16)

    part = _swiglu_part(h_ref[...], wg_ref[...], wu_ref[...], wd_ref[...])

    @pl.when(f == 0)
    def _():
        o_ref[...] = x_ref[...] + part

    @pl.when(f > 0)
    def _():
        o_ref[...] += part


def _ffn_dense(x, g_all, layer, wg, wu, wd, m, tm, tf):
    T = x.shape[0]
    return pl.pallas_call(
        _ffn_kernel,
        grid=(T // tm, D_FF // tf),
        in_specs=[
            pl.BlockSpec((tm, D_MODEL), lambda i, f: (i, 0)),
            pl.BlockSpec((None, 1, D_MODEL), lambda i, f: (layer, 0, 0)),
            pl.BlockSpec((None, D_MODEL, tf), lambda i, f: (m, 0, f)),
            pl.BlockSpec((None, D_MODEL, tf), lambda i, f: (m, 0, f)),
            pl.BlockSpec((None, tf, D_MODEL), lambda i, f: (m, f, 0)),
        ],
        out_specs=pl.BlockSpec((tm, D_MODEL), lambda i, f: (i, 0)),
        out_shape=jax.ShapeDtypeStruct((T, D_MODEL), F32),
        scratch_shapes=[pltpu.VMEM((tm, D_MODEL), BF16)],
        compiler_params=_params("parallel", "arbitrary"),
        name="ffn_dense",
    )(x, g_all, wg, wu, wd)


def _matmul_res_kernel(a_ref, w_ref, x_ref, o_ref):
    o_ref[...] = x_ref[...] + jnp.dot(a_ref[...], w_ref[...], preferred_element_type=F32)


def _matmul_res(a, w_all, j, x, tm):
    T = x.shape[0]
    return pl.pallas_call(
        _matmul_res_kernel,
        grid=(T // tm,),
        in_specs=[
            pl.BlockSpec((tm, D_MODEL), lambda i: (i, 0)),
            pl.BlockSpec((None, D_MODEL, D_MODEL), lambda i: (j, 0, 0)),
            pl.BlockSpec((tm, D_MODEL), lambda i: (i, 0)),
        ],
        out_specs=pl.BlockSpec((tm, D_MODEL), lambda i: (i, 0)),
        out_shape=jax.ShapeDtypeStruct((T, D_MODEL), F32),
        compiler_params=_params("parallel"),
        name="matmul_res",
    )(a, w_all, x)


_NT = (((1,), (1,)), ((), ()))
_TN = (((0,), (0,)), ((), ()))


HG_GROUP = 8
MAX_BLOCK_DECAY = 75.0


def _hgrn_gates(qpre, fpre, lb):
    q = qpre * jax.nn.sigmoid(qpre)
    sig = jax.nn.sigmoid(fpre)
    f = lb + (1.0 - lb) * sig
    logf = jnp.log(jnp.maximum(f, FORGET_FLOOR))
    k = (1.0 - lb) * (1.0 - sig)
    return q, k, logf


def _cumsum_rows(x):
    C = x.shape[0]
    if C >= HG_SUB:
        r = lax.broadcasted_iota(jnp.int32, (C, C), 0)
        c = lax.broadcasted_iota(jnp.int32, (C, C), 1)
        tri = jnp.where(r >= c, 1.0, 0.0).astype(F32)
        return jnp.dot(tri, x, preferred_element_type=F32, precision=lax.Precision.HIGHEST)
    row = lax.broadcasted_iota(jnp.int32, (C, 1), 0)
    out = jnp.zeros_like(x)
    for s in range(C):
        out = out + jnp.where(row >= s, x[s:s + 1], 0.0)
    return out


def _chunk_prep(q, k, logf):
    C = q.shape[0]
    b = _cumsum_rows(logf)
    b_last = b[C - 1:C]
    qe = (q * jnp.exp(b)).astype(BF16)
    kd = (k * jnp.exp(b_last - b)).astype(BF16)
    return b, qe, kd, jnp.exp(b_last)


def _intra_exact(q, k, v, b, sub):
    C = q.shape[0]
    row = lax.broadcasted_iota(jnp.int32, (sub, 1), 0)
    parts = []
    for blk in range(C // sub):
        lo = blk * sub
        b_i, q_i, k_i, v_i = b[lo:lo + sub], q[lo:lo + sub], k[lo:lo + sub], v[lo:lo + sub]
        if blk == 0:
            o_i = jnp.zeros((sub, HG_DK), F32)
        else:
            ref = b[lo - 1:lo]
            qs = (q_i * jnp.exp(b_i - ref)).astype(BF16)
            ks = (k[:lo] * jnp.exp(ref - b[:lo])).astype(BF16)
            a = lax.dot_general(qs, ks, _NT, preferred_element_type=F32)
            o_i = jnp.dot(a.astype(BF16), v[:lo].astype(BF16), preferred_element_type=F32)
        for s in range(sub):
            rel = jnp.where(row >= s, b_i - b_i[s:s + 1], MASKED_LOG)
            a_col = jnp.sum(q_i * k_i[s:s + 1] * jnp.exp(rel), axis=-1, keepdims=True)
            o_i = o_i + a_col * v_i[s:s + 1]
        parts.append(o_i)
    return parts[0] if len(parts) == 1 else jnp.concatenate(parts, axis=0)


def _intra_factored(q, k, v, b, sub):
    C = q.shape[0]
    vb = v.astype(BF16)
    heads = [slice(h * HG_DK, (h + 1) * HG_DK) for h in range(q.shape[1] // HG_DK)]
    scores = []
    for blk in range(C // sub):
        lo, hi = blk * sub, (blk + 1) * sub
        if blk == 0:
            qs = q[:hi] * jnp.exp(b[:hi])
            ks = k[:hi] * jnp.exp(-b[:hi])
        else:
            ref = b[lo - 1:lo]
            qs = q[lo:hi] * jnp.exp(b[lo:hi] - ref)
            ks = k[:hi] * jnp.exp(ref - b[:hi])
        qs, ks = qs.astype(BF16), ks.astype(BF16)
        r = lax.broadcasted_iota(jnp.int32, (sub, hi), 0)
        c = lax.broadcasted_iota(jnp.int32, (sub, hi), 1)
        row = []
        for cols in heads:
            a = lax.dot_general(qs[:, cols], ks[:, cols], _NT, preferred_element_type=F32)
            row.append(jnp.where(c <= r + lo, a, 0.0).astype(BF16))
        scores.append(row)
    parts = []
    for blk, row in enumerate(scores):
        hi = (blk + 1) * sub
        outs = [jnp.dot(a, vb[:hi, cols], preferred_element_type=F32) for a, cols in zip(row, heads)]
        parts.append(outs[0] if len(outs) == 1 else jnp.concatenate(outs, axis=1))
    return jnp.concatenate(parts, axis=0)


def _state_step(o_intra, qe, kd, eb_last, v, st):
    o = o_intra + lax.dot_general(qe, st.astype(BF16), _NT, preferred_element_type=F32)
    st_new = st * eb_last + lax.dot_general(v.astype(BF16), kd, _TN, preferred_element_type=F32)
    return o, st_new


def _hgrn_chunk(q, k, v, logf, st, sub):
    b, qe, kd, eb_last = _chunk_prep(q, k, logf)
    return _state_step(_intra_exact(q, k, v, b, sub), qe, kd, eb_last, v, st)


def _head_norm_gate(o, ng, gate):
    return (o * lax.rsqrt(jnp.mean(o * o, axis=-1, keepdims=True) + NORM_EPS) * ng * gate)


def _hgrn_prompt_kernel(q_ref, f_ref, i_ref, g_ref, lb_ref, ng_ref, o_ref, sfin_ref,
                        st_ref, qs_ref, ks_ref, lf_ref, qe_ref, kd_ref, eb_ref, oi_ref):
    tb = pl.program_id(2)
    rows_total = q_ref.shape[0]
    n_chunks = rows_total // HG_CHUNK

    @pl.when(tb == 0)
    def _():
        st_ref[...] = jnp.zeros_like(st_ref)

    q, k, logf = _hgrn_gates(q_ref[...], f_ref[...], lb_ref[...])
    qs_ref[...] = q
    ks_ref[...] = k
    lf_ref[...] = logf
    block_decay = jnp.sum(logf.reshape(rows_total // HG_SUB, HG_SUB, logf.shape[-1]), axis=1)
    mild = jnp.min(block_decay) >= -MAX_BLOCK_DECAY

    heads = [slice(h * HG_DK, (h + 1) * HG_DK) for h in range(HG_GROUP)]

    def exact_group(qg, kg, vg, b, sub):
        return jnp.concatenate([_intra_exact(qg[:, c], kg[:, c], vg[:, c], b[:, c], sub) for c in heads], axis=1)

    def intra_pass(intra):
        def body(ci, carry):
            rows = pl.ds(pl.multiple_of(ci * HG_CHUNK, HG_CHUNK), HG_CHUNK)
            qg, kg, vg = qs_ref[rows, :], ks_ref[rows, :], i_ref[rows, :]
            b, qe, kd, eb_last = _chunk_prep(qg, kg, lf_ref[rows, :])
            qe_ref[rows, :] = qe
            kd_ref[rows, :] = kd
            eb_ref[pl.ds(pl.multiple_of(ci * SUBLANES, SUBLANES), SUBLANES), :] = \
                jnp.broadcast_to(eb_last, (SUBLANES, eb_last.shape[1]))
            oi_ref[rows, :] = intra(qg, kg, vg, b, HG_SUB)
            return carry
        lax.fori_loop(0, n_chunks, body, 0)

    @pl.when(mild)
    def _():
        intra_pass(_intra_factored)

    @pl.when(jnp.logical_not(mild))
    def _():
        intra_pass(exact_group)

    ng = ng_ref[...]

    def state_body(ci, carry):
        rows = pl.ds(pl.multiple_of(ci * HG_CHUNK, HG_CHUNK), HG_CHUNK)
        st = st_ref[...]
        stb = st.astype(BF16)
        qe, kd, oi = qe_ref[rows, :], kd_ref[rows, :], oi_ref[rows, :]
        vb = i_ref[rows, :].astype(BF16)
        eb_last = eb_ref[pl.ds(pl.multiple_of(ci * SUBLANES, SUBLANES), 1), :]
        outs, adds = [], []
        for c in heads:
            o = oi[:, c] + lax.dot_general(qe[:, c], stb[:, c], _NT, preferred_element_type=F32)
            outs.append(o * lax.rsqrt(jnp.mean(o * o, axis=-1, keepdims=True) + NORM_EPS))
            adds.append(lax.dot_general(vb[:, c], kd[:, c], _TN, preferred_element_type=F32))
        st_ref[...] = st * eb_last + jnp.concatenate(adds, axis=1)
        gate = jax.nn.sigmoid(g_ref[rows, :])
        o_ref[rows, :] = (jnp.concatenate(outs, axis=1) * ng * gate).astype(o_ref.dtype)
        return carry

    lax.fori_loop(0, n_chunks, state_body, 0)

    @pl.when(tb == pl.num_programs(2) - 1)
    def _():
        for h, c in enumerate(heads):
            sfin_ref[h] = st_ref[:, c].T


def _hgrn_prompt(proj, lbs, ng_all, j, n_seq, L, lb_rows):
    nb = L // lb_rows
    ng_groups = HG_HEADS // HG_GROUP
    gw = HG_GROUP * HG_DK

    def field(k):
        return pl.BlockSpec((lb_rows, gw), lambda n, h, t: (n * nb + t, k * ng_groups + h))

    vec = pl.BlockSpec((None, 1, gw), lambda n, h, t: (j, 0, h))
    return pl.pallas_call(
        _hgrn_prompt_kernel,
        grid=(n_seq, ng_groups, nb),
        in_specs=[field(0), field(1), field(2), field(3), vec, vec],
        out_specs=[
            pl.BlockSpec((lb_rows, gw), lambda n, h, t: (n * nb + t, h)),
            pl.BlockSpec((None, HG_GROUP, HG_DK, HG_DK), lambda n, h, t: (n, h, 0, 0)),
        ],
        out_shape=[
            jax.ShapeDtypeStruct((n_seq * L, D_MODEL), BF16),
            jax.ShapeDtypeStruct((n_seq, HG_HEADS, HG_DK, HG_DK), F32),
        ],
        scratch_shapes=[
            pltpu.VMEM((HG_DK, gw), F32),
            pltpu.VMEM((lb_rows, gw), F32),
            pltpu.VMEM((lb_rows, gw), F32),
            pltpu.VMEM((lb_rows, gw), F32),
            pltpu.VMEM((lb_rows, gw), BF16),
            pltpu.VMEM((lb_rows, gw), BF16),
            pltpu.VMEM((lb_rows // HG_CHUNK * SUBLANES, gw), F32),
            pltpu.VMEM((lb_rows, gw), F32),
        ],
        compiler_params=_params("parallel", "parallel", "arbitrary"),
        name="hgrn_prompt",
    )(proj, proj, proj, proj, lbs, ng_all)


def _hgrn_sample_kernel(q_ref, f_ref, i_ref, g_ref, lb_ref, ng_ref, s0_ref, o_ref, sfin_ref):
    lb = lb_ref[...]
    ng = ng_ref[...]
    n_seq = s0_ref.shape[0]
    row = lax.broadcasted_iota(jnp.int32, (SUBLANES, 1), 0)
    for p in range(n_seq // 2):
        rows = slice(p * SUBLANES, (p + 1) * SUBLANES)
        q, k, logf = _hgrn_gates(q_ref[rows, :], f_ref[rows, :], lb)
        gate = jax.nn.sigmoid(g_ref[rows, :])
        v = i_ref[rows, :]
        o_tile = jnp.zeros((SUBLANES, HG_DK), F32)
        for half in range(2):
            mine = (row >= half * SAMPLE_LEN) & (row < (half + 1) * SAMPLE_LEN)
            seq = 2 * p + half
            o, st_new = _hgrn_chunk(jnp.where(mine, q, 0.0), jnp.where(mine, k, 0.0),
                                    jnp.where(mine, v, 0.0), jnp.where(mine, logf, 0.0),
                                    s0_ref[seq].T, SUBLANES)
            sfin_ref[seq] = st_new.T
            o_tile = jnp.where(mine, o, o_tile)
        o_ref[rows, :] = _head_norm_gate(o_tile, ng, gate).astype(o_ref.dtype)


def _hgrn_sample(proj, row0, state_all, lbs, ng_all, j, n_seq, sb):
    H = HG_HEADS
    rb = sb * SAMPLE_LEN
    assert row0 % rb == 0 and sb % 2 == 0
    r0 = row0 // rb

    def field(k):
        return pl.BlockSpec((rb, HG_DK), lambda s, h: (r0 + s, k * H + h))

    vec = pl.BlockSpec((None, 1, HG_DK), lambda s, h: (j, 0, h))
    return pl.pallas_call(
        _hgrn_sample_kernel,
        grid=(n_seq // sb, H),
        in_specs=[field(0), field(1), field(2), field(3), vec, vec,
                  pl.BlockSpec((None, sb, None, HG_DK, HG_DK), lambda s, h: (j, s, h, 0, 0))],
        out_specs=[
            pl.BlockSpec((rb, HG_DK), lambda s, h: (s, h)),
            pl.BlockSpec((None, sb, None, HG_DK, HG_DK), lambda s, h: (j, s, h, 0, 0)),
        ],
        out_shape=[
            jax.ShapeDtypeStruct((n_seq * SAMPLE_LEN, D_MODEL), BF16),
            jax.ShapeDtypeStruct(state_all.shape, state_all.dtype),
        ],
        input_output_aliases={6: 1},
        compiler_params=_params("parallel", "parallel"),
        name="hgrn_sample",
    )(proj, proj, proj, proj, lbs, ng_all, state_all)


META_E1, META_E2, META_G1, META_G2, META_R1, META_R2 = range(6)
TOKEN_BLOCK = 128


def _router_kernel(x_ref, g_ref, rw_ref, h_ref, meta_ref, cnt_ref, cb_ref, carry_ref):
    i = pl.program_id(0)
    tm = x_ref.shape[0]

    @pl.when(i == 0)
    def _():
        carry_ref[...] = jnp.zeros_like(carry_ref)

    h = _rms(x_ref[...], g_ref[...])
    h_ref[...] = h.astype(h_ref.dtype)
    logits = jnp.dot(h.astype(BF16), rw_ref[...].astype(BF16), preferred_element_type=F32)
    lane = lax.broadcasted_iota(jnp.int32, (tm, LANES), 1)
    neg = -jnp.inf
    lg = jnp.where(lane < N_EXPERTS, logits, neg)
    m1 = jnp.max(lg, axis=-1, keepdims=True)
    e1 = jnp.min(jnp.where(lg == m1, lane, LANES), axis=-1, keepdims=True)
    lg2 = jnp.where(lane == e1, neg, lg)
    m2 = jnp.max(lg2, axis=-1, keepdims=True)
    e2 = jnp.min(jnp.where(lg2 == m2, lane, LANES), axis=-1, keepdims=True)
    ex = jnp.exp(m2 - m1)
    g1 = 1.0 / (1.0 + ex)
    g2 = ex / (1.0 + ex)

    onehot = jnp.where((lane == e1) | (lane == e2), 1.0, 0.0)
    r = lax.broadcasted_iota(jnp.int32, (tm, tm), 0)
    c = lax.broadcasted_iota(jnp.int32, (tm, tm), 1)
    before = jnp.where(r > c, 1.0, 0.0).astype(BF16)
    seen = jnp.dot(before, onehot.astype(BF16), preferred_element_type=F32) + carry_ref[0:1, :]
    r1 = jnp.sum(jnp.where(lane == e1, seen, 0.0), axis=-1, keepdims=True)
    r2 = jnp.sum(jnp.where(lane == e2, seen, 0.0), axis=-1, keepdims=True)
    for blk in range(tm // TOKEN_BLOCK):
        cb_ref[blk] = jnp.broadcast_to(seen[blk * TOKEN_BLOCK:blk * TOKEN_BLOCK + 1], (SUBLANES, LANES))
    total = carry_ref[0:1, :] + jnp.sum(onehot, axis=0, keepdims=True)
    carry_ref[...] = jnp.broadcast_to(total, carry_ref.shape)
    cnt_ref[...] = jnp.broadcast_to(total, cnt_ref.shape)

    meta = jnp.zeros((tm, LANES), F32)
    for idx, val in ((META_E1, e1.astype(F32)), (META_E2, e2.astype(F32)), (META_G1, g1),
                     (META_G2, g2), (META_R1, r1), (META_R2, r2)):
        meta = jnp.where(lane == idx, val, meta)
    meta_ref[...] = meta


def _router(x, g_all, layer, rw_pad, m, tm):
    T = x.shape[0]
    return pl.pallas_call(
        _router_kernel,
        grid=(T // tm,),
        in_specs=[
            pl.BlockSpec((tm, D_MODEL), lambda i: (i, 0)),
            pl.BlockSpec((None, 1, D_MODEL), lambda i: (layer, 0, 0)),
            pl.BlockSpec((None, D_MODEL, LANES), lambda i: (m, 0, 0)),
        ],
        out_specs=[
            pl.BlockSpec((tm, D_MODEL), lambda i: (i, 0)),
            pl.BlockSpec((tm, LANES), lambda i: (i, 0)),
            pl.BlockSpec((SUBLANES, LANES), lambda i: (0, 0)),
            pl.BlockSpec((tm // TOKEN_BLOCK, SUBLANES, LANES), lambda i: (i, 0, 0)),
        ],
        out_shape=[
            jax.ShapeDtypeStruct((T, D_MODEL), BF16),
            jax.ShapeDtypeStruct((T, LANES), F32),
            jax.ShapeDtypeStruct((SUBLANES, LANES), F32),
            jax.ShapeDtypeStruct((T // TOKEN_BLOCK, SUBLANES, LANES), F32),
        ],
        scratch_shapes=[pltpu.VMEM((SUBLANES, LANES), F32)],
        compiler_params=_params("arbitrary"),
        name="moe_router",
    )(x, g_all, rw_pad)


SLOT_BLOCK = 128
WINDOW_BLOCKS = 6
TOKEN_WINDOW = WINDOW_BLOCKS * TOKEN_BLOCK


def _dispatch_kernel(sb_ref, ws_ref, lo_ref, first_ref, n_ref, h_ref, dest_ref, o_ref):
    i = pl.program_id(0)

    @pl.when(i < n_ref[0])
    def _():
        slot = sb_ref[i] * SLOT_BLOCK + lax.broadcasted_iota(jnp.int32, (SLOT_BLOCK, 1), 0)
        token = ws_ref[i] + lax.broadcasted_iota(jnp.int32, (1, TOKEN_WINDOW), 1)
        dest = jnp.where(token >= lo_ref[i], dest_ref[...], -1)
        hit = (dest[0:1, :] == slot) | (dest[1:2, :] == slot)
        rows = jnp.dot(jnp.where(hit, 1.0, 0.0).astype(BF16), h_ref[...], preferred_element_type=F32)

        @pl.when(first_ref[i] == 1)
        def _():
            o_ref[...] = rows.astype(o_ref.dtype)

        @pl.when(first_ref[i] == 0)
        def _():
            o_ref[...] = (o_ref[...].astype(F32) + rows).astype(o_ref.dtype)


def _dispatch(h, dest_t, items, n_slots):
    def at(i, sb, ws, lo, fi, ni):
        return pl.multiple_of(ws[i], TOKEN_BLOCK)

    grid_spec = pltpu.PrefetchScalarGridSpec(
        num_scalar_prefetch=5,
        grid=(items[0].shape[0],),
        in_specs=[
            pl.BlockSpec((pl.Element(TOKEN_WINDOW), pl.Element(D_MODEL)), lambda *a: (at(*a), 0)),
            pl.BlockSpec((pl.Element(SUBLANES), pl.Element(TOKEN_WINDOW)), lambda *a: (0, at(*a))),
        ],
        out_specs=pl.BlockSpec((SLOT_BLOCK, D_MODEL), lambda i, sb, ws, lo, fi, ni: (sb[i], 0)),
    )
    return pl.pallas_call(
        _dispatch_kernel,
        grid_spec=grid_spec,
        out_shape=jax.ShapeDtypeStruct((n_slots, D_MODEL), BF16),
        compiler_params=_params("arbitrary"),
        name="moe_dispatch",
    )(*items, h, dest_t)


def _dispatch_items(cblk, counts, pstart, pend, block_expert, bm, n_slots, T):
    n_tb = T // TOKEN_BLOCK
    n_sb = n_slots // SLOT_BLOCK
    sb_start = jnp.arange(n_sb, dtype=jnp.int32) * SLOT_BLOCK
    e_sb = block_expert[sb_start // bm]
    r0 = sb_start - pstart[e_sb]
    c_sb = counts[e_sb]
    in_region = sb_start < pend[e_sb]
    has_tokens = r0 < c_sb
    r1 = jnp.minimum(r0 + SLOT_BLOCK, c_sb) - 1
    before = cblk[:, 0, :N_EXPERTS].astype(jnp.int32)[:, e_sb]
    tb_first = jnp.where(has_tokens, jnp.sum(before <= r0[None, :], axis=0) - 1, 0)
    tb_last = jnp.where(has_tokens, jnp.sum(before <= r1[None, :], axis=0) - 1, 0)
    n_win = jnp.where(in_region, (tb_last - tb_first) // WINDOW_BLOCKS + 1, 0)
    ends = jnp.cumsum(n_win)
    total = ends[-1]
    max_items = n_sb + N_EXPERTS * (-(-n_tb // WINDOW_BLOCKS) + 1)
    i = jnp.minimum(jnp.arange(max_items, dtype=jnp.int32), total - 1)
    sb = jnp.sum(ends[None, :] <= i[:, None], axis=1).astype(jnp.int32)
    k = i - (ends - n_win)[sb]
    lo = ((tb_first[sb] + k * WINDOW_BLOCKS) * TOKEN_BLOCK).astype(jnp.int32)
    start = jnp.minimum(lo, T - TOKEN_WINDOW)
    return sb, start, lo, (k == 0).astype(jnp.int32), total.astype(jnp.int32).reshape(1)


def _experts_kernel(be_ref, nu_ref, xs_ref, wg_ref, wu_ref, wd_ref, o_ref, acc_ref):
    b = pl.program_id(0)
    f = pl.program_id(1)
    used = b < nu_ref[0]

    @pl.when(used)
    def _():
        part = _swiglu_part(xs_ref[...], wg_ref[...], wu_ref[...], wd_ref[...])

        @pl.when(f == 0)
        def _():
            acc_ref[...] = part

        @pl.when(f > 0)
        def _():
            acc_ref[...] += part

        @pl.when(f == pl.num_programs(1) - 1)
        def _():
            o_ref[...] = acc_ref[...].astype(o_ref.dtype)

    @pl.when(jnp.logical_not(used) & (f == 0))
    def _():
        o_ref[...] = jnp.zeros_like(o_ref)


def _experts(xs, block_expert, n_used, wg, wu, wd, m, bm, tf):
    n_blocks = xs.shape[0] // bm
    nf = D_FF // tf

    def row(b, nu):
        return jnp.minimum(b, nu[0] - 1)

    def fidx(b, f, nu):
        return jnp.where(b < nu[0], f, nf - 1)

    grid_spec = pltpu.PrefetchScalarGridSpec(
        num_scalar_prefetch=2,
        grid=(n_blocks, nf),
        in_specs=[
            pl.BlockSpec((bm, D_MODEL), lambda b, f, be, nu: (row(b, nu), 0)),
            pl.BlockSpec((None, None, D_MODEL, tf), lambda b, f, be, nu: (m, be[b], 0, fidx(b, f, nu))),
            pl.BlockSpec((None, None, D_MODEL, tf), lambda b, f, be, nu: (m, be[b], 0, fidx(b, f, nu))),
            pl.BlockSpec((None, None, tf, D_MODEL), lambda b, f, be, nu: (m, be[b], fidx(b, f, nu), 0)),
        ],
        out_specs=pl.BlockSpec((bm, D_MODEL), lambda b, f, be, nu: (b, 0)),
        scratch_shapes=[pltpu.VMEM((bm, D_MODEL), F32)],
    )
    return pl.pallas_call(
        _experts_kernel,
        grid_spec=grid_spec,
        out_shape=jax.ShapeDtypeStruct((n_blocks * bm, D_MODEL), BF16),
        compiler_params=_params("arbitrary", "arbitrary"),
        name="moe_experts",
    )(block_expert, n_used, xs, wg, wu, wd)


BF16_ROWS = 16
WINDOW = TOKEN_BLOCK + BF16_ROWS


def _combine_kernel(win_ref, ps_ref, x_ref, meta_ref, *refs):
    ys_refs, o_ref = refs[:N_EXPERTS], refs[N_EXPERTS]
    blk = pl.program_id(0)
    meta = meta_ref[...]
    e1, e2 = meta[:, META_E1:META_E1 + 1], meta[:, META_E2:META_E2 + 1]
    g1, g2 = meta[:, META_G1:META_G1 + 1], meta[:, META_G2:META_G2 + 1]
    d1, d2 = meta[:, META_R1:META_R1 + 1], meta[:, META_R2:META_R2 + 1]
    for e in range(N_EXPERTS):
        start = ps_ref[e].astype(F32)
        d1 = d1 + jnp.where(e1 == e, start, 0.0)
        d2 = d2 + jnp.where(e2 == e, start, 0.0)
    lane = lax.broadcasted_iota(jnp.int32, (1, WINDOW), 1)
    acc = x_ref[...]
    for e in range(N_EXPERTS):
        slot = (win_ref[blk * N_EXPERTS + e] + lane).astype(F32)
        sel = jnp.where((d1 == slot) & (e1 == e), g1, 0.0) + jnp.where((d2 == slot) & (e2 == e), g2, 0.0)
        acc = acc + jnp.dot(sel.astype(BF16), ys_refs[e][...], preferred_element_type=F32)
    o_ref[...] = acc


def _combine(x, meta, ys, windows, pstart):
    T = x.shape[0]
    nb = T // TOKEN_BLOCK

    def window(e):
        return pl.BlockSpec((pl.Element(WINDOW), pl.Element(D_MODEL)),
                            lambda i, win, ps: (pl.multiple_of(win[i * N_EXPERTS + e], BF16_ROWS), 0))

    grid_spec = pltpu.PrefetchScalarGridSpec(
        num_scalar_prefetch=2,
        grid=(nb,),
        in_specs=[pl.BlockSpec((TOKEN_BLOCK, D_MODEL), lambda i, win, ps: (i, 0)),
                  pl.BlockSpec((TOKEN_BLOCK, LANES), lambda i, win, ps: (i, 0))]
                 + [window(e) for e in range(N_EXPERTS)],
        out_specs=pl.BlockSpec((TOKEN_BLOCK, D_MODEL), lambda i, win, ps: (i, 0)),
    )
    return pl.pallas_call(
        _combine_kernel,
        grid_spec=grid_spec,
        out_shape=jax.ShapeDtypeStruct((T, D_MODEL), F32),
        compiler_params=_params("arbitrary"),
        name="moe_combine",
    )(windows, pstart, x, meta, *([ys] * N_EXPERTS))


def _final_norm_kernel(x_ref, g_ref, o_ref):
    o_ref[...] = _rms(x_ref[...], g_ref[...])


def _final_norm(x, g, row0, n_rows, tm):
    r0 = row0 // tm
    return pl.pallas_call(
        _final_norm_kernel,
        grid=(n_rows // tm,),
        in_specs=[pl.BlockSpec((tm, D_MODEL), lambda i: (r0 + i, 0)),
                  pl.BlockSpec((1, D_MODEL), lambda i: (0, 0))],
        out_specs=pl.BlockSpec((tm, D_MODEL), lambda i: (i, 0)),
        out_shape=jax.ShapeDtypeStruct((n_rows, D_MODEL), F32),
        compiler_params=_params("parallel"),
        name="final_norm",
    )(x, g)


def _moe_layer(x, norm_ffn_g, layer, rw_pad, wg, wu, wd, m, tm, bm, tf):
    T = x.shape[0]
    h, meta, cnt, cblk = _router(x, norm_ffn_g, layer, rw_pad, m, tm)
    e = meta[:, META_E1:META_E2 + 1].astype(jnp.int32)
    rank = meta[:, META_R1:META_R2 + 1].astype(jnp.int32)
    counts = cnt[0, :N_EXPERTS].astype(jnp.int32)
    padded = (counts + bm - 1) // bm * bm
    pend = jnp.cumsum(padded)
    pstart = pend - padded
    dest = pstart[e] + rank
    n_blocks = (T * 2) // bm + N_EXPERTS + 1
    block_start = jnp.arange(n_blocks, dtype=jnp.int32) * bm
    block_expert = jnp.minimum(jnp.sum(pend[None, :] <= block_start[:, None], axis=1),
                               N_EXPERTS - 1).astype(jnp.int32)
    n_used = (pend[-1] // bm).astype(jnp.int32).reshape(1)
    dest_t = jnp.full((SUBLANES, T), -1, jnp.int32).at[0:2].set(dest.T)
    items = _dispatch_items(cblk, counts, pstart, pend, block_expert, bm, n_blocks * bm, T)
    xs = _dispatch(h, dest_t, items, n_blocks * bm)
    ys = _experts(xs, block_expert, n_used, wg, wu, wd, m, bm, tf)
    run_start = pstart[None, :] + cblk[:, 0, :N_EXPERTS].astype(jnp.int32)
    windows = (run_start // BF16_ROWS * BF16_ROWS).reshape(-1).astype(jnp.int32)
    return _combine(x, meta, ys, windows, pstart.astype(jnp.int32))


def _tril(w):
    n = w.shape[-1]
    return jnp.where(jnp.tril(jnp.ones((n, n), dtype=bool)), w, jnp.zeros((), w.dtype))


def _mix_tables(w_s, b_s, sample_len):
    reps = A_CHUNK // sample_len
    eye = jnp.eye(reps, dtype=w_s.dtype)
    w_prompt = _tril(w_s)
    w_small = _tril(w_s[:, :sample_len, :sample_len])
    w_sample = jax.vmap(lambda w: jnp.kron(eye, w))(w_small)
    b_prompt = b_s
    b_sample = jnp.tile(b_s[:, :sample_len], (1, reps))
    wmix = jnp.stack([w_prompt, w_sample])
    bias = jnp.stack([b_prompt, b_sample])[..., None]
    return wmix, jnp.broadcast_to(bias, bias.shape[:-1] + (LANES,))


def kernel(x_prompt, x_sample, state_hgrn, norm_mix_g, norm_ffn_g, final_norm_g, a_w_in, a_ln_g, a_ln_b, a_w_s, a_b_s, a_w_out, b_w_in, b_lb_logits, b_norm_g, b_w_out, ffn_w_gate, ffn_w_up, ffn_w_down, moe_router, moe_w_gate, moe_w_up, moe_w_down):
    n_p, L, d = x_prompt.shape
    n_s, l_s, _ = x_sample.shape
    assert d == D_MODEL and l_s == SAMPLE_LEN and L % A_CHUNK == 0
    T_p, T_s = n_p * L, n_s * l_s
    T = T_p + T_s
    depth = norm_mix_g.shape[0]

    tm_gate = 2 * A_CHUNK
    assert T_p % tm_gate == 0 and T_s % tm_gate == 0
    tm = next(t for t in (768, 512, 256, 128) if T % t == 0)
    tm_ffn = next(t for t in (1536, 768, 512, 256, 128) if T % t == 0)
    bm = 1024
    tf = 512
    lb_rows = min(L, 512)

    x = jnp.concatenate([x_prompt.reshape(T_p, d), x_sample.reshape(T_s, d)], axis=0)

    p = jax.nn.softmax(b_lb_logits.astype(F32), axis=0)
    lbs = (jnp.cumsum(p, axis=0) - p[0:1])[:, None, :]
    mix_g = norm_mix_g[:, None, :]
    ffn_g = norm_ffn_g[:, None, :]
    b_ng = b_norm_g[:, None, :]
    ln_g = a_ln_g[:, None, :]
    ln_b = a_ln_b[:, None, :]
    rw_pad = jnp.pad(moe_router, ((0, 0), (0, 0), (0, LANES - N_EXPERTS)))
    a_w_in, a_w_out, b_w_in, b_w_out, ffn_w_gate, ffn_w_up, ffn_w_down = (
        w.astype(BF16) for w in (a_w_in, a_w_out, b_w_in, b_w_out, ffn_w_gate, ffn_w_up, ffn_w_down))

    hg_prompt, v_sample = [], []
    hg_sample = state_hgrn
    for layer in range(depth):
        j = layer // 2
        if layer % 2 == 0:
            z = _norm_matmul(x, mix_g, layer, a_w_in, j, _gelu, BF16, tm, 1024)
            wmix, bias = _mix_tables(a_w_s[j], a_b_s[j], l_s)
            x, v = _gmlp_gate(z, x, ln_g, ln_b, wmix, bias, a_w_out, j, T_p // tm_gate, tm_gate)
            v_sample.append(v.reshape(n_s, l_s, A_HALF))
            x = _ffn_dense(x, ffn_g, layer, ffn_w_gate, ffn_w_up, ffn_w_down, j, tm_ffn, tf)
        else:
            proj = _norm_matmul(x, mix_g, layer, b_w_in, j, _identity, F32, tm, 1024)
            o_p, s_p = _hgrn_prompt(proj, lbs, b_ng, j, n_p, L, lb_rows)
            o_s, hg_sample = _hgrn_sample(proj, T_p, hg_sample, lbs, b_ng, j, n_s, 8)
            hg_prompt.append(s_p)
            o = jnp.concatenate([o_p, o_s], axis=0)
            x = _matmul_res(o, b_w_out, j, x, tm)
            x = _moe_layer(x, ffn_g, layer, rw_pad, moe_w_gate, moe_w_up, moe_w_down, j, 512, bm, tf)

    fg = final_norm_g[None, :]
    y_prompt = _final_norm(x, fg, 0, T_p, 512).reshape(n_p, L, d)
    y_sample = _final_norm(x, fg, T_p, T_s, T_s).reshape(n_s, l_s, d)
    return (y_prompt, y_sample, jnp.stack(hg_prompt), hg_sample, jnp.stack(v_sample))
```

```python
import functools
import math

import jax
import jax.numpy as jnp
from jax import lax
from jax.experimental import pallas as pl
from jax.experimental.pallas import tpu as pltpu

F32 = jnp.float32
BF16 = jnp.bfloat16

D_MODEL = 1024
A_CHUNK = 128
A_HALF = 3 * D_MODEL
A_GROUPS = 8
A_GROUP_DIM = A_HALF // A_GROUPS
HG_HEADS = 8
HG_DK = 128
HG_CHUNK = 64
HG_SUB = 16
FORGET_FLOOR = 1e-20
D_FF = 7 * D_MODEL // 2
N_EXPERTS = 8
NORM_EPS = 1e-6
LANES = 128
SUBLANES = 8
SAMPLE_LEN = 4
MASKED_LOG = -1e30

VMEM_LIMIT = 56 * 1024 * 1024


def _params(*sem):
    return pltpu.CompilerParams(dimension_semantics=sem, vmem_limit_bytes=VMEM_LIMIT)


def _rms(x, g):
    ms = jnp.mean(x * x, axis=-1, keepdims=True)
    return x * lax.rsqrt(ms + NORM_EPS) * g


def _gelu(y):
    return 0.5 * y * (1.0 + lax.erf(y * math.sqrt(0.5)))


def _identity(y):
    return y


def _norm_matmul_kernel(x_ref, g_ref, w_ref, o_ref, h_ref, *, act):
    @pl.when(pl.program_id(1) == 0)
    def _():
        h_ref[...] = _rms(x_ref[...], g_ref[...]).astype(BF16)

    y = jnp.dot(h_ref[...], w_ref[...].astype(BF16), preferred_element_type=F32)
    o_ref[...] = act(y).astype(o_ref.dtype)


def _norm_matmul(x, g_all, layer, w_all, w_layer, act, out_dtype, tm, tn):
    T = x.shape[0]
    N = w_all.shape[-1]
    return pl.pallas_call(
        functools.partial(_norm_matmul_kernel, act=act),
        grid=(T // tm, N // tn),
        in_specs=[
            pl.BlockSpec((tm, D_MODEL), lambda i, j: (i, 0)),
            pl.BlockSpec((None, 1, D_MODEL), lambda i, j: (layer, 0, 0)),
            pl.BlockSpec((None, D_MODEL, tn), lambda i, j: (w_layer, 0, j)),
        ],
        out_specs=pl.BlockSpec((tm, tn), lambda i, j: (i, j)),
        out_shape=jax.ShapeDtypeStruct((T, N), out_dtype),
        scratch_shapes=[pltpu.VMEM((tm, D_MODEL), BF16)],
        compiler_params=_params("parallel", "arbitrary"),
        name="norm_matmul",
    )(x, g_all, w_all)


def _gmlp_gate_kernel(z_ref, lng_ref, lnb_ref, wmix_ref, bias_ref, wout_ref, x_ref, o_ref, v_ref):
    tm = z_ref.shape[0]
    zv = z_ref[:, A_HALF:].astype(F32)
    mu = jnp.mean(zv, axis=-1, keepdims=True)
    xc = zv - mu
    rstd = lax.rsqrt(jnp.mean(xc * xc, axis=-1, keepdims=True) + NORM_EPS)
    v = xc * rstd * lng_ref[...] + lnb_ref[...]
    v_ref[...] = v
    vb = v.astype(BF16)
    groups = []
    for g in range(A_GROUPS):
        cols = slice(g * A_GROUP_DIM, (g + 1) * A_GROUP_DIM)
        wm = wmix_ref[g].astype(BF16)
        bias = jnp.concatenate([bias_ref[g]] * (A_GROUP_DIM // LANES), axis=1)
        rows = []
        for c in range(tm // A_CHUNK):
            sl = slice(c * A_CHUNK, (c + 1) * A_CHUNK)
            s = jnp.dot(wm, vb[sl, cols], preferred_element_type=F32) + bias
            rows.append((z_ref[sl, cols].astype(F32) * s).astype(BF16))
        groups.append(jnp.concatenate(rows, axis=0))
    gated = jnp.concatenate(groups, axis=1)
    o_ref[...] = x_ref[...] + jnp.dot(gated, wout_ref[...], preferred_element_type=F32)


def _gmlp_gate(z, x, ln_g, ln_b, wmix, bias, w_out, j, n_prompt_blocks, tm):
    T = x.shape[0]
    nb = T // tm

    def kind(i):
        return jnp.where(i >= n_prompt_blocks, 1, 0)

    return pl.pallas_call(
        _gmlp_gate_kernel,
        grid=(nb,),
        in_specs=[
            pl.BlockSpec((tm, 2 * A_HALF), lambda i: (i, 0)),
            pl.BlockSpec((None, 1, A_HALF), lambda i: (j, 0, 0)),
            pl.BlockSpec((None, 1, A_HALF), lambda i: (j, 0, 0)),
            pl.BlockSpec((None, A_GROUPS, A_CHUNK, A_CHUNK), lambda i: (kind(i), 0, 0, 0)),
            pl.BlockSpec((None, A_GROUPS, A_CHUNK, LANES), lambda i: (kind(i), 0, 0, 0)),
            pl.BlockSpec((None, A_HALF, D_MODEL), lambda i: (j, 0, 0)),
            pl.BlockSpec((tm, D_MODEL), lambda i: (i, 0)),
        ],
        out_specs=[
            pl.BlockSpec((tm, D_MODEL), lambda i: (i, 0)),
            pl.BlockSpec((tm, A_HALF), lambda i: (jnp.maximum(i - n_prompt_blocks, 0), 0)),
        ],
        out_shape=[
            jax.ShapeDtypeStruct((T, D_MODEL), F32),
            jax.ShapeDtypeStruct(((nb - n_prompt_blocks) * tm, A_HALF), F32),
        ],
        compiler_params=_params("arbitrary"),
        name="gmlp_gate",
    )(z, ln_g, ln_b, wmix, bias, w_out, x)


FF_SPLIT = 2


def _swiglu_acc(acc, h, wg, wu, wd):
    sub = wg.shape[1] // FF_SPLIT
    for c in range(FF_SPLIT):
        cols = slice(c * sub, (c + 1) * sub)
        a = jnp.dot(h, wg[:, cols].astype(BF16), preferred_element_type=F32)
        b = jnp.dot(h, wu[:, cols].astype(BF16), preferred_element_type=F32)
        m = (a * jax.nn.sigmoid(a) * b).astype(BF16)
        acc = acc + jnp.dot(m, wd[cols, :].astype(BF16), preferred_element_type=F32)
    return acc


def _ffn_kernel(x_ref, g_ref, wg_ref, wu_ref, wd_ref, o_ref, h_ref):
    @pl.when(pl.program_id(1) == 0)
    def _():
        x = x_ref[...]
        h_ref[...] = _rms(x, g_ref[...]).astype(BF16)
        o_ref[...] = x

    o_ref[...] = _swiglu_acc(o_ref[...], h_ref[...], wg_ref[...], wu_ref[...], wd_ref[...])


def _ffn_dense(x, g_all, layer, wg, wu, wd, m, tm, tf):
    T = x.shape[0]
    return pl.pallas_call(
        _ffn_kernel,
        grid=(T // tm, D_FF // tf),
        in_specs=[
            pl.BlockSpec((tm, D_MODEL), lambda i, f: (i, 0)),
            pl.BlockSpec((None, 1, D_MODEL), lambda i, f: (layer, 0, 0)),
            pl.BlockSpec((None, D_MODEL, tf), lambda i, f: (m, 0, f)),
            pl.BlockSpec((None, D_MODEL, tf), lambda i, f: (m, 0, f)),
            pl.BlockSpec((None, tf, D_MODEL), lambda i, f: (m, f, 0)),
        ],
        out_specs=pl.BlockSpec((tm, D_MODEL), lambda i, f: (i, 0)),
        out_shape=jax.ShapeDtypeStruct((T, D_MODEL), F32),
        scratch_shapes=[pltpu.VMEM((tm, D_MODEL), BF16)],
        compiler_params=_params("parallel", "arbitrary"),
        name="ffn_dense",
    )(x, g_all, wg, wu, wd)


def _matmul_res_kernel(a_ref, w_ref, x_ref, o_ref):
    o_ref[...] = x_ref[...] + jnp.dot(a_ref[...], w_ref[...], preferred_element_type=F32)


def _matmul_res(a, w_all, j, x, tm):
    T = x.shape[0]
    return pl.pallas_call(
        _matmul_res_kernel,
        grid=(T // tm,),
        in_specs=[
            pl.BlockSpec((tm, D_MODEL), lambda i: (i, 0)),
            pl.BlockSpec((None, D_MODEL, D_MODEL), lambda i: (j, 0, 0)),
            pl.BlockSpec((tm, D_MODEL), lambda i: (i, 0)),
        ],
        out_specs=pl.BlockSpec((tm, D_MODEL), lambda i: (i, 0)),
        out_shape=jax.ShapeDtypeStruct((T, D_MODEL), F32),
        compiler_params=_params("parallel"),
        name="matmul_res",
    )(a, w_all, x)


_NT = (((1,), (1,)), ((), ()))
_TN = (((0,), (0,)), ((), ()))


HG_GROUP = 8
MAX_BLOCK_DECAY = 75.0


def _hgrn_gates(qpre, fpre, lb):
    q = qpre * jax.nn.sigmoid(qpre)
    sig = jax.nn.sigmoid(fpre)
    f = lb + (1.0 - lb) * sig
    logf = jnp.log(jnp.maximum(f, FORGET_FLOOR))
    k = (1.0 - lb) * (1.0 - sig)
    return q, k, logf


def _cumsum_rows(x):
    C = x.shape[0]
    if C >= HG_SUB:
        r = lax.broadcasted_iota(jnp.int32, (C, C), 0)
        c = lax.broadcasted_iota(jnp.int32, (C, C), 1)
        tri = jnp.where(r >= c, 1.0, 0.0).astype(F32)
        return jnp.dot(tri, x, preferred_element_type=F32, precision=lax.Precision.HIGHEST)
    row = lax.broadcasted_iota(jnp.int32, (C, 1), 0)
    out = jnp.zeros_like(x)
    for s in range(C):
        out = out + jnp.where(row >= s, x[s:s + 1], 0.0)
    return out


def _chunk_prep(q, k, logf):
    C = q.shape[0]
    b = _cumsum_rows(logf)
    b_last = b[C - 1:C]
    qe = (q * jnp.exp(b)).astype(BF16)
    kd = (k * jnp.exp(b_last - b)).astype(BF16)
    return b, qe, kd, jnp.exp(b_last)


def _intra_exact(q, k, v, b, sub):
    C = q.shape[0]
    row = lax.broadcasted_iota(jnp.int32, (sub, 1), 0)
    parts = []
    for blk in range(C // sub):
        lo = blk * sub
        b_i, q_i, k_i, v_i = b[lo:lo + sub], q[lo:lo + sub], k[lo:lo + sub], v[lo:lo + sub]
        if blk == 0:
            o_i = jnp.zeros((sub, HG_DK), F32)
        else:
            ref = b[lo - 1:lo]
            qs = (q_i * jnp.exp(b_i - ref)).astype(BF16)
            ks = (k[:lo] * jnp.exp(ref - b[:lo])).astype(BF16)
            a = lax.dot_general(qs, ks, _NT, preferred_element_type=F32)
            o_i = jnp.dot(a.astype(BF16), v[:lo].astype(BF16), preferred_element_type=F32)
        for s in range(sub):
            rel = jnp.where(row >= s, b_i - b_i[s:s + 1], MASKED_LOG)
            a_col = jnp.sum(q_i * k_i[s:s + 1] * jnp.exp(rel), axis=-1, keepdims=True)
            o_i = o_i + a_col * v_i[s:s + 1]
        parts.append(o_i)
    return parts[0] if len(parts) == 1 else jnp.concatenate(parts, axis=0)


def _intra_factored(q, k, v, b, sub):
    C = q.shape[0]
    vb = v.astype(BF16)
    heads = [slice(h * HG_DK, (h + 1) * HG_DK) for h in range(q.shape[1] // HG_DK)]
    scores = []
    for blk in range(C // sub):
        lo, hi = blk * sub, (blk + 1) * sub
        if blk == 0:
            qs = q[:hi] * jnp.exp(b[:hi])
            ks = k[:hi] * jnp.exp(-b[:hi])
        else:
            ref = b[lo - 1:lo]
            qs = q[lo:hi] * jnp.exp(b[lo:hi] - ref)
            ks = k[:hi] * jnp.exp(ref - b[:hi])
        qs, ks = qs.astype(BF16), ks.astype(BF16)
        r = lax.broadcasted_iota(jnp.int32, (sub, hi), 0)
        c = lax.broadcasted_iota(jnp.int32, (sub, hi), 1)
        row = []
        for cols in heads:
            a = lax.dot_general(qs[:, cols], ks[:, cols], _NT, preferred_element_type=F32)
            row.append(jnp.where(c <= r + lo, a, 0.0).astype(BF16))
        scores.append(row)
    parts = []
    for blk, row in enumerate(scores):
        hi = (blk + 1) * sub
        outs = [jnp.dot(a, vb[:hi, cols], preferred_element_type=F32) for a, cols in zip(row, heads)]
        parts.append(outs[0] if len(outs) == 1 else jnp.concatenate(outs, axis=1))
    return jnp.concatenate(parts, axis=0)


def _head_norm_gate(o, ng, gate):
    return (o * lax.rsqrt(jnp.mean(o * o, axis=-1, keepdims=True) + NORM_EPS) * ng * gate)


def _hgrn_prompt_kernel(q_ref, f_ref, i_ref, g_ref, lb_ref, ng_ref, o_ref, sfin_ref,
                        st_ref, qs_ref, ks_ref, lf_ref, qe_ref, kd_ref, eb_ref, oi_ref):
    tb = pl.program_id(2)
    rows_total = q_ref.shape[0]
    n_chunks = rows_total // HG_CHUNK

    @pl.when(tb == 0)
    def _():
        st_ref[...] = jnp.zeros_like(st_ref)

    q, k, logf = _hgrn_gates(q_ref[...], f_ref[...], lb_ref[...])
    qs_ref[...] = q
    ks_ref[...] = k
    lf_ref[...] = logf
    block_decay = jnp.sum(logf.reshape(rows_total // HG_SUB, HG_SUB, logf.shape[-1]), axis=1)
    mild = jnp.min(block_decay) >= -MAX_BLOCK_DECAY

    heads = [slice(h * HG_DK, (h + 1) * HG_DK) for h in range(HG_GROUP)]

    def exact_group(qg, kg, vg, b, sub):
        return jnp.concatenate([_intra_exact(qg[:, c], kg[:, c], vg[:, c], b[:, c], sub) for c in heads], axis=1)

    def intra_pass(intra):
        def body(ci, carry):
            rows = pl.ds(pl.multiple_of(ci * HG_CHUNK, HG_CHUNK), HG_CHUNK)
            qg, kg, vg = qs_ref[rows, :], ks_ref[rows, :], i_ref[rows, :]
            b, qe, kd, eb_last = _chunk_prep(qg, kg, lf_ref[rows, :])
            qe_ref[rows, :] = qe
            kd_ref[rows, :] = kd
            eb_ref[pl.ds(pl.multiple_of(ci * SUBLANES, SUBLANES), SUBLANES), :] = \
                jnp.broadcast_to(eb_last, (SUBLANES, eb_last.shape[1]))
            oi_ref[rows, :] = intra(qg, kg, vg, b, HG_SUB)
            return carry
        lax.fori_loop(0, n_chunks, body, 0)

    @pl.when(mild)
    def _():
        intra_pass(_intra_factored)

    @pl.when(jnp.logical_not(mild))
    def _():
        intra_pass(exact_group)

    ng = ng_ref[...]

    def state_body(ci, carry):
        rows = pl.ds(pl.multiple_of(ci * HG_CHUNK, HG_CHUNK), HG_CHUNK)
        st = st_ref[...]
        stb = st.astype(BF16)
        qe, kd, oi = qe_ref[rows, :], kd_ref[rows, :], oi_ref[rows, :]
        vb = i_ref[rows, :].astype(BF16)
        eb_last = eb_ref[pl.ds(pl.multiple_of(ci * SUBLANES, SUBLANES), 1), :]
        outs, adds = [], []
        for c in heads:
            o = oi[:, c] + lax.dot_general(qe[:, c], stb[:, c], _NT, preferred_element_type=F32)
            outs.append(o * lax.rsqrt(jnp.mean(o * o, axis=-1, keepdims=True) + NORM_EPS))
            adds.append(lax.dot_general(vb[:, c], kd[:, c], _TN, preferred_element_type=F32))
        st_ref[...] = st * eb_last + jnp.concatenate(adds, axis=1)
        gate = jax.nn.sigmoid(g_ref[rows, :])
        o_ref[rows, :] = (jnp.concatenate(outs, axis=1) * ng * gate).astype(o_ref.dtype)
        return carry

    lax.fori_loop(0, n_chunks, state_body, 0)

    @pl.when(tb == pl.num_programs(2) - 1)
    def _():
        for h, c in enumerate(heads):
            sfin_ref[h] = st_ref[:, c].T


def _hgrn_prompt(proj, lbs, ng_all, j, n_seq, L, lb_rows):
    nb = L // lb_rows
    ng_groups = HG_HEADS // HG_GROUP
    gw = HG_GROUP * HG_DK

    def field(k):
        return pl.BlockSpec((lb_rows, gw), lambda n, h, t: (n * nb + t, k * ng_groups + h))

    vec = pl.BlockSpec((None, 1, gw), lambda n, h, t: (j, 0, h))
    return pl.pallas_call(
        _hgrn_prompt_kernel,
        grid=(n_seq, ng_groups, nb),
        in_specs=[field(0), field(1), field(2), field(3), vec, vec],
        out_specs=[
            pl.BlockSpec((lb_rows, gw), lambda n, h, t: (n * nb + t, h)),
            pl.BlockSpec((None, HG_GROUP, HG_DK, HG_DK), lambda n, h, t: (n, h, 0, 0)),
        ],
        out_shape=[
            jax.ShapeDtypeStruct((n_seq * L, D_MODEL), BF16),
            jax.ShapeDtypeStruct((n_seq, HG_HEADS, HG_DK, HG_DK), F32),
        ],
        scratch_shapes=[
            pltpu.VMEM((HG_DK, gw), F32),
            pltpu.VMEM((lb_rows, gw), F32),
            pltpu.VMEM((lb_rows, gw), F32),
            pltpu.VMEM((lb_rows, gw), F32),
            pltpu.VMEM((lb_rows, gw), BF16),
            pltpu.VMEM((lb_rows, gw), BF16),
            pltpu.VMEM((lb_rows // HG_CHUNK * SUBLANES, gw), F32),
            pltpu.VMEM((lb_rows, gw), F32),
        ],
        compiler_params=_params("parallel", "parallel", "arbitrary"),
        name="hgrn_prompt",
    )(proj, proj, proj, proj, lbs, ng_all)


def _hgrn_sample_kernel(q_ref, f_ref, i_ref, g_ref, lb_ref, ng_ref, s0_ref, o_ref, sfin_ref):
    n_seq = s0_ref.shape[0]
    q, k, logf = _hgrn_gates(q_ref[...], f_ref[...], lb_ref[...])
    v = i_ref[...]
    row = lax.broadcasted_iota(jnp.int32, (SUBLANES, 1), 0)
    first_half = row < SAMPLE_LEN
    prep = []
    for seq in range(n_seq):
        rows = slice(seq // 2 * SUBLANES, (seq // 2 + 1) * SUBLANES)
        mine = first_half if seq % 2 == 0 else jnp.logical_not(first_half)
        qm, km, vm, lm = (jnp.where(mine, t[rows], 0.0) for t in (q, k, v, logf))
        b, qe, kd, eb_last = _chunk_prep(qm, km, lm)
        prep.append((qe, kd, eb_last, vm.astype(BF16), _intra_exact(qm, km, vm, b, SUBLANES)))
    states = [s0_ref[seq].T for seq in range(n_seq)]
    outs = [oi + lax.dot_general(qe, st.astype(BF16), _NT, preferred_element_type=F32)
            for (qe, _, _, _, oi), st in zip(prep, states)]
    news = [st * eb_last + lax.dot_general(vb, kd, _TN, preferred_element_type=F32)
            for (_, kd, eb_last, vb, _), st in zip(prep, states)]
    sfin_ref[...] = jnp.stack([st.T for st in news])
    tiles = [jnp.where(first_half, outs[2 * p], outs[2 * p + 1]) for p in range(n_seq // 2)]
    gate = jax.nn.sigmoid(g_ref[...])
    o_ref[...] = _head_norm_gate(jnp.concatenate(tiles, axis=0), ng_ref[...], gate).astype(o_ref.dtype)


def _hgrn_sample(proj, row0, state_all, lbs, ng_all, j, n_seq, sb):
    H = HG_HEADS
    rb = sb * SAMPLE_LEN
    assert row0 % rb == 0 and sb % 2 == 0
    r0 = row0 // rb

    def field(k):
        return pl.BlockSpec((rb, HG_DK), lambda s, h: (r0 + s, k * H + h))

    vec = pl.BlockSpec((None, 1, HG_DK), lambda s, h: (j, 0, h))
    return pl.pallas_call(
        _hgrn_sample_kernel,
        grid=(n_seq // sb, H),
        in_specs=[field(0), field(1), field(2), field(3), vec, vec,
                  pl.BlockSpec((None, sb, None, HG_DK, HG_DK), lambda s, h: (j, s, h, 0, 0))],
        out_specs=[
            pl.BlockSpec((rb, HG_DK), lambda s, h: (s, h)),
            pl.BlockSpec((None, sb, None, HG_DK, HG_DK), lambda s, h: (j, s, h, 0, 0)),
        ],
        out_shape=[
            jax.ShapeDtypeStruct((n_seq * SAMPLE_LEN, D_MODEL), BF16),
            jax.ShapeDtypeStruct(state_all.shape, state_all.dtype),
        ],
        input_output_aliases={6: 1},
        compiler_params=_params("parallel", "parallel"),
        name="hgrn_sample",
    )(proj, proj, proj, proj, lbs, ng_all, state_all)


META_E1, META_E2, META_G1, META_G2, META_R1, META_R2 = range(6)
TOKEN_BLOCK = 128


def _router_kernel(x_ref, g_ref, rw_ref, h_ref, meta_ref, cnt_ref, cb_ref, carry_ref):
    i = pl.program_id(0)
    tm = x_ref.shape[0]

    @pl.when(i == 0)
    def _():
        carry_ref[...] = jnp.zeros_like(carry_ref)

    h = _rms(x_ref[...], g_ref[...])
    h_ref[...] = h.astype(h_ref.dtype)
    logits = jnp.dot(h.astype(BF16), rw_ref[...].astype(BF16), preferred_element_type=F32)
    lane = lax.broadcasted_iota(jnp.int32, (tm, LANES), 1)
    neg = -jnp.inf
    lg = jnp.where(lane < N_EXPERTS, logits, neg)
    m1 = jnp.max(lg, axis=-1, keepdims=True)
    e1 = jnp.min(jnp.where(lg == m1, lane, LANES), axis=-1, keepdims=True)
    lg2 = jnp.where(lane == e1, neg, lg)
    m2 = jnp.max(lg2, axis=-1, keepdims=True)
    e2 = jnp.min(jnp.where(lg2 == m2, lane, LANES), axis=-1, keepdims=True)
    ex = jnp.exp(m2 - m1)
    g1 = 1.0 / (1.0 + ex)
    g2 = ex / (1.0 + ex)

    onehot = jnp.where((lane == e1) | (lane == e2), 1.0, 0.0)
    r = lax.broadcasted_iota(jnp.int32, (tm, tm), 0)
    c = lax.broadcasted_iota(jnp.int32, (tm, tm), 1)
    before = jnp.where(r > c, 1.0, 0.0).astype(BF16)
    seen = jnp.dot(before, onehot.astype(BF16), preferred_element_type=F32) + carry_ref[0:1, :]
    r1 = jnp.sum(jnp.where(lane == e1, seen, 0.0), axis=-1, keepdims=True)
    r2 = jnp.sum(jnp.where(lane == e2, seen, 0.0), axis=-1, keepdims=True)
    for blk in range(tm // TOKEN_BLOCK):
        cb_ref[blk] = jnp.broadcast_to(seen[blk * TOKEN_BLOCK:blk * TOKEN_BLOCK + 1], (SUBLANES, LANES))
    total = carry_ref[0:1, :] + jnp.sum(onehot, axis=0, keepdims=True)
    carry_ref[...] = jnp.broadcast_to(total, carry_ref.shape)
    cnt_ref[...] = jnp.broadcast_to(total, cnt_ref.shape)

    meta = jnp.zeros((tm, LANES), F32)
    for idx, val in ((META_E1, e1.astype(F32)), (META_E2, e2.astype(F32)), (META_G1, g1),
                     (META_G2, g2), (META_R1, r1), (META_R2, r2)):
        meta = jnp.where(lane == idx, val, meta)
    meta_ref[...] = meta


def _router(x, g_all, layer, rw_pad, m, tm):
    T = x.shape[0]
    return pl.pallas_call(
        _router_kernel,
        grid=(T // tm,),
        in_specs=[
            pl.BlockSpec((tm, D_MODEL), lambda i: (i, 0)),
            pl.BlockSpec((None, 1, D_MODEL), lambda i: (layer, 0, 0)),
            pl.BlockSpec((None, D_MODEL, LANES), lambda i: (m, 0, 0)),
        ],
        out_specs=[
            pl.BlockSpec((tm, D_MODEL), lambda i: (i, 0)),
            pl.BlockSpec((tm, LANES), lambda i: (i, 0)),
            pl.BlockSpec((SUBLANES, LANES), lambda i: (0, 0)),
            pl.BlockSpec((tm // TOKEN_BLOCK, SUBLANES, LANES), lambda i: (i, 0, 0)),
        ],
        out_shape=[
            jax.ShapeDtypeStruct((T, D_MODEL), BF16),
            jax.ShapeDtypeStruct((T, LANES), F32),
            jax.ShapeDtypeStruct((SUBLANES, LANES), F32),
            jax.ShapeDtypeStruct((T // TOKEN_BLOCK, SUBLANES, LANES), F32),
        ],
        scratch_shapes=[pltpu.VMEM((SUBLANES, LANES), F32)],
        compiler_params=_params("arbitrary"),
        name="moe_router",
    )(x, g_all, rw_pad)


SLOT_BLOCK = 128
WINDOW_BLOCKS = 6
TOKEN_WINDOW = WINDOW_BLOCKS * TOKEN_BLOCK


def _dispatch_kernel(sb_ref, ws_ref, lo_ref, first_ref, n_ref, h_ref, dest_ref, o_ref):
    i = pl.program_id(0)

    @pl.when(i < n_ref[0])
    def _():
        slot = sb_ref[i] * SLOT_BLOCK + lax.broadcasted_iota(jnp.int32, (SLOT_BLOCK, 1), 0)
        token = ws_ref[i] + lax.broadcasted_iota(jnp.int32, (1, TOKEN_WINDOW), 1)
        dest = jnp.where(token >= lo_ref[i], dest_ref[...], -1)
        hit = (dest[0:1, :] == slot) | (dest[1:2, :] == slot)
        rows = jnp.dot(jnp.where(hit, 1.0, 0.0).astype(BF16), h_ref[...], preferred_element_type=F32)

        @pl.when(first_ref[i] == 1)
        def _():
            o_ref[...] = rows.astype(o_ref.dtype)

        @pl.when(first_ref[i] == 0)
        def _():
            o_ref[...] = (o_ref[...].astype(F32) + rows).astype(o_ref.dtype)


def _dispatch(h, dest_t, items, n_slots):
    def at(i, sb, ws, lo, fi, ni):
        return pl.multiple_of(ws[i], TOKEN_BLOCK)

    grid_spec = pltpu.PrefetchScalarGridSpec(
        num_scalar_prefetch=5,
        grid=(items[0].shape[0],),
        in_specs=[
            pl.BlockSpec((pl.Element(TOKEN_WINDOW), pl.Element(D_MODEL)), lambda *a: (at(*a), 0)),
            pl.BlockSpec((pl.Element(SUBLANES), pl.Element(TOKEN_WINDOW)), lambda *a: (0, at(*a))),
        ],
        out_specs=pl.BlockSpec((SLOT_BLOCK, D_MODEL), lambda i, sb, ws, lo, fi, ni: (sb[i], 0)),
    )
    return pl.pallas_call(
        _dispatch_kernel,
        grid_spec=grid_spec,
        out_shape=jax.ShapeDtypeStruct((n_slots, D_MODEL), BF16),
        compiler_params=_params("arbitrary"),
        name="moe_dispatch",
    )(*items, h, dest_t)


def _dispatch_items(cblk, counts, pstart, pend, block_expert, bm, n_slots, T):
    n_tb = T // TOKEN_BLOCK
    n_sb = n_slots // SLOT_BLOCK
    sb_start = jnp.arange(n_sb, dtype=jnp.int32) * SLOT_BLOCK
    e_sb = block_expert[sb_start // bm]
    r0 = sb_start - pstart[e_sb]
    c_sb = counts[e_sb]
    in_region = sb_start < pend[e_sb]
    has_tokens = r0 < c_sb
    r1 = jnp.minimum(r0 + SLOT_BLOCK, c_sb) - 1
    before = cblk[:, 0, :N_EXPERTS].astype(jnp.int32)[:, e_sb]
    tb_first = jnp.where(has_tokens, jnp.sum(before <= r0[None, :], axis=0) - 1, 0)
    tb_last = jnp.where(has_tokens, jnp.sum(before <= r1[None, :], axis=0) - 1, 0)
    n_win = jnp.where(in_region, (tb_last - tb_first) // WINDOW_BLOCKS + 1, 0)
    ends = jnp.cumsum(n_win)
    total = ends[-1]
    max_items = n_sb + N_EXPERTS * (-(-n_tb // WINDOW_BLOCKS) + 1)
    i = jnp.minimum(jnp.arange(max_items, dtype=jnp.int32), total - 1)
    sb = jnp.sum(ends[None, :] <= i[:, None], axis=1).astype(jnp.int32)
    k = i - (ends - n_win)[sb]
    lo = ((tb_first[sb] + k * WINDOW_BLOCKS) * TOKEN_BLOCK).astype(jnp.int32)
    start = jnp.minimum(lo, T - TOKEN_WINDOW)
    return sb, start, lo, (k == 0).astype(jnp.int32), total.astype(jnp.int32).reshape(1)


def _experts_kernel(be_ref, nu_ref, xs_ref, wg_ref, wu_ref, wd_ref, o_ref, acc_ref):
    b = pl.program_id(0)
    f = pl.program_id(1)
    used = b < nu_ref[0]

    @pl.when(used & (f == 0))
    def _():
        acc_ref[...] = jnp.zeros_like(acc_ref)

    @pl.when(used)
    def _():
        acc_ref[...] = _swiglu_acc(acc_ref[...], xs_ref[...], wg_ref[...], wu_ref[...], wd_ref[...])

        @pl.when(f == pl.num_programs(1) - 1)
        def _():
            o_ref[...] = acc_ref[...].astype(o_ref.dtype)

    @pl.when(jnp.logical_not(used) & (f == 0))
    def _():
        o_ref[...] = jnp.zeros_like(o_ref)


def _experts(xs, block_expert, n_used, wg, wu, wd, m, bm, tf):
    n_blocks = xs.shape[0] // bm
    nf = D_FF // tf

    def row(b, nu):
        return jnp.minimum(b, nu[0] - 1)

    def fidx(b, f, nu):
        return jnp.where(b < nu[0], f, nf - 1)

    grid_spec = pltpu.PrefetchScalarGridSpec(
        num_scalar_prefetch=2,
        grid=(n_blocks, nf),
        in_specs=[
            pl.BlockSpec((bm, D_MODEL), lambda b, f, be, nu: (row(b, nu), 0)),
            pl.BlockSpec((None, None, D_MODEL, tf), lambda b, f, be, nu: (m, be[b], 0, fidx(b, f, nu))),
            pl.BlockSpec((None, None, D_MODEL, tf), lambda b, f, be, nu: (m, be[b], 0, fidx(b, f, nu))),
            pl.BlockSpec((None, None, tf, D_MODEL), lambda b, f, be, nu: (m, be[b], fidx(b, f, nu), 0)),
        ],
        out_specs=pl.BlockSpec((bm, D_MODEL), lambda b, f, be, nu: (b, 0)),
        scratch_shapes=[pltpu.VMEM((bm, D_MODEL), F32)],
    )
    return pl.pallas_call(
        _experts_kernel,
        grid_spec=grid_spec,
        out_shape=jax.ShapeDtypeStruct((n_blocks * bm, D_MODEL), BF16),
        compiler_params=_params("arbitrary", "arbitrary"),
        name="moe_experts",
    )(block_expert, n_used, xs, wg, wu, wd)


BF16_ROWS = 16
WINDOW = TOKEN_BLOCK + BF16_ROWS


def _combine_kernel(win_ref, ps_ref, x_ref, meta_ref, *refs):
    ys_refs, o_ref = refs[:N_EXPERTS], refs[N_EXPERTS]
    blk = pl.program_id(0)
    meta = meta_ref[...]
    e1, e2 = meta[:, META_E1:META_E1 + 1], meta[:, META_E2:META_E2 + 1]
    g1, g2 = meta[:, META_G1:META_G1 + 1], meta[:, META_G2:META_G2 + 1]
    d1, d2 = meta[:, META_R1:META_R1 + 1], meta[:, META_R2:META_R2 + 1]
    for e in range(N_EXPERTS):
        start = ps_ref[e].astype(F32)
        d1 = d1 + jnp.where(e1 == e, start, 0.0)
        d2 = d2 + jnp.where(e2 == e, start, 0.0)
    lane = lax.broadcasted_iota(jnp.int32, (1, WINDOW), 1)
    acc = x_ref[...]
    for e in range(N_EXPERTS):
        slot = (win_ref[blk * N_EXPERTS + e] + lane).astype(F32)
        sel = jnp.where((d1 == slot) & (e1 == e), g1, 0.0) + jnp.where((d2 == slot) & (e2 == e), g2, 0.0)
        acc = acc + jnp.dot(sel.astype(BF16), ys_refs[e][...], preferred_element_type=F32)
    o_ref[...] = acc


def _combine(x, meta, ys, windows, pstart):
    T = x.shape[0]
    nb = T // TOKEN_BLOCK

    def window(e):
        return pl.BlockSpec((pl.Element(WINDOW), pl.Element(D_MODEL)),
                            lambda i, win, ps: (pl.multiple_of(win[i * N_EXPERTS + e], BF16_ROWS), 0))

    grid_spec = pltpu.PrefetchScalarGridSpec(
        num_scalar_prefetch=2,
        grid=(nb,),
        in_specs=[pl.BlockSpec((TOKEN_BLOCK, D_MODEL), lambda i, win, ps: (i, 0)),
                  pl.BlockSpec((TOKEN_BLOCK, LANES), lambda i, win, ps: (i, 0))]
                 + [window(e) for e in range(N_EXPERTS)],
        out_specs=pl.BlockSpec((TOKEN_BLOCK, D_MODEL), lambda i, win, ps: (i, 0)),
    )
    return pl.pallas_call(
        _combine_kernel,
        grid_spec=grid_spec,
        out_shape=jax.ShapeDtypeStruct((T, D_MODEL), F32),
        compiler_params=_params("arbitrary"),
        name="moe_combine",
    )(windows, pstart, x, meta, *([ys] * N_EXPERTS))


def _final_norm_kernel(x_ref, g_ref, o_ref):
    o_ref[...] = _rms(x_ref[...], g_ref[...])


def _final_norm(x, g, row0, n_rows, tm):
    r0 = row0 // tm
    return pl.pallas_call(
        _final_norm_kernel,
        grid=(n_rows // tm,),
        in_specs=[pl.BlockSpec((tm, D_MODEL), lambda i: (r0 + i, 0)),
                  pl.BlockSpec((1, D_MODEL), lambda i: (0, 0))],
        out_specs=pl.BlockSpec((tm, D_MODEL), lambda i: (i, 0)),
        out_shape=jax.ShapeDtypeStruct((n_rows, D_MODEL), F32),
        compiler_params=_params("parallel"),
        name="final_norm",
    )(x, g)


def _moe_layer(x, norm_ffn_g, layer, rw_pad, wg, wu, wd, m, tm, bm, tf):
    T = x.shape[0]
    h, meta, cnt, cblk = _router(x, norm_ffn_g, layer, rw_pad, m, tm)
    e = meta[:, META_E1:META_E2 + 1].astype(jnp.int32)
    rank = meta[:, META_R1:META_R2 + 1].astype(jnp.int32)
    counts = cnt[0, :N_EXPERTS].astype(jnp.int32)
    padded = (counts + bm - 1) // bm * bm
    pend = jnp.cumsum(padded)
    pstart = pend - padded
    dest = pstart[e] + rank
    n_blocks = (T * 2) // bm + N_EXPERTS + 1
    block_start = jnp.arange(n_blocks, dtype=jnp.int32) * bm
    block_expert = jnp.minimum(jnp.sum(pend[None, :] <= block_start[:, None], axis=1),
                               N_EXPERTS - 1).astype(jnp.int32)
    n_used = (pend[-1] // bm).astype(jnp.int32).reshape(1)
    dest_t = jnp.full((SUBLANES, T), -1, jnp.int32).at[0:2].set(dest.T)
    items = _dispatch_items(cblk, counts, pstart, pend, block_expert, bm, n_blocks * bm, T)
    xs = _dispatch(h, dest_t, items, n_blocks * bm)
    ys = _experts(xs, block_expert, n_used, wg, wu, wd, m, bm, tf)
    run_start = pstart[None, :] + cblk[:, 0, :N_EXPERTS].astype(jnp.int32)
    windows = (run_start // BF16_ROWS * BF16_ROWS).reshape(-1).astype(jnp.int32)
    return _combine(x, meta, ys, windows, pstart.astype(jnp.int32))


def _tril(w):
    n = w.shape[-1]
    return jnp.where(jnp.tril(jnp.ones((n, n), dtype=bool)), w, jnp.zeros((), w.dtype))


def _mix_tables(w_s, b_s, sample_len):
    reps = A_CHUNK // sample_len
    eye = jnp.eye(reps, dtype=w_s.dtype)
    w_prompt = _tril(w_s)
    w_small = _tril(w_s[:, :sample_len, :sample_len])
    w_sample = jax.vmap(lambda w: jnp.kron(eye, w))(w_small)
    b_prompt = b_s
    b_sample = jnp.tile(b_s[:, :sample_len], (1, reps))
    wmix = jnp.stack([w_prompt, w_sample])
    bias = jnp.stack([b_prompt, b_sample])[..., None]
    return wmix, jnp.broadcast_to(bias, bias.shape[:-1] + (LANES,))


def kernel(x_prompt, x_sample, state_hgrn, norm_mix_g, norm_ffn_g, final_norm_g, a_w_in, a_ln_g, a_ln_b, a_w_s, a_b_s, a_w_out, b_w_in, b_lb_logits, b_norm_g, b_w_out, ffn_w_gate, ffn_w_up, ffn_w_down, moe_router, moe_w_gate, moe_w_up, moe_w_down):
    n_p, L, d = x_prompt.shape
    n_s, l_s, _ = x_sample.shape
    assert d == D_MODEL and l_s == SAMPLE_LEN and L % A_CHUNK == 0
    T_p, T_s = n_p * L, n_s * l_s
    T = T_p + T_s
    depth = norm_mix_g.shape[0]

    tm_gate = 2 * A_CHUNK
    assert T_p % tm_gate == 0 and T_s % tm_gate == 0
    tm = next(t for t in (768, 512, 256, 128) if T % t == 0)
    tm_ffn = next(t for t in (1536, 768, 512, 256, 128) if T % t == 0)
    bm = 1024
    tf = 512
    lb_rows = min(L, 512)

    x = jnp.concatenate([x_prompt.reshape(T_p, d), x_sample.reshape(T_s, d)], axis=0)

    p = jax.nn.softmax(b_lb_logits.astype(F32), axis=0)
    lbs = (jnp.cumsum(p, axis=0) - p[0:1])[:, None, :]
    mix_g = norm_mix_g[:, None, :]
    ffn_g = norm_ffn_g[:, None, :]
    b_ng = b_norm_g[:, None, :]
    ln_g = a_ln_g[:, None, :]
    ln_b = a_ln_b[:, None, :]
    rw_pad = jnp.pad(moe_router, ((0, 0), (0, 0), (0, LANES - N_EXPERTS)))
    a_w_in, a_w_out, b_w_in, b_w_out, ffn_w_gate, ffn_w_up, ffn_w_down = (
        w.astype(BF16) for w in (a_w_in, a_w_out, b_w_in, b_w_out, ffn_w_gate, ffn_w_up, ffn_w_down))

    hg_prompt, v_sample = [], []
    hg_sample = state_hgrn
    for layer in range(depth):
        j = layer // 2
        if layer % 2 == 0:
            z = _norm_matmul(x, mix_g, layer, a_w_in, j, _gelu, BF16, tm, 1024)
            wmix, bias = _mix_tables(a_w_s[j], a_b_s[j], l_s)
            x, v = _gmlp_gate(z, x, ln_g, ln_b, wmix, bias, a_w_out, j, T_p // tm_gate, tm_gate)
            v_sample.append(v.reshape(n_s, l_s, A_HALF))
            x = _ffn_dense(x, ffn_g, layer, ffn_w_gate, ffn_w_up, ffn_w_down, j, tm_ffn, tf)
        else:
            proj = _norm_matmul(x, mix_g, layer, b_w_in, j, _identity, F32, tm, 1024)
            o_p, s_p = _hgrn_prompt(proj, lbs, b_ng, j, n_p, L, lb_rows)
            o_s, hg_sample = _hgrn_sample(proj, T_p, hg_sample, lbs, b_ng, j, n_s, 8)
            hg_prompt.append(s_p)
            o = jnp.concatenate([o_p, o_s], axis=0)
            x = _matmul_res(o, b_w_out, j, x, tm)
            x = _moe_layer(x, ffn_g, layer, rw_pad, moe_w_gate, moe_w_up, moe_w_down, j, 512, bm, tf)

    fg = final_norm_g[None, :]
    y_prompt = _final_norm(x, fg, 0, T_p, 512).reshape(n_p, L, d)
    y_sample = _final_norm(x, fg, T_p, T_s, T_s).reshape(n_s, l_s, d)
    return (y_prompt, y_sample, jnp.stack(hg_prompt), hg_sample, jnp.stack(v_sample))
```

```python
import functools
import math

import jax
import jax.numpy as jnp
from jax import lax
from jax.experimental import pallas as pl
from jax.experimental.pallas import tpu as pltpu

F32 = jnp.float32
BF16 = jnp.bfloat16

D_MODEL = 1024
A_CHUNK = 128
A_HALF = 3 * D_MODEL
A_GROUPS = 8
A_GROUP_DIM = A_HALF // A_GROUPS
HG_HEADS = 8
HG_DK = 128
HG_CHUNK = 64
HG_SUB = 16
FORGET_FLOOR = 1e-20
D_FF = 7 * D_MODEL // 2
N_EXPERTS = 8
NORM_EPS = 1e-6
LANES = 128
SUBLANES = 8
SAMPLE_LEN = 4
MASKED_LOG = -1e30

VMEM_LIMIT = 56 * 1024 * 1024


def _params(*sem):
    return pltpu.CompilerParams(dimension_semantics=sem, vmem_limit_bytes=VMEM_LIMIT)


def _rms(x, g):
    ms = jnp.mean(x * x, axis=-1, keepdims=True)
    return x * lax.rsqrt(ms + NORM_EPS) * g


def _gelu(y):
    return 0.5 * y * (1.0 + lax.erf(y * math.sqrt(0.5)))


def _identity(y):
    return y


def _norm_matmul_kernel(x_ref, g_ref, w_ref, o_ref, h_ref, *, act):
    @pl.when(pl.program_id(1) == 0)
    def _():
        h_ref[...] = _rms(x_ref[...], g_ref[...]).astype(BF16)

    y = jnp.dot(h_ref[...], w_ref[...].astype(BF16), preferred_element_type=F32)
    o_ref[...] = act(y).astype(o_ref.dtype)


def _norm_matmul(x, g_all, layer, w_all, w_layer, act, out_dtype, tm, tn):
    T = x.shape[0]
    N = w_all.shape[-1]
    return pl.pallas_call(
        functools.partial(_norm_matmul_kernel, act=act),
        grid=(T // tm, N // tn),
        in_specs=[
            pl.BlockSpec((tm, D_MODEL), lambda i, j: (i, 0)),
            pl.BlockSpec((None, 1, D_MODEL), lambda i, j: (layer, 0, 0)),
            pl.BlockSpec((None, D_MODEL, tn), lambda i, j: (w_layer, 0, j)),
        ],
        out_specs=pl.BlockSpec((tm, tn), lambda i, j: (i, j)),
        out_shape=jax.ShapeDtypeStruct((T, N), out_dtype),
        scratch_shapes=[pltpu.VMEM((tm, D_MODEL), BF16)],
        compiler_params=_params("parallel", "arbitrary"),
        name="norm_matmul",
    )(x, g_all, w_all)


def _gmlp_gate_kernel(z_ref, lng_ref, lnb_ref, wmix_ref, bias_ref, wout_ref, x_ref, o_ref, v_ref):
    tm = z_ref.shape[0]
    zv = z_ref[:, A_HALF:].astype(F32)
    mu = jnp.mean(zv, axis=-1, keepdims=True)
    xc = zv - mu
    rstd = lax.rsqrt(jnp.mean(xc * xc, axis=-1, keepdims=True) + NORM_EPS)
    v = xc * rstd * lng_ref[...] + lnb_ref[...]
    v_ref[...] = v
    vb = v.astype(BF16)
    groups = []
    for g in range(A_GROUPS):
        cols = slice(g * A_GROUP_DIM, (g + 1) * A_GROUP_DIM)
        wm = wmix_ref[g].astype(BF16)
        bias = jnp.concatenate([bias_ref[g]] * (A_GROUP_DIM // LANES), axis=1)
        rows = []
        for c in range(tm // A_CHUNK):
            sl = slice(c * A_CHUNK, (c + 1) * A_CHUNK)
            s = jnp.dot(wm, vb[sl, cols], preferred_element_type=F32) + bias
            rows.append((z_ref[sl, cols].astype(F32) * s).astype(BF16))
        groups.append(jnp.concatenate(rows, axis=0))
    gated = jnp.concatenate(groups, axis=1)
    o_ref[...] = x_ref[...] + jnp.dot(gated, wout_ref[...], preferred_element_type=F32)


def _gmlp_gate(z, x, ln_g, ln_b, wmix, bias, w_out, j, n_prompt_blocks, tm):
    T = x.shape[0]
    nb = T // tm

    def kind(i):
        return jnp.where(i >= n_prompt_blocks, 1, 0)

    return pl.pallas_call(
        _gmlp_gate_kernel,
        grid=(nb,),
        in_specs=[
            pl.BlockSpec((tm, 2 * A_HALF), lambda i: (i, 0)),
            pl.BlockSpec((None, 1, A_HALF), lambda i: (j, 0, 0)),
            pl.BlockSpec((None, 1, A_HALF), lambda i: (j, 0, 0)),
            pl.BlockSpec((None, A_GROUPS, A_CHUNK, A_CHUNK), lambda i: (kind(i), 0, 0, 0)),
            pl.BlockSpec((None, A_GROUPS, A_CHUNK, LANES), lambda i: (kind(i), 0, 0, 0)),
            pl.BlockSpec((None, A_HALF, D_MODEL), lambda i: (j, 0, 0)),
            pl.BlockSpec((tm, D_MODEL), lambda i: (i, 0)),
        ],
        out_specs=[
            pl.BlockSpec((tm, D_MODEL), lambda i: (i, 0)),
            pl.BlockSpec((tm, A_HALF), lambda i: (jnp.maximum(i - n_prompt_blocks, 0), 0)),
        ],
        out_shape=[
            jax.ShapeDtypeStruct((T, D_MODEL), F32),
            jax.ShapeDtypeStruct(((nb - n_prompt_blocks) * tm, A_HALF), F32),
        ],
        compiler_params=_params("arbitrary"),
        name="gmlp_gate",
    )(z, ln_g, ln_b, wmix, bias, w_out, x)


FF_SPLIT = 2


def _swiglu_acc(acc, h, wg, wu, wd):
    sub = wg.shape[1] // FF_SPLIT
    for c in range(FF_SPLIT):
        cols = slice(c * sub, (c + 1) * sub)
        a = jnp.dot(h, wg[:, cols].astype(BF16), preferred_element_type=F32)
        b = jnp.dot(h, wu[:, cols].astype(BF16), preferred_element_type=F32)
        m = (a * jax.nn.sigmoid(a) * b).astype(BF16)
        acc = acc + jnp.dot(m, wd[cols, :].astype(BF16), preferred_element_type=F32)
    return acc


def _ffn_kernel(x_ref, g_ref, wg_ref, wu_ref, wd_ref, o_ref, h_ref):
    @pl.when(pl.program_id(1) == 0)
    def _():
        x = x_ref[...]
        h_ref[...] = _rms(x, g_ref[...]).astype(BF16)
        o_ref[...] = x

    o_ref[...] = _swiglu_acc(o_ref[...], h_ref[...], wg_ref[...], wu_ref[...], wd_ref[...])


def _ffn_dense(x, g_all, layer, wg, wu, wd, m, tm, tf):
    T = x.shape[0]
    return pl.pallas_call(
        _ffn_kernel,
        grid=(T // tm, D_FF // tf),
        in_specs=[
            pl.BlockSpec((tm, D_MODEL), lambda i, f: (i, 0)),
            pl.BlockSpec((None, 1, D_MODEL), lambda i, f: (layer, 0, 0)),
            pl.BlockSpec((None, D_MODEL, tf), lambda i, f: (m, 0, f)),
            pl.BlockSpec((None, D_MODEL, tf), lambda i, f: (m, 0, f)),
            pl.BlockSpec((None, tf, D_MODEL), lambda i, f: (m, f, 0)),
        ],
        out_specs=pl.BlockSpec((tm, D_MODEL), lambda i, f: (i, 0)),
        out_shape=jax.ShapeDtypeStruct((T, D_MODEL), F32),
        scratch_shapes=[pltpu.VMEM((tm, D_MODEL), BF16)],
        compiler_params=_params("parallel", "arbitrary"),
        name="ffn_dense",
    )(x, g_all, wg, wu, wd)


def _matmul_res_kernel(a_ref, w_ref, x_ref, o_ref):
    o_ref[...] = x_ref[...] + jnp.dot(a_ref[...], w_ref[...], preferred_element_type=F32)


def _matmul_res(a, w_all, j, x, tm):
    T = x.shape[0]
    return pl.pallas_call(
        _matmul_res_kernel,
        grid=(T // tm,),
        in_specs=[
            pl.BlockSpec((tm, D_MODEL), lambda i: (i, 0)),
            pl.BlockSpec((None, D_MODEL, D_MODEL), lambda i: (j, 0, 0)),
            pl.BlockSpec((tm, D_MODEL), lambda i: (i, 0)),
        ],
        out_specs=pl.BlockSpec((tm, D_MODEL), lambda i: (i, 0)),
        out_shape=jax.ShapeDtypeStruct((T, D_MODEL), F32),
        compiler_params=_params("parallel"),
        name="matmul_res",
    )(a, w_all, x)


_NT = (((1,), (1,)), ((), ()))
_TN = (((0,), (0,)), ((), ()))


HG_GROUP = 8
MAX_BLOCK_DECAY = 75.0


def _hgrn_gates(qpre, fpre, lb):
    q = qpre * jax.nn.sigmoid(qpre)
    sig = jax.nn.sigmoid(fpre)
    f = lb + (1.0 - lb) * sig
    logf = jnp.log(jnp.maximum(f, FORGET_FLOOR))
    k = (1.0 - lb) * (1.0 - sig)
    return q, k, logf


def _cumsum_rows(x):
    C = x.shape[0]
    if C >= HG_SUB:
        r = lax.broadcasted_iota(jnp.int32, (C, C), 0)
        c = lax.broadcasted_iota(jnp.int32, (C, C), 1)
        tri = jnp.where(r >= c, 1.0, 0.0).astype(F32)
        return jnp.dot(tri, x, preferred_element_type=F32, precision=lax.Precision.HIGHEST)
    row = lax.broadcasted_iota(jnp.int32, (C, 1), 0)
    out = jnp.zeros_like(x)
    for s in range(C):
        out = out + jnp.where(row >= s, x[s:s + 1], 0.0)
    return out


def _chunk_prep(q, k, logf):
    C = q.shape[0]
    b = _cumsum_rows(logf)
    b_last = b[C - 1:C]
    qe = (q * jnp.exp(b)).astype(BF16)
    kd = (k * jnp.exp(b_last - b)).astype(BF16)
    return b, qe, kd, jnp.exp(b_last)


def _intra_exact(q, k, v, b, sub):
    C = q.shape[0]
    row = lax.broadcasted_iota(jnp.int32, (sub, 1), 0)
    parts = []
    for blk in range(C // sub):
        lo = blk * sub
        b_i, q_i, k_i, v_i = b[lo:lo + sub], q[lo:lo + sub], k[lo:lo + sub], v[lo:lo + sub]
        if blk == 0:
            o_i = jnp.zeros((sub, HG_DK), F32)
        else:
            ref = b[lo - 1:lo]
            qs = (q_i * jnp.exp(b_i - ref)).astype(BF16)
            ks = (k[:lo] * jnp.exp(ref - b[:lo])).astype(BF16)
            a = lax.dot_general(qs, ks, _NT, preferred_element_type=F32)
            o_i = jnp.dot(a.astype(BF16), v[:lo].astype(BF16), preferred_element_type=F32)
        for s in range(sub):
            rel = jnp.where(row >= s, b_i - b_i[s:s + 1], MASKED_LOG)
            a_col = jnp.sum(q_i * k_i[s:s + 1] * jnp.exp(rel), axis=-1, keepdims=True)
            o_i = o_i + a_col * v_i[s:s + 1]
        parts.append(o_i)
    return parts[0] if len(parts) == 1 else jnp.concatenate(parts, axis=0)


def _intra_factored(q, k, v, b, sub):
    C = q.shape[0]
    vb = v.astype(BF16)
    heads = [slice(h * HG_DK, (h + 1) * HG_DK) for h in range(q.shape[1] // HG_DK)]
    scores = []
    for blk in range(C // sub):
        lo, hi = blk * sub, (blk + 1) * sub
        if blk == 0:
            qs = q[:hi] * jnp.exp(b[:hi])
            ks = k[:hi] * jnp.exp(-b[:hi])
        else:
            ref = b[lo - 1:lo]
            qs = q[lo:hi] * jnp.exp(b[lo:hi] - ref)
            ks = k[:hi] * jnp.exp(ref - b[:hi])
        qs, ks = qs.astype(BF16), ks.astype(BF16)
        r = lax.broadcasted_iota(jnp.int32, (sub, hi), 0)
        c = lax.broadcasted_iota(jnp.int32, (sub, hi), 1)
        row = []
        for cols in heads:
            a = lax.dot_general(qs[:, cols], ks[:, cols], _NT, preferred_element_type=F32)
            row.append(jnp.where(c <= r + lo, a, 0.0).astype(BF16))
        scores.append(row)
    parts = []
    for blk, row in enumerate(scores):
        hi = (blk + 1) * sub
        outs = [jnp.dot(a, vb[:hi, cols], preferred_element_type=F32) for a, cols in zip(row, heads)]
        parts.append(outs[0] if len(outs) == 1 else jnp.concatenate(outs, axis=1))
    return jnp.concatenate(parts, axis=0)


def _head_norm_gate(o, ng, gate):
    return (o * lax.rsqrt(jnp.mean(o * o, axis=-1, keepdims=True) + NORM_EPS) * ng * gate)


def _hgrn_prompt_kernel(q_ref, f_ref, i_ref, g_ref, lb_ref, ng_ref, o_ref, sfin_ref,
                        st_ref, qs_ref, ks_ref, lf_ref, qe_ref, kd_ref, eb_ref, oi_ref):
    tb = pl.program_id(2)
    rows_total = q_ref.shape[0]
    n_chunks = rows_total // HG_CHUNK

    @pl.when(tb == 0)
    def _():
        st_ref[...] = jnp.zeros_like(st_ref)

    q, k, logf = _hgrn_gates(q_ref[...], f_ref[...], lb_ref[...])
    qs_ref[...] = q
    ks_ref[...] = k
    lf_ref[...] = logf
    block_decay = jnp.sum(logf.reshape(rows_total // HG_SUB, HG_SUB, logf.shape[-1]), axis=1)
    mild = jnp.min(block_decay) >= -MAX_BLOCK_DECAY

    heads = [slice(h * HG_DK, (h + 1) * HG_DK) for h in range(HG_GROUP)]

    def exact_group(qg, kg, vg, b, sub):
        return jnp.concatenate([_intra_exact(qg[:, c], kg[:, c], vg[:, c], b[:, c], sub) for c in heads], axis=1)

    def intra_pass(intra):
        def body(ci, carry):
            rows = pl.ds(pl.multiple_of(ci * HG_CHUNK, HG_CHUNK), HG_CHUNK)
            qg, kg, vg = qs_ref[rows, :], ks_ref[rows, :], i_ref[rows, :]
            b, qe, kd, eb_last = _chunk_prep(qg, kg, lf_ref[rows, :])
            qe_ref[rows, :] = qe
            kd_ref[rows, :] = kd
            eb_ref[pl.ds(pl.multiple_of(ci * SUBLANES, SUBLANES), SUBLANES), :] = \
                jnp.broadcast_to(eb_last, (SUBLANES, eb_last.shape[1]))
            oi_ref[rows, :] = intra(qg, kg, vg, b, HG_SUB)
            return carry
        lax.fori_loop(0, n_chunks, body, 0)

    @pl.when(mild)
    def _():
        intra_pass(_intra_factored)

    @pl.when(jnp.logical_not(mild))
    def _():
        intra_pass(exact_group)

    ng = ng_ref[...]

    def state_body(ci, carry):
        rows = pl.ds(pl.multiple_of(ci * HG_CHUNK, HG_CHUNK), HG_CHUNK)
        st = st_ref[...]
        stb = st.astype(BF16)
        qe, kd, oi = qe_ref[rows, :], kd_ref[rows, :], oi_ref[rows, :]
        vb = i_ref[rows, :].astype(BF16)
        eb_last = eb_ref[pl.ds(pl.multiple_of(ci * SUBLANES, SUBLANES), 1), :]
        outs, adds = [], []
        for c in heads:
            o = oi[:, c] + lax.dot_general(qe[:, c], stb[:, c], _NT, preferred_element_type=F32)
            outs.append(o * lax.rsqrt(jnp.mean(o * o, axis=-1, keepdims=True) + NORM_EPS))
            adds.append(lax.dot_general(vb[:, c], kd[:, c], _TN, preferred_element_type=F32))
        st_ref[...] = st * eb_last + jnp.concatenate(adds, axis=1)
        gate = jax.nn.sigmoid(g_ref[rows, :])
        o_ref[rows, :] = (jnp.concatenate(outs, axis=1) * ng * gate).astype(o_ref.dtype)
        return carry

    lax.fori_loop(0, n_chunks, state_body, 0)

    @pl.when(tb == pl.num_programs(2) - 1)
    def _():
        for h, c in enumerate(heads):
            sfin_ref[h] = st_ref[:, c].T


def _hgrn_prompt(proj, lbs, ng_all, j, n_seq, L, lb_rows):
    nb = L // lb_rows
    ng_groups = HG_HEADS // HG_GROUP
    gw = HG_GROUP * HG_DK

    def field(k):
        return pl.BlockSpec((lb_rows, gw), lambda n, h, t: (n * nb + t, k * ng_groups + h))

    vec = pl.BlockSpec((None, 1, gw), lambda n, h, t: (j, 0, h))
    return pl.pallas_call(
        _hgrn_prompt_kernel,
        grid=(n_seq, ng_groups, nb),
        in_specs=[field(0), field(1), field(2), field(3), vec, vec],
        out_specs=[
            pl.BlockSpec((lb_rows, gw), lambda n, h, t: (n * nb + t, h)),
            pl.BlockSpec((None, HG_GROUP, HG_DK, HG_DK), lambda n, h, t: (n, h, 0, 0)),
        ],
        out_shape=[
            jax.ShapeDtypeStruct((n_seq * L, D_MODEL), BF16),
            jax.ShapeDtypeStruct((n_seq, HG_HEADS, HG_DK, HG_DK), F32),
        ],
        scratch_shapes=[
            pltpu.VMEM((HG_DK, gw), F32),
            pltpu.VMEM((lb_rows, gw), F32),
            pltpu.VMEM((lb_rows, gw), F32),
            pltpu.VMEM((lb_rows, gw), F32),
            pltpu.VMEM((lb_rows, gw), BF16),
            pltpu.VMEM((lb_rows, gw), BF16),
            pltpu.VMEM((lb_rows // HG_CHUNK * SUBLANES, gw), F32),
            pltpu.VMEM((lb_rows, gw), F32),
        ],
        compiler_params=_params("parallel", "parallel", "arbitrary"),
        name="hgrn_prompt",
    )(proj, proj, proj, proj, lbs, ng_all)


def _hgrn_sample_kernel(q_ref, f_ref, i_ref, g_ref, lb_ref, ng_ref, s0_ref, o_ref, sfin_ref):
    n_seq = s0_ref.shape[0]
    q, k, logf = _hgrn_gates(q_ref[...], f_ref[...], lb_ref[...])
    v = i_ref[...]
    row = lax.broadcasted_iota(jnp.int32, (SUBLANES, 1), 0)
    first_half = row < SAMPLE_LEN
    prep = []
    for seq in range(n_seq):
        rows = slice(seq // 2 * SUBLANES, (seq // 2 + 1) * SUBLANES)
        mine = first_half if seq % 2 == 0 else jnp.logical_not(first_half)
        qm, km, vm, lm = (jnp.where(mine, t[rows], 0.0) for t in (q, k, v, logf))
        b, qe, kd, eb_last = _chunk_prep(qm, km, lm)
        prep.append((qe, kd, eb_last, vm.astype(BF16), _intra_exact(qm, km, vm, b, SUBLANES)))
    states = [s0_ref[seq].T for seq in range(n_seq)]
    outs = [oi + lax.dot_general(qe, st.astype(BF16), _NT, preferred_element_type=F32)
            for (qe, _, _, _, oi), st in zip(prep, states)]
    news = [st * eb_last + lax.dot_general(vb, kd, _TN, preferred_element_type=F32)
            for (_, kd, eb_last, vb, _), st in zip(prep, states)]
    sfin_ref[...] = jnp.stack([st.T for st in news])
    tiles = [jnp.where(first_half, outs[2 * p], outs[2 * p + 1]) for p in range(n_seq // 2)]
    gate = jax.nn.sigmoid(g_ref[...])
    o_ref[...] = _head_norm_gate(jnp.concatenate(tiles, axis=0), ng_ref[...], gate).astype(o_ref.dtype)


def _hgrn_sample(proj, row0, state_all, lbs, ng_all, j, n_seq, sb):
    H = HG_HEADS
    rb = sb * SAMPLE_LEN
    assert row0 % rb == 0 and sb % 2 == 0
    r0 = row0 // rb

    def field(k):
        return pl.BlockSpec((rb, HG_DK), lambda s, h: (r0 + s, k * H + h))

    vec = pl.BlockSpec((None, 1, HG_DK), lambda s, h: (j, 0, h))
    return pl.pallas_call(
        _hgrn_sample_kernel,
        grid=(n_seq // sb, H),
        in_specs=[field(0), field(1), field(2), field(3), vec, vec,
                  pl.BlockSpec((None, sb, None, HG_DK, HG_DK), lambda s, h: (j, s, h, 0, 0))],
        out_specs=[
            pl.BlockSpec((rb, HG_DK), lambda s, h: (s, h)),
            pl.BlockSpec((None, sb, None, HG_DK, HG_DK), lambda s, h: (j, s, h, 0, 0)),
        ],
        out_shape=[
            jax.ShapeDtypeStruct((n_seq * SAMPLE_LEN, D_MODEL), BF16),
            jax.ShapeDtypeStruct(state_all.shape, state_all.dtype),
        ],
        input_output_aliases={6: 1},
        compiler_params=_params("parallel", "parallel"),
        name="hgrn_sample",
    )(proj, proj, proj, proj, lbs, ng_all, state_all)


META_E1, META_E2, META_G1, META_G2, META_R1, META_R2 = range(6)
TOKEN_BLOCK = 128


def _router_kernel(x_ref, g_ref, rw_ref, h_ref, meta_ref, cnt_ref, cb_ref, carry_ref):
    i = pl.program_id(0)
    tm = x_ref.shape[0]

    @pl.when(i == 0)
    def _():
        carry_ref[...] = jnp.zeros_like(carry_ref)

    h = _rms(x_ref[...], g_ref[...])
    h_ref[...] = h.astype(h_ref.dtype)
    logits = jnp.dot(h.astype(BF16), rw_ref[...].astype(BF16), preferred_element_type=F32)
    lane = lax.broadcasted_iota(jnp.int32, (tm, LANES), 1)
    neg = -jnp.inf
    lg = jnp.where(lane < N_EXPERTS, logits, neg)
    m1 = jnp.max(lg, axis=-1, keepdims=True)
    e1 = jnp.min(jnp.where(lg == m1, lane, LANES), axis=-1, keepdims=True)
    lg2 = jnp.where(lane == e1, neg, lg)
    m2 = jnp.max(lg2, axis=-1, keepdims=True)
    e2 = jnp.min(jnp.where(lg2 == m2, lane, LANES), axis=-1, keepdims=True)
    ex = jnp.exp(m2 - m1)
    g1 = 1.0 / (1.0 + ex)
    g2 = ex / (1.0 + ex)

    onehot = jnp.where((lane == e1) | (lane == e2), 1.0, 0.0)
    r = lax.broadcasted_iota(jnp.int32, (tm, tm), 0)
    c = lax.broadcasted_iota(jnp.int32, (tm, tm), 1)
    before = jnp.where(r > c, 1.0, 0.0).astype(BF16)
    seen = jnp.dot(before, onehot.astype(BF16), preferred_element_type=F32) + carry_ref[0:1, :]
    r1 = jnp.sum(jnp.where(lane == e1, seen, 0.0), axis=-1, keepdims=True)
    r2 = jnp.sum(jnp.where(lane == e2, seen, 0.0), axis=-1, keepdims=True)
    for blk in range(tm // TOKEN_BLOCK):
        cb_ref[blk] = jnp.broadcast_to(seen[blk * TOKEN_BLOCK:blk * TOKEN_BLOCK + 1], (SUBLANES, LANES))
    total = carry_ref[0:1, :] + jnp.sum(onehot, axis=0, keepdims=True)
    carry_ref[...] = jnp.broadcast_to(total, carry_ref.shape)
    cnt_ref[...] = jnp.broadcast_to(total, cnt_ref.shape)

    meta = jnp.zeros((tm, LANES), F32)
    for idx, val in ((META_E1, e1.astype(F32)), (META_E2, e2.astype(F32)), (META_G1, g1),
                     (META_G2, g2), (META_R1, r1), (META_R2, r2)):
        meta = jnp.where(lane == idx, val, meta)
    meta_ref[...] = meta


def _router(x, g_all, layer, rw_pad, m, tm):
    T = x.shape[0]
    return pl.pallas_call(
        _router_kernel,
        grid=(T // tm,),
        in_specs=[
            pl.BlockSpec((tm, D_MODEL), lambda i: (i, 0)),
            pl.BlockSpec((None, 1, D_MODEL), lambda i: (layer, 0, 0)),
            pl.BlockSpec((None, D_MODEL, LANES), lambda i: (m, 0, 0)),
        ],
        out_specs=[
            pl.BlockSpec((tm, D_MODEL), lambda i: (i, 0)),
            pl.BlockSpec((tm, LANES), lambda i: (i, 0)),
            pl.BlockSpec((SUBLANES, LANES), lambda i: (0, 0)),
            pl.BlockSpec((tm // TOKEN_BLOCK, SUBLANES, LANES), lambda i: (i, 0, 0)),
        ],
        out_shape=[
            jax.ShapeDtypeStruct((T, D_MODEL), BF16),
            jax.ShapeDtypeStruct((T, LANES), F32),
            jax.ShapeDtypeStruct((SUBLANES, LANES), F32),
            jax.ShapeDtypeStruct((T // TOKEN_BLOCK, SUBLANES, LANES), F32),
        ],
        scratch_shapes=[pltpu.VMEM((SUBLANES, LANES), F32)],
        compiler_params=_params("arbitrary"),
        name="moe_router",
    )(x, g_all, rw_pad)


SLOT_BLOCK = 128
WINDOW_BLOCKS = 6
TOKEN_WINDOW = WINDOW_BLOCKS * TOKEN_BLOCK


def _dispatch_kernel(sb_ref, ws_ref, lo_ref, first_ref, n_ref, h_ref, dest_ref, o_ref):
    i = pl.program_id(0)

    @pl.when(i < n_ref[0])
    def _():
        slot = sb_ref[i] * SLOT_BLOCK + lax.broadcasted_iota(jnp.int32, (SLOT_BLOCK, 1), 0)
        token = ws_ref[i] + lax.broadcasted_iota(jnp.int32, (1, TOKEN_WINDOW), 1)
        dest = jnp.where(token >= lo_ref[i], dest_ref[...], -1)
        hit = (dest[0:1, :] == slot) | (dest[1:2, :] == slot)
        rows = jnp.dot(jnp.where(hit, 1.0, 0.0).astype(BF16), h_ref[...], preferred_element_type=F32)

        @pl.when(first_ref[i] == 1)
        def _():
            o_ref[...] = rows.astype(o_ref.dtype)

        @pl.when(first_ref[i] == 0)
        def _():
            o_ref[...] = (o_ref[...].astype(F32) + rows).astype(o_ref.dtype)


def _dispatch(h, dest_t, items, n_slots):
    def at(i, sb, ws, lo, fi, ni):
        return pl.multiple_of(ws[i], TOKEN_BLOCK)

    grid_spec = pltpu.PrefetchScalarGridSpec(
        num_scalar_prefetch=5,
        grid=(items[0].shape[0],),
        in_specs=[
            pl.BlockSpec((pl.Element(TOKEN_WINDOW), pl.Element(D_MODEL)), lambda *a: (at(*a), 0)),
            pl.BlockSpec((pl.Element(SUBLANES), pl.Element(TOKEN_WINDOW)), lambda *a: (0, at(*a))),
        ],
        out_specs=pl.BlockSpec((SLOT_BLOCK, D_MODEL), lambda i, sb, ws, lo, fi, ni: (sb[i], 0)),
    )
    return pl.pallas_call(
        _dispatch_kernel,
        grid_spec=grid_spec,
        out_shape=jax.ShapeDtypeStruct((n_slots, D_MODEL), BF16),
        compiler_params=_params("arbitrary"),
        name="moe_dispatch",
    )(*items, h, dest_t)


def _dispatch_items(cblk, counts, pstart, pend, block_expert, bm, n_slots, T):
    n_tb = T // TOKEN_BLOCK
    n_sb = n_slots // SLOT_BLOCK
    sb_start = jnp.arange(n_sb, dtype=jnp.int32) * SLOT_BLOCK
    e_sb = block_expert[sb_start // bm]
    r0 = sb_start - pstart[e_sb]
    c_sb = counts[e_sb]
    in_region = sb_start < pend[e_sb]
    has_tokens = r0 < c_sb
    r1 = jnp.minimum(r0 + SLOT_BLOCK, c_sb) - 1
    before = cblk[:, 0, :N_EXPERTS].astype(jnp.int32)[:, e_sb]
    tb_first = jnp.where(has_tokens, jnp.sum(before <= r0[None, :], axis=0) - 1, 0)
    tb_last = jnp.where(has_tokens, jnp.sum(before <= r1[None, :], axis=0) - 1, 0)
    n_win = jnp.where(in_region, (tb_last - tb_first) // WINDOW_BLOCKS + 1, 1)
    ends = jnp.cumsum(n_win)
    total = ends[-1]
    max_items = n_sb + N_EXPERTS * (-(-n_tb // WINDOW_BLOCKS) + 1)
    i = jnp.minimum(jnp.arange(max_items, dtype=jnp.int32), total - 1)
    sb = jnp.sum(ends[None, :] <= i[:, None], axis=1).astype(jnp.int32)
    k = i - (ends - n_win)[sb]
    lo = ((tb_first[sb] + k * WINDOW_BLOCKS) * TOKEN_BLOCK).astype(jnp.int32)
    start = jnp.minimum(lo, T - TOKEN_WINDOW)
    return sb, start, lo, (k == 0).astype(jnp.int32), total.astype(jnp.int32).reshape(1)


def _experts_kernel(be_ref, nu_ref, xs_ref, wg_ref, wu_ref, wd_ref, o_ref, acc_ref):
    b = pl.program_id(0)
    f = pl.program_id(1)
    used = b < nu_ref[0]

    @pl.when(used & (f == 0))
    def _():
        acc_ref[...] = jnp.zeros_like(acc_ref)

    @pl.when(used)
    def _():
        acc_ref[...] = _swiglu_acc(acc_ref[...], xs_ref[...], wg_ref[...], wu_ref[...], wd_ref[...])

        @pl.when(f == pl.num_programs(1) - 1)
        def _():
            o_ref[...] = acc_ref[...].astype(o_ref.dtype)

    @pl.when(jnp.logical_not(used) & (f == 0))
    def _():
        o_ref[...] = jnp.zeros_like(o_ref)


def _experts(xs, block_expert, n_used, wg, wu, wd, m, bm, tf):
    n_blocks = xs.shape[0] // bm
    nf = D_FF // tf

    def row(b, nu):
        return jnp.minimum(b, nu[0] - 1)

    def fidx(b, f, nu):
        return jnp.where(b < nu[0], f, nf - 1)

    grid_spec = pltpu.PrefetchScalarGridSpec(
        num_scalar_prefetch=2,
        grid=(n_blocks, nf),
        in_specs=[
            pl.BlockSpec((bm, D_MODEL), lambda b, f, be, nu: (row(b, nu), 0)),
            pl.BlockSpec((None, None, D_MODEL, tf), lambda b, f, be, nu: (m, be[b], 0, fidx(b, f, nu))),
            pl.BlockSpec((None, None, D_MODEL, tf), lambda b, f, be, nu: (m, be[b], 0, fidx(b, f, nu))),
            pl.BlockSpec((None, None, tf, D_MODEL), lambda b, f, be, nu: (m, be[b], fidx(b, f, nu), 0)),
        ],
        out_specs=pl.BlockSpec((bm, D_MODEL), lambda b, f, be, nu: (b, 0)),
        scratch_shapes=[pltpu.VMEM((bm, D_MODEL), F32)],
    )
    return pl.pallas_call(
        _experts_kernel,
        grid_spec=grid_spec,
        out_shape=jax.ShapeDtypeStruct((n_blocks * bm, D_MODEL), BF16),
        compiler_params=_params("arbitrary", "arbitrary"),
        name="moe_experts",
    )(block_expert, n_used, xs, wg, wu, wd)


BF16_ROWS = 16
WINDOW = TOKEN_BLOCK + BF16_ROWS


def _combine_kernel(win_ref, ps_ref, pe_ref, x_ref, meta_ref, *refs):
    ys_refs, o_ref = refs[:N_EXPERTS], refs[N_EXPERTS]
    blk = pl.program_id(0)
    meta = meta_ref[...]
    e1, e2 = meta[:, META_E1:META_E1 + 1], meta[:, META_E2:META_E2 + 1]
    d1, d2 = meta[:, META_R1:META_R1 + 1], meta[:, META_R2:META_R2 + 1]
    for e in range(N_EXPERTS):
        start = ps_ref[e].astype(F32)
        d1 = d1 + jnp.where(e1 == e, start, 0.0)
        d2 = d2 + jnp.where(e2 == e, start, 0.0)
    d1, d2, g1, g2 = (jnp.broadcast_to(t, (TOKEN_BLOCK, WINDOW))
                      for t in (d1, d2, meta[:, META_G1:META_G1 + 1], meta[:, META_G2:META_G2 + 1]))
    lane = lax.broadcasted_iota(jnp.int32, (1, WINDOW), 1)
    acc = x_ref[...]
    for e in range(N_EXPERTS):
        slot = win_ref[blk * N_EXPERTS + e] + lane
        slot = jnp.where((slot >= ps_ref[e]) & (slot < pe_ref[e]), slot, -1).astype(F32)
        sel = jnp.where(d1 == slot, g1, 0.0) + jnp.where(d2 == slot, g2, 0.0)
        acc = acc + jnp.dot(sel.astype(BF16), ys_refs[e][...], preferred_element_type=F32)
    o_ref[...] = acc


def _combine(x, meta, ys, windows, pstart, pend):
    T = x.shape[0]
    nb = T // TOKEN_BLOCK

    def window(e):
        return pl.BlockSpec((pl.Element(WINDOW), pl.Element(D_MODEL)),
                            lambda i, win, ps, pe: (pl.multiple_of(win[i * N_EXPERTS + e], BF16_ROWS), 0))

    grid_spec = pltpu.PrefetchScalarGridSpec(
        num_scalar_prefetch=3,
        grid=(nb,),
        in_specs=[pl.BlockSpec((TOKEN_BLOCK, D_MODEL), lambda i, win, ps, pe: (i, 0)),
                  pl.BlockSpec((TOKEN_BLOCK, LANES), lambda i, win, ps, pe: (i, 0))]
                 + [window(e) for e in range(N_EXPERTS)],
        out_specs=pl.BlockSpec((TOKEN_BLOCK, D_MODEL), lambda i, win, ps, pe: (i, 0)),
    )
    return pl.pallas_call(
        _combine_kernel,
        grid_spec=grid_spec,
        out_shape=jax.ShapeDtypeStruct((T, D_MODEL), F32),
        compiler_params=_params("arbitrary"),
        name="moe_combine",
    )(windows, pstart, pend, x, meta, *([ys] * N_EXPERTS))


def _final_norm_kernel(x_ref, g_ref, o_ref):
    o_ref[...] = _rms(x_ref[...], g_ref[...])


def _final_norm(x, g, row0, n_rows, tm):
    r0 = row0 // tm
    return pl.pallas_call(
        _final_norm_kernel,
        grid=(n_rows // tm,),
        in_specs=[pl.BlockSpec((tm, D_MODEL), lambda i: (r0 + i, 0)),
                  pl.BlockSpec((1, D_MODEL), lambda i: (0, 0))],
        out_specs=pl.BlockSpec((tm, D_MODEL), lambda i: (i, 0)),
        out_shape=jax.ShapeDtypeStruct((n_rows, D_MODEL), F32),
        compiler_params=_params("parallel"),
        name="final_norm",
    )(x, g)


def _moe_layer(x, norm_ffn_g, layer, rw_pad, wg, wu, wd, m, tm, bm, tf):
    T = x.shape[0]
    h, meta, cnt, cblk = _router(x, norm_ffn_g, layer, rw_pad, m, tm)
    e = meta[:, META_E1:META_E2 + 1].astype(jnp.int32)
    rank = meta[:, META_R1:META_R2 + 1].astype(jnp.int32)
    counts = cnt[0, :N_EXPERTS].astype(jnp.int32)
    padded = (counts + bm - 1) // bm * bm
    pend = jnp.cumsum(padded)
    pstart = pend - padded
    dest = pstart[e] + rank
    n_blocks = (T * 2) // bm + N_EXPERTS + 1
    block_start = jnp.arange(n_blocks, dtype=jnp.int32) * bm
    block_expert = jnp.minimum(jnp.sum(pend[None, :] <= block_start[:, None], axis=1),
                               N_EXPERTS - 1).astype(jnp.int32)
    n_used = (pend[-1] // bm).astype(jnp.int32).reshape(1)
    dest_t = jnp.full((SUBLANES, T), -1, jnp.int32).at[0:2].set(dest.T)
    items = _dispatch_items(cblk, counts, pstart, pend, block_expert, bm, n_blocks * bm, T)
    xs = _dispatch(h, dest_t, items, n_blocks * bm)
    ys = _experts(xs, block_expert, n_used, wg, wu, wd, m, bm, tf)
    run_start = pstart[None, :] + cblk[:, 0, :N_EXPERTS].astype(jnp.int32)
    windows = (run_start // BF16_ROWS * BF16_ROWS).reshape(-1).astype(jnp.int32)
    return _combine(x, meta, ys, windows, pstart.astype(jnp.int32), pend.astype(jnp.int32))


def _tril(w):
    n = w.shape[-1]
    return jnp.where(jnp.tril(jnp.ones((n, n), dtype=bool)), w, jnp.zeros((), w.dtype))


def _mix_tables(w_s, b_s, sample_len):
    reps = A_CHUNK // sample_len
    eye = jnp.eye(reps, dtype=w_s.dtype)
    w_prompt = _tril(w_s)
    w_small = _tril(w_s[:, :sample_len, :sample_len])
    w_sample = jax.vmap(lambda w: jnp.kron(eye, w))(w_small)
    b_prompt = b_s
    b_sample = jnp.tile(b_s[:, :sample_len], (1, reps))
    wmix = jnp.stack([w_prompt, w_sample])
    bias = jnp.stack([b_prompt, b_sample])[..., None]
    return wmix, jnp.broadcast_to(bias, bias.shape[:-1] + (LANES,))


def kernel(x_prompt, x_sample, state_hgrn, norm_mix_g, norm_ffn_g, final_norm_g, a_w_in, a_ln_g, a_ln_b, a_w_s, a_b_s, a_w_out, b_w_in, b_lb_logits, b_norm_g, b_w_out, ffn_w_gate, ffn_w_up, ffn_w_down, moe_router, moe_w_gate, moe_w_up, moe_w_down):
    n_p, L, d = x_prompt.shape
    n_s, l_s, _ = x_sample.shape
    assert d == D_MODEL and l_s == SAMPLE_LEN and L % A_CHUNK == 0
    T_p, T_s = n_p * L, n_s * l_s
    T = T_p + T_s
    depth = norm_mix_g.shape[0]

    tm_gate = 2 * A_CHUNK
    assert T_p % tm_gate == 0 and T_s % tm_gate == 0
    tm = next(t for t in (768, 512, 256, 128) if T % t == 0)
    tm_big = next(t for t in (1536, 768, 512, 256, 128) if T % t == 0)
    bm = 1024
    tf = 512
    lb_rows = min(L, 512)

    x = jnp.concatenate([x_prompt.reshape(T_p, d), x_sample.reshape(T_s, d)], axis=0)

    p = jax.nn.softmax(b_lb_logits.astype(F32), axis=0)
    lbs = (jnp.cumsum(p, axis=0) - p[0:1])[:, None, :]
    mix_g = norm_mix_g[:, None, :]
    ffn_g = norm_ffn_g[:, None, :]
    b_ng = b_norm_g[:, None, :]
    ln_g = a_ln_g[:, None, :]
    ln_b = a_ln_b[:, None, :]
    rw_pad = jnp.pad(moe_router, ((0, 0), (0, 0), (0, LANES - N_EXPERTS)))
    a_w_in, a_w_out, b_w_in, b_w_out, ffn_w_gate, ffn_w_up, ffn_w_down = (
        w.astype(BF16) for w in (a_w_in, a_w_out, b_w_in, b_w_out, ffn_w_gate, ffn_w_up, ffn_w_down))

    hg_prompt, v_sample = [], []
    hg_sample = state_hgrn
    for layer in range(depth):
        j = layer // 2
        if layer % 2 == 0:
            z = _norm_matmul(x, mix_g, layer, a_w_in, j, _gelu, BF16, tm_big, 1024)
            wmix, bias = _mix_tables(a_w_s[j], a_b_s[j], l_s)
            x, v = _gmlp_gate(z, x, ln_g, ln_b, wmix, bias, a_w_out, j, T_p // tm_gate, tm_gate)
            v_sample.append(v.reshape(n_s, l_s, A_HALF))
            x = _ffn_dense(x, ffn_g, layer, ffn_w_gate, ffn_w_up, ffn_w_down, j, tm_big, tf)
        else:
            proj = _norm_matmul(x, mix_g, layer, b_w_in, j, _identity, F32, tm_big, 1024)
            o_p, s_p = _hgrn_prompt(proj, lbs, b_ng, j, n_p, L, lb_rows)
            o_s, hg_sample = _hgrn_sample(proj, T_p, hg_sample, lbs, b_ng, j, n_s, 8)
            hg_prompt.append(s_p)
            o = jnp.concatenate([o_p, o_s], axis=0)
            x = _matmul_res(o, b_w_out, j, x, tm)
            x = _moe_layer(x, ffn_g, layer, rw_pad, moe_w_gate, moe_w_up, moe_w_down, j, 512, bm, tf)

    fg = final_norm_g[None, :]
    y_prompt = _final_norm(x, fg, 0, T_p, 512).reshape(n_p, L, d)
    y_sample = _final_norm(x, fg, T_p, T_s, T_s).reshape(n_s, l_s, d)
    return (y_prompt, y_sample, jnp.stack(hg_prompt), hg_sample, jnp.stack(v_sample))
```

```python
import functools
import math

import jax
import jax.numpy as jnp
from jax import lax
from jax.experimental import pallas as pl
from jax.experimental.pallas import tpu as pltpu

F32 = jnp.float32
BF16 = jnp.bfloat16

D_MODEL = 1024
A_CHUNK = 128
A_HALF = 3 * D_MODEL
A_GROUPS = 8
A_GROUP_DIM = A_HALF // A_GROUPS
HG_HEADS = 8
HG_DK = 128
HG_CHUNK = 64
HG_SUB = 16
FORGET_FLOOR = 1e-20
D_FF = 7 * D_MODEL // 2
N_EXPERTS = 8
NORM_EPS = 1e-6
LANES = 128
SUBLANES = 8
SAMPLE_LEN = 4
MASKED_LOG = -1e30

VMEM_LIMIT = 56 * 1024 * 1024


def _params(*sem):
    return pltpu.CompilerParams(dimension_semantics=sem, vmem_limit_bytes=VMEM_LIMIT)


def _rms(x, g):
    ms = jnp.mean(x * x, axis=-1, keepdims=True)
    return x * lax.rsqrt(ms + NORM_EPS) * g


def _gelu(y):
    return 0.5 * y * (1.0 + lax.erf(y * math.sqrt(0.5)))


def _identity(y):
    return y


def _norm_matmul_kernel(x_ref, g_ref, w_ref, o_ref, h_ref, *, act):
    @pl.when(pl.program_id(1) == 0)
    def _():
        h_ref[...] = _rms(x_ref[...], g_ref[...]).astype(BF16)

    y = jnp.dot(h_ref[...], w_ref[...].astype(BF16), preferred_element_type=F32)
    o_ref[...] = act(y).astype(o_ref.dtype)


def _norm_matmul(x, g_all, layer, w_all, w_layer, act, out_dtype, tm, tn):
    T = x.shape[0]
    N = w_all.shape[-1]
    return pl.pallas_call(
        functools.partial(_norm_matmul_kernel, act=act),
        grid=(T // tm, N // tn),
        in_specs=[
            pl.BlockSpec((tm, D_MODEL), lambda i, j: (i, 0)),
            pl.BlockSpec((None, 1, D_MODEL), lambda i, j: (layer, 0, 0)),
            pl.BlockSpec((None, D_MODEL, tn), lambda i, j: (w_layer, 0, j)),
        ],
        out_specs=pl.BlockSpec((tm, tn), lambda i, j: (i, j)),
        out_shape=jax.ShapeDtypeStruct((T, N), out_dtype),
        scratch_shapes=[pltpu.VMEM((tm, D_MODEL), BF16)],
        compiler_params=_params("parallel", "arbitrary"),
        name="norm_matmul",
    )(x, g_all, w_all)


def _gmlp_gate_kernel(z_ref, lng_ref, lnb_ref, wmix_ref, bias_ref, wout_ref, x_ref, o_ref, v_ref):
    tm = z_ref.shape[0]
    zv = z_ref[:, A_HALF:].astype(F32)
    mu = jnp.mean(zv, axis=-1, keepdims=True)
    xc = zv - mu
    rstd = lax.rsqrt(jnp.mean(xc * xc, axis=-1, keepdims=True) + NORM_EPS)
    v = xc * rstd * lng_ref[...] + lnb_ref[...]
    v_ref[...] = v
    vb = v.astype(BF16)
    groups = []
    for g in range(A_GROUPS):
        cols = slice(g * A_GROUP_DIM, (g + 1) * A_GROUP_DIM)
        wm = wmix_ref[g].astype(BF16)
        bias = jnp.concatenate([bias_ref[g]] * (A_GROUP_DIM // LANES), axis=1)
        rows = []
        for c in range(tm // A_CHUNK):
            sl = slice(c * A_CHUNK, (c + 1) * A_CHUNK)
            s = jnp.dot(wm, vb[sl, cols], preferred_element_type=F32) + bias
            rows.append((z_ref[sl, cols].astype(F32) * s).astype(BF16))
        groups.append(jnp.concatenate(rows, axis=0))
    gated = jnp.concatenate(groups, axis=1)
    o_ref[...] = x_ref[...] + jnp.dot(gated, wout_ref[...], preferred_element_type=F32)


def _gmlp_gate(z, x, ln_g, ln_b, wmix, bias, w_out, j, n_prompt_blocks, tm):
    T = x.shape[0]
    nb = T // tm

    def kind(i):
        return jnp.where(i >= n_prompt_blocks, 1, 0)

    return pl.pallas_call(
        _gmlp_gate_kernel,
        grid=(nb,),
        in_specs=[
            pl.BlockSpec((tm, 2 * A_HALF), lambda i: (i, 0)),
            pl.BlockSpec((None, 1, A_HALF), lambda i: (j, 0, 0)),
            pl.BlockSpec((None, 1, A_HALF), lambda i: (j, 0, 0)),
            pl.BlockSpec((None, A_GROUPS, A_CHUNK, A_CHUNK), lambda i: (kind(i), 0, 0, 0)),
            pl.BlockSpec((None, A_GROUPS, A_CHUNK, LANES), lambda i: (kind(i), 0, 0, 0)),
            pl.BlockSpec((None, A_HALF, D_MODEL), lambda i: (j, 0, 0)),
            pl.BlockSpec((tm, D_MODEL), lambda i: (i, 0)),
        ],
        out_specs=[
            pl.BlockSpec((tm, D_MODEL), lambda i: (i, 0)),
            pl.BlockSpec((tm, A_HALF), lambda i: (jnp.maximum(i - n_prompt_blocks, 0), 0)),
        ],
        out_shape=[
            jax.ShapeDtypeStruct((T, D_MODEL), F32),
            jax.ShapeDtypeStruct(((nb - n_prompt_blocks) * tm, A_HALF), F32),
        ],
        compiler_params=_params("arbitrary"),
        name="gmlp_gate",
    )(z, ln_g, ln_b, wmix, bias, w_out, x)


FF_SPLIT = 2


def _swiglu_acc(acc, h, wg, wu, wd):
    sub = wg.shape[1] // FF_SPLIT
    for c in range(FF_SPLIT):
        cols = slice(c * sub, (c + 1) * sub)
        a = jnp.dot(h, wg[:, cols].astype(BF16), preferred_element_type=F32)
        b = jnp.dot(h, wu[:, cols].astype(BF16), preferred_element_type=F32)
        m = (a * jax.nn.sigmoid(a) * b).astype(BF16)
        acc = acc + jnp.dot(m, wd[cols, :].astype(BF16), preferred_element_type=F32)
    return acc


def _ffn_kernel(x_ref, g_ref, wg_ref, wu_ref, wd_ref, o_ref, h_ref):
    @pl.when(pl.program_id(1) == 0)
    def _():
        x = x_ref[...]
        h_ref[...] = _rms(x, g_ref[...]).astype(BF16)
        o_ref[...] = x

    o_ref[...] = _swiglu_acc(o_ref[...], h_ref[...], wg_ref[...], wu_ref[...], wd_ref[...])


def _ffn_dense(x, g_all, layer, wg, wu, wd, m, tm, tf):
    T = x.shape[0]
    return pl.pallas_call(
        _ffn_kernel,
        grid=(T // tm, D_FF // tf),
        in_specs=[
            pl.BlockSpec((tm, D_MODEL), lambda i, f: (i, 0)),
            pl.BlockSpec((None, 1, D_MODEL), lambda i, f: (layer, 0, 0)),
            pl.BlockSpec((None, D_MODEL, tf), lambda i, f: (m, 0, f)),
            pl.BlockSpec((None, D_MODEL, tf), lambda i, f: (m, 0, f)),
            pl.BlockSpec((None, tf, D_MODEL), lambda i, f: (m, f, 0)),
        ],
        out_specs=pl.BlockSpec((tm, D_MODEL), lambda i, f: (i, 0)),
        out_shape=jax.ShapeDtypeStruct((T, D_MODEL), F32),
        scratch_shapes=[pltpu.VMEM((tm, D_MODEL), BF16)],
        compiler_params=_params("parallel", "arbitrary"),
        name="ffn_dense",
    )(x, g_all, wg, wu, wd)


def _matmul_res_kernel(a_ref, w_ref, x_ref, o_ref):
    o_ref[...] = x_ref[...] + jnp.dot(a_ref[...], w_ref[...], preferred_element_type=F32)


def _matmul_res(a, w_all, j, x, tm):
    T = x.shape[0]
    return pl.pallas_call(
        _matmul_res_kernel,
        grid=(T // tm,),
        in_specs=[
            pl.BlockSpec((tm, D_MODEL), lambda i: (i, 0)),
            pl.BlockSpec((None, D_MODEL, D_MODEL), lambda i: (j, 0, 0)),
            pl.BlockSpec((tm, D_MODEL), lambda i: (i, 0)),
        ],
        out_specs=pl.BlockSpec((tm, D_MODEL), lambda i: (i, 0)),
        out_shape=jax.ShapeDtypeStruct((T, D_MODEL), F32),
        compiler_params=_params("parallel"),
        name="matmul_res",
    )(a, w_all, x)


_NT = (((1,), (1,)), ((), ()))
_TN = (((0,), (0,)), ((), ()))


HG_GROUP = 8
MAX_BLOCK_DECAY = 75.0


def _hgrn_gates(qpre, fpre, lb):
    q = qpre * jax.nn.sigmoid(qpre)
    sig = jax.nn.sigmoid(fpre)
    f = lb + (1.0 - lb) * sig
    logf = jnp.log(jnp.maximum(f, FORGET_FLOOR))
    k = (1.0 - lb) * (1.0 - sig)
    return q, k, logf


def _cumsum_rows(x):
    C = x.shape[0]
    if C >= HG_SUB:
        r = lax.broadcasted_iota(jnp.int32, (C, C), 0)
        c = lax.broadcasted_iota(jnp.int32, (C, C), 1)
        tri = jnp.where(r >= c, 1.0, 0.0).astype(F32)
        return jnp.dot(tri, x, preferred_element_type=F32, precision=lax.Precision.HIGHEST)
    row = lax.broadcasted_iota(jnp.int32, (C, 1), 0)
    out = jnp.zeros_like(x)
    for s in range(C):
        out = out + jnp.where(row >= s, x[s:s + 1], 0.0)
    return out


def _chunk_prep(q, k, logf):
    C = q.shape[0]
    b = _cumsum_rows(logf)
    b_last = b[C - 1:C]
    qe = (q * jnp.exp(b)).astype(BF16)
    kd = (k * jnp.exp(b_last - b)).astype(BF16)
    return b, qe, kd, jnp.exp(b_last)


def _intra_exact(q, k, v, b, sub):
    C = q.shape[0]
    row = lax.broadcasted_iota(jnp.int32, (sub, 1), 0)
    parts = []
    for blk in range(C // sub):
        lo = blk * sub
        b_i, q_i, k_i, v_i = b[lo:lo + sub], q[lo:lo + sub], k[lo:lo + sub], v[lo:lo + sub]
        if blk == 0:
            o_i = jnp.zeros((sub, HG_DK), F32)
        else:
            ref = b[lo - 1:lo]
            qs = (q_i * jnp.exp(b_i - ref)).astype(BF16)
            ks = (k[:lo] * jnp.exp(ref - b[:lo])).astype(BF16)
            a = lax.dot_general(qs, ks, _NT, preferred_element_type=F32)
            o_i = jnp.dot(a.astype(BF16), v[:lo].astype(BF16), preferred_element_type=F32)
        for s in range(sub):
            rel = jnp.where(row >= s, b_i - b_i[s:s + 1], MASKED_LOG)
            a_col = jnp.sum(q_i * k_i[s:s + 1] * jnp.exp(rel), axis=-1, keepdims=True)
            o_i = o_i + a_col * v_i[s:s + 1]
        parts.append(o_i)
    return parts[0] if len(parts) == 1 else jnp.concatenate(parts, axis=0)


def _intra_factored(q, k, v, b, sub):
    C = q.shape[0]
    vb = v.astype(BF16)
    heads = [slice(h * HG_DK, (h + 1) * HG_DK) for h in range(q.shape[1] // HG_DK)]
    scores = []
    for blk in range(C // sub):
        lo, hi = blk * sub, (blk + 1) * sub
        if blk == 0:
            qs = q[:hi] * jnp.exp(b[:hi])
            ks = k[:hi] * jnp.exp(-b[:hi])
        else:
            ref = b[lo - 1:lo]
            qs = q[lo:hi] * jnp.exp(b[lo:hi] - ref)
            ks = k[:hi] * jnp.exp(ref - b[:hi])
        qs, ks = qs.astype(BF16), ks.astype(BF16)
        r = lax.broadcasted_iota(jnp.int32, (sub, hi), 0)
        c = lax.broadcasted_iota(jnp.int32, (sub, hi), 1)
        row = []
        for cols in heads:
            a = lax.dot_general(qs[:, cols], ks[:, cols], _NT, preferred_element_type=F32)
            row.append(jnp.where(c <= r + lo, a, 0.0).astype(BF16))
        scores.append(row)
    parts = []
    for blk, row in enumerate(scores):
        hi = (blk + 1) * sub
        outs = [jnp.dot(a, vb[:hi, cols], preferred_element_type=F32) for a, cols in zip(row, heads)]
        parts.append(outs[0] if len(outs) == 1 else jnp.concatenate(outs, axis=1))
    return jnp.concatenate(parts, axis=0)


def _head_norm_gate(o, ng, gate):
    return (o * lax.rsqrt(jnp.mean(o * o, axis=-1, keepdims=True) + NORM_EPS) * ng * gate)


def _hgrn_prompt_kernel(q_ref, f_ref, i_ref, g_ref, lb_ref, ng_ref, o_ref, sfin_ref,
                        st_ref, qs_ref, ks_ref, lf_ref, qe_ref, kd_ref, eb_ref, oi_ref):
    tb = pl.program_id(2)
    rows_total = q_ref.shape[0]
    n_chunks = rows_total // HG_CHUNK

    @pl.when(tb == 0)
    def _():
        st_ref[...] = jnp.zeros_like(st_ref)

    q, k, logf = _hgrn_gates(q_ref[...], f_ref[...], lb_ref[...])
    qs_ref[...] = q
    ks_ref[...] = k
    lf_ref[...] = logf
    block_decay = jnp.sum(logf.reshape(rows_total // HG_SUB, HG_SUB, logf.shape[-1]), axis=1)
    mild = jnp.min(block_decay) >= -MAX_BLOCK_DECAY

    heads = [slice(h * HG_DK, (h + 1) * HG_DK) for h in range(HG_GROUP)]

    def exact_group(qg, kg, vg, b, sub):
        return jnp.concatenate([_intra_exact(qg[:, c], kg[:, c], vg[:, c], b[:, c], sub) for c in heads], axis=1)

    def intra_pass(intra):
        def body(ci, carry):
            rows = pl.ds(pl.multiple_of(ci * HG_CHUNK, HG_CHUNK), HG_CHUNK)
            qg, kg, vg = qs_ref[rows, :], ks_ref[rows, :], i_ref[rows, :]
            b, qe, kd, eb_last = _chunk_prep(qg, kg, lf_ref[rows, :])
            qe_ref[rows, :] = qe
            kd_ref[rows, :] = kd
            eb_ref[pl.ds(pl.multiple_of(ci * SUBLANES, SUBLANES), SUBLANES), :] = \
                jnp.broadcast_to(eb_last, (SUBLANES, eb_last.shape[1]))
            oi_ref[rows, :] = intra(qg, kg, vg, b, HG_SUB)
            return carry
        lax.fori_loop(0, n_chunks, body, 0, unroll=2)

    @pl.when(mild)
    def _():
        intra_pass(_intra_factored)

    @pl.when(jnp.logical_not(mild))
    def _():
        intra_pass(exact_group)

    ng = ng_ref[...]

    def state_body(ci, carry):
        rows = pl.ds(pl.multiple_of(ci * HG_CHUNK, HG_CHUNK), HG_CHUNK)
        st = st_ref[...]
        stb = st.astype(BF16)
        qe, kd, oi = qe_ref[rows, :], kd_ref[rows, :], oi_ref[rows, :]
        vb = i_ref[rows, :].astype(BF16)
        eb_last = eb_ref[pl.ds(pl.multiple_of(ci * SUBLANES, SUBLANES), 1), :]
        outs, adds = [], []
        for c in heads:
            o = oi[:, c] + lax.dot_general(qe[:, c], stb[:, c], _NT, preferred_element_type=F32)
            outs.append(o * lax.rsqrt(jnp.mean(o * o, axis=-1, keepdims=True) + NORM_EPS))
            adds.append(lax.dot_general(vb[:, c], kd[:, c], _TN, preferred_element_type=F32))
        st_ref[...] = st * eb_last + jnp.concatenate(adds, axis=1)
        gate = jax.nn.sigmoid(g_ref[rows, :])
        o_ref[rows, :] = (jnp.concatenate(outs, axis=1) * ng * gate).astype(o_ref.dtype)
        return carry

    lax.fori_loop(0, n_chunks, state_body, 0)

    @pl.when(tb == pl.num_programs(2) - 1)
    def _():
        for h, c in enumerate(heads):
            sfin_ref[h] = st_ref[:, c].T


def _hgrn_prompt(proj, lbs, ng_all, j, n_seq, L, lb_rows):
    nb = L // lb_rows
    ng_groups = HG_HEADS // HG_GROUP
    gw = HG_GROUP * HG_DK

    def field(k):
        return pl.BlockSpec((lb_rows, gw), lambda n, h, t: (n * nb + t, k * ng_groups + h))

    vec = pl.BlockSpec((None, 1, gw), lambda n, h, t: (j, 0, h))
    return pl.pallas_call(
        _hgrn_prompt_kernel,
        grid=(n_seq, ng_groups, nb),
        in_specs=[field(0), field(1), field(2), field(3), vec, vec],
        out_specs=[
            pl.BlockSpec((lb_rows, gw), lambda n, h, t: (n * nb + t, h)),
            pl.BlockSpec((None, HG_GROUP, HG_DK, HG_DK), lambda n, h, t: (n, h, 0, 0)),
        ],
        out_shape=[
            jax.ShapeDtypeStruct((n_seq * L, D_MODEL), BF16),
            jax.ShapeDtypeStruct((n_seq, HG_HEADS, HG_DK, HG_DK), F32),
        ],
        scratch_shapes=[
            pltpu.VMEM((HG_DK, gw), F32),
            pltpu.VMEM((lb_rows, gw), F32),
            pltpu.VMEM((lb_rows, gw), F32),
            pltpu.VMEM((lb_rows, gw), F32),
            pltpu.VMEM((lb_rows, gw), BF16),
            pltpu.VMEM((lb_rows, gw), BF16),
            pltpu.VMEM((lb_rows // HG_CHUNK * SUBLANES, gw), F32),
            pltpu.VMEM((lb_rows, gw), F32),
        ],
        compiler_params=_params("parallel", "parallel", "arbitrary"),
        name="hgrn_prompt",
    )(proj, proj, proj, proj, lbs, ng_all)


def _hgrn_sample_kernel(q_ref, f_ref, i_ref, g_ref, lb_ref, ng_ref, s0_ref, o_ref, sfin_ref):
    n_seq = s0_ref.shape[0]
    q, k, logf = _hgrn_gates(q_ref[...], f_ref[...], lb_ref[...])
    v = i_ref[...]
    row = lax.broadcasted_iota(jnp.int32, (SUBLANES, 1), 0)
    first_half = row < SAMPLE_LEN
    prep = []
    for seq in range(n_seq):
        rows = slice(seq // 2 * SUBLANES, (seq // 2 + 1) * SUBLANES)
        mine = first_half if seq % 2 == 0 else jnp.logical_not(first_half)
        qm, km, vm, lm = (jnp.where(mine, t[rows], 0.0) for t in (q, k, v, logf))
        b, qe, kd, eb_last = _chunk_prep(qm, km, lm)
        prep.append((qe, kd, eb_last, vm.astype(BF16), _intra_exact(qm, km, vm, b, SUBLANES)))
    states = [s0_ref[seq].T for seq in range(n_seq)]
    outs = [oi + lax.dot_general(qe, st.astype(BF16), _NT, preferred_element_type=F32)
            for (qe, _, _, _, oi), st in zip(prep, states)]
    news = [st * eb_last + lax.dot_general(vb, kd, _TN, preferred_element_type=F32)
            for (_, kd, eb_last, vb, _), st in zip(prep, states)]
    sfin_ref[...] = jnp.stack([st.T for st in news])
    tiles = [jnp.where(first_half, outs[2 * p], outs[2 * p + 1]) for p in range(n_seq // 2)]
    gate = jax.nn.sigmoid(g_ref[...])
    o_ref[...] = _head_norm_gate(jnp.concatenate(tiles, axis=0), ng_ref[...], gate).astype(o_ref.dtype)


def _hgrn_sample(proj, row0, state_all, lbs, ng_all, j, n_seq, sb):
    H = HG_HEADS
    rb = sb * SAMPLE_LEN
    assert row0 % rb == 0 and sb % 2 == 0
    r0 = row0 // rb

    def field(k):
        return pl.BlockSpec((rb, HG_DK), lambda s, h: (r0 + s, k * H + h))

    vec = pl.BlockSpec((None, 1, HG_DK), lambda s, h: (j, 0, h))
    return pl.pallas_call(
        _hgrn_sample_kernel,
        grid=(n_seq // sb, H),
        in_specs=[field(0), field(1), field(2), field(3), vec, vec,
                  pl.BlockSpec((None, sb, None, HG_DK, HG_DK), lambda s, h: (j, s, h, 0, 0))],
        out_specs=[
            pl.BlockSpec((rb, HG_DK), lambda s, h: (s, h)),
            pl.BlockSpec((None, sb, None, HG_DK, HG_DK), lambda s, h: (j, s, h, 0, 0)),
        ],
        out_shape=[
            jax.ShapeDtypeStruct((n_seq * SAMPLE_LEN, D_MODEL), BF16),
            jax.ShapeDtypeStruct(state_all.shape, state_all.dtype),
        ],
        input_output_aliases={6: 1},
        compiler_params=_params("parallel", "parallel"),
        name="hgrn_sample",
    )(proj, proj, proj, proj, lbs, ng_all, state_all)


META_E1, META_E2, META_G1, META_G2, META_R1, META_R2 = range(6)
TOKEN_BLOCK = 128


def _router_kernel(x_ref, g_ref, rw_ref, h_ref, meta_ref, cnt_ref, cb_ref, carry_ref):
    i = pl.program_id(0)
    tm = x_ref.shape[0]

    @pl.when(i == 0)
    def _():
        carry_ref[...] = jnp.zeros_like(carry_ref)

    h = _rms(x_ref[...], g_ref[...])
    h_ref[...] = h.astype(h_ref.dtype)
    logits = jnp.dot(h.astype(BF16), rw_ref[...].astype(BF16), preferred_element_type=F32)
    lane = lax.broadcasted_iota(jnp.int32, (tm, LANES), 1)
    neg = -jnp.inf
    lg = jnp.where(lane < N_EXPERTS, logits, neg)
    m1 = jnp.max(lg, axis=-1, keepdims=True)
    e1 = jnp.min(jnp.where(lg == m1, lane, LANES), axis=-1, keepdims=True)
    lg2 = jnp.where(lane == e1, neg, lg)
    m2 = jnp.max(lg2, axis=-1, keepdims=True)
    e2 = jnp.min(jnp.where(lg2 == m2, lane, LANES), axis=-1, keepdims=True)
    ex = jnp.exp(m2 - m1)
    g1 = 1.0 / (1.0 + ex)
    g2 = ex / (1.0 + ex)

    onehot = jnp.where((lane == e1) | (lane == e2), 1.0, 0.0)
    r = lax.broadcasted_iota(jnp.int32, (tm, tm), 0)
    c = lax.broadcasted_iota(jnp.int32, (tm, tm), 1)
    before = jnp.where(r > c, 1.0, 0.0).astype(BF16)
    seen = jnp.dot(before, onehot.astype(BF16), preferred_element_type=F32) + carry_ref[0:1, :]
    r1 = jnp.sum(jnp.where(lane == e1, seen, 0.0), axis=-1, keepdims=True)
    r2 = jnp.sum(jnp.where(lane == e2, seen, 0.0), axis=-1, keepdims=True)
    for blk in range(tm // TOKEN_BLOCK):
        cb_ref[blk] = jnp.broadcast_to(seen[blk * TOKEN_BLOCK:blk * TOKEN_BLOCK + 1], (SUBLANES, LANES))
    total = carry_ref[0:1, :] + jnp.sum(onehot, axis=0, keepdims=True)
    carry_ref[...] = jnp.broadcast_to(total, carry_ref.shape)
    cnt_ref[...] = jnp.broadcast_to(total, cnt_ref.shape)

    meta = jnp.zeros((tm, LANES), F32)
    for idx, val in ((META_E1, e1.astype(F32)), (META_E2, e2.astype(F32)), (META_G1, g1),
                     (META_G2, g2), (META_R1, r1), (META_R2, r2)):
        meta = jnp.where(lane == idx, val, meta)
    meta_ref[...] = meta


def _router(x, g_all, layer, rw_pad, m, tm):
    T = x.shape[0]
    return pl.pallas_call(
        _router_kernel,
        grid=(T // tm,),
        in_specs=[
            pl.BlockSpec((tm, D_MODEL), lambda i: (i, 0)),
            pl.BlockSpec((None, 1, D_MODEL), lambda i: (layer, 0, 0)),
            pl.BlockSpec((None, D_MODEL, LANES), lambda i: (m, 0, 0)),
        ],
        out_specs=[
            pl.BlockSpec((tm, D_MODEL), lambda i: (i, 0)),
            pl.BlockSpec((tm, LANES), lambda i: (i, 0)),
            pl.BlockSpec((SUBLANES, LANES), lambda i: (0, 0)),
            pl.BlockSpec((tm // TOKEN_BLOCK, SUBLANES, LANES), lambda i: (i, 0, 0)),
        ],
        out_shape=[
            jax.ShapeDtypeStruct((T, D_MODEL), BF16),
            jax.ShapeDtypeStruct((T, LANES), F32),
            jax.ShapeDtypeStruct((SUBLANES, LANES), F32),
            jax.ShapeDtypeStruct((T // TOKEN_BLOCK, SUBLANES, LANES), F32),
        ],
        scratch_shapes=[pltpu.VMEM((SUBLANES, LANES), F32)],
        compiler_params=_params("arbitrary"),
        name="moe_router",
    )(x, g_all, rw_pad)


SLOT_BLOCK = 256
WINDOW_BLOCKS = 10
TOKEN_WINDOW = WINDOW_BLOCKS * TOKEN_BLOCK


def _dispatch_kernel(sb_ref, ws_ref, lo_ref, first_ref, n_ref, h_ref, dest_ref, o_ref):
    i = pl.program_id(0)

    @pl.when(i < n_ref[0])
    def _():
        slot = sb_ref[i] * SLOT_BLOCK + lax.broadcasted_iota(jnp.int32, (SLOT_BLOCK, 1), 0)
        token = ws_ref[i] + lax.broadcasted_iota(jnp.int32, (1, TOKEN_WINDOW), 1)
        dest = jnp.where(token >= lo_ref[i], dest_ref[...], -1)
        hit = (dest[0:1, :] == slot) | (dest[1:2, :] == slot)
        rows = jnp.dot(jnp.where(hit, 1.0, 0.0).astype(BF16), h_ref[...], preferred_element_type=F32)

        @pl.when(first_ref[i] == 1)
        def _():
            o_ref[...] = rows.astype(o_ref.dtype)

        @pl.when(first_ref[i] == 0)
        def _():
            o_ref[...] = (o_ref[...].astype(F32) + rows).astype(o_ref.dtype)


def _dispatch(h, dest_t, items, n_slots):
    def at(i, sb, ws, lo, fi, ni):
        return pl.multiple_of(ws[i], TOKEN_BLOCK)

    grid_spec = pltpu.PrefetchScalarGridSpec(
        num_scalar_prefetch=5,
        grid=(items[0].shape[0],),
        in_specs=[
            pl.BlockSpec((pl.Element(TOKEN_WINDOW), pl.Element(D_MODEL)), lambda *a: (at(*a), 0)),
            pl.BlockSpec((pl.Element(SUBLANES), pl.Element(TOKEN_WINDOW)), lambda *a: (0, at(*a))),
        ],
        out_specs=pl.BlockSpec((SLOT_BLOCK, D_MODEL), lambda i, sb, ws, lo, fi, ni: (sb[i], 0)),
    )
    return pl.pallas_call(
        _dispatch_kernel,
        grid_spec=grid_spec,
        out_shape=jax.ShapeDtypeStruct((n_slots, D_MODEL), BF16),
        compiler_params=_params("arbitrary"),
        name="moe_dispatch",
    )(*items, h, dest_t)


def _dispatch_items(cblk, counts, pstart, pend, block_expert, bm, n_slots, T):
    n_tb = T // TOKEN_BLOCK
    n_sb = n_slots // SLOT_BLOCK
    sb_start = jnp.arange(n_sb, dtype=jnp.int32) * SLOT_BLOCK
    e_sb = block_expert[sb_start // bm]
    r0 = sb_start - pstart[e_sb]
    c_sb = counts[e_sb]
    in_region = sb_start < pend[e_sb]
    has_tokens = r0 < c_sb
    r1 = jnp.minimum(r0 + SLOT_BLOCK, c_sb) - 1
    before = cblk[:, 0, :N_EXPERTS].astype(jnp.int32)[:, e_sb]
    tb_first = jnp.where(has_tokens, jnp.sum(before <= r0[None, :], axis=0) - 1, 0)
    tb_last = jnp.where(has_tokens, jnp.sum(before <= r1[None, :], axis=0) - 1, 0)
    n_win = jnp.where(in_region, (tb_last - tb_first) // WINDOW_BLOCKS + 1, 1)
    ends = jnp.cumsum(n_win)
    total = ends[-1]
    max_items = n_sb + N_EXPERTS * (-(-n_tb // WINDOW_BLOCKS) + 1)
    i = jnp.minimum(jnp.arange(max_items, dtype=jnp.int32), total - 1)
    sb = jnp.sum(ends[None, :] <= i[:, None], axis=1).astype(jnp.int32)
    k = i - (ends - n_win)[sb]
    lo = ((tb_first[sb] + k * WINDOW_BLOCKS) * TOKEN_BLOCK).astype(jnp.int32)
    start = jnp.minimum(lo, T - TOKEN_WINDOW)
    return sb, start, lo, (k == 0).astype(jnp.int32), total.astype(jnp.int32).reshape(1)


def _experts_kernel(be_ref, nu_ref, xs_ref, wg_ref, wu_ref, wd_ref, o_ref, acc_ref):
    b = pl.program_id(0)
    f = pl.program_id(1)
    used = b < nu_ref[0]

    @pl.when(used & (f == 0))
    def _():
        acc_ref[...] = jnp.zeros_like(acc_ref)

    @pl.when(used)
    def _():
        acc_ref[...] = _swiglu_acc(acc_ref[...], xs_ref[...], wg_ref[...], wu_ref[...], wd_ref[...])

        @pl.when(f == pl.num_programs(1) - 1)
        def _():
            o_ref[...] = acc_ref[...].astype(o_ref.dtype)

    @pl.when(jnp.logical_not(used) & (f == 0))
    def _():
        o_ref[...] = jnp.zeros_like(o_ref)


def _experts(xs, block_expert, n_used, wg, wu, wd, m, bm, tf):
    n_blocks = xs.shape[0] // bm
    nf = D_FF // tf

    def row(b, nu):
        return jnp.minimum(b, nu[0] - 1)

    def fidx(b, f, nu):
        return jnp.where(b < nu[0], f, nf - 1)

    grid_spec = pltpu.PrefetchScalarGridSpec(
        num_scalar_prefetch=2,
        grid=(n_blocks, nf),
        in_specs=[
            pl.BlockSpec((bm, D_MODEL), lambda b, f, be, nu: (row(b, nu), 0)),
            pl.BlockSpec((None, None, D_MODEL, tf), lambda b, f, be, nu: (m, be[b], 0, fidx(b, f, nu))),
            pl.BlockSpec((None, None, D_MODEL, tf), lambda b, f, be, nu: (m, be[b], 0, fidx(b, f, nu))),
            pl.BlockSpec((None, None, tf, D_MODEL), lambda b, f, be, nu: (m, be[b], fidx(b, f, nu), 0)),
        ],
        out_specs=pl.BlockSpec((bm, D_MODEL), lambda b, f, be, nu: (b, 0)),
        scratch_shapes=[pltpu.VMEM((bm, D_MODEL), F32)],
    )
    return pl.pallas_call(
        _experts_kernel,
        grid_spec=grid_spec,
        out_shape=jax.ShapeDtypeStruct((n_blocks * bm, D_MODEL), BF16),
        compiler_params=_params("arbitrary", "arbitrary"),
        name="moe_experts",
    )(block_expert, n_used, xs, wg, wu, wd)


BF16_ROWS = 16
WINDOW = TOKEN_BLOCK + BF16_ROWS


def _combine_kernel(win_ref, ps_ref, pe_ref, x_ref, meta_ref, *refs):
    ys_refs, o_ref = refs[:N_EXPERTS], refs[N_EXPERTS]
    blk = pl.program_id(0)
    meta = meta_ref[...]
    e1, e2 = meta[:, META_E1:META_E1 + 1], meta[:, META_E2:META_E2 + 1]
    d1, d2 = meta[:, META_R1:META_R1 + 1], meta[:, META_R2:META_R2 + 1]
    for e in range(N_EXPERTS):
        start = ps_ref[e].astype(F32)
        d1 = d1 + jnp.where(e1 == e, start, 0.0)
        d2 = d2 + jnp.where(e2 == e, start, 0.0)
    d1, d2, g1, g2 = (jnp.broadcast_to(t, (TOKEN_BLOCK, WINDOW))
                      for t in (d1, d2, meta[:, META_G1:META_G1 + 1], meta[:, META_G2:META_G2 + 1]))
    lane = lax.broadcasted_iota(jnp.int32, (1, WINDOW), 1)
    acc = x_ref[...]
    for e in range(N_EXPERTS):
        slot = win_ref[blk * N_EXPERTS + e] + lane
        slot = jnp.where((slot >= ps_ref[e]) & (slot < pe_ref[e]), slot, -1).astype(F32)
        sel = jnp.where(d1 == slot, g1, 0.0) + jnp.where(d2 == slot, g2, 0.0)
        acc = acc + jnp.dot(sel.astype(BF16), ys_refs[e][...], preferred_element_type=F32)
    o_ref[...] = acc


def _combine(x, meta, ys, windows, pstart, pend):
    T = x.shape[0]
    nb = T // TOKEN_BLOCK

    def window(e):
        return pl.BlockSpec((pl.Element(WINDOW), pl.Element(D_MODEL)),
                            lambda i, win, ps, pe: (pl.multiple_of(win[i * N_EXPERTS + e], BF16_ROWS), 0))

    grid_spec = pltpu.PrefetchScalarGridSpec(
        num_scalar_prefetch=3,
        grid=(nb,),
        in_specs=[pl.BlockSpec((TOKEN_BLOCK, D_MODEL), lambda i, win, ps, pe: (i, 0)),
                  pl.BlockSpec((TOKEN_BLOCK, LANES), lambda i, win, ps, pe: (i, 0))]
                 + [window(e) for e in range(N_EXPERTS)],
        out_specs=pl.BlockSpec((TOKEN_BLOCK, D_MODEL), lambda i, win, ps, pe: (i, 0)),
    )
    return pl.pallas_call(
        _combine_kernel,
        grid_spec=grid_spec,
        out_shape=jax.ShapeDtypeStruct((T, D_MODEL), F32),
        compiler_params=_params("arbitrary"),
        name="moe_combine",
    )(windows, pstart, pend, x, meta, *([ys] * N_EXPERTS))


def _final_norm_kernel(x_ref, g_ref, o_ref):
    o_ref[...] = _rms(x_ref[...], g_ref[...])


def _final_norm(x, g, row0, n_rows, tm):
    r0 = row0 // tm
    return pl.pallas_call(
        _final_norm_kernel,
        grid=(n_rows // tm,),
        in_specs=[pl.BlockSpec((tm, D_MODEL), lambda i: (r0 + i, 0)),
                  pl.BlockSpec((1, D_MODEL), lambda i: (0, 0))],
        out_specs=pl.BlockSpec((tm, D_MODEL), lambda i: (i, 0)),
        out_shape=jax.ShapeDtypeStruct((n_rows, D_MODEL), F32),
        compiler_params=_params("parallel"),
        name="final_norm",
    )(x, g)


def _moe_layer(x, norm_ffn_g, layer, rw_pad, wg, wu, wd, m, tm, bm, tf):
    T = x.shape[0]
    h, meta, cnt, cblk = _router(x, norm_ffn_g, layer, rw_pad, m, tm)
    e = meta[:, META_E1:META_E2 + 1].astype(jnp.int32)
    rank = meta[:, META_R1:META_R2 + 1].astype(jnp.int32)
    counts = cnt[0, :N_EXPERTS].astype(jnp.int32)
    padded = (counts + bm - 1) // bm * bm
    pend = jnp.cumsum(padded)
    pstart = pend - padded
    dest = pstart[e] + rank
    n_blocks = (T * 2) // bm + N_EXPERTS + 1
    block_start = jnp.arange(n_blocks, dtype=jnp.int32) * bm
    block_expert = jnp.minimum(jnp.sum(pend[None, :] <= block_start[:, None], axis=1),
                               N_EXPERTS - 1).astype(jnp.int32)
    n_used = (pend[-1] // bm).astype(jnp.int32).reshape(1)
    dest_t = jnp.full((SUBLANES, T), -1, jnp.int32).at[0:2].set(dest.T)
    items = _dispatch_items(cblk, counts, pstart, pend, block_expert, bm, n_blocks * bm, T)
    xs = _dispatch(h, dest_t, items, n_blocks * bm)
    ys = _experts(xs, block_expert, n_used, wg, wu, wd, m, bm, tf)
    run_start = pstart[None, :] + cblk[:, 0, :N_EXPERTS].astype(jnp.int32)
    windows = (run_start // BF16_ROWS * BF16_ROWS).reshape(-1).astype(jnp.int32)
    return _combine(x, meta, ys, windows, pstart.astype(jnp.int32), pend.astype(jnp.int32))


def _tril(w):
    n = w.shape[-1]
    return jnp.where(jnp.tril(jnp.ones((n, n), dtype=bool)), w, jnp.zeros((), w.dtype))


def _mix_tables(w_s, b_s, sample_len):
    reps = A_CHUNK // sample_len
    eye = jnp.eye(reps, dtype=w_s.dtype)
    w_prompt = _tril(w_s)
    w_small = _tril(w_s[:, :sample_len, :sample_len])
    w_sample = jax.vmap(lambda w: jnp.kron(eye, w))(w_small)
    b_prompt = b_s
    b_sample = jnp.tile(b_s[:, :sample_len], (1, reps))
    wmix = jnp.stack([w_prompt, w_sample])
    bias = jnp.stack([b_prompt, b_sample])[..., None]
    return wmix, jnp.broadcast_to(bias, bias.shape[:-1] + (LANES,))


def kernel(x_prompt, x_sample, state_hgrn, norm_mix_g, norm_ffn_g, final_norm_g, a_w_in, a_ln_g, a_ln_b, a_w_s, a_b_s, a_w_out, b_w_in, b_lb_logits, b_norm_g, b_w_out, ffn_w_gate, ffn_w_up, ffn_w_down, moe_router, moe_w_gate, moe_w_up, moe_w_down):
    n_p, L, d = x_prompt.shape
    n_s, l_s, _ = x_sample.shape
    assert d == D_MODEL and l_s == SAMPLE_LEN and L % A_CHUNK == 0
    T_p, T_s = n_p * L, n_s * l_s
    T = T_p + T_s
    depth = norm_mix_g.shape[0]

    tm_gate = 2 * A_CHUNK
    assert T_p % tm_gate == 0 and T_s % tm_gate == 0
    tm = next(t for t in (768, 512, 256, 128) if T % t == 0)
    tm_big = next(t for t in (1536, 768, 512, 256, 128) if T % t == 0)
    bm = 1024
    tf = 512
    lb_rows = min(L, 512)

    x = jnp.concatenate([x_prompt.reshape(T_p, d), x_sample.reshape(T_s, d)], axis=0)

    p = jax.nn.softmax(b_lb_logits.astype(F32), axis=0)
    lbs = (jnp.cumsum(p, axis=0) - p[0:1])[:, None, :]
    mix_g = norm_mix_g[:, None, :]
    ffn_g = norm_ffn_g[:, None, :]
    b_ng = b_norm_g[:, None, :]
    ln_g = a_ln_g[:, None, :]
    ln_b = a_ln_b[:, None, :]
    rw_pad = jnp.pad(moe_router, ((0, 0), (0, 0), (0, LANES - N_EXPERTS)))
    a_w_in, a_w_out, b_w_in, b_w_out, ffn_w_gate, ffn_w_up, ffn_w_down = (
        w.astype(BF16) for w in (a_w_in, a_w_out, b_w_in, b_w_out, ffn_w_gate, ffn_w_up, ffn_w_down))

    hg_prompt, v_sample = [], []
    hg_sample = state_hgrn
    for layer in range(depth):
        j = layer // 2
        if layer % 2 == 0:
            z = _norm_matmul(x, mix_g, layer, a_w_in, j, _gelu, BF16, tm_big, 1024)
            wmix, bias = _mix_tables(a_w_s[j], a_b_s[j], l_s)
            x, v = _gmlp_gate(z, x, ln_g, ln_b, wmix, bias, a_w_out, j, T_p // tm_gate, tm_gate)
            v_sample.append(v.reshape(n_s, l_s, A_HALF))
            x = _ffn_dense(x, ffn_g, layer, ffn_w_gate, ffn_w_up, ffn_w_down, j, tm_big, tf)
        else:
            proj = _norm_matmul(x, mix_g, layer, b_w_in, j, _identity, F32, tm_big, 1024)
            o_p, s_p = _hgrn_prompt(proj, lbs, b_ng, j, n_p, L, lb_rows)
            o_s, hg_sample = _hgrn_sample(proj, T_p, hg_sample, lbs, b_ng, j, n_s, 8)
            hg_prompt.append(s_p)
            o = jnp.concatenate([o_p, o_s], axis=0)
            x = _matmul_res(o, b_w_out, j, x, tm)
            x = _moe_layer(x, ffn_g, layer, rw_pad, moe_w_gate, moe_w_up, moe_w_down, j, 512, bm, tf)

    fg = final_norm_g[None, :]
    y_prompt = _final_norm(x, fg, 0, T_p, 512).reshape(n_p, L, d)
    y_sample = _final_norm(x, fg, T_p, T_s, T_s).reshape(n_s, l_s, d)
    return (y_prompt, y_sample, jnp.stack(hg_prompt), hg_sample, jnp.stack(v_sample))
```

```python
import functools
import math

import jax
import jax.numpy as jnp
from jax import lax
from jax.experimental import pallas as pl
from jax.experimental.pallas import tpu as pltpu

F32 = jnp.float32
BF16 = jnp.bfloat16

D_MODEL = 1024
A_CHUNK = 128
A_HALF = 3 * D_MODEL
A_GROUPS = 8
A_GROUP_DIM = A_HALF // A_GROUPS
HG_HEADS = 8
HG_DK = 128
HG_CHUNK = 64
HG_SUB = 16
FORGET_FLOOR = 1e-20
D_FF = 7 * D_MODEL // 2
N_EXPERTS = 8
NORM_EPS = 1e-6
LANES = 128
SUBLANES = 8
SAMPLE_LEN = 4
MASKED_LOG = -1e30

VMEM_LIMIT = 56 * 1024 * 1024


def _params(*sem):
    return pltpu.CompilerParams(dimension_semantics=sem, vmem_limit_bytes=VMEM_LIMIT)


def _rms(x, g):
    ms = jnp.mean(x * x, axis=-1, keepdims=True)
    return x * lax.rsqrt(ms + NORM_EPS) * g


def _gelu(y):
    return 0.5 * y * (1.0 + lax.erf(y * math.sqrt(0.5)))


def _identity(y):
    return y


def _norm_matmul_kernel(x_ref, g_ref, w_ref, o_ref, h_ref, *, act):
    @pl.when(pl.program_id(1) == 0)
    def _():
        h_ref[...] = _rms(x_ref[...], g_ref[...]).astype(BF16)

    y = jnp.dot(h_ref[...], w_ref[...].astype(BF16), preferred_element_type=F32)
    o_ref[...] = act(y).astype(o_ref.dtype)


def _norm_matmul(x, g_all, layer, w_all, w_layer, act, out_dtype, tm, tn):
    T = x.shape[0]
    N = w_all.shape[-1]
    return pl.pallas_call(
        functools.partial(_norm_matmul_kernel, act=act),
        grid=(T // tm, N // tn),
        in_specs=[
            pl.BlockSpec((tm, D_MODEL), lambda i, j: (i, 0)),
            pl.BlockSpec((None, 1, D_MODEL), lambda i, j: (layer, 0, 0)),
            pl.BlockSpec((None, D_MODEL, tn), lambda i, j: (w_layer, 0, j)),
        ],
        out_specs=pl.BlockSpec((tm, tn), lambda i, j: (i, j)),
        out_shape=jax.ShapeDtypeStruct((T, N), out_dtype),
        scratch_shapes=[pltpu.VMEM((tm, D_MODEL), BF16)],
        compiler_params=_params("parallel", "arbitrary"),
        name="norm_matmul",
    )(x, g_all, w_all)


def _gmlp_gate_kernel(z_ref, lng_ref, lnb_ref, wmix_ref, bias_ref, wout_ref, x_ref, o_ref, v_ref):
    tm = z_ref.shape[0]
    zv = z_ref[:, A_HALF:].astype(F32)
    mu = jnp.mean(zv, axis=-1, keepdims=True)
    xc = zv - mu
    rstd = lax.rsqrt(jnp.mean(xc * xc, axis=-1, keepdims=True) + NORM_EPS)
    v = xc * rstd * lng_ref[...] + lnb_ref[...]
    v_ref[...] = v
    vb = v.astype(BF16)
    groups = []
    for g in range(A_GROUPS):
        cols = slice(g * A_GROUP_DIM, (g + 1) * A_GROUP_DIM)
        wm = wmix_ref[g].astype(BF16)
        bias = jnp.concatenate([bias_ref[g]] * (A_GROUP_DIM // LANES), axis=1)
        rows = []
        for c in range(tm // A_CHUNK):
            sl = slice(c * A_CHUNK, (c + 1) * A_CHUNK)
            s = jnp.dot(wm, vb[sl, cols], preferred_element_type=F32) + bias
            rows.append((z_ref[sl, cols].astype(F32) * s).astype(BF16))
        groups.append(jnp.concatenate(rows, axis=0))
    gated = jnp.concatenate(groups, axis=1)
    o_ref[...] = x_ref[...] + jnp.dot(gated, wout_ref[...], preferred_element_type=F32)


def _gmlp_gate(z, x, ln_g, ln_b, wmix, bias, w_out, j, n_prompt_blocks, tm):
    T = x.shape[0]
    nb = T // tm

    def kind(i):
        return jnp.where(i >= n_prompt_blocks, 1, 0)

    return pl.pallas_call(
        _gmlp_gate_kernel,
        grid=(nb,),
        in_specs=[
            pl.BlockSpec((tm, 2 * A_HALF), lambda i: (i, 0)),
            pl.BlockSpec((None, 1, A_HALF), lambda i: (j, 0, 0)),
            pl.BlockSpec((None, 1, A_HALF), lambda i: (j, 0, 0)),
            pl.BlockSpec((None, A_GROUPS, A_CHUNK, A_CHUNK), lambda i: (kind(i), 0, 0, 0)),
            pl.BlockSpec((None, A_GROUPS, A_CHUNK, LANES), lambda i: (kind(i), 0, 0, 0)),
            pl.BlockSpec((None, A_HALF, D_MODEL), lambda i: (j, 0, 0)),
            pl.BlockSpec((tm, D_MODEL), lambda i: (i, 0)),
        ],
        out_specs=[
            pl.BlockSpec((tm, D_MODEL), lambda i: (i, 0)),
            pl.BlockSpec((tm, A_HALF), lambda i: (jnp.maximum(i - n_prompt_blocks, 0), 0)),
        ],
        out_shape=[
            jax.ShapeDtypeStruct((T, D_MODEL), F32),
            jax.ShapeDtypeStruct(((nb - n_prompt_blocks) * tm, A_HALF), F32),
        ],
        compiler_params=_params("arbitrary"),
        name="gmlp_gate",
    )(z, ln_g, ln_b, wmix, bias, w_out, x)


FF_SPLIT = 2


def _swiglu_acc(acc, h, wg, wu, wd):
    sub = wg.shape[1] // FF_SPLIT
    for c in range(FF_SPLIT):
        cols = slice(c * sub, (c + 1) * sub)
        a = jnp.dot(h, wg[:, cols].astype(BF16), preferred_element_type=F32)
        b = jnp.dot(h, wu[:, cols].astype(BF16), preferred_element_type=F32)
        m = (a * jax.nn.sigmoid(a) * b).astype(BF16)
        acc = acc + jnp.dot(m, wd[cols, :].astype(BF16), preferred_element_type=F32)
    return acc


def _ffn_kernel(x_ref, g_ref, wg_ref, wu_ref, wd_ref, o_ref, h_ref):
    @pl.when(pl.program_id(1) == 0)
    def _():
        x = x_ref[...]
        h_ref[...] = _rms(x, g_ref[...]).astype(BF16)
        o_ref[...] = x

    o_ref[...] = _swiglu_acc(o_ref[...], h_ref[...], wg_ref[...], wu_ref[...], wd_ref[...])


def _ffn_dense(x, g_all, layer, wg, wu, wd, m, tm, tf):
    T = x.shape[0]
    return pl.pallas_call(
        _ffn_kernel,
        grid=(T // tm, D_FF // tf),
        in_specs=[
            pl.BlockSpec((tm, D_MODEL), lambda i, f: (i, 0)),
            pl.BlockSpec((None, 1, D_MODEL), lambda i, f: (layer, 0, 0)),
            pl.BlockSpec((None, D_MODEL, tf), lambda i, f: (m, 0, f)),
            pl.BlockSpec((None, D_MODEL, tf), lambda i, f: (m, 0, f)),
            pl.BlockSpec((None, tf, D_MODEL), lambda i, f: (m, f, 0)),
        ],
        out_specs=pl.BlockSpec((tm, D_MODEL), lambda i, f: (i, 0)),
        out_shape=jax.ShapeDtypeStruct((T, D_MODEL), F32),
        scratch_shapes=[pltpu.VMEM((tm, D_MODEL), BF16)],
        compiler_params=_params("parallel", "arbitrary"),
        name="ffn_dense",
    )(x, g_all, wg, wu, wd)


def _matmul_res_kernel(ap_ref, as_ref, w_ref, x_ref, o_ref, *, n_prompt_blocks):
    a = jnp.where(pl.program_id(0) < n_prompt_blocks, ap_ref[...], as_ref[...])
    o_ref[...] = x_ref[...] + jnp.dot(a, w_ref[...], preferred_element_type=F32)


def _matmul_res(a_prompt, a_sample, w_all, j, x, tm):
    T = x.shape[0]
    npb = a_prompt.shape[0] // tm
    assert a_prompt.shape[0] % tm == 0 and a_sample.shape[0] % tm == 0
    return pl.pallas_call(
        functools.partial(_matmul_res_kernel, n_prompt_blocks=npb),
        grid=(T // tm,),
        in_specs=[
            pl.BlockSpec((tm, D_MODEL), lambda i: (jnp.minimum(i, npb - 1), 0)),
            pl.BlockSpec((tm, D_MODEL), lambda i: (jnp.maximum(i - npb, 0), 0)),
            pl.BlockSpec((None, D_MODEL, D_MODEL), lambda i: (j, 0, 0)),
            pl.BlockSpec((tm, D_MODEL), lambda i: (i, 0)),
        ],
        out_specs=pl.BlockSpec((tm, D_MODEL), lambda i: (i, 0)),
        out_shape=jax.ShapeDtypeStruct((T, D_MODEL), F32),
        compiler_params=_params("parallel"),
        name="matmul_res",
    )(a_prompt, a_sample, w_all, x)


_NT = (((1,), (1,)), ((), ()))
_TN = (((0,), (0,)), ((), ()))


HG_GROUP = 8
MAX_BLOCK_DECAY = 75.0


def _hgrn_gates(qpre, fpre, lb):
    q = qpre * jax.nn.sigmoid(qpre)
    sig = jax.nn.sigmoid(fpre)
    f = lb + (1.0 - lb) * sig
    logf = jnp.log(jnp.maximum(f, FORGET_FLOOR))
    k = (1.0 - lb) * (1.0 - sig)
    return q, k, logf


def _cumsum_rows(x):
    C = x.shape[0]
    if C >= HG_SUB:
        r = lax.broadcasted_iota(jnp.int32, (C, C), 0)
        c = lax.broadcasted_iota(jnp.int32, (C, C), 1)
        tri = jnp.where(r >= c, 1.0, 0.0).astype(F32)
        return jnp.dot(tri, x, preferred_element_type=F32, precision=lax.Precision.HIGHEST)
    row = lax.broadcasted_iota(jnp.int32, (C, 1), 0)
    out = jnp.zeros_like(x)
    for s in range(C):
        out = out + jnp.where(row >= s, x[s:s + 1], 0.0)
    return out


def _chunk_prep(q, k, logf):
    C = q.shape[0]
    b = _cumsum_rows(logf)
    b_last = b[C - 1:C]
    qe = (q * jnp.exp(b)).astype(BF16)
    kd = (k * jnp.exp(b_last - b)).astype(BF16)
    return b, qe, kd, jnp.exp(b_last)


def _intra_exact(q, k, v, b, sub):
    C = q.shape[0]
    row = lax.broadcasted_iota(jnp.int32, (sub, 1), 0)
    parts = []
    for blk in range(C // sub):
        lo = blk * sub
        b_i, q_i, k_i, v_i = b[lo:lo + sub], q[lo:lo + sub], k[lo:lo + sub], v[lo:lo + sub]
        if blk == 0:
            o_i = jnp.zeros((sub, HG_DK), F32)
        else:
            ref = b[lo - 1:lo]
            qs = (q_i * jnp.exp(b_i - ref)).astype(BF16)
            ks = (k[:lo] * jnp.exp(ref - b[:lo])).astype(BF16)
            a = lax.dot_general(qs, ks, _NT, preferred_element_type=F32)
            o_i = jnp.dot(a.astype(BF16), v[:lo].astype(BF16), preferred_element_type=F32)
        for s in range(sub):
            rel = jnp.where(row >= s, b_i - b_i[s:s + 1], MASKED_LOG)
            a_col = jnp.sum(q_i * k_i[s:s + 1] * jnp.exp(rel), axis=-1, keepdims=True)
            o_i = o_i + a_col * v_i[s:s + 1]
        parts.append(o_i)
    return parts[0] if len(parts) == 1 else jnp.concatenate(parts, axis=0)


def _intra_factored(q, k, v, b, sub):
    C = q.shape[0]
    vb = v.astype(BF16)
    heads = [slice(h * HG_DK, (h + 1) * HG_DK) for h in range(q.shape[1] // HG_DK)]
    scores = []
    for blk in range(C // sub):
        lo, hi = blk * sub, (blk + 1) * sub
        if blk == 0:
            qs = q[:hi] * jnp.exp(b[:hi])
            ks = k[:hi] * jnp.exp(-b[:hi])
        else:
            ref = b[lo - 1:lo]
            qs = q[lo:hi] * jnp.exp(b[lo:hi] - ref)
            ks = k[:hi] * jnp.exp(ref - b[:hi])
        qs, ks = qs.astype(BF16), ks.astype(BF16)
        r = lax.broadcasted_iota(jnp.int32, (sub, hi), 0)
        c = lax.broadcasted_iota(jnp.int32, (sub, hi), 1)
        row = []
        for cols in heads:
            a = lax.dot_general(qs[:, cols], ks[:, cols], _NT, preferred_element_type=F32)
            row.append(jnp.where(c <= r + lo, a, 0.0).astype(BF16))
        scores.append(row)
    parts = []
    for blk, row in enumerate(scores):
        hi = (blk + 1) * sub
        outs = [jnp.dot(a, vb[:hi, cols], preferred_element_type=F32) for a, cols in zip(row, heads)]
        parts.append(outs[0] if len(outs) == 1 else jnp.concatenate(outs, axis=1))
    return jnp.concatenate(parts, axis=0)


def _head_norm_gate(o, ng, gate):
    return (o * lax.rsqrt(jnp.mean(o * o, axis=-1, keepdims=True) + NORM_EPS) * ng * gate)


def _hgrn_prompt_kernel(q_ref, f_ref, i_ref, g_ref, lb_ref, ng_ref, o_ref, sfin_ref,
                        st_ref, qs_ref, ks_ref, lf_ref, qe_ref, kd_ref, eb_ref, oi_ref):
    tb = pl.program_id(2)
    rows_total = q_ref.shape[0]
    n_chunks = rows_total // HG_CHUNK

    @pl.when(tb == 0)
    def _():
        st_ref[...] = jnp.zeros_like(st_ref)

    q, k, logf = _hgrn_gates(q_ref[...], f_ref[...], lb_ref[...])
    qs_ref[...] = q
    ks_ref[...] = k
    lf_ref[...] = logf
    block_decay = jnp.sum(logf.reshape(rows_total // HG_SUB, HG_SUB, logf.shape[-1]), axis=1)
    mild = jnp.min(block_decay) >= -MAX_BLOCK_DECAY

    heads = [slice(h * HG_DK, (h + 1) * HG_DK) for h in range(HG_GROUP)]

    def exact_group(qg, kg, vg, b, sub):
        return jnp.concatenate([_intra_exact(qg[:, c], kg[:, c], vg[:, c], b[:, c], sub) for c in heads], axis=1)

    def intra_pass(intra):
        def body(ci, carry):
            rows = pl.ds(pl.multiple_of(ci * HG_CHUNK, HG_CHUNK), HG_CHUNK)
            qg, kg, vg = qs_ref[rows, :], ks_ref[rows, :], i_ref[rows, :]
            b, qe, kd, eb_last = _chunk_prep(qg, kg, lf_ref[rows, :])
            qe_ref[rows, :] = qe
            kd_ref[rows, :] = kd
            eb_ref[pl.ds(pl.multiple_of(ci * SUBLANES, SUBLANES), SUBLANES), :] = \
                jnp.broadcast_to(eb_last, (SUBLANES, eb_last.shape[1]))
            oi_ref[rows, :] = intra(qg, kg, vg, b, HG_SUB)
            return carry
        lax.fori_loop(0, n_chunks, body, 0, unroll=2)

    @pl.when(mild)
    def _():
        intra_pass(_intra_factored)

    @pl.when(jnp.logical_not(mild))
    def _():
        intra_pass(exact_group)

    ng = ng_ref[...]

    def state_body(ci, carry):
        rows = pl.ds(pl.multiple_of(ci * HG_CHUNK, HG_CHUNK), HG_CHUNK)
        st = st_ref[...]
        stb = st.astype(BF16)
        qe, kd, oi = qe_ref[rows, :], kd_ref[rows, :], oi_ref[rows, :]
        vb = i_ref[rows, :].astype(BF16)
        eb_last = eb_ref[pl.ds(pl.multiple_of(ci * SUBLANES, SUBLANES), 1), :]
        outs, adds = [], []
        for c in heads:
            o = oi[:, c] + lax.dot_general(qe[:, c], stb[:, c], _NT, preferred_element_type=F32)
            outs.append(o * lax.rsqrt(jnp.mean(o * o, axis=-1, keepdims=True) + NORM_EPS))
            adds.append(lax.dot_general(vb[:, c], kd[:, c], _TN, preferred_element_type=F32))
        st_ref[...] = st * eb_last + jnp.concatenate(adds, axis=1)
        gate = jax.nn.sigmoid(g_ref[rows, :])
        o_ref[rows, :] = (jnp.concatenate(outs, axis=1) * ng * gate).astype(o_ref.dtype)
        return carry

    lax.fori_loop(0, n_chunks, state_body, 0)

    @pl.when(tb == pl.num_programs(2) - 1)
    def _():
        for h, c in enumerate(heads):
            sfin_ref[h] = st_ref[:, c].T


def _hgrn_prompt(proj, lbs, ng_all, j, n_seq, L, lb_rows):
    nb = L // lb_rows
    ng_groups = HG_HEADS // HG_GROUP
    gw = HG_GROUP * HG_DK

    def field(k):
        return pl.BlockSpec((lb_rows, gw), lambda n, h, t: (n * nb + t, k * ng_groups + h))

    vec = pl.BlockSpec((None, 1, gw), lambda n, h, t: (j, 0, h))
    return pl.pallas_call(
        _hgrn_prompt_kernel,
        grid=(n_seq, ng_groups, nb),
        in_specs=[field(0), field(1), field(2), field(3), vec, vec],
        out_specs=[
            pl.BlockSpec((lb_rows, gw), lambda n, h, t: (n * nb + t, h)),
            pl.BlockSpec((None, HG_GROUP, HG_DK, HG_DK), lambda n, h, t: (n, h, 0, 0)),
        ],
        out_shape=[
            jax.ShapeDtypeStruct((n_seq * L, D_MODEL), BF16),
            jax.ShapeDtypeStruct((n_seq, HG_HEADS, HG_DK, HG_DK), F32),
        ],
        scratch_shapes=[
            pltpu.VMEM((HG_DK, gw), F32),
            pltpu.VMEM((lb_rows, gw), F32),
            pltpu.VMEM((lb_rows, gw), F32),
            pltpu.VMEM((lb_rows, gw), F32),
            pltpu.VMEM((lb_rows, gw), BF16),
            pltpu.VMEM((lb_rows, gw), BF16),
            pltpu.VMEM((lb_rows // HG_CHUNK * SUBLANES, gw), F32),
            pltpu.VMEM((lb_rows, gw), F32),
        ],
        compiler_params=_params("parallel", "parallel", "arbitrary"),
        name="hgrn_prompt",
    )(proj, proj, proj, proj, lbs, ng_all)


def _hgrn_sample_kernel(q_ref, f_ref, i_ref, g_ref, lb_ref, ng_ref, s0_ref, o_ref, sfin_ref):
    n_seq = s0_ref.shape[0]
    q, k, logf = _hgrn_gates(q_ref[...], f_ref[...], lb_ref[...])
    v = i_ref[...]
    row = lax.broadcasted_iota(jnp.int32, (SUBLANES, 1), 0)
    first_half = row < SAMPLE_LEN
    prep = []
    for seq in range(n_seq):
        rows = slice(seq // 2 * SUBLANES, (seq // 2 + 1) * SUBLANES)
        mine = first_half if seq % 2 == 0 else jnp.logical_not(first_half)
        qm, km, vm, lm = (jnp.where(mine, t[rows], 0.0) for t in (q, k, v, logf))
        b, qe, kd, eb_last = _chunk_prep(qm, km, lm)
        prep.append((qe, kd, eb_last, vm.astype(BF16), _intra_exact(qm, km, vm, b, SUBLANES)))
    states = [s0_ref[seq].T for seq in range(n_seq)]
    outs = [oi + lax.dot_general(qe, st.astype(BF16), _NT, preferred_element_type=F32)
            for (qe, _, _, _, oi), st in zip(prep, states)]
    news = [st * eb_last + lax.dot_general(vb, kd, _TN, preferred_element_type=F32)
            for (_, kd, eb_last, vb, _), st in zip(prep, states)]
    sfin_ref[...] = jnp.stack([st.T for st in news])
    tiles = [jnp.where(first_half, outs[2 * p], outs[2 * p + 1]) for p in range(n_seq // 2)]
    gate = jax.nn.sigmoid(g_ref[...])
    o_ref[...] = _head_norm_gate(jnp.concatenate(tiles, axis=0), ng_ref[...], gate).astype(o_ref.dtype)


def _hgrn_sample(proj, row0, state_all, lbs, ng_all, j, n_seq, sb):
    H = HG_HEADS
    rb = sb * SAMPLE_LEN
    assert row0 % rb == 0 and sb % 2 == 0
    r0 = row0 // rb

    def field(k):
        return pl.BlockSpec((rb, HG_DK), lambda s, h: (r0 + s, k * H + h))

    vec = pl.BlockSpec((None, 1, HG_DK), lambda s, h: (j, 0, h))
    return pl.pallas_call(
        _hgrn_sample_kernel,
        grid=(n_seq // sb, H),
        in_specs=[field(0), field(1), field(2), field(3), vec, vec,
                  pl.BlockSpec((None, sb, None, HG_DK, HG_DK), lambda s, h: (j, s, h, 0, 0))],
        out_specs=[
            pl.BlockSpec((rb, HG_DK), lambda s, h: (s, h)),
            pl.BlockSpec((None, sb, None, HG_DK, HG_DK), lambda s, h: (j, s, h, 0, 0)),
        ],
        out_shape=[
            jax.ShapeDtypeStruct((n_seq * SAMPLE_LEN, D_MODEL), BF16),
            jax.ShapeDtypeStruct(state_all.shape, state_all.dtype),
        ],
        input_output_aliases={6: 1},
        compiler_params=_params("parallel", "parallel"),
        name="hgrn_sample",
    )(proj, proj, proj, proj, lbs, ng_all, state_all)


META_E1, META_E2, META_G1, META_G2, META_R1, META_R2 = range(6)
TOKEN_BLOCK = 128


def _router_kernel(x_ref, g_ref, rw_ref, h_ref, meta_ref, cnt_ref, cb_ref, carry_ref):
    i = pl.program_id(0)
    tm = x_ref.shape[0]

    @pl.when(i == 0)
    def _():
        carry_ref[...] = jnp.zeros_like(carry_ref)

    h = _rms(x_ref[...], g_ref[...])
    h_ref[...] = h.astype(h_ref.dtype)
    logits = jnp.dot(h.astype(BF16), rw_ref[...].astype(BF16), preferred_element_type=F32)
    lane = lax.broadcasted_iota(jnp.int32, (tm, LANES), 1)
    neg = -jnp.inf
    lg = jnp.where(lane < N_EXPERTS, logits, neg)
    m1 = jnp.max(lg, axis=-1, keepdims=True)
    e1 = jnp.min(jnp.where(lg == m1, lane, LANES), axis=-1, keepdims=True)
    lg2 = jnp.where(lane == e1, neg, lg)
    m2 = jnp.max(lg2, axis=-1, keepdims=True)
    e2 = jnp.min(jnp.where(lg2 == m2, lane, LANES), axis=-1, keepdims=True)
    ex = jnp.exp(m2 - m1)
    g1 = 1.0 / (1.0 + ex)
    g2 = ex / (1.0 + ex)

    onehot = jnp.where((lane == e1) | (lane == e2), 1.0, 0.0)
    r = lax.broadcasted_iota(jnp.int32, (tm, tm), 0)
    c = lax.broadcasted_iota(jnp.int32, (tm, tm), 1)
    before = jnp.where(r > c, 1.0, 0.0).astype(BF16)
    seen = jnp.dot(before, onehot.astype(BF16), preferred_element_type=F32) + carry_ref[0:1, :]
    r1 = jnp.sum(jnp.where(lane == e1, seen, 0.0), axis=-1, keepdims=True)
    r2 = jnp.sum(jnp.where(lane == e2, seen, 0.0), axis=-1, keepdims=True)
    for blk in range(tm // TOKEN_BLOCK):
        cb_ref[blk] = jnp.broadcast_to(seen[blk * TOKEN_BLOCK:blk * TOKEN_BLOCK + 1], (SUBLANES, LANES))
    total = carry_ref[0:1, :] + jnp.sum(onehot, axis=0, keepdims=True)
    carry_ref[...] = jnp.broadcast_to(total, carry_ref.shape)
    cnt_ref[...] = jnp.broadcast_to(total, cnt_ref.shape)

    meta = jnp.zeros((tm, LANES), F32)
    for idx, val in ((META_E1, e1.astype(F32)), (META_E2, e2.astype(F32)), (META_G1, g1),
                     (META_G2, g2), (META_R1, r1), (META_R2, r2)):
        meta = jnp.where(lane == idx, val, meta)
    meta_ref[...] = meta


def _router(x, g_all, layer, rw_pad, m, tm):
    T = x.shape[0]
    return pl.pallas_call(
        _router_kernel,
        grid=(T // tm,),
        in_specs=[
            pl.BlockSpec((tm, D_MODEL), lambda i: (i, 0)),
            pl.BlockSpec((None, 1, D_MODEL), lambda i: (layer, 0, 0)),
            pl.BlockSpec((None, D_MODEL, LANES), lambda i: (m, 0, 0)),
        ],
        out_specs=[
            pl.BlockSpec((tm, D_MODEL), lambda i: (i, 0)),
            pl.BlockSpec((tm, LANES), lambda i: (i, 0)),
            pl.BlockSpec((SUBLANES, LANES), lambda i: (0, 0)),
            pl.BlockSpec((tm // TOKEN_BLOCK, SUBLANES, LANES), lambda i: (i, 0, 0)),
        ],
        out_shape=[
            jax.ShapeDtypeStruct((T, D_MODEL), BF16),
            jax.ShapeDtypeStruct((T, LANES), F32),
            jax.ShapeDtypeStruct((SUBLANES, LANES), F32),
            jax.ShapeDtypeStruct((T // TOKEN_BLOCK, SUBLANES, LANES), F32),
        ],
        scratch_shapes=[pltpu.VMEM((SUBLANES, LANES), F32)],
        compiler_params=_params("arbitrary"),
        name="moe_router",
    )(x, g_all, rw_pad)


SLOT_BLOCK = 256
WINDOW_BLOCKS = 10
TOKEN_WINDOW = WINDOW_BLOCKS * TOKEN_BLOCK


def _dispatch_kernel(sb_ref, ws_ref, lo_ref, first_ref, n_ref, h_ref, dest_ref, o_ref):
    i = pl.program_id(0)

    @pl.when(i < n_ref[0])
    def _():
        slot = sb_ref[i] * SLOT_BLOCK + lax.broadcasted_iota(jnp.int32, (SLOT_BLOCK, 1), 0)
        token = ws_ref[i] + lax.broadcasted_iota(jnp.int32, (1, TOKEN_WINDOW), 1)
        dest = jnp.where(token >= lo_ref[i], dest_ref[...], -1)
        hit = (dest[0:1, :] == slot) | (dest[1:2, :] == slot)
        rows = jnp.dot(jnp.where(hit, 1.0, 0.0).astype(BF16), h_ref[...], preferred_element_type=F32)

        @pl.when(first_ref[i] == 1)
        def _():
            o_ref[...] = rows.astype(o_ref.dtype)

        @pl.when(first_ref[i] == 0)
        def _():
            o_ref[...] = (o_ref[...].astype(F32) + rows).astype(o_ref.dtype)


def _dispatch(h, dest_t, items, n_slots):
    def at(i, sb, ws, lo, fi, ni):
        return pl.multiple_of(ws[i], TOKEN_BLOCK)

    grid_spec = pltpu.PrefetchScalarGridSpec(
        num_scalar_prefetch=5,
        grid=(items[0].shape[0],),
        in_specs=[
            pl.BlockSpec((pl.Element(TOKEN_WINDOW), pl.Element(D_MODEL)), lambda *a: (at(*a), 0)),
            pl.BlockSpec((pl.Element(SUBLANES), pl.Element(TOKEN_WINDOW)), lambda *a: (0, at(*a))),
        ],
        out_specs=pl.BlockSpec((SLOT_BLOCK, D_MODEL), lambda i, sb, ws, lo, fi, ni: (sb[i], 0)),
    )
    return pl.pallas_call(
        _dispatch_kernel,
        grid_spec=grid_spec,
        out_shape=jax.ShapeDtypeStruct((n_slots, D_MODEL), BF16),
        compiler_params=_params("arbitrary"),
        name="moe_dispatch",
    )(*items, h, dest_t)


def _dispatch_items(cblk, counts, pstart, pend, block_expert, bm, n_slots, T):
    n_tb = T // TOKEN_BLOCK
    n_sb = n_slots // SLOT_BLOCK
    sb_start = jnp.arange(n_sb, dtype=jnp.int32) * SLOT_BLOCK
    e_sb = block_expert[sb_start // bm]
    r0 = sb_start - pstart[e_sb]
    c_sb = counts[e_sb]
    in_region = sb_start < pend[e_sb]
    has_tokens = r0 < c_sb
    r1 = jnp.minimum(r0 + SLOT_BLOCK, c_sb) - 1
    before = cblk[:, 0, :N_EXPERTS].astype(jnp.int32)[:, e_sb]
    tb_first = jnp.where(has_tokens, jnp.sum(before <= r0[None, :], axis=0) - 1, 0)
    tb_last = jnp.where(has_tokens, jnp.sum(before <= r1[None, :], axis=0) - 1, 0)
    n_win = jnp.where(in_region, (tb_last - tb_first) // WINDOW_BLOCKS + 1, 1)
    ends = jnp.cumsum(n_win)
    total = ends[-1]
    max_items = n_sb + N_EXPERTS * (-(-n_tb // WINDOW_BLOCKS) + 1)
    i = jnp.minimum(jnp.arange(max_items, dtype=jnp.int32), total - 1)
    sb = jnp.sum(ends[None, :] <= i[:, None], axis=1).astype(jnp.int32)
    k = i - (ends - n_win)[sb]
    lo = ((tb_first[sb] + k * WINDOW_BLOCKS) * TOKEN_BLOCK).astype(jnp.int32)
    start = jnp.minimum(lo, T - TOKEN_WINDOW)
    return sb, start, lo, (k == 0).astype(jnp.int32), total.astype(jnp.int32).reshape(1)


def _experts_kernel(be_ref, nu_ref, xs_ref, wg_ref, wu_ref, wd_ref, o_ref, acc_ref):
    b = pl.program_id(0)
    f = pl.program_id(1)
    used = b < nu_ref[0]

    @pl.when(used & (f == 0))
    def _():
        acc_ref[...] = jnp.zeros_like(acc_ref)

    @pl.when(used)
    def _():
        acc_ref[...] = _swiglu_acc(acc_ref[...], xs_ref[...], wg_ref[...], wu_ref[...], wd_ref[...])

        @pl.when(f == pl.num_programs(1) - 1)
        def _():
            o_ref[...] = acc_ref[...].astype(o_ref.dtype)

    @pl.when(jnp.logical_not(used) & (f == 0))
    def _():
        o_ref[...] = jnp.zeros_like(o_ref)


def _experts(xs, block_expert, n_used, wg, wu, wd, m, bm, tf):
    n_blocks = xs.shape[0] // bm
    nf = D_FF // tf

    def row(b, nu):
        return jnp.minimum(b, nu[0] - 1)

    def fidx(b, f, nu):
        return jnp.where(b < nu[0], f, nf - 1)

    grid_spec = pltpu.PrefetchScalarGridSpec(
        num_scalar_prefetch=2,
        grid=(n_blocks, nf),
        in_specs=[
            pl.BlockSpec((bm, D_MODEL), lambda b, f, be, nu: (row(b, nu), 0)),
            pl.BlockSpec((None, None, D_MODEL, tf), lambda b, f, be, nu: (m, be[b], 0, fidx(b, f, nu))),
            pl.BlockSpec((None, None, D_MODEL, tf), lambda b, f, be, nu: (m, be[b], 0, fidx(b, f, nu))),
            pl.BlockSpec((None, None, tf, D_MODEL), lambda b, f, be, nu: (m, be[b], fidx(b, f, nu), 0)),
        ],
        out_specs=pl.BlockSpec((bm, D_MODEL), lambda b, f, be, nu: (b, 0)),
        scratch_shapes=[pltpu.VMEM((bm, D_MODEL), F32)],
    )
    return pl.pallas_call(
        _experts_kernel,
        grid_spec=grid_spec,
        out_shape=jax.ShapeDtypeStruct((n_blocks * bm, D_MODEL), BF16),
        compiler_params=_params("arbitrary", "arbitrary"),
        name="moe_experts",
    )(block_expert, n_used, xs, wg, wu, wd)


BF16_ROWS = 16
WINDOW = TOKEN_BLOCK + BF16_ROWS


def _combine_kernel(win_ref, ps_ref, pe_ref, x_ref, meta_ref, *refs):
    ys_refs, o_ref = refs[:N_EXPERTS], refs[N_EXPERTS]
    blk = pl.program_id(0)
    meta = meta_ref[...]
    e1, e2 = meta[:, META_E1:META_E1 + 1], meta[:, META_E2:META_E2 + 1]
    d1, d2 = meta[:, META_R1:META_R1 + 1], meta[:, META_R2:META_R2 + 1]
    for e in range(N_EXPERTS):
        start = ps_ref[e].astype(F32)
        d1 = d1 + jnp.where(e1 == e, start, 0.0)
        d2 = d2 + jnp.where(e2 == e, start, 0.0)
    d1, d2, g1, g2 = (jnp.broadcast_to(t, (TOKEN_BLOCK, WINDOW))
                      for t in (d1, d2, meta[:, META_G1:META_G1 + 1], meta[:, META_G2:META_G2 + 1]))
    lane = lax.broadcasted_iota(jnp.int32, (1, WINDOW), 1)
    acc = x_ref[...]
    for e in range(N_EXPERTS):
        slot = win_ref[blk * N_EXPERTS + e] + lane
        slot = jnp.where((slot >= ps_ref[e]) & (slot < pe_ref[e]), slot, -1).astype(F32)
        sel = jnp.where(d1 == slot, g1, 0.0) + jnp.where(d2 == slot, g2, 0.0)
        acc = acc + jnp.dot(sel.astype(BF16), ys_refs[e][...], preferred_element_type=F32)
    o_ref[...] = acc


def _combine(x, meta, ys, windows, pstart, pend):
    T = x.shape[0]
    nb = T // TOKEN_BLOCK

    def window(e):
        return pl.BlockSpec((pl.Element(WINDOW), pl.Element(D_MODEL)),
                            lambda i, win, ps, pe: (pl.multiple_of(win[i * N_EXPERTS + e], BF16_ROWS), 0))

    grid_spec = pltpu.PrefetchScalarGridSpec(
        num_scalar_prefetch=3,
        grid=(nb,),
        in_specs=[pl.BlockSpec((TOKEN_BLOCK, D_MODEL), lambda i, win, ps, pe: (i, 0)),
                  pl.BlockSpec((TOKEN_BLOCK, LANES), lambda i, win, ps, pe: (i, 0))]
                 + [window(e) for e in range(N_EXPERTS)],
        out_specs=pl.BlockSpec((TOKEN_BLOCK, D_MODEL), lambda i, win, ps, pe: (i, 0)),
    )
    return pl.pallas_call(
        _combine_kernel,
        grid_spec=grid_spec,
        out_shape=jax.ShapeDtypeStruct((T, D_MODEL), F32),
        compiler_params=_params("arbitrary"),
        name="moe_combine",
    )(windows, pstart, pend, x, meta, *([ys] * N_EXPERTS))


def _final_norm_kernel(x_ref, g_ref, o_ref):
    o_ref[...] = _rms(x_ref[...], g_ref[...])


def _final_norm(x, g, row0, n_rows, tm):
    r0 = row0 // tm
    return pl.pallas_call(
        _final_norm_kernel,
        grid=(n_rows // tm,),
        in_specs=[pl.BlockSpec((tm, D_MODEL), lambda i: (r0 + i, 0)),
                  pl.BlockSpec((1, D_MODEL), lambda i: (0, 0))],
        out_specs=pl.BlockSpec((tm, D_MODEL), lambda i: (i, 0)),
        out_shape=jax.ShapeDtypeStruct((n_rows, D_MODEL), F32),
        compiler_params=_params("parallel"),
        name="final_norm",
    )(x, g)


def _moe_layer(x, norm_ffn_g, layer, rw_pad, wg, wu, wd, m, tm, bm, tf):
    T = x.shape[0]
    h, meta, cnt, cblk = _router(x, norm_ffn_g, layer, rw_pad, m, tm)
    e = meta[:, META_E1:META_E2 + 1].astype(jnp.int32)
    rank = meta[:, META_R1:META_R2 + 1].astype(jnp.int32)
    counts = cnt[0, :N_EXPERTS].astype(jnp.int32)
    padded = (counts + bm - 1) // bm * bm
    pend = jnp.cumsum(padded)
    pstart = pend - padded
    dest = pstart[e] + rank
    n_blocks = (T * 2) // bm + N_EXPERTS + 1
    block_start = jnp.arange(n_blocks, dtype=jnp.int32) * bm
    block_expert = jnp.minimum(jnp.sum(pend[None, :] <= block_start[:, None], axis=1),
                               N_EXPERTS - 1).astype(jnp.int32)
    n_used = (pend[-1] // bm).astype(jnp.int32).reshape(1)
    dest_t = jnp.full((SUBLANES, T), -1, jnp.int32).at[0:2].set(dest.T)
    items = _dispatch_items(cblk, counts, pstart, pend, block_expert, bm, n_blocks * bm, T)
    xs = _dispatch(h, dest_t, items, n_blocks * bm)
    ys = _experts(xs, block_expert, n_used, wg, wu, wd, m, bm, tf)
    run_start = pstart[None, :] + cblk[:, 0, :N_EXPERTS].astype(jnp.int32)
    windows = (run_start // BF16_ROWS * BF16_ROWS).reshape(-1).astype(jnp.int32)
    return _combine(x, meta, ys, windows, pstart.astype(jnp.int32), pend.astype(jnp.int32))


def _tril(w):
    n = w.shape[-1]
    return jnp.where(jnp.tril(jnp.ones((n, n), dtype=bool)), w, jnp.zeros((), w.dtype))


def _mix_tables(w_s, b_s, sample_len):
    reps = A_CHUNK // sample_len
    eye = jnp.eye(reps, dtype=w_s.dtype)
    w_prompt = _tril(w_s)
    w_small = _tril(w_s[:, :sample_len, :sample_len])
    w_sample = jax.vmap(lambda w: jnp.kron(eye, w))(w_small)
    b_prompt = b_s
    b_sample = jnp.tile(b_s[:, :sample_len], (1, reps))
    wmix = jnp.stack([w_prompt, w_sample])
    bias = jnp.stack([b_prompt, b_sample])[..., None]
    return wmix, jnp.broadcast_to(bias, bias.shape[:-1] + (LANES,))


def kernel(x_prompt, x_sample, state_hgrn, norm_mix_g, norm_ffn_g, final_norm_g, a_w_in, a_ln_g, a_ln_b, a_w_s, a_b_s, a_w_out, b_w_in, b_lb_logits, b_norm_g, b_w_out, ffn_w_gate, ffn_w_up, ffn_w_down, moe_router, moe_w_gate, moe_w_up, moe_w_down):
    n_p, L, d = x_prompt.shape
    n_s, l_s, _ = x_sample.shape
    assert d == D_MODEL and l_s == SAMPLE_LEN and L % A_CHUNK == 0
    T_p, T_s = n_p * L, n_s * l_s
    T = T_p + T_s
    depth = norm_mix_g.shape[0]

    tm_gate = 2 * A_CHUNK
    assert T_p % tm_gate == 0 and T_s % tm_gate == 0
    tm_big = next(t for t in (1536, 768, 512, 256, 128) if T % t == 0)
    bm = 1536
    tf = 512
    lb_rows = min(L, 512)

    x = jnp.concatenate([x_prompt.reshape(T_p, d), x_sample.reshape(T_s, d)], axis=0)

    p = jax.nn.softmax(b_lb_logits.astype(F32), axis=0)
    lbs = (jnp.cumsum(p, axis=0) - p[0:1])[:, None, :]
    mix_g = norm_mix_g[:, None, :]
    ffn_g = norm_ffn_g[:, None, :]
    b_ng = b_norm_g[:, None, :]
    ln_g = a_ln_g[:, None, :]
    ln_b = a_ln_b[:, None, :]
    rw_pad = jnp.pad(moe_router, ((0, 0), (0, 0), (0, LANES - N_EXPERTS)))
    a_w_in, a_w_out, b_w_in, b_w_out, ffn_w_gate, ffn_w_up, ffn_w_down = (
        w.astype(BF16) for w in (a_w_in, a_w_out, b_w_in, b_w_out, ffn_w_gate, ffn_w_up, ffn_w_down))

    hg_prompt, v_sample = [], []
    hg_sample = state_hgrn
    for layer in range(depth):
        j = layer // 2
        if layer % 2 == 0:
            z = _norm_matmul(x, mix_g, layer, a_w_in, j, _gelu, BF16, tm_big, 1024)
            wmix, bias = _mix_tables(a_w_s[j], a_b_s[j], l_s)
            x, v = _gmlp_gate(z, x, ln_g, ln_b, wmix, bias, a_w_out, j, T_p // tm_gate, tm_gate)
            v_sample.append(v.reshape(n_s, l_s, A_HALF))
            x = _ffn_dense(x, ffn_g, layer, ffn_w_gate, ffn_w_up, ffn_w_down, j, tm_big, tf)
        else:
            proj = _norm_matmul(x, mix_g, layer, b_w_in, j, _identity, F32, tm_big, 1024)
            o_p, s_p = _hgrn_prompt(proj, lbs, b_ng, j, n_p, L, lb_rows)
            o_s, hg_sample = _hgrn_sample(proj, T_p, hg_sample, lbs, b_ng, j, n_s, 8)
            hg_prompt.append(s_p)
            x = _matmul_res(o_p, o_s, b_w_out, j, x, math.gcd(T_p, T_s, 512))
            x = _moe_layer(x, ffn_g, layer, rw_pad, moe_w_gate, moe_w_up, moe_w_down, j, 512, bm, tf)

    fg = final_norm_g[None, :]
    y_prompt = _final_norm(x, fg, 0, T_p, 512).reshape(n_p, L, d)
    y_sample = _final_norm(x, fg, T_p, T_s, T_s).reshape(n_s, l_s, d)
    return (y_prompt, y_sample, jnp.stack(hg_prompt), hg_sample, jnp.stack(v_sample))
```

```python
import functools
import math

import jax
import jax.numpy as jnp
from jax import lax
from jax.experimental import pallas as pl
from jax.experimental.pallas import tpu as pltpu

F32 = jnp.float32
BF16 = jnp.bfloat16

D_MODEL = 1024
A_CHUNK = 128
A_HALF = 3 * D_MODEL
A_GROUPS = 8
A_GROUP_DIM = A_HALF // A_GROUPS
HG_HEADS = 8
HG_DK = 128
HG_CHUNK = 64
HG_SUB = 16
FORGET_FLOOR = 1e-20
D_FF = 7 * D_MODEL // 2
N_EXPERTS = 8
NORM_EPS = 1e-6
LANES = 128
SUBLANES = 8
SAMPLE_LEN = 4
MASKED_LOG = -1e30

VMEM_LIMIT = 56 * 1024 * 1024


def _params(*sem):
    return pltpu.CompilerParams(dimension_semantics=sem, vmem_limit_bytes=VMEM_LIMIT)


def _rms(x, g):
    ms = jnp.mean(x * x, axis=-1, keepdims=True)
    return x * lax.rsqrt(ms + NORM_EPS) * g


def _gelu(y):
    return 0.5 * y * (1.0 + lax.erf(y * math.sqrt(0.5)))


def _identity(y):
    return y


def _norm_matmul_kernel(x_ref, g_ref, w_ref, o_ref, h_ref, *, act):
    @pl.when(pl.program_id(1) == 0)
    def _():
        h_ref[...] = _rms(x_ref[...], g_ref[...]).astype(BF16)

    y = jnp.dot(h_ref[...], w_ref[...].astype(BF16), preferred_element_type=F32)
    o_ref[...] = act(y).astype(o_ref.dtype)


def _norm_matmul(x, g_all, layer, w_all, w_layer, act, out_dtype, tm, tn):
    T = x.shape[0]
    N = w_all.shape[-1]
    return pl.pallas_call(
        functools.partial(_norm_matmul_kernel, act=act),
        grid=(T // tm, N // tn),
        in_specs=[
            pl.BlockSpec((tm, D_MODEL), lambda i, j: (i, 0)),
            pl.BlockSpec((None, 1, D_MODEL), lambda i, j: (layer, 0, 0)),
            pl.BlockSpec((None, D_MODEL, tn), lambda i, j: (w_layer, 0, j)),
        ],
        out_specs=pl.BlockSpec((tm, tn), lambda i, j: (i, j)),
        out_shape=jax.ShapeDtypeStruct((T, N), out_dtype),
        scratch_shapes=[pltpu.VMEM((tm, D_MODEL), BF16)],
        compiler_params=_params("parallel", "arbitrary"),
        name="norm_matmul",
    )(x, g_all, w_all)


def _gmlp_gate_kernel(z_ref, lng_ref, lnb_ref, wmix_ref, bias_ref, wout_ref, x_ref, o_ref, v_ref):
    tm = z_ref.shape[0]
    zv = z_ref[:, A_HALF:].astype(F32)
    mu = jnp.mean(zv, axis=-1, keepdims=True)
    xc = zv - mu
    rstd = lax.rsqrt(jnp.mean(xc * xc, axis=-1, keepdims=True) + NORM_EPS)
    v = xc * rstd * lng_ref[...] + lnb_ref[...]
    v_ref[...] = v
    vb = v.astype(BF16)
    groups = []
    for g in range(A_GROUPS):
        cols = slice(g * A_GROUP_DIM, (g + 1) * A_GROUP_DIM)
        wm = wmix_ref[g].astype(BF16)
        bias = jnp.concatenate([bias_ref[g]] * (A_GROUP_DIM // LANES), axis=1)
        rows = []
        for c in range(tm // A_CHUNK):
            sl = slice(c * A_CHUNK, (c + 1) * A_CHUNK)
            s = jnp.dot(wm, vb[sl, cols], preferred_element_type=F32) + bias
            rows.append((z_ref[sl, cols].astype(F32) * s).astype(BF16))
        groups.append(jnp.concatenate(rows, axis=0))
    gated = jnp.concatenate(groups, axis=1)
    o_ref[...] = x_ref[...] + jnp.dot(gated, wout_ref[...], preferred_element_type=F32)


def _gmlp_gate(z, x, ln_g, ln_b, wmix, bias, w_out, j, n_prompt_blocks, tm):
    T = x.shape[0]
    nb = T // tm

    def kind(i):
        return jnp.where(i >= n_prompt_blocks, 1, 0)

    return pl.pallas_call(
        _gmlp_gate_kernel,
        grid=(nb,),
        in_specs=[
            pl.BlockSpec((tm, 2 * A_HALF), lambda i: (i, 0)),
            pl.BlockSpec((None, 1, A_HALF), lambda i: (j, 0, 0)),
            pl.BlockSpec((None, 1, A_HALF), lambda i: (j, 0, 0)),
            pl.BlockSpec((None, A_GROUPS, A_CHUNK, A_CHUNK), lambda i: (kind(i), 0, 0, 0)),
            pl.BlockSpec((None, A_GROUPS, A_CHUNK, LANES), lambda i: (kind(i), 0, 0, 0)),
            pl.BlockSpec((None, A_HALF, D_MODEL), lambda i: (j, 0, 0)),
            pl.BlockSpec((tm, D_MODEL), lambda i: (i, 0)),
        ],
        out_specs=[
            pl.BlockSpec((tm, D_MODEL), lambda i: (i, 0)),
            pl.BlockSpec((tm, A_HALF), lambda i: (jnp.maximum(i - n_prompt_blocks, 0), 0)),
        ],
        out_shape=[
            jax.ShapeDtypeStruct((T, D_MODEL), F32),
            jax.ShapeDtypeStruct(((nb - n_prompt_blocks) * tm, A_HALF), F32),
        ],
        compiler_params=_params("arbitrary"),
        name="gmlp_gate",
    )(z, ln_g, ln_b, wmix, bias, w_out, x)


FF_SPLIT = 2


def _swiglu_acc(acc, h, wg, wu, wd):
    sub = wg.shape[1] // FF_SPLIT
    for c in range(FF_SPLIT):
        cols = slice(c * sub, (c + 1) * sub)
        a = jnp.dot(h, wg[:, cols].astype(BF16), preferred_element_type=F32)
        b = jnp.dot(h, wu[:, cols].astype(BF16), preferred_element_type=F32)
        m = (a * jax.nn.sigmoid(a) * b).astype(BF16)
        acc = acc + jnp.dot(m, wd[cols, :].astype(BF16), preferred_element_type=F32)
    return acc


def _ffn_kernel(x_ref, g_ref, wg_ref, wu_ref, wd_ref, o_ref, h_ref):
    @pl.when(pl.program_id(1) == 0)
    def _():
        x = x_ref[...]
        h_ref[...] = _rms(x, g_ref[...]).astype(BF16)
        o_ref[...] = x

    o_ref[...] = _swiglu_acc(o_ref[...], h_ref[...], wg_ref[...], wu_ref[...], wd_ref[...])


def _ffn_dense(x, g_all, layer, wg, wu, wd, m, tm, tf):
    T = x.shape[0]
    return pl.pallas_call(
        _ffn_kernel,
        grid=(T // tm, D_FF // tf),
        in_specs=[
            pl.BlockSpec((tm, D_MODEL), lambda i, f: (i, 0)),
            pl.BlockSpec((None, 1, D_MODEL), lambda i, f: (layer, 0, 0)),
            pl.BlockSpec((None, D_MODEL, tf), lambda i, f: (m, 0, f)),
            pl.BlockSpec((None, D_MODEL, tf), lambda i, f: (m, 0, f)),
            pl.BlockSpec((None, tf, D_MODEL), lambda i, f: (m, f, 0)),
        ],
        out_specs=pl.BlockSpec((tm, D_MODEL), lambda i, f: (i, 0)),
        out_shape=jax.ShapeDtypeStruct((T, D_MODEL), F32),
        scratch_shapes=[pltpu.VMEM((tm, D_MODEL), BF16)],
        compiler_params=_params("parallel", "arbitrary"),
        name="ffn_dense",
    )(x, g_all, wg, wu, wd)


def _matmul_res_kernel(ap_ref, as_ref, w_ref, x_ref, o_ref, *, n_prompt_blocks):
    a = jnp.where(pl.program_id(0) < n_prompt_blocks, ap_ref[...], as_ref[...])
    o_ref[...] = x_ref[...] + jnp.dot(a, w_ref[...], preferred_element_type=F32)


def _matmul_res(a_prompt, a_sample, w_all, j, x, tm):
    T = x.shape[0]
    npb = a_prompt.shape[0] // tm
    assert a_prompt.shape[0] % tm == 0 and a_sample.shape[0] % tm == 0
    return pl.pallas_call(
        functools.partial(_matmul_res_kernel, n_prompt_blocks=npb),
        grid=(T // tm,),
        in_specs=[
            pl.BlockSpec((tm, D_MODEL), lambda i: (jnp.minimum(i, npb - 1), 0)),
            pl.BlockSpec((tm, D_MODEL), lambda i: (jnp.maximum(i - npb, 0), 0)),
            pl.BlockSpec((None, D_MODEL, D_MODEL), lambda i: (j, 0, 0)),
            pl.BlockSpec((tm, D_MODEL), lambda i: (i, 0)),
        ],
        out_specs=pl.BlockSpec((tm, D_MODEL), lambda i: (i, 0)),
        out_shape=jax.ShapeDtypeStruct((T, D_MODEL), F32),
        compiler_params=_params("parallel"),
        name="matmul_res",
    )(a_prompt, a_sample, w_all, x)


_NT = (((1,), (1,)), ((), ()))
_TN = (((0,), (0,)), ((), ()))


HG_GROUP = 8
MAX_BLOCK_DECAY = 75.0


def _hgrn_gates(qpre, fpre, lb):
    q = qpre * jax.nn.sigmoid(qpre)
    sig = jax.nn.sigmoid(fpre)
    f = lb + (1.0 - lb) * sig
    logf = jnp.log(jnp.maximum(f, FORGET_FLOOR))
    k = (1.0 - lb) * (1.0 - sig)
    return q, k, logf


def _cumsum_rows(x):
    C = x.shape[0]
    if C >= HG_SUB:
        r = lax.broadcasted_iota(jnp.int32, (C, C), 0)
        c = lax.broadcasted_iota(jnp.int32, (C, C), 1)
        tri = jnp.where(r >= c, 1.0, 0.0).astype(F32)
        return jnp.dot(tri, x, preferred_element_type=F32, precision=lax.Precision.HIGHEST)
    row = lax.broadcasted_iota(jnp.int32, (C, 1), 0)
    out = jnp.zeros_like(x)
    for s in range(C):
        out = out + jnp.where(row >= s, x[s:s + 1], 0.0)
    return out


def _chunk_prep(q, k, logf):
    C = q.shape[0]
    b = _cumsum_rows(logf)
    b_last = b[C - 1:C]
    qe = (q * jnp.exp(b)).astype(BF16)
    kd = (k * jnp.exp(b_last - b)).astype(BF16)
    return b, qe, kd, jnp.exp(b_last)


def _intra_exact(q, k, v, b, sub):
    C = q.shape[0]
    row = lax.broadcasted_iota(jnp.int32, (sub, 1), 0)
    parts = []
    for blk in range(C // sub):
        lo = blk * sub
        b_i, q_i, k_i, v_i = b[lo:lo + sub], q[lo:lo + sub], k[lo:lo + sub], v[lo:lo + sub]
        if blk == 0:
            o_i = jnp.zeros((sub, HG_DK), F32)
        else:
            ref = b[lo - 1:lo]
            qs = (q_i * jnp.exp(b_i - ref)).astype(BF16)
            ks = (k[:lo] * jnp.exp(ref - b[:lo])).astype(BF16)
            a = lax.dot_general(qs, ks, _NT, preferred_element_type=F32)
            o_i = jnp.dot(a.astype(BF16), v[:lo].astype(BF16), preferred_element_type=F32)
        for s in range(sub):
            rel = jnp.where(row >= s, b_i - b_i[s:s + 1], MASKED_LOG)
            a_col = jnp.sum(q_i * k_i[s:s + 1] * jnp.exp(rel), axis=-1, keepdims=True)
            o_i = o_i + a_col * v_i[s:s + 1]
        parts.append(o_i)
    return parts[0] if len(parts) == 1 else jnp.concatenate(parts, axis=0)


def _intra_factored(q, k, v, b, sub):
    C = q.shape[0]
    vb = v.astype(BF16)
    heads = [slice(h * HG_DK, (h + 1) * HG_DK) for h in range(q.shape[1] // HG_DK)]
    scores = []
    for blk in range(C // sub):
        lo, hi = blk * sub, (blk + 1) * sub
        if blk == 0:
            qs = q[:hi] * jnp.exp(b[:hi])
            ks = k[:hi] * jnp.exp(-b[:hi])
        else:
            ref = b[lo - 1:lo]
            qs = q[lo:hi] * jnp.exp(b[lo:hi] - ref)
            ks = k[:hi] * jnp.exp(ref - b[:hi])
        qs, ks = qs.astype(BF16), ks.astype(BF16)
        r = lax.broadcasted_iota(jnp.int32, (sub, hi), 0)
        c = lax.broadcasted_iota(jnp.int32, (sub, hi), 1)
        row = []
        for cols in heads:
            a = lax.dot_general(qs[:, cols], ks[:, cols], _NT, preferred_element_type=F32)
            row.append(jnp.where(c <= r + lo, a, 0.0).astype(BF16))
        scores.append(row)
    parts = []
    for blk, row in enumerate(scores):
        hi = (blk + 1) * sub
        outs = [jnp.dot(a, vb[:hi, cols], preferred_element_type=F32) for a, cols in zip(row, heads)]
        parts.append(outs[0] if len(outs) == 1 else jnp.concatenate(outs, axis=1))
    return jnp.concatenate(parts, axis=0)


def _head_norm_gate(o, ng, gate):
    return (o * lax.rsqrt(jnp.mean(o * o, axis=-1, keepdims=True) + NORM_EPS) * ng * gate)


def _hgrn_prompt_kernel(q_ref, f_ref, i_ref, g_ref, lb_ref, ng_ref, o_ref, sfin_ref,
                        st_ref, qs_ref, ks_ref, lf_ref, qe_ref, kd_ref, eb_ref, oi_ref):
    tb = pl.program_id(2)
    rows_total = q_ref.shape[0]
    n_chunks = rows_total // HG_CHUNK

    @pl.when(tb == 0)
    def _():
        st_ref[...] = jnp.zeros_like(st_ref)

    q, k, logf = _hgrn_gates(q_ref[...], f_ref[...], lb_ref[...])
    qs_ref[...] = q
    ks_ref[...] = k
    lf_ref[...] = logf
    block_decay = jnp.sum(logf.reshape(rows_total // HG_SUB, HG_SUB, logf.shape[-1]), axis=1)
    mild = jnp.min(block_decay) >= -MAX_BLOCK_DECAY

    heads = [slice(h * HG_DK, (h + 1) * HG_DK) for h in range(HG_GROUP)]

    def exact_group(qg, kg, vg, b, sub):
        return jnp.concatenate([_intra_exact(qg[:, c], kg[:, c], vg[:, c], b[:, c], sub) for c in heads], axis=1)

    def intra_pass(intra):
        def body(ci, carry):
            rows = pl.ds(pl.multiple_of(ci * HG_CHUNK, HG_CHUNK), HG_CHUNK)
            qg, kg, vg = qs_ref[rows, :], ks_ref[rows, :], i_ref[rows, :]
            b, qe, kd, eb_last = _chunk_prep(qg, kg, lf_ref[rows, :])
            qe_ref[rows, :] = qe
            kd_ref[rows, :] = kd
            eb_ref[pl.ds(pl.multiple_of(ci * SUBLANES, SUBLANES), SUBLANES), :] = \
                jnp.broadcast_to(eb_last, (SUBLANES, eb_last.shape[1]))
            oi_ref[rows, :] = intra(qg, kg, vg, b, HG_SUB)
            return carry
        lax.fori_loop(0, n_chunks, body, 0, unroll=2)

    @pl.when(mild)
    def _():
        intra_pass(_intra_factored)

    @pl.when(jnp.logical_not(mild))
    def _():
        intra_pass(exact_group)

    ng = ng_ref[...]

    def state_body(ci, carry):
        rows = pl.ds(pl.multiple_of(ci * HG_CHUNK, HG_CHUNK), HG_CHUNK)
        st = st_ref[...]
        stb = st.astype(BF16)
        qe, kd, oi = qe_ref[rows, :], kd_ref[rows, :], oi_ref[rows, :]
        vb = i_ref[rows, :].astype(BF16)
        eb_last = eb_ref[pl.ds(pl.multiple_of(ci * SUBLANES, SUBLANES), 1), :]
        outs, adds = [], []
        for c in heads:
            o = oi[:, c] + lax.dot_general(qe[:, c], stb[:, c], _NT, preferred_element_type=F32)
            outs.append(o * lax.rsqrt(jnp.mean(o * o, axis=-1, keepdims=True) + NORM_EPS))
            adds.append(lax.dot_general(vb[:, c], kd[:, c], _TN, preferred_element_type=F32))
        st_ref[...] = st * eb_last + jnp.concatenate(adds, axis=1)
        gate = jax.nn.sigmoid(g_ref[rows, :])
        o_ref[rows, :] = (jnp.concatenate(outs, axis=1) * ng * gate).astype(o_ref.dtype)
        return carry

    lax.fori_loop(0, n_chunks, state_body, 0)

    @pl.when(tb == pl.num_programs(2) - 1)
    def _():
        for h, c in enumerate(heads):
            sfin_ref[h] = st_ref[:, c].T


def _hgrn_prompt(proj, lbs, ng_all, j, n_seq, L, lb_rows):
    nb = L // lb_rows
    ng_groups = HG_HEADS // HG_GROUP
    gw = HG_GROUP * HG_DK

    def field(k):
        return pl.BlockSpec((lb_rows, gw), lambda n, h, t: (n * nb + t, k * ng_groups + h))

    vec = pl.BlockSpec((None, 1, gw), lambda n, h, t: (j, 0, h))
    return pl.pallas_call(
        _hgrn_prompt_kernel,
        grid=(n_seq, ng_groups, nb),
        in_specs=[field(0), field(1), field(2), field(3), vec, vec],
        out_specs=[
            pl.BlockSpec((lb_rows, gw), lambda n, h, t: (n * nb + t, h)),
            pl.BlockSpec((None, HG_GROUP, HG_DK, HG_DK), lambda n, h, t: (n, h, 0, 0)),
        ],
        out_shape=[
            jax.ShapeDtypeStruct((n_seq * L, D_MODEL), BF16),
            jax.ShapeDtypeStruct((n_seq, HG_HEADS, HG_DK, HG_DK), F32),
        ],
        scratch_shapes=[
            pltpu.VMEM((HG_DK, gw), F32),
            pltpu.VMEM((lb_rows, gw), F32),
            pltpu.VMEM((lb_rows, gw), F32),
            pltpu.VMEM((lb_rows, gw), F32),
            pltpu.VMEM((lb_rows, gw), BF16),
            pltpu.VMEM((lb_rows, gw), BF16),
            pltpu.VMEM((lb_rows // HG_CHUNK * SUBLANES, gw), F32),
            pltpu.VMEM((lb_rows, gw), F32),
        ],
        compiler_params=_params("parallel", "parallel", "arbitrary"),
        name="hgrn_prompt",
    )(proj, proj, proj, proj, lbs, ng_all)


def _hgrn_sample_kernel(q_ref, f_ref, i_ref, g_ref, lb_ref, ng_ref, s0_ref, o_ref, sfin_ref):
    n_seq = s0_ref.shape[0]
    q, k, logf = _hgrn_gates(q_ref[...], f_ref[...], lb_ref[...])
    v = i_ref[...]
    row = lax.broadcasted_iota(jnp.int32, (SUBLANES, 1), 0)
    first_half = row < SAMPLE_LEN
    prep = []
    for seq in range(n_seq):
        rows = slice(seq // 2 * SUBLANES, (seq // 2 + 1) * SUBLANES)
        mine = first_half if seq % 2 == 0 else jnp.logical_not(first_half)
        qm, km, vm, lm = (jnp.where(mine, t[rows], 0.0) for t in (q, k, v, logf))
        b, qe, kd, eb_last = _chunk_prep(qm, km, lm)
        prep.append((qe, kd, eb_last, vm.astype(BF16), _intra_exact(qm, km, vm, b, SUBLANES)))
    states = [s0_ref[seq].T for seq in range(n_seq)]
    outs = [oi + lax.dot_general(qe, st.astype(BF16), _NT, preferred_element_type=F32)
            for (qe, _, _, _, oi), st in zip(prep, states)]
    news = [st * eb_last + lax.dot_general(vb, kd, _TN, preferred_element_type=F32)
            for (_, kd, eb_last, vb, _), st in zip(prep, states)]
    sfin_ref[...] = jnp.stack([st.T for st in news])
    tiles = [jnp.where(first_half, outs[2 * p], outs[2 * p + 1]) for p in range(n_seq // 2)]
    gate = jax.nn.sigmoid(g_ref[...])
    o_ref[...] = _head_norm_gate(jnp.concatenate(tiles, axis=0), ng_ref[...], gate).astype(o_ref.dtype)


def _hgrn_sample(proj, row0, state_all, lbs, ng_all, j, n_seq, sb):
    H = HG_HEADS
    rb = sb * SAMPLE_LEN
    assert row0 % rb == 0 and sb % 2 == 0
    r0 = row0 // rb

    def field(k):
        return pl.BlockSpec((rb, HG_DK), lambda s, h: (r0 + s, k * H + h))

    vec = pl.BlockSpec((None, 1, HG_DK), lambda s, h: (j, 0, h))
    return pl.pallas_call(
        _hgrn_sample_kernel,
        grid=(n_seq // sb, H),
        in_specs=[field(0), field(1), field(2), field(3), vec, vec,
                  pl.BlockSpec((None, sb, None, HG_DK, HG_DK), lambda s, h: (j, s, h, 0, 0))],
        out_specs=[
            pl.BlockSpec((rb, HG_DK), lambda s, h: (s, h)),
            pl.BlockSpec((None, sb, None, HG_DK, HG_DK), lambda s, h: (j, s, h, 0, 0)),
        ],
        out_shape=[
            jax.ShapeDtypeStruct((n_seq * SAMPLE_LEN, D_MODEL), BF16),
            jax.ShapeDtypeStruct(state_all.shape, state_all.dtype),
        ],
        input_output_aliases={6: 1},
        compiler_params=_params("parallel", "parallel"),
        name="hgrn_sample",
    )(proj, proj, proj, proj, lbs, ng_all, state_all)


META_E1, META_E2, META_G1, META_G2, META_R1, META_R2 = range(6)
TOKEN_BLOCK = 128


def _router_kernel(x_ref, g_ref, rw_ref, h_ref, meta_ref, cnt_ref, cb_ref, carry_ref):
    i = pl.program_id(0)
    tm = x_ref.shape[0]

    @pl.when(i == 0)
    def _():
        carry_ref[...] = jnp.zeros_like(carry_ref)

    h = _rms(x_ref[...], g_ref[...])
    h_ref[...] = h.astype(h_ref.dtype)
    logits = jnp.dot(h.astype(BF16), rw_ref[...].astype(BF16), preferred_element_type=F32)
    lane = lax.broadcasted_iota(jnp.int32, (tm, LANES), 1)
    neg = -jnp.inf
    lg = jnp.where(lane < N_EXPERTS, logits, neg)
    m1 = jnp.max(lg, axis=-1, keepdims=True)
    e1 = jnp.min(jnp.where(lg == m1, lane, LANES), axis=-1, keepdims=True)
    lg2 = jnp.where(lane == e1, neg, lg)
    m2 = jnp.max(lg2, axis=-1, keepdims=True)
    e2 = jnp.min(jnp.where(lg2 == m2, lane, LANES), axis=-1, keepdims=True)
    ex = jnp.exp(m2 - m1)
    g1 = 1.0 / (1.0 + ex)
    g2 = ex / (1.0 + ex)

    onehot = jnp.where((lane == e1) | (lane == e2), 1.0, 0.0)
    r = lax.broadcasted_iota(jnp.int32, (tm, tm), 0)
    c = lax.broadcasted_iota(jnp.int32, (tm, tm), 1)
    before = jnp.where(r > c, 1.0, 0.0).astype(BF16)
    seen = jnp.dot(before, onehot.astype(BF16), preferred_element_type=F32) + carry_ref[0:1, :]
    r1 = jnp.sum(jnp.where(lane == e1, seen, 0.0), axis=-1, keepdims=True)
    r2 = jnp.sum(jnp.where(lane == e2, seen, 0.0), axis=-1, keepdims=True)
    for blk in range(tm // TOKEN_BLOCK):
        cb_ref[blk] = jnp.broadcast_to(seen[blk * TOKEN_BLOCK:blk * TOKEN_BLOCK + 1], (SUBLANES, LANES))
    total = carry_ref[0:1, :] + jnp.sum(onehot, axis=0, keepdims=True)
    carry_ref[...] = jnp.broadcast_to(total, carry_ref.shape)
    cnt_ref[...] = jnp.broadcast_to(total, cnt_ref.shape)

    meta = jnp.zeros((tm, LANES), F32)
    for idx, val in ((META_E1, e1.astype(F32)), (META_E2, e2.astype(F32)), (META_G1, g1),
                     (META_G2, g2), (META_R1, r1), (META_R2, r2)):
        meta = jnp.where(lane == idx, val, meta)
    meta_ref[...] = meta


def _router(x, g_all, layer, rw_pad, m, tm):
    T = x.shape[0]
    return pl.pallas_call(
        _router_kernel,
        grid=(T // tm,),
        in_specs=[
            pl.BlockSpec((tm, D_MODEL), lambda i: (i, 0)),
            pl.BlockSpec((None, 1, D_MODEL), lambda i: (layer, 0, 0)),
            pl.BlockSpec((None, D_MODEL, LANES), lambda i: (m, 0, 0)),
        ],
        out_specs=[
            pl.BlockSpec((tm, D_MODEL), lambda i: (i, 0)),
            pl.BlockSpec((tm, LANES), lambda i: (i, 0)),
            pl.BlockSpec((SUBLANES, LANES), lambda i: (0, 0)),
            pl.BlockSpec((tm // TOKEN_BLOCK, SUBLANES, LANES), lambda i: (i, 0, 0)),
        ],
        out_shape=[
            jax.ShapeDtypeStruct((T, D_MODEL), BF16),
            jax.ShapeDtypeStruct((T, LANES), F32),
            jax.ShapeDtypeStruct((SUBLANES, LANES), F32),
            jax.ShapeDtypeStruct((T // TOKEN_BLOCK, SUBLANES, LANES), F32),
        ],
        scratch_shapes=[pltpu.VMEM((SUBLANES, LANES), F32)],
        compiler_params=_params("arbitrary"),
        name="moe_router",
    )(x, g_all, rw_pad)


SLOT_BLOCK = 256
WINDOW_BLOCKS = 10
TOKEN_WINDOW = WINDOW_BLOCKS * TOKEN_BLOCK


def _dispatch_kernel(sb_ref, ws_ref, lo_ref, first_ref, n_ref, h_ref, dest_ref, o_ref):
    i = pl.program_id(0)

    @pl.when(i < n_ref[0])
    def _():
        slot = sb_ref[i] * SLOT_BLOCK + lax.broadcasted_iota(jnp.int32, (SLOT_BLOCK, 1), 0)
        token = ws_ref[i] + lax.broadcasted_iota(jnp.int32, (1, TOKEN_WINDOW), 1)
        dest = jnp.where(token >= lo_ref[i], dest_ref[...], -1)
        hit = (dest[0:1, :] == slot) | (dest[1:2, :] == slot)
        rows = jnp.dot(jnp.where(hit, 1.0, 0.0).astype(BF16), h_ref[...], preferred_element_type=F32)

        @pl.when(first_ref[i] == 1)
        def _():
            o_ref[...] = rows.astype(o_ref.dtype)

        @pl.when(first_ref[i] == 0)
        def _():
            o_ref[...] = (o_ref[...].astype(F32) + rows).astype(o_ref.dtype)


def _dispatch(h, dest_t, items, n_slots):
    def at(i, sb, ws, lo, fi, ni):
        return pl.multiple_of(ws[i], TOKEN_BLOCK)

    grid_spec = pltpu.PrefetchScalarGridSpec(
        num_scalar_prefetch=5,
        grid=(items[0].shape[0],),
        in_specs=[
            pl.BlockSpec((pl.Element(TOKEN_WINDOW), pl.Element(D_MODEL)), lambda *a: (at(*a), 0)),
            pl.BlockSpec((pl.Element(SUBLANES), pl.Element(TOKEN_WINDOW)), lambda *a: (0, at(*a))),
        ],
        out_specs=pl.BlockSpec((SLOT_BLOCK, D_MODEL), lambda i, sb, ws, lo, fi, ni: (sb[i], 0)),
    )
    return pl.pallas_call(
        _dispatch_kernel,
        grid_spec=grid_spec,
        out_shape=jax.ShapeDtypeStruct((n_slots, D_MODEL), BF16),
        compiler_params=_params("arbitrary"),
        name="moe_dispatch",
    )(*items, h, dest_t)


def _dispatch_items(cblk, counts, pstart, pend, block_expert, bm, n_slots, T):
    n_tb = T // TOKEN_BLOCK
    n_sb = n_slots // SLOT_BLOCK
    sb_start = jnp.arange(n_sb, dtype=jnp.int32) * SLOT_BLOCK
    e_sb = block_expert[sb_start // bm]
    r0 = sb_start - pstart[e_sb]
    c_sb = counts[e_sb]
    in_region = sb_start < pend[e_sb]
    has_tokens = r0 < c_sb
    r1 = jnp.minimum(r0 + SLOT_BLOCK, c_sb) - 1
    before = cblk[:, 0, :N_EXPERTS].astype(jnp.int32)[:, e_sb]
    tb_first = jnp.where(has_tokens, jnp.sum(before <= r0[None, :], axis=0) - 1, 0)
    tb_last = jnp.where(has_tokens, jnp.sum(before <= r1[None, :], axis=0) - 1, 0)
    n_win = jnp.where(in_region, (tb_last - tb_first) // WINDOW_BLOCKS + 1, 1)
    ends = jnp.cumsum(n_win)
    total = ends[-1]
    max_items = n_sb + N_EXPERTS * (-(-n_tb // WINDOW_BLOCKS) + 1)
    i = jnp.minimum(jnp.arange(max_items, dtype=jnp.int32), total - 1)
    sb = jnp.sum(ends[None, :] <= i[:, None], axis=1).astype(jnp.int32)
    k = i - (ends - n_win)[sb]
    lo = ((tb_first[sb] + k * WINDOW_BLOCKS) * TOKEN_BLOCK).astype(jnp.int32)
    start = jnp.minimum(lo, T - TOKEN_WINDOW)
    return sb, start, lo, (k == 0).astype(jnp.int32), total.astype(jnp.int32).reshape(1)


def _experts_kernel(be_ref, nu_ref, xs_ref, wg_ref, wu_ref, wd_ref, o_ref, acc_ref):
    b = pl.program_id(0)
    f = pl.program_id(1)
    used = b < nu_ref[0]

    @pl.when(used & (f == 0))
    def _():
        acc_ref[...] = jnp.zeros_like(acc_ref)

    @pl.when(used)
    def _():
        acc_ref[...] = _swiglu_acc(acc_ref[...], xs_ref[...], wg_ref[...], wu_ref[...], wd_ref[...])

        @pl.when(f == pl.num_programs(1) - 1)
        def _():
            o_ref[...] = acc_ref[...].astype(o_ref.dtype)

    @pl.when(jnp.logical_not(used) & (f == 0))
    def _():
        o_ref[...] = jnp.zeros_like(o_ref)


def _experts(xs, block_expert, n_used, wg, wu, wd, m, bm, tf):
    n_blocks = xs.shape[0] // bm
    nf = D_FF // tf

    def row(b, nu):
        return jnp.minimum(b, nu[0] - 1)

    def fidx(b, f, nu):
        return jnp.where(b < nu[0], f, nf - 1)

    grid_spec = pltpu.PrefetchScalarGridSpec(
        num_scalar_prefetch=2,
        grid=(n_blocks, nf),
        in_specs=[
            pl.BlockSpec((bm, D_MODEL), lambda b, f, be, nu: (row(b, nu), 0)),
            pl.BlockSpec((None, None, D_MODEL, tf), lambda b, f, be, nu: (m, be[b], 0, fidx(b, f, nu))),
            pl.BlockSpec((None, None, D_MODEL, tf), lambda b, f, be, nu: (m, be[b], 0, fidx(b, f, nu))),
            pl.BlockSpec((None, None, tf, D_MODEL), lambda b, f, be, nu: (m, be[b], fidx(b, f, nu), 0)),
        ],
        out_specs=pl.BlockSpec((bm, D_MODEL), lambda b, f, be, nu: (b, 0)),
        scratch_shapes=[pltpu.VMEM((bm, D_MODEL), F32)],
    )
    return pl.pallas_call(
        _experts_kernel,
        grid_spec=grid_spec,
        out_shape=jax.ShapeDtypeStruct((n_blocks * bm, D_MODEL), BF16),
        compiler_params=_params("arbitrary", "arbitrary"),
        name="moe_experts",
    )(block_expert, n_used, xs, wg, wu, wd)


BF16_ROWS = 16
WINDOW = TOKEN_BLOCK + BF16_ROWS


def _combine_kernel(win_ref, ps_ref, pe_ref, x_ref, meta_ref, *refs, n_prompt_blocks=None):
    ys_refs, out_refs = refs[:N_EXPERTS], refs[N_EXPERTS:]
    blk = pl.program_id(0)
    meta = meta_ref[...]
    e1, e2 = meta[:, META_E1:META_E1 + 1], meta[:, META_E2:META_E2 + 1]
    d1, d2 = meta[:, META_R1:META_R1 + 1], meta[:, META_R2:META_R2 + 1]
    for e in range(N_EXPERTS):
        start = ps_ref[e].astype(F32)
        d1 = d1 + jnp.where(e1 == e, start, 0.0)
        d2 = d2 + jnp.where(e2 == e, start, 0.0)
    d1, d2, g1, g2 = (jnp.broadcast_to(t, (TOKEN_BLOCK, WINDOW))
                      for t in (d1, d2, meta[:, META_G1:META_G1 + 1], meta[:, META_G2:META_G2 + 1]))
    lane = lax.broadcasted_iota(jnp.int32, (1, WINDOW), 1)
    acc = x_ref[...]
    for e in range(N_EXPERTS):
        slot = win_ref[blk * N_EXPERTS + e] + lane
        slot = jnp.where((slot >= ps_ref[e]) & (slot < pe_ref[e]), slot, -1).astype(F32)
        sel = jnp.where(d1 == slot, g1, 0.0) + jnp.where(d2 == slot, g2, 0.0)
        acc = acc + jnp.dot(sel.astype(BF16), ys_refs[e][...], preferred_element_type=F32)
    if n_prompt_blocks is None:
        out_refs[0][...] = acc
    else:
        g_ref, prompt_ref, sample_ref = out_refs
        y = _rms(acc, g_ref[...])

        @pl.when(blk < n_prompt_blocks)
        def _():
            prompt_ref[...] = y

        @pl.when(blk >= n_prompt_blocks)
        def _():
            sample_ref[...] = y


def _combine(x, meta, ys, windows, pstart, pend, final_g=None, n_prompt_rows=None):
    T = x.shape[0]
    nb = T // TOKEN_BLOCK

    def window(e):
        return pl.BlockSpec((pl.Element(WINDOW), pl.Element(D_MODEL)),
                            lambda i, win, ps, pe: (pl.multiple_of(win[i * N_EXPERTS + e], BF16_ROWS), 0))

    in_specs = [pl.BlockSpec((TOKEN_BLOCK, D_MODEL), lambda i, win, ps, pe: (i, 0)),
                pl.BlockSpec((TOKEN_BLOCK, LANES), lambda i, win, ps, pe: (i, 0))] \
        + [window(e) for e in range(N_EXPERTS)]
    operands = [windows, pstart, pend, x, meta] + [ys] * N_EXPERTS
    if final_g is None:
        npb = None
        out_specs = pl.BlockSpec((TOKEN_BLOCK, D_MODEL), lambda i, win, ps, pe: (i, 0))
        out_shape = jax.ShapeDtypeStruct((T, D_MODEL), F32)
    else:
        assert n_prompt_rows % TOKEN_BLOCK == 0
        npb = n_prompt_rows // TOKEN_BLOCK
        in_specs.append(pl.BlockSpec((1, D_MODEL), lambda i, win, ps, pe: (0, 0)))
        operands.append(final_g)
        out_specs = [pl.BlockSpec((TOKEN_BLOCK, D_MODEL), lambda i, win, ps, pe: (jnp.minimum(i, npb - 1), 0)),
                     pl.BlockSpec((TOKEN_BLOCK, D_MODEL), lambda i, win, ps, pe: (jnp.maximum(i - npb, 0), 0))]
        out_shape = [jax.ShapeDtypeStruct((n_prompt_rows, D_MODEL), F32),
                     jax.ShapeDtypeStruct((T - n_prompt_rows, D_MODEL), F32)]
    grid_spec = pltpu.PrefetchScalarGridSpec(
        num_scalar_prefetch=3, grid=(nb,), in_specs=in_specs, out_specs=out_specs)
    return pl.pallas_call(
        functools.partial(_combine_kernel, n_prompt_blocks=npb),
        grid_spec=grid_spec,
        out_shape=out_shape,
        compiler_params=_params("arbitrary"),
        name="moe_combine",
    )(*operands)


def _moe_layer(x, norm_ffn_g, layer, rw_pad, wg, wu, wd, m, tm, bm, tf, final_g=None, n_prompt_rows=None):
    T = x.shape[0]
    h, meta, cnt, cblk = _router(x, norm_ffn_g, layer, rw_pad, m, tm)
    e = meta[:, META_E1:META_E2 + 1].astype(jnp.int32)
    rank = meta[:, META_R1:META_R2 + 1].astype(jnp.int32)
    counts = cnt[0, :N_EXPERTS].astype(jnp.int32)
    padded = (counts + bm - 1) // bm * bm
    pend = jnp.cumsum(padded)
    pstart = pend - padded
    dest = pstart[e] + rank
    n_blocks = (T * 2) // bm + N_EXPERTS + 1
    block_start = jnp.arange(n_blocks, dtype=jnp.int32) * bm
    block_expert = jnp.minimum(jnp.sum(pend[None, :] <= block_start[:, None], axis=1),
                               N_EXPERTS - 1).astype(jnp.int32)
    n_used = (pend[-1] // bm).astype(jnp.int32).reshape(1)
    dest_t = jnp.full((SUBLANES, T), -1, jnp.int32).at[0:2].set(dest.T)
    items = _dispatch_items(cblk, counts, pstart, pend, block_expert, bm, n_blocks * bm, T)
    xs = _dispatch(h, dest_t, items, n_blocks * bm)
    ys = _experts(xs, block_expert, n_used, wg, wu, wd, m, bm, tf)
    run_start = pstart[None, :] + cblk[:, 0, :N_EXPERTS].astype(jnp.int32)
    windows = (run_start // BF16_ROWS * BF16_ROWS).reshape(-1).astype(jnp.int32)
    return _combine(x, meta, ys, windows, pstart.astype(jnp.int32), pend.astype(jnp.int32),
                    final_g, n_prompt_rows)


def _tril(w):
    n = w.shape[-1]
    return jnp.where(jnp.tril(jnp.ones((n, n), dtype=bool)), w, jnp.zeros((), w.dtype))


def _mix_tables(w_s, b_s, sample_len):
    reps = A_CHUNK // sample_len
    eye = jnp.eye(reps, dtype=w_s.dtype)
    w_prompt = _tril(w_s)
    w_small = _tril(w_s[:, :sample_len, :sample_len])
    w_sample = jax.vmap(lambda w: jnp.kron(eye, w))(w_small)
    b_prompt = b_s
    b_sample = jnp.tile(b_s[:, :sample_len], (1, reps))
    wmix = jnp.stack([w_prompt, w_sample])
    bias = jnp.stack([b_prompt, b_sample])[..., None]
    return wmix, jnp.broadcast_to(bias, bias.shape[:-1] + (LANES,))


def kernel(x_prompt, x_sample, state_hgrn, norm_mix_g, norm_ffn_g, final_norm_g, a_w_in, a_ln_g, a_ln_b, a_w_s, a_b_s, a_w_out, b_w_in, b_lb_logits, b_norm_g, b_w_out, ffn_w_gate, ffn_w_up, ffn_w_down, moe_router, moe_w_gate, moe_w_up, moe_w_down):
    n_p, L, d = x_prompt.shape
    n_s, l_s, _ = x_sample.shape
    assert d == D_MODEL and l_s == SAMPLE_LEN and L % A_CHUNK == 0
    T_p, T_s = n_p * L, n_s * l_s
    T = T_p + T_s
    depth = norm_mix_g.shape[0]
    assert depth % 2 == 0, "the last layer must be an HGRN2 + MoE layer"

    tm_gate = 2 * A_CHUNK
    assert T_p % tm_gate == 0 and T_s % tm_gate == 0
    tm_big = next(t for t in (1536, 768, 512, 256, 128) if T % t == 0)
    bm = 1024
    tf = 512
    lb_rows = min(L, 512)

    x = jnp.concatenate([x_prompt.reshape(T_p, d), x_sample.reshape(T_s, d)], axis=0)

    p = jax.nn.softmax(b_lb_logits.astype(F32), axis=0)
    lbs = (jnp.cumsum(p, axis=0) - p[0:1])[:, None, :]
    mix_g = norm_mix_g[:, None, :]
    ffn_g = norm_ffn_g[:, None, :]
    b_ng = b_norm_g[:, None, :]
    ln_g = a_ln_g[:, None, :]
    ln_b = a_ln_b[:, None, :]
    rw_pad = jnp.pad(moe_router, ((0, 0), (0, 0), (0, LANES - N_EXPERTS)))
    a_w_in, a_w_out, b_w_in, b_w_out, ffn_w_gate, ffn_w_up, ffn_w_down = (
        w.astype(BF16) for w in (a_w_in, a_w_out, b_w_in, b_w_out, ffn_w_gate, ffn_w_up, ffn_w_down))

    hg_prompt, v_sample = [], []
    hg_sample = state_hgrn
    for layer in range(depth):
        j = layer // 2
        if layer % 2 == 0:
            z = _norm_matmul(x, mix_g, layer, a_w_in, j, _gelu, BF16, tm_big, 1024)
            wmix, bias = _mix_tables(a_w_s[j], a_b_s[j], l_s)
            x, v = _gmlp_gate(z, x, ln_g, ln_b, wmix, bias, a_w_out, j, T_p // tm_gate, tm_gate)
            v_sample.append(v.reshape(n_s, l_s, A_HALF))
            x = _ffn_dense(x, ffn_g, layer, ffn_w_gate, ffn_w_up, ffn_w_down, j, tm_big, tf)
        else:
            proj = _norm_matmul(x, mix_g, layer, b_w_in, j, _identity, F32, tm_big, 1024)
            o_p, s_p = _hgrn_prompt(proj, lbs, b_ng, j, n_p, L, lb_rows)
            o_s, hg_sample = _hgrn_sample(proj, T_p, hg_sample, lbs, b_ng, j, n_s, 8)
            hg_prompt.append(s_p)
            x = _matmul_res(o_p, o_s, b_w_out, j, x, math.gcd(T_p, T_s, 512))
            last = layer == depth - 1
            x = _moe_layer(x, ffn_g, layer, rw_pad, moe_w_gate, moe_w_up, moe_w_down, j, 512, bm, tf,
                           final_norm_g[None, :] if last else None, T_p if last else None)

    y_prompt, y_sample = x
    return (y_prompt.reshape(n_p, L, d), y_sample.reshape(n_s, l_s, d),
            jnp.stack(hg_prompt), hg_sample, jnp.stack(v_sample))
```

```python
import functools
import math

import jax
import jax.numpy as jnp
from jax import lax
from jax.experimental import pallas as pl
from jax.experimental.pallas import tpu as pltpu

F32 = jnp.float32
BF16 = jnp.bfloat16

D_MODEL = 1024
A_CHUNK = 128
A_HALF = 3 * D_MODEL
A_GROUPS = 8
A_GROUP_DIM = A_HALF // A_GROUPS
HG_HEADS = 8
HG_DK = 128
HG_CHUNK = 64
HG_SUB = 32
FORGET_FLOOR = 1e-20
D_FF = 7 * D_MODEL // 2
N_EXPERTS = 8
NORM_EPS = 1e-6
LANES = 128
SUBLANES = 8
SAMPLE_LEN = 4
MASKED_LOG = -1e30

VMEM_LIMIT = 56 * 1024 * 1024


def _params(*sem):
    return pltpu.CompilerParams(dimension_semantics=sem, vmem_limit_bytes=VMEM_LIMIT)


def _rms(x, g):
    ms = jnp.mean(x * x, axis=-1, keepdims=True)
    return x * lax.rsqrt(ms + NORM_EPS) * g


def _gelu(y):
    return 0.5 * y * (1.0 + lax.erf(y * math.sqrt(0.5)))


def _identity(y):
    return y


def _norm_matmul_kernel(x_ref, g_ref, w_ref, o_ref, h_ref, *, act):
    @pl.when(pl.program_id(1) == 0)
    def _():
        h_ref[...] = _rms(x_ref[...], g_ref[...]).astype(BF16)

    y = jnp.dot(h_ref[...], w_ref[...].astype(BF16), preferred_element_type=F32)
    o_ref[...] = act(y).astype(o_ref.dtype)


def _norm_matmul(x, g_all, layer, w_all, w_layer, act, out_dtype, tm, tn):
    T = x.shape[0]
    N = w_all.shape[-1]
    return pl.pallas_call(
        functools.partial(_norm_matmul_kernel, act=act),
        grid=(T // tm, N // tn),
        in_specs=[
            pl.BlockSpec((tm, D_MODEL), lambda i, j: (i, 0)),
            pl.BlockSpec((None, 1, D_MODEL), lambda i, j: (layer, 0, 0)),
            pl.BlockSpec((None, D_MODEL, tn), lambda i, j: (w_layer, 0, j)),
        ],
        out_specs=pl.BlockSpec((tm, tn), lambda i, j: (i, j)),
        out_shape=jax.ShapeDtypeStruct((T, N), out_dtype),
        scratch_shapes=[pltpu.VMEM((tm, D_MODEL), BF16)],
        compiler_params=_params("parallel", "arbitrary"),
        name="norm_matmul",
    )(x, g_all, w_all)


def _gmlp_gate_kernel(z_ref, lng_ref, lnb_ref, wmix_ref, bias_ref, wout_ref, x_ref, o_ref, v_ref):
    tm = z_ref.shape[0]
    zv = z_ref[:, A_HALF:].astype(F32)
    mu = jnp.mean(zv, axis=-1, keepdims=True)
    xc = zv - mu
    rstd = lax.rsqrt(jnp.mean(xc * xc, axis=-1, keepdims=True) + NORM_EPS)
    v = xc * rstd * lng_ref[...] + lnb_ref[...]
    v_ref[...] = v
    vb = v.astype(BF16)
    groups = []
    for g in range(A_GROUPS):
        cols = slice(g * A_GROUP_DIM, (g + 1) * A_GROUP_DIM)
        wm = wmix_ref[g].astype(BF16)
        bias = jnp.concatenate([bias_ref[g]] * (A_GROUP_DIM // LANES), axis=1)
        rows = []
        for c in range(tm // A_CHUNK):
            sl = slice(c * A_CHUNK, (c + 1) * A_CHUNK)
            s = jnp.dot(wm, vb[sl, cols], preferred_element_type=F32) + bias
            rows.append((z_ref[sl, cols].astype(F32) * s).astype(BF16))
        groups.append(jnp.concatenate(rows, axis=0))
    gated = jnp.concatenate(groups, axis=1)
    o_ref[...] = x_ref[...] + jnp.dot(gated, wout_ref[...], preferred_element_type=F32)


def _gmlp_gate(z, x, ln_g, ln_b, wmix, bias, w_out, j, n_prompt_blocks, tm):
    T = x.shape[0]
    nb = T // tm

    def kind(i):
        return jnp.where(i >= n_prompt_blocks, 1, 0)

    return pl.pallas_call(
        _gmlp_gate_kernel,
        grid=(nb,),
        in_specs=[
            pl.BlockSpec((tm, 2 * A_HALF), lambda i: (i, 0)),
            pl.BlockSpec((None, 1, A_HALF), lambda i: (j, 0, 0)),
            pl.BlockSpec((None, 1, A_HALF), lambda i: (j, 0, 0)),
            pl.BlockSpec((None, A_GROUPS, A_CHUNK, A_CHUNK), lambda i: (kind(i), 0, 0, 0)),
            pl.BlockSpec((None, A_GROUPS, A_CHUNK, LANES), lambda i: (kind(i), 0, 0, 0)),
            pl.BlockSpec((None, A_HALF, D_MODEL), lambda i: (j, 0, 0)),
            pl.BlockSpec((tm, D_MODEL), lambda i: (i, 0)),
        ],
        out_specs=[
            pl.BlockSpec((tm, D_MODEL), lambda i: (i, 0)),
            pl.BlockSpec((tm, A_HALF), lambda i: (jnp.maximum(i - n_prompt_blocks, 0), 0)),
        ],
        out_shape=[
            jax.ShapeDtypeStruct((T, D_MODEL), F32),
            jax.ShapeDtypeStruct(((nb - n_prompt_blocks) * tm, A_HALF), F32),
        ],
        compiler_params=_params("arbitrary"),
        name="gmlp_gate",
    )(z, ln_g, ln_b, wmix, bias, w_out, x)


FF_SPLIT = 2


def _swiglu_acc(acc, h, wg, wu, wd):
    sub = wg.shape[1] // FF_SPLIT
    for c in range(FF_SPLIT):
        cols = slice(c * sub, (c + 1) * sub)
        a = jnp.dot(h, wg[:, cols].astype(BF16), preferred_element_type=F32)
        b = jnp.dot(h, wu[:, cols].astype(BF16), preferred_element_type=F32)
        m = (a * jax.nn.sigmoid(a) * b).astype(BF16)
        acc = acc + jnp.dot(m, wd[cols, :].astype(BF16), preferred_element_type=F32)
    return acc


def _ffn_kernel(x_ref, g_ref, wg_ref, wu_ref, wd_ref, o_ref, h_ref):
    @pl.when(pl.program_id(1) == 0)
    def _():
        x = x_ref[...]
        h_ref[...] = _rms(x, g_ref[...]).astype(BF16)
        o_ref[...] = x

    o_ref[...] = _swiglu_acc(o_ref[...], h_ref[...], wg_ref[...], wu_ref[...], wd_ref[...])


def _ffn_dense(x, g_all, layer, wg, wu, wd, m, tm, tf):
    T = x.shape[0]
    return pl.pallas_call(
        _ffn_kernel,
        grid=(T // tm, D_FF // tf),
        in_specs=[
            pl.BlockSpec((tm, D_MODEL), lambda i, f: (i, 0)),
            pl.BlockSpec((None, 1, D_MODEL), lambda i, f: (layer, 0, 0)),
            pl.BlockSpec((None, D_MODEL, tf), lambda i, f: (m, 0, f)),
            pl.BlockSpec((None, D_MODEL, tf), lambda i, f: (m, 0, f)),
            pl.BlockSpec((None, tf, D_MODEL), lambda i, f: (m, f, 0)),
        ],
        out_specs=pl.BlockSpec((tm, D_MODEL), lambda i, f: (i, 0)),
        out_shape=jax.ShapeDtypeStruct((T, D_MODEL), F32),
        scratch_shapes=[pltpu.VMEM((tm, D_MODEL), BF16)],
        compiler_params=_params("parallel", "arbitrary"),
        name="ffn_dense",
    )(x, g_all, wg, wu, wd)


_NT = (((1,), (1,)), ((), ()))
_TN = (((0,), (0,)), ((), ()))


HG_GROUP = 8
MAX_BLOCK_DECAY = 75.0


def _hgrn_gates(qpre, fpre, lb):
    q = qpre * jax.nn.sigmoid(qpre)
    sig = jax.nn.sigmoid(fpre)
    f = lb + (1.0 - lb) * sig
    logf = jnp.log(jnp.maximum(f, FORGET_FLOOR))
    k = (1.0 - lb) * (1.0 - sig)
    return q, k, logf


def _cumsum_rows(x):
    C = x.shape[0]
    if C >= HG_SUB:
        r = lax.broadcasted_iota(jnp.int32, (C, C), 0)
        c = lax.broadcasted_iota(jnp.int32, (C, C), 1)
        tri = jnp.where(r >= c, 1.0, 0.0).astype(F32)
        return jnp.dot(tri, x, preferred_element_type=F32, precision=lax.Precision.HIGHEST)
    row = lax.broadcasted_iota(jnp.int32, (C, 1), 0)
    out = jnp.zeros_like(x)
    for s in range(C):
        out = out + jnp.where(row >= s, x[s:s + 1], 0.0)
    return out


def _chunk_prep(q, k, logf):
    C = q.shape[0]
    b = _cumsum_rows(logf)
    b_last = b[C - 1:C]
    qe = (q * jnp.exp(b)).astype(BF16)
    kd = (k * jnp.exp(b_last - b)).astype(BF16)
    return b, qe, kd, jnp.exp(b_last)


def _intra_exact(q, k, v, b, sub):
    C = q.shape[0]
    row = lax.broadcasted_iota(jnp.int32, (sub, 1), 0)
    parts = []
    for blk in range(C // sub):
        lo = blk * sub
        b_i, q_i, k_i, v_i = b[lo:lo + sub], q[lo:lo + sub], k[lo:lo + sub], v[lo:lo + sub]
        if blk == 0:
            o_i = jnp.zeros((sub, HG_DK), F32)
        else:
            ref = b[lo - 1:lo]
            qs = (q_i * jnp.exp(b_i - ref)).astype(BF16)
            ks = (k[:lo] * jnp.exp(ref - b[:lo])).astype(BF16)
            a = lax.dot_general(qs, ks, _NT, preferred_element_type=F32)
            o_i = jnp.dot(a.astype(BF16), v[:lo].astype(BF16), preferred_element_type=F32)
        for s in range(sub):
            rel = jnp.where(row >= s, b_i - b_i[s:s + 1], MASKED_LOG)
            a_col = jnp.sum(q_i * k_i[s:s + 1] * jnp.exp(rel), axis=-1, keepdims=True)
            o_i = o_i + a_col * v_i[s:s + 1]
        parts.append(o_i)
    return parts[0] if len(parts) == 1 else jnp.concatenate(parts, axis=0)


def _intra_factored(q, k, v, b, sub):
    C = q.shape[0]
    vb = v.astype(BF16)
    heads = [slice(h * HG_DK, (h + 1) * HG_DK) for h in range(q.shape[1] // HG_DK)]
    scores = []
    for blk in range(C // sub):
        lo, hi = blk * sub, (blk + 1) * sub
        if blk == 0:
            qs = q[:hi] * jnp.exp(b[:hi])
            ks = k[:hi] * jnp.exp(-b[:hi])
        else:
            ref = b[lo - 1:lo]
            qs = q[lo:hi] * jnp.exp(b[lo:hi] - ref)
            ks = k[:hi] * jnp.exp(ref - b[:hi])
        qs, ks = qs.astype(BF16), ks.astype(BF16)
        r = lax.broadcasted_iota(jnp.int32, (sub, hi), 0)
        c = lax.broadcasted_iota(jnp.int32, (sub, hi), 1)
        row = []
        for cols in heads:
            a = lax.dot_general(qs[:, cols], ks[:, cols], _NT, preferred_element_type=F32)
            row.append(jnp.where(c <= r + lo, a, 0.0).astype(BF16))
        scores.append(row)
    parts = []
    for blk, row in enumerate(scores):
        hi = (blk + 1) * sub
        outs = [jnp.dot(a, vb[:hi, cols], preferred_element_type=F32) for a, cols in zip(row, heads)]
        parts.append(outs[0] if len(outs) == 1 else jnp.concatenate(outs, axis=1))
    return jnp.concatenate(parts, axis=0)


def _head_norm_gate(o, ng, gate):
    return (o * lax.rsqrt(jnp.mean(o * o, axis=-1, keepdims=True) + NORM_EPS) * ng * gate)


def _hgrn_prompt_kernel(q_ref, f_ref, i_ref, g_ref, lb_ref, ng_ref, o_ref, sfin_ref,
                        st_ref, qs_ref, ks_ref, lf_ref, qe_ref, kd_ref, eb_ref, oi_ref):
    tb = pl.program_id(2)
    rows_total = q_ref.shape[0]
    n_chunks = rows_total // HG_CHUNK

    @pl.when(tb == 0)
    def _():
        st_ref[...] = jnp.zeros_like(st_ref)

    q, k, logf = _hgrn_gates(q_ref[...], f_ref[...], lb_ref[...])
    qs_ref[...] = q
    ks_ref[...] = k
    lf_ref[...] = logf
    block_decay = jnp.sum(logf.reshape(rows_total // HG_SUB, HG_SUB, logf.shape[-1]), axis=1)
    mild = jnp.min(block_decay) >= -MAX_BLOCK_DECAY

    heads = [slice(h * HG_DK, (h + 1) * HG_DK) for h in range(HG_GROUP)]

    def exact_group(qg, kg, vg, b, sub):
        return jnp.concatenate([_intra_exact(qg[:, c], kg[:, c], vg[:, c], b[:, c], sub) for c in heads], axis=1)

    def intra_pass(intra):
        def body(ci, carry):
            rows = pl.ds(pl.multiple_of(ci * HG_CHUNK, HG_CHUNK), HG_CHUNK)
            qg, kg, vg = qs_ref[rows, :], ks_ref[rows, :], i_ref[rows, :]
            b, qe, kd, eb_last = _chunk_prep(qg, kg, lf_ref[rows, :])
            qe_ref[rows, :] = qe
            kd_ref[rows, :] = kd
            eb_ref[pl.ds(pl.multiple_of(ci * SUBLANES, SUBLANES), SUBLANES), :] = \
                jnp.broadcast_to(eb_last, (SUBLANES, eb_last.shape[1]))
            oi_ref[rows, :] = intra(qg, kg, vg, b, HG_SUB)
            return carry
        lax.fori_loop(0, n_chunks, body, 0, unroll=2)

    @pl.when(mild)
    def _():
        intra_pass(_intra_factored)

    @pl.when(jnp.logical_not(mild))
    def _():
        intra_pass(exact_group)

    ng = ng_ref[...]

    def state_body(ci, carry):
        rows = pl.ds(pl.multiple_of(ci * HG_CHUNK, HG_CHUNK), HG_CHUNK)
        st = st_ref[...]
        stb = st.astype(BF16)
        qe, kd, oi = qe_ref[rows, :], kd_ref[rows, :], oi_ref[rows, :]
        vb = i_ref[rows, :].astype(BF16)
        eb_last = eb_ref[pl.ds(pl.multiple_of(ci * SUBLANES, SUBLANES), 1), :]
        outs, adds = [], []
        for c in heads:
            o = oi[:, c] + lax.dot_general(qe[:, c], stb[:, c], _NT, preferred_element_type=F32)
            outs.append(o * lax.rsqrt(jnp.mean(o * o, axis=-1, keepdims=True) + NORM_EPS))
            adds.append(lax.dot_general(vb[:, c], kd[:, c], _TN, preferred_element_type=F32))
        st_ref[...] = st * eb_last + jnp.concatenate(adds, axis=1)
        gate = jax.nn.sigmoid(g_ref[rows, :])
        o_ref[rows, :] = (jnp.concatenate(outs, axis=1) * ng * gate).astype(o_ref.dtype)
        return carry

    lax.fori_loop(0, n_chunks, state_body, 0)

    @pl.when(tb == pl.num_programs(2) - 1)
    def _():
        for h, c in enumerate(heads):
            sfin_ref[h] = st_ref[:, c].T


def _hgrn_prompt(proj, lbs, ng_all, j, n_seq, L, lb_rows):
    nb = L // lb_rows
    ng_groups = HG_HEADS // HG_GROUP
    gw = HG_GROUP * HG_DK

    def field(k):
        return pl.BlockSpec((lb_rows, gw), lambda n, h, t: (n * nb + t, k * ng_groups + h))

    vec = pl.BlockSpec((None, 1, gw), lambda n, h, t: (j, 0, h))
    return pl.pallas_call(
        _hgrn_prompt_kernel,
        grid=(n_seq, ng_groups, nb),
        in_specs=[field(0), field(1), field(2), field(3), vec, vec],
        out_specs=[
            pl.BlockSpec((lb_rows, gw), lambda n, h, t: (n * nb + t, h)),
            pl.BlockSpec((None, HG_GROUP, HG_DK, HG_DK), lambda n, h, t: (n, h, 0, 0)),
        ],
        out_shape=[
            jax.ShapeDtypeStruct((n_seq * L, D_MODEL), BF16),
            jax.ShapeDtypeStruct((n_seq, HG_HEADS, HG_DK, HG_DK), F32),
        ],
        scratch_shapes=[
            pltpu.VMEM((HG_DK, gw), F32),
            pltpu.VMEM((lb_rows, gw), F32),
            pltpu.VMEM((lb_rows, gw), F32),
            pltpu.VMEM((lb_rows, gw), F32),
            pltpu.VMEM((lb_rows, gw), BF16),
            pltpu.VMEM((lb_rows, gw), BF16),
            pltpu.VMEM((lb_rows // HG_CHUNK * SUBLANES, gw), F32),
            pltpu.VMEM((lb_rows, gw), F32),
        ],
        compiler_params=_params("parallel", "parallel", "arbitrary"),
        name="hgrn_prompt",
    )(proj, proj, proj, proj, lbs, ng_all)


def _hgrn_sample_kernel(q_ref, f_ref, i_ref, g_ref, lb_ref, ng_ref, s0_ref, o_ref, sfin_ref):
    n_seq = s0_ref.shape[0]
    q, k, logf = _hgrn_gates(q_ref[...], f_ref[...], lb_ref[...])
    v = i_ref[...]
    row = lax.broadcasted_iota(jnp.int32, (SUBLANES, 1), 0)
    first_half = row < SAMPLE_LEN
    prep = []
    for seq in range(n_seq):
        rows = slice(seq // 2 * SUBLANES, (seq // 2 + 1) * SUBLANES)
        mine = first_half if seq % 2 == 0 else jnp.logical_not(first_half)
        qm, km, vm, lm = (jnp.where(mine, t[rows], 0.0) for t in (q, k, v, logf))
        b, qe, kd, eb_last = _chunk_prep(qm, km, lm)
        prep.append((qe, kd, eb_last, vm.astype(BF16), _intra_exact(qm, km, vm, b, SUBLANES)))
    states = [s0_ref[seq].T for seq in range(n_seq)]
    outs = [oi + lax.dot_general(qe, st.astype(BF16), _NT, preferred_element_type=F32)
            for (qe, _, _, _, oi), st in zip(prep, states)]
    news = [st * eb_last + lax.dot_general(vb, kd, _TN, preferred_element_type=F32)
            for (_, kd, eb_last, vb, _), st in zip(prep, states)]
    sfin_ref[...] = jnp.stack([st.T for st in news])
    tiles = [jnp.where(first_half, outs[2 * p], outs[2 * p + 1]) for p in range(n_seq // 2)]
    gate = jax.nn.sigmoid(g_ref[...])
    o_ref[...] = _head_norm_gate(jnp.concatenate(tiles, axis=0), ng_ref[...], gate).astype(o_ref.dtype)


def _hgrn_sample(proj, row0, state_all, lbs, ng_all, j, n_seq, sb):
    H = HG_HEADS
    rb = sb * SAMPLE_LEN
    assert row0 % rb == 0 and sb % 2 == 0
    r0 = row0 // rb

    def field(k):
        return pl.BlockSpec((rb, HG_DK), lambda s, h: (r0 + s, k * H + h))

    vec = pl.BlockSpec((None, 1, HG_DK), lambda s, h: (j, 0, h))
    return pl.pallas_call(
        _hgrn_sample_kernel,
        grid=(n_seq // sb, H),
        in_specs=[field(0), field(1), field(2), field(3), vec, vec,
                  pl.BlockSpec((None, sb, None, HG_DK, HG_DK), lambda s, h: (j, s, h, 0, 0))],
        out_specs=[
            pl.BlockSpec((rb, HG_DK), lambda s, h: (s, h)),
            pl.BlockSpec((None, sb, None, HG_DK, HG_DK), lambda s, h: (j, s, h, 0, 0)),
        ],
        out_shape=[
            jax.ShapeDtypeStruct((n_seq * SAMPLE_LEN, D_MODEL), BF16),
            jax.ShapeDtypeStruct(state_all.shape, state_all.dtype),
        ],
        input_output_aliases={6: 1},
        compiler_params=_params("parallel", "parallel"),
        name="hgrn_sample",
    )(proj, proj, proj, proj, lbs, ng_all, state_all)


META_E1, META_E2, META_G1, META_G2, META_R1, META_R2 = range(6)
TOKEN_BLOCK = 128


def _router_kernel(ap_ref, as_ref, w_ref, x_ref, g_ref, rw_ref, xo_ref, h_ref, meta_ref, cnt_ref, cb_ref,
                   carry_ref, *, n_prompt_blocks):
    i = pl.program_id(0)
    tm = x_ref.shape[0]

    @pl.when(i == 0)
    def _():
        carry_ref[...] = jnp.zeros_like(carry_ref)

    a = jnp.where(i < n_prompt_blocks, ap_ref[...], as_ref[...])
    x = x_ref[...] + jnp.dot(a, w_ref[...], preferred_element_type=F32)
    xo_ref[...] = x
    h = _rms(x, g_ref[...])
    h_ref[...] = h.astype(h_ref.dtype)
    logits = jnp.dot(h.astype(BF16), rw_ref[...].astype(BF16), preferred_element_type=F32)
    lane = lax.broadcasted_iota(jnp.int32, (tm, LANES), 1)
    neg = -jnp.inf
    lg = jnp.where(lane < N_EXPERTS, logits, neg)
    m1 = jnp.max(lg, axis=-1, keepdims=True)
    e1 = jnp.min(jnp.where(lg == m1, lane, LANES), axis=-1, keepdims=True)
    lg2 = jnp.where(lane == e1, neg, lg)
    m2 = jnp.max(lg2, axis=-1, keepdims=True)
    e2 = jnp.min(jnp.where(lg2 == m2, lane, LANES), axis=-1, keepdims=True)
    ex = jnp.exp(m2 - m1)
    g1 = 1.0 / (1.0 + ex)
    g2 = ex / (1.0 + ex)

    onehot = jnp.where((lane == e1) | (lane == e2), 1.0, 0.0)
    r = lax.broadcasted_iota(jnp.int32, (tm, tm), 0)
    c = lax.broadcasted_iota(jnp.int32, (tm, tm), 1)
    before = jnp.where(r > c, 1.0, 0.0).astype(BF16)
    seen = jnp.dot(before, onehot.astype(BF16), preferred_element_type=F32) + carry_ref[0:1, :]
    r1 = jnp.sum(jnp.where(lane == e1, seen, 0.0), axis=-1, keepdims=True)
    r2 = jnp.sum(jnp.where(lane == e2, seen, 0.0), axis=-1, keepdims=True)
    for blk in range(tm // TOKEN_BLOCK):
        cb_ref[blk] = jnp.broadcast_to(seen[blk * TOKEN_BLOCK:blk * TOKEN_BLOCK + 1], (SUBLANES, LANES))
    total = carry_ref[0:1, :] + jnp.sum(onehot, axis=0, keepdims=True)
    carry_ref[...] = jnp.broadcast_to(total, carry_ref.shape)
    cnt_ref[...] = jnp.broadcast_to(total, cnt_ref.shape)

    meta = jnp.zeros((tm, LANES), F32)
    for idx, val in ((META_E1, e1.astype(F32)), (META_E2, e2.astype(F32)), (META_G1, g1),
                     (META_G2, g2), (META_R1, r1), (META_R2, r2)):
        meta = jnp.where(lane == idx, val, meta)
    meta_ref[...] = meta


def _router(a_prompt, a_sample, w_all, x, g_all, layer, rw_pad, m, tm):
    T = x.shape[0]
    npb = a_prompt.shape[0] // tm
    assert a_prompt.shape[0] % tm == 0 and a_sample.shape[0] % tm == 0
    return pl.pallas_call(
        functools.partial(_router_kernel, n_prompt_blocks=npb),
        grid=(T // tm,),
        in_specs=[
            pl.BlockSpec((tm, D_MODEL), lambda i: (jnp.minimum(i, npb - 1), 0)),
            pl.BlockSpec((tm, D_MODEL), lambda i: (jnp.maximum(i - npb, 0), 0)),
            pl.BlockSpec((None, D_MODEL, D_MODEL), lambda i: (m, 0, 0)),
            pl.BlockSpec((tm, D_MODEL), lambda i: (i, 0)),
            pl.BlockSpec((None, 1, D_MODEL), lambda i: (layer, 0, 0)),
            pl.BlockSpec((None, D_MODEL, LANES), lambda i: (m, 0, 0)),
        ],
        out_specs=[
            pl.BlockSpec((tm, D_MODEL), lambda i: (i, 0)),
            pl.BlockSpec((tm, D_MODEL), lambda i: (i, 0)),
            pl.BlockSpec((tm, LANES), lambda i: (i, 0)),
            pl.BlockSpec((SUBLANES, LANES), lambda i: (0, 0)),
            pl.BlockSpec((tm // TOKEN_BLOCK, SUBLANES, LANES), lambda i: (i, 0, 0)),
        ],
        out_shape=[
            jax.ShapeDtypeStruct((T, D_MODEL), F32),
            jax.ShapeDtypeStruct((T, D_MODEL), BF16),
            jax.ShapeDtypeStruct((T, LANES), F32),
            jax.ShapeDtypeStruct((SUBLANES, LANES), F32),
            jax.ShapeDtypeStruct((T // TOKEN_BLOCK, SUBLANES, LANES), F32),
        ],
        scratch_shapes=[pltpu.VMEM((SUBLANES, LANES), F32)],
        compiler_params=_params("arbitrary"),
        name="moe_router",
    )(a_prompt, a_sample, w_all, x, g_all, rw_pad)


SLOT_BLOCK = 256
WINDOW_BLOCKS = 10
TOKEN_WINDOW = WINDOW_BLOCKS * TOKEN_BLOCK


def _dispatch_kernel(sb_ref, ws_ref, lo_ref, first_ref, n_ref, h_ref, dest_ref, o_ref):
    i = pl.program_id(0)

    @pl.when(i < n_ref[0])
    def _():
        slot = sb_ref[i] * SLOT_BLOCK + lax.broadcasted_iota(jnp.int32, (SLOT_BLOCK, 1), 0)
        token = ws_ref[i] + lax.broadcasted_iota(jnp.int32, (1, TOKEN_WINDOW), 1)
        dest = jnp.where(token >= lo_ref[i], dest_ref[...], -1)
        hit = (dest[0:1, :] == slot) | (dest[1:2, :] == slot)
        rows = jnp.dot(jnp.where(hit, 1.0, 0.0).astype(BF16), h_ref[...], preferred_element_type=F32)

        @pl.when(first_ref[i] == 1)
        def _():
            o_ref[...] = rows.astype(o_ref.dtype)

        @pl.when(first_ref[i] == 0)
        def _():
            o_ref[...] = (o_ref[...].astype(F32) + rows).astype(o_ref.dtype)


def _dispatch(h, dest_t, items, n_slots):
    def at(i, sb, ws, lo, fi, ni):
        return pl.multiple_of(ws[i], TOKEN_BLOCK)

    grid_spec = pltpu.PrefetchScalarGridSpec(
        num_scalar_prefetch=5,
        grid=(items[0].shape[0],),
        in_specs=[
            pl.BlockSpec((pl.Element(TOKEN_WINDOW), pl.Element(D_MODEL)), lambda *a: (at(*a), 0)),
            pl.BlockSpec((pl.Element(SUBLANES), pl.Element(TOKEN_WINDOW)), lambda *a: (0, at(*a))),
        ],
        out_specs=pl.BlockSpec((SLOT_BLOCK, D_MODEL), lambda i, sb, ws, lo, fi, ni: (sb[i], 0)),
    )
    return pl.pallas_call(
        _dispatch_kernel,
        grid_spec=grid_spec,
        out_shape=jax.ShapeDtypeStruct((n_slots, D_MODEL), BF16),
        compiler_params=_params("arbitrary"),
        name="moe_dispatch",
    )(*items, h, dest_t)


def _dispatch_items(cblk, counts, pstart, pend, block_expert, bm, n_slots, T):
    n_tb = T // TOKEN_BLOCK
    n_sb = n_slots // SLOT_BLOCK
    sb_start = jnp.arange(n_sb, dtype=jnp.int32) * SLOT_BLOCK
    e_sb = block_expert[sb_start // bm]
    r0 = sb_start - pstart[e_sb]
    c_sb = counts[e_sb]
    in_region = sb_start < pend[e_sb]
    has_tokens = r0 < c_sb
    r1 = jnp.minimum(r0 + SLOT_BLOCK, c_sb) - 1
    before = cblk[:, 0, :N_EXPERTS].astype(jnp.int32)[:, e_sb]
    tb_first = jnp.where(has_tokens, jnp.sum(before <= r0[None, :], axis=0) - 1, 0)
    tb_last = jnp.where(has_tokens, jnp.sum(before <= r1[None, :], axis=0) - 1, 0)
    n_win = jnp.where(in_region, (tb_last - tb_first) // WINDOW_BLOCKS + 1, 1)
    ends = jnp.cumsum(n_win)
    total = ends[-1]
    max_items = n_sb + N_EXPERTS * (-(-n_tb // WINDOW_BLOCKS) + 1)
    i = jnp.minimum(jnp.arange(max_items, dtype=jnp.int32), total - 1)
    sb = jnp.sum(ends[None, :] <= i[:, None], axis=1).astype(jnp.int32)
    k = i - (ends - n_win)[sb]
    lo = ((tb_first[sb] + k * WINDOW_BLOCKS) * TOKEN_BLOCK).astype(jnp.int32)
    start = jnp.minimum(lo, T - TOKEN_WINDOW)
    return sb, start, lo, (k == 0).astype(jnp.int32), total.astype(jnp.int32).reshape(1)


def _experts_kernel(be_ref, nu_ref, xs_ref, wg_ref, wu_ref, wd_ref, o_ref, acc_ref):
    b = pl.program_id(0)
    f = pl.program_id(1)
    used = b < nu_ref[0]

    @pl.when(used & (f == 0))
    def _():
        acc_ref[...] = jnp.zeros_like(acc_ref)

    @pl.when(used)
    def _():
        acc_ref[...] = _swiglu_acc(acc_ref[...], xs_ref[...], wg_ref[...], wu_ref[...], wd_ref[...])

        @pl.when(f == pl.num_programs(1) - 1)
        def _():
            o_ref[...] = acc_ref[...].astype(o_ref.dtype)

    @pl.when(jnp.logical_not(used) & (f == 0))
    def _():
        o_ref[...] = jnp.zeros_like(o_ref)


def _experts(xs, block_expert, n_used, wg, wu, wd, m, bm, tf):
    n_blocks = xs.shape[0] // bm
    nf = D_FF // tf

    def row(b, nu):
        return jnp.minimum(b, nu[0] - 1)

    def fidx(b, f, nu):
        return jnp.where(b < nu[0], f, nf - 1)

    grid_spec = pltpu.PrefetchScalarGridSpec(
        num_scalar_prefetch=2,
        grid=(n_blocks, nf),
        in_specs=[
            pl.BlockSpec((bm, D_MODEL), lambda b, f, be, nu: (row(b, nu), 0)),
            pl.BlockSpec((None, None, D_MODEL, tf), lambda b, f, be, nu: (m, be[b], 0, fidx(b, f, nu))),
            pl.BlockSpec((None, None, D_MODEL, tf), lambda b, f, be, nu: (m, be[b], 0, fidx(b, f, nu))),
            pl.BlockSpec((None, None, tf, D_MODEL), lambda b, f, be, nu: (m, be[b], fidx(b, f, nu), 0)),
        ],
        out_specs=pl.BlockSpec((bm, D_MODEL), lambda b, f, be, nu: (b, 0)),
        scratch_shapes=[pltpu.VMEM((bm, D_MODEL), F32)],
    )
    return pl.pallas_call(
        _experts_kernel,
        grid_spec=grid_spec,
        out_shape=jax.ShapeDtypeStruct((n_blocks * bm, D_MODEL), BF16),
        compiler_params=_params("arbitrary", "arbitrary"),
        name="moe_experts",
    )(block_expert, n_used, xs, wg, wu, wd)


BF16_ROWS = 16
WINDOW = TOKEN_BLOCK + BF16_ROWS


def _combine_kernel(win_ref, ps_ref, pe_ref, x_ref, meta_ref, *refs, n_prompt_blocks=None):
    ys_refs, out_refs = refs[:N_EXPERTS], refs[N_EXPERTS:]
    blk = pl.program_id(0)
    meta = meta_ref[...]
    e1, e2 = meta[:, META_E1:META_E1 + 1], meta[:, META_E2:META_E2 + 1]
    d1, d2 = meta[:, META_R1:META_R1 + 1], meta[:, META_R2:META_R2 + 1]
    for e in range(N_EXPERTS):
        start = ps_ref[e].astype(F32)
        d1 = d1 + jnp.where(e1 == e, start, 0.0)
        d2 = d2 + jnp.where(e2 == e, start, 0.0)
    d1, d2, g1, g2 = (jnp.broadcast_to(t, (TOKEN_BLOCK, WINDOW))
                      for t in (d1, d2, meta[:, META_G1:META_G1 + 1], meta[:, META_G2:META_G2 + 1]))
    lane = lax.broadcasted_iota(jnp.int32, (1, WINDOW), 1)
    acc = x_ref[...]
    for e in range(N_EXPERTS):
        slot = win_ref[blk * N_EXPERTS + e] + lane
        slot = jnp.where((slot >= ps_ref[e]) & (slot < pe_ref[e]), slot, -1).astype(F32)
        sel = jnp.where(d1 == slot, g1, 0.0) + jnp.where(d2 == slot, g2, 0.0)
        acc = acc + jnp.dot(sel.astype(BF16), ys_refs[e][...], preferred_element_type=F32)
    if n_prompt_blocks is None:
        out_refs[0][...] = acc
    else:
        g_ref, prompt_ref, sample_ref = out_refs
        y = _rms(acc, g_ref[...])

        @pl.when(blk < n_prompt_blocks)
        def _():
            prompt_ref[...] = y

        @pl.when(blk >= n_prompt_blocks)
        def _():
            sample_ref[...] = y


def _combine(x, meta, ys, windows, pstart, pend, final_g=None, n_prompt_rows=None):
    T = x.shape[0]
    nb = T // TOKEN_BLOCK

    def window(e):
        return pl.BlockSpec((pl.Element(WINDOW), pl.Element(D_MODEL)),
                            lambda i, win, ps, pe: (pl.multiple_of(win[i * N_EXPERTS + e], BF16_ROWS), 0))

    in_specs = [pl.BlockSpec((TOKEN_BLOCK, D_MODEL), lambda i, win, ps, pe: (i, 0)),
                pl.BlockSpec((TOKEN_BLOCK, LANES), lambda i, win, ps, pe: (i, 0))] \
        + [window(e) for e in range(N_EXPERTS)]
    operands = [windows, pstart, pend, x, meta] + [ys] * N_EXPERTS
    if final_g is None:
        npb = None
        out_specs = pl.BlockSpec((TOKEN_BLOCK, D_MODEL), lambda i, win, ps, pe: (i, 0))
        out_shape = jax.ShapeDtypeStruct((T, D_MODEL), F32)
    else:
        assert n_prompt_rows % TOKEN_BLOCK == 0
        npb = n_prompt_rows // TOKEN_BLOCK
        in_specs.append(pl.BlockSpec((1, D_MODEL), lambda i, win, ps, pe: (0, 0)))
        operands.append(final_g)
        out_specs = [pl.BlockSpec((TOKEN_BLOCK, D_MODEL), lambda i, win, ps, pe: (jnp.minimum(i, npb - 1), 0)),
                     pl.BlockSpec((TOKEN_BLOCK, D_MODEL), lambda i, win, ps, pe: (jnp.maximum(i - npb, 0), 0))]
        out_shape = [jax.ShapeDtypeStruct((n_prompt_rows, D_MODEL), F32),
                     jax.ShapeDtypeStruct((T - n_prompt_rows, D_MODEL), F32)]
    grid_spec = pltpu.PrefetchScalarGridSpec(
        num_scalar_prefetch=3, grid=(nb,), in_specs=in_specs, out_specs=out_specs)
    return pl.pallas_call(
        functools.partial(_combine_kernel, n_prompt_blocks=npb),
        grid_spec=grid_spec,
        out_shape=out_shape,
        compiler_params=_params("arbitrary"),
        name="moe_combine",
    )(*operands)


def _mixer_out_and_moe(o_p, o_s, w_out, x, norm_ffn_g, layer, rw_pad, wg, wu, wd, m, tm, bm, tf,
                       final_g=None, n_prompt_rows=None):
    T = x.shape[0]
    x, h, meta, cnt, cblk = _router(o_p, o_s, w_out, x, norm_ffn_g, layer, rw_pad, m, tm)
    e = meta[:, META_E1:META_E2 + 1].astype(jnp.int32)
    rank = meta[:, META_R1:META_R2 + 1].astype(jnp.int32)
    counts = cnt[0, :N_EXPERTS].astype(jnp.int32)
    padded = (counts + bm - 1) // bm * bm
    pend = jnp.cumsum(padded)
    pstart = pend - padded
    dest = pstart[e] + rank
    n_blocks = (T * 2) // bm + N_EXPERTS + 1
    block_start = jnp.arange(n_blocks, dtype=jnp.int32) * bm
    block_expert = jnp.minimum(jnp.sum(pend[None, :] <= block_start[:, None], axis=1),
                               N_EXPERTS - 1).astype(jnp.int32)
    n_used = (pend[-1] // bm).astype(jnp.int32).reshape(1)
    dest_t = jnp.full((SUBLANES, T), -1, jnp.int32).at[0:2].set(dest.T)
    items = _dispatch_items(cblk, counts, pstart, pend, block_expert, bm, n_blocks * bm, T)
    xs = _dispatch(h, dest_t, items, n_blocks * bm)
    ys = _experts(xs, block_expert, n_used, wg, wu, wd, m, bm, tf)
    run_start = pstart[None, :] + cblk[:, 0, :N_EXPERTS].astype(jnp.int32)
    windows = (run_start // BF16_ROWS * BF16_ROWS).reshape(-1).astype(jnp.int32)
    return _combine(x, meta, ys, windows, pstart.astype(jnp.int32), pend.astype(jnp.int32),
                    final_g, n_prompt_rows)


def _tril(w):
    n = w.shape[-1]
    return jnp.where(jnp.tril(jnp.ones((n, n), dtype=bool)), w, jnp.zeros((), w.dtype))


def _mix_tables(w_s, b_s, sample_len):
    reps = A_CHUNK // sample_len
    eye = jnp.eye(reps, dtype=w_s.dtype)
    w_prompt = _tril(w_s)
    w_small = _tril(w_s[:, :sample_len, :sample_len])
    w_sample = jax.vmap(lambda w: jnp.kron(eye, w))(w_small)
    b_prompt = b_s
    b_sample = jnp.tile(b_s[:, :sample_len], (1, reps))
    wmix = jnp.stack([w_prompt, w_sample])
    bias = jnp.stack([b_prompt, b_sample])[..., None]
    return wmix, jnp.broadcast_to(bias, bias.shape[:-1] + (LANES,))


def kernel(x_prompt, x_sample, state_hgrn, norm_mix_g, norm_ffn_g, final_norm_g, a_w_in, a_ln_g, a_ln_b, a_w_s, a_b_s, a_w_out, b_w_in, b_lb_logits, b_norm_g, b_w_out, ffn_w_gate, ffn_w_up, ffn_w_down, moe_router, moe_w_gate, moe_w_up, moe_w_down):
    n_p, L, d = x_prompt.shape
    n_s, l_s, _ = x_sample.shape
    assert d == D_MODEL and l_s == SAMPLE_LEN and L % A_CHUNK == 0
    T_p, T_s = n_p * L, n_s * l_s
    T = T_p + T_s
    depth = norm_mix_g.shape[0]
    assert depth % 2 == 0, "the last layer must be an HGRN2 + MoE layer"

    tm_gate = 2 * A_CHUNK
    assert T_p % tm_gate == 0 and T_s % tm_gate == 0
    tm_big = next(t for t in (1536, 768, 512, 256, 128) if T % t == 0)
    bm = 1024
    tf = 512
    lb_rows = min(L, 512)

    x = jnp.concatenate([x_prompt.reshape(T_p, d), x_sample.reshape(T_s, d)], axis=0)

    p = jax.nn.softmax(b_lb_logits.astype(F32), axis=0)
    lbs = (jnp.cumsum(p, axis=0) - p[0:1])[:, None, :]
    mix_g = norm_mix_g[:, None, :]
    ffn_g = norm_ffn_g[:, None, :]
    b_ng = b_norm_g[:, None, :]
    ln_g = a_ln_g[:, None, :]
    ln_b = a_ln_b[:, None, :]
    rw_pad = jnp.pad(moe_router, ((0, 0), (0, 0), (0, LANES - N_EXPERTS)))
    a_w_in, a_w_out, b_w_in, b_w_out, ffn_w_gate, ffn_w_up, ffn_w_down = (
        w.astype(BF16) for w in (a_w_in, a_w_out, b_w_in, b_w_out, ffn_w_gate, ffn_w_up, ffn_w_down))

    hg_prompt, v_sample = [], []
    hg_sample = state_hgrn
    for layer in range(depth):
        j = layer // 2
        if layer % 2 == 0:
            z = _norm_matmul(x, mix_g, layer, a_w_in, j, _gelu, BF16, tm_big, 1024)
            wmix, bias = _mix_tables(a_w_s[j], a_b_s[j], l_s)
            x, v = _gmlp_gate(z, x, ln_g, ln_b, wmix, bias, a_w_out, j, T_p // tm_gate, tm_gate)
            v_sample.append(v.reshape(n_s, l_s, A_HALF))
            x = _ffn_dense(x, ffn_g, layer, ffn_w_gate, ffn_w_up, ffn_w_down, j, tm_big, tf)
        else:
            proj = _norm_matmul(x, mix_g, layer, b_w_in, j, _identity, F32, tm_big, 1024)
            o_p, s_p = _hgrn_prompt(proj, lbs, b_ng, j, n_p, L, lb_rows)
            o_s, hg_sample = _hgrn_sample(proj, T_p, hg_sample, lbs, b_ng, j, n_s, 8)
            hg_prompt.append(s_p)
            last = layer == depth - 1
            x = _mixer_out_and_moe(o_p, o_s, b_w_out, x, ffn_g, layer, rw_pad, moe_w_gate, moe_w_up, moe_w_down,
                                   j, math.gcd(T_p, T_s, 512), bm, tf,
                                   final_norm_g[None, :] if last else None, T_p if last else None)

    y_prompt, y_sample = x
    return (y_prompt.reshape(n_p, L, d), y_sample.reshape(n_s, l_s, d),
            jnp.stack(hg_prompt), hg_sample, jnp.stack(v_sample))
```

```python
import functools
import math

import jax
import jax.numpy as jnp
from jax import lax
from jax.experimental import pallas as pl
from jax.experimental.pallas import tpu as pltpu

F32 = jnp.float32
BF16 = jnp.bfloat16

D_MODEL = 1024
A_CHUNK = 128
A_HALF = 3 * D_MODEL
A_GROUPS = 8
A_GROUP_DIM = A_HALF // A_GROUPS
HG_HEADS = 8
HG_DK = 128
HG_CHUNK = 64
HG_SUB = 32
FORGET_FLOOR = 1e-20
D_FF = 7 * D_MODEL // 2
N_EXPERTS = 8
NORM_EPS = 1e-6
LANES = 128
SUBLANES = 8
SAMPLE_LEN = 4
MASKED_LOG = -1e30

VMEM_LIMIT = 56 * 1024 * 1024


def _params(*sem):
    return pltpu.CompilerParams(dimension_semantics=sem, vmem_limit_bytes=VMEM_LIMIT)


def _rms(x, g):
    ms = jnp.mean(x * x, axis=-1, keepdims=True)
    return x * lax.rsqrt(ms + NORM_EPS) * g


def _gelu(y):
    return 0.5 * y * (1.0 + lax.erf(y * math.sqrt(0.5)))


def _identity(y):
    return y


def _norm_matmul_kernel(x_ref, g_ref, w_ref, o_ref, h_ref, *, act):
    @pl.when(pl.program_id(1) == 0)
    def _():
        h_ref[...] = _rms(x_ref[...], g_ref[...]).astype(BF16)

    y = jnp.dot(h_ref[...], w_ref[...].astype(BF16), preferred_element_type=F32)
    o_ref[...] = act(y).astype(o_ref.dtype)


def _norm_matmul(x, g_all, layer, w_all, w_layer, act, out_dtype, tm, tn):
    T = x.shape[0]
    N = w_all.shape[-1]
    return pl.pallas_call(
        functools.partial(_norm_matmul_kernel, act=act),
        grid=(T // tm, N // tn),
        in_specs=[
            pl.BlockSpec((tm, D_MODEL), lambda i, j: (i, 0)),
            pl.BlockSpec((None, 1, D_MODEL), lambda i, j: (layer, 0, 0)),
            pl.BlockSpec((None, D_MODEL, tn), lambda i, j: (w_layer, 0, j)),
        ],
        out_specs=pl.BlockSpec((tm, tn), lambda i, j: (i, j)),
        out_shape=jax.ShapeDtypeStruct((T, N), out_dtype),
        scratch_shapes=[pltpu.VMEM((tm, D_MODEL), BF16)],
        compiler_params=_params("parallel", "arbitrary"),
        name="norm_matmul",
    )(x, g_all, w_all)


def _gmlp_gate_kernel(z_ref, lng_ref, lnb_ref, wmix_ref, bias_ref, wout_ref, x_ref, o_ref, v_ref):
    tm = z_ref.shape[0]
    zv = z_ref[:, A_HALF:].astype(F32)
    mu = jnp.mean(zv, axis=-1, keepdims=True)
    xc = zv - mu
    rstd = lax.rsqrt(jnp.mean(xc * xc, axis=-1, keepdims=True) + NORM_EPS)
    v = xc * rstd * lng_ref[...] + lnb_ref[...]
    v_ref[...] = v
    vb = v.astype(BF16)
    groups = []
    for g in range(A_GROUPS):
        cols = slice(g * A_GROUP_DIM, (g + 1) * A_GROUP_DIM)
        wm = wmix_ref[g].astype(BF16)
        bias = jnp.concatenate([bias_ref[g]] * (A_GROUP_DIM // LANES), axis=1)
        rows = []
        for c in range(tm // A_CHUNK):
            sl = slice(c * A_CHUNK, (c + 1) * A_CHUNK)
            s = jnp.dot(wm, vb[sl, cols], preferred_element_type=F32) + bias
            rows.append((z_ref[sl, cols].astype(F32) * s).astype(BF16))
        groups.append(jnp.concatenate(rows, axis=0))
    gated = jnp.concatenate(groups, axis=1)
    o_ref[...] = x_ref[...] + jnp.dot(gated, wout_ref[...], preferred_element_type=F32)


def _gmlp_gate(z, x, ln_g, ln_b, wmix, bias, w_out, j, n_prompt_blocks, tm):
    T = x.shape[0]
    nb = T // tm

    def kind(i):
        return jnp.where(i >= n_prompt_blocks, 1, 0)

    return pl.pallas_call(
        _gmlp_gate_kernel,
        grid=(nb,),
        in_specs=[
            pl.BlockSpec((tm, 2 * A_HALF), lambda i: (i, 0)),
            pl.BlockSpec((None, 1, A_HALF), lambda i: (j, 0, 0)),
            pl.BlockSpec((None, 1, A_HALF), lambda i: (j, 0, 0)),
            pl.BlockSpec((None, A_GROUPS, A_CHUNK, A_CHUNK), lambda i: (kind(i), 0, 0, 0)),
            pl.BlockSpec((None, A_GROUPS, A_CHUNK, LANES), lambda i: (kind(i), 0, 0, 0)),
            pl.BlockSpec((None, A_HALF, D_MODEL), lambda i: (j, 0, 0)),
            pl.BlockSpec((tm, D_MODEL), lambda i: (i, 0)),
        ],
        out_specs=[
            pl.BlockSpec((tm, D_MODEL), lambda i: (i, 0)),
            pl.BlockSpec((tm, A_HALF), lambda i: (jnp.maximum(i - n_prompt_blocks, 0), 0)),
        ],
        out_shape=[
            jax.ShapeDtypeStruct((T, D_MODEL), F32),
            jax.ShapeDtypeStruct(((nb - n_prompt_blocks) * tm, A_HALF), F32),
        ],
        compiler_params=_params("arbitrary"),
        name="gmlp_gate",
    )(z, ln_g, ln_b, wmix, bias, w_out, x)


FF_SPLIT = 2


def _swiglu_acc(acc, h, wg, wu, wd):
    sub = wg.shape[1] // FF_SPLIT
    for c in range(FF_SPLIT):
        cols = slice(c * sub, (c + 1) * sub)
        a = jnp.dot(h, wg[:, cols].astype(BF16), preferred_element_type=F32)
        b = jnp.dot(h, wu[:, cols].astype(BF16), preferred_element_type=F32)
        m = (a * jax.nn.sigmoid(a) * b).astype(BF16)
        acc = acc + jnp.dot(m, wd[cols, :].astype(BF16), preferred_element_type=F32)
    return acc


def _ffn_kernel(x_ref, g_ref, wg_ref, wu_ref, wd_ref, o_ref, h_ref):
    @pl.when(pl.program_id(1) == 0)
    def _():
        x = x_ref[...]
        h_ref[...] = _rms(x, g_ref[...]).astype(BF16)
        o_ref[...] = x

    o_ref[...] = _swiglu_acc(o_ref[...], h_ref[...], wg_ref[...], wu_ref[...], wd_ref[...])


def _ffn_dense(x, g_all, layer, wg, wu, wd, m, tm, tf):
    T = x.shape[0]
    return pl.pallas_call(
        _ffn_kernel,
        grid=(T // tm, D_FF // tf),
        in_specs=[
            pl.BlockSpec((tm, D_MODEL), lambda i, f: (i, 0)),
            pl.BlockSpec((None, 1, D_MODEL), lambda i, f: (layer, 0, 0)),
            pl.BlockSpec((None, D_MODEL, tf), lambda i, f: (m, 0, f)),
            pl.BlockSpec((None, D_MODEL, tf), lambda i, f: (m, 0, f)),
            pl.BlockSpec((None, tf, D_MODEL), lambda i, f: (m, f, 0)),
        ],
        out_specs=pl.BlockSpec((tm, D_MODEL), lambda i, f: (i, 0)),
        out_shape=jax.ShapeDtypeStruct((T, D_MODEL), F32),
        scratch_shapes=[pltpu.VMEM((tm, D_MODEL), BF16)],
        compiler_params=_params("parallel", "arbitrary"),
        name="ffn_dense",
    )(x, g_all, wg, wu, wd)


_NT = (((1,), (1,)), ((), ()))
_TN = (((0,), (0,)), ((), ()))


HG_GROUP = 8
MAX_BLOCK_DECAY = 75.0


def _hgrn_gates(qpre, fpre, lb):
    q = qpre * jax.nn.sigmoid(qpre)
    sig = jax.nn.sigmoid(fpre)
    f = lb + (1.0 - lb) * sig
    logf = jnp.log(jnp.maximum(f, FORGET_FLOOR))
    k = (1.0 - lb) * (1.0 - sig)
    return q, k, logf


def _cumsum_rows(x):
    C = x.shape[0]
    if C >= HG_SUB:
        r = lax.broadcasted_iota(jnp.int32, (C, C), 0)
        c = lax.broadcasted_iota(jnp.int32, (C, C), 1)
        tri = jnp.where(r >= c, 1.0, 0.0).astype(F32)
        return jnp.dot(tri, x, preferred_element_type=F32, precision=lax.Precision.HIGHEST)
    row = lax.broadcasted_iota(jnp.int32, (C, 1), 0)
    out = jnp.zeros_like(x)
    for s in range(C):
        out = out + jnp.where(row >= s, x[s:s + 1], 0.0)
    return out


def _chunk_prep(q, k, logf):
    C = q.shape[0]
    b = _cumsum_rows(logf)
    b_last = b[C - 1:C]
    qe = (q * jnp.exp(b)).astype(BF16)
    kd = (k * jnp.exp(b_last - b)).astype(BF16)
    return b, qe, kd, jnp.exp(b_last)


def _intra_exact(q, k, v, b, sub):
    C = q.shape[0]
    row = lax.broadcasted_iota(jnp.int32, (sub, 1), 0)
    parts = []
    for blk in range(C // sub):
        lo = blk * sub
        b_i, q_i, k_i, v_i = b[lo:lo + sub], q[lo:lo + sub], k[lo:lo + sub], v[lo:lo + sub]
        if blk == 0:
            o_i = jnp.zeros((sub, HG_DK), F32)
        else:
            ref = b[lo - 1:lo]
            qs = (q_i * jnp.exp(b_i - ref)).astype(BF16)
            ks = (k[:lo] * jnp.exp(ref - b[:lo])).astype(BF16)
            a = lax.dot_general(qs, ks, _NT, preferred_element_type=F32)
            o_i = jnp.dot(a.astype(BF16), v[:lo].astype(BF16), preferred_element_type=F32)
        for s in range(sub):
            rel = jnp.where(row >= s, b_i - b_i[s:s + 1], MASKED_LOG)
            a_col = jnp.sum(q_i * k_i[s:s + 1] * jnp.exp(rel), axis=-1, keepdims=True)
            o_i = o_i + a_col * v_i[s:s + 1]
        parts.append(o_i)
    return parts[0] if len(parts) == 1 else jnp.concatenate(parts, axis=0)


def _intra_factored(q, k, v, b, sub):
    C = q.shape[0]
    vb = v.astype(BF16)
    heads = [slice(h * HG_DK, (h + 1) * HG_DK) for h in range(q.shape[1] // HG_DK)]
    scores = []
    for blk in range(C // sub):
        lo, hi = blk * sub, (blk + 1) * sub
        if blk == 0:
            qs = q[:hi] * jnp.exp(b[:hi])
            ks = k[:hi] * jnp.exp(-b[:hi])
        else:
            ref = b[lo - 1:lo]
            qs = q[lo:hi] * jnp.exp(b[lo:hi] - ref)
            ks = k[:hi] * jnp.exp(ref - b[:hi])
        qs, ks = qs.astype(BF16), ks.astype(BF16)
        r = lax.broadcasted_iota(jnp.int32, (sub, hi), 0)
        c = lax.broadcasted_iota(jnp.int32, (sub, hi), 1)
        row = []
        for cols in heads:
            a = lax.dot_general(qs[:, cols], ks[:, cols], _NT, preferred_element_type=F32)
            row.append(jnp.where(c <= r + lo, a, 0.0).astype(BF16))
        scores.append(row)
    parts = []
    for blk, row in enumerate(scores):
        hi = (blk + 1) * sub
        outs = [jnp.dot(a, vb[:hi, cols], preferred_element_type=F32) for a, cols in zip(row, heads)]
        parts.append(outs[0] if len(outs) == 1 else jnp.concatenate(outs, axis=1))
    return jnp.concatenate(parts, axis=0)


def _head_norm_gate(o, ng, gate):
    return (o * lax.rsqrt(jnp.mean(o * o, axis=-1, keepdims=True) + NORM_EPS) * ng * gate)


def _hgrn_prompt_kernel(q_ref, f_ref, i_ref, g_ref, lb_ref, ng_ref, o_ref, sfin_ref,
                        st_ref, qs_ref, ks_ref, lf_ref, qe_ref, kd_ref, eb_ref, oi_ref):
    tb = pl.program_id(2)
    rows_total = q_ref.shape[0]
    n_chunks = rows_total // HG_CHUNK

    @pl.when(tb == 0)
    def _():
        st_ref[...] = jnp.zeros_like(st_ref)

    q, k, logf = _hgrn_gates(q_ref[...], f_ref[...], lb_ref[...])
    qs_ref[...] = q
    ks_ref[...] = k
    lf_ref[...] = logf
    block_decay = jnp.sum(logf.reshape(rows_total // HG_SUB, HG_SUB, logf.shape[-1]), axis=1)
    mild = jnp.min(block_decay) >= -MAX_BLOCK_DECAY

    heads = [slice(h * HG_DK, (h + 1) * HG_DK) for h in range(HG_GROUP)]

    def exact_group(qg, kg, vg, b, sub):
        return jnp.concatenate([_intra_exact(qg[:, c], kg[:, c], vg[:, c], b[:, c], sub) for c in heads], axis=1)

    def intra_pass(intra):
        def body(ci, carry):
            rows = pl.ds(pl.multiple_of(ci * HG_CHUNK, HG_CHUNK), HG_CHUNK)
            qg, kg, vg = qs_ref[rows, :], ks_ref[rows, :], i_ref[rows, :]
            b, qe, kd, eb_last = _chunk_prep(qg, kg, lf_ref[rows, :])
            qe_ref[rows, :] = qe
            kd_ref[rows, :] = kd
            eb_ref[pl.ds(pl.multiple_of(ci * SUBLANES, SUBLANES), SUBLANES), :] = \
                jnp.broadcast_to(eb_last, (SUBLANES, eb_last.shape[1]))
            oi_ref[rows, :] = intra(qg, kg, vg, b, HG_SUB)
            return carry
        lax.fori_loop(0, n_chunks, body, 0, unroll=4)

    @pl.when(mild)
    def _():
        intra_pass(_intra_factored)

    @pl.when(jnp.logical_not(mild))
    def _():
        intra_pass(exact_group)

    ng = ng_ref[...]

    def state_body(ci, carry):
        rows = pl.ds(pl.multiple_of(ci * HG_CHUNK, HG_CHUNK), HG_CHUNK)
        st = st_ref[...]
        stb = st.astype(BF16)
        qe, kd, oi = qe_ref[rows, :], kd_ref[rows, :], oi_ref[rows, :]
        vb = i_ref[rows, :].astype(BF16)
        eb_last = eb_ref[pl.ds(pl.multiple_of(ci * SUBLANES, SUBLANES), 1), :]
        outs, adds = [], []
        for c in heads:
            o = oi[:, c] + lax.dot_general(qe[:, c], stb[:, c], _NT, preferred_element_type=F32)
            outs.append(o * lax.rsqrt(jnp.mean(o * o, axis=-1, keepdims=True) + NORM_EPS))
            adds.append(lax.dot_general(vb[:, c], kd[:, c], _TN, preferred_element_type=F32))
        st_ref[...] = st * eb_last + jnp.concatenate(adds, axis=1)
        gate = jax.nn.sigmoid(g_ref[rows, :])
        o_ref[rows, :] = (jnp.concatenate(outs, axis=1) * ng * gate).astype(o_ref.dtype)
        return carry

    lax.fori_loop(0, n_chunks, state_body, 0)

    @pl.when(tb == pl.num_programs(2) - 1)
    def _():
        for h, c in enumerate(heads):
            sfin_ref[h] = st_ref[:, c].T


def _hgrn_prompt(proj, lbs, ng_all, j, n_seq, L, lb_rows):
    nb = L // lb_rows
    ng_groups = HG_HEADS // HG_GROUP
    gw = HG_GROUP * HG_DK

    def field(k):
        return pl.BlockSpec((lb_rows, gw), lambda n, h, t: (n * nb + t, k * ng_groups + h))

    vec = pl.BlockSpec((None, 1, gw), lambda n, h, t: (j, 0, h))
    return pl.pallas_call(
        _hgrn_prompt_kernel,
        grid=(n_seq, ng_groups, nb),
        in_specs=[field(0), field(1), field(2), field(3), vec, vec],
        out_specs=[
            pl.BlockSpec((lb_rows, gw), lambda n, h, t: (n * nb + t, h)),
            pl.BlockSpec((None, HG_GROUP, HG_DK, HG_DK), lambda n, h, t: (n, h, 0, 0)),
        ],
        out_shape=[
            jax.ShapeDtypeStruct((n_seq * L, D_MODEL), BF16),
            jax.ShapeDtypeStruct((n_seq, HG_HEADS, HG_DK, HG_DK), F32),
        ],
        scratch_shapes=[
            pltpu.VMEM((HG_DK, gw), F32),
            pltpu.VMEM((lb_rows, gw), F32),
            pltpu.VMEM((lb_rows, gw), F32),
            pltpu.VMEM((lb_rows, gw), F32),
            pltpu.VMEM((lb_rows, gw), BF16),
            pltpu.VMEM((lb_rows, gw), BF16),
            pltpu.VMEM((lb_rows // HG_CHUNK * SUBLANES, gw), F32),
            pltpu.VMEM((lb_rows, gw), F32),
        ],
        compiler_params=_params("parallel", "parallel", "arbitrary"),
        name="hgrn_prompt",
    )(proj, proj, proj, proj, lbs, ng_all)


def _hgrn_sample_kernel(q_ref, f_ref, i_ref, g_ref, lb_ref, ng_ref, s0_ref, o_ref, sfin_ref):
    n_seq = s0_ref.shape[0]
    q, k, logf = _hgrn_gates(q_ref[...], f_ref[...], lb_ref[...])
    v = i_ref[...]
    row = lax.broadcasted_iota(jnp.int32, (SUBLANES, 1), 0)
    first_half = row < SAMPLE_LEN
    prep = []
    for seq in range(n_seq):
        rows = slice(seq // 2 * SUBLANES, (seq // 2 + 1) * SUBLANES)
        mine = first_half if seq % 2 == 0 else jnp.logical_not(first_half)
        qm, km, vm, lm = (jnp.where(mine, t[rows], 0.0) for t in (q, k, v, logf))
        b, qe, kd, eb_last = _chunk_prep(qm, km, lm)
        prep.append((qe, kd, eb_last, vm.astype(BF16), _intra_exact(qm, km, vm, b, SUBLANES)))
    states = [s0_ref[seq].T for seq in range(n_seq)]
    outs = [oi + lax.dot_general(qe, st.astype(BF16), _NT, preferred_element_type=F32)
            for (qe, _, _, _, oi), st in zip(prep, states)]
    news = [st * eb_last + lax.dot_general(vb, kd, _TN, preferred_element_type=F32)
            for (_, kd, eb_last, vb, _), st in zip(prep, states)]
    sfin_ref[...] = jnp.stack([st.T for st in news])
    tiles = [jnp.where(first_half, outs[2 * p], outs[2 * p + 1]) for p in range(n_seq // 2)]
    gate = jax.nn.sigmoid(g_ref[...])
    o_ref[...] = _head_norm_gate(jnp.concatenate(tiles, axis=0), ng_ref[...], gate).astype(o_ref.dtype)


def _hgrn_sample(proj, row0, state_all, lbs, ng_all, j, n_seq, sb):
    H = HG_HEADS
    rb = sb * SAMPLE_LEN
    assert row0 % rb == 0 and sb % 2 == 0
    r0 = row0 // rb

    def field(k):
        return pl.BlockSpec((rb, HG_DK), lambda s, h: (r0 + s, k * H + h))

    vec = pl.BlockSpec((None, 1, HG_DK), lambda s, h: (j, 0, h))
    return pl.pallas_call(
        _hgrn_sample_kernel,
        grid=(n_seq // sb, H),
        in_specs=[field(0), field(1), field(2), field(3), vec, vec,
                  pl.BlockSpec((None, sb, None, HG_DK, HG_DK), lambda s, h: (j, s, h, 0, 0))],
        out_specs=[
            pl.BlockSpec((rb, HG_DK), lambda s, h: (s, h)),
            pl.BlockSpec((None, sb, None, HG_DK, HG_DK), lambda s, h: (j, s, h, 0, 0)),
        ],
        out_shape=[
            jax.ShapeDtypeStruct((n_seq * SAMPLE_LEN, D_MODEL), BF16),
            jax.ShapeDtypeStruct(state_all.shape, state_all.dtype),
        ],
        input_output_aliases={6: 1},
        compiler_params=_params("parallel", "parallel"),
        name="hgrn_sample",
    )(proj, proj, proj, proj, lbs, ng_all, state_all)


META_E1, META_E2, META_G1, META_G2, META_R1, META_R2 = range(6)
TOKEN_BLOCK = 128


def _router_kernel(ap_ref, as_ref, w_ref, x_ref, g_ref, rw_ref, xo_ref, h_ref, meta_ref, cnt_ref, cb_ref,
                   carry_ref, *, n_prompt_blocks):
    i = pl.program_id(0)
    tm = x_ref.shape[0]

    @pl.when(i == 0)
    def _():
        carry_ref[...] = jnp.zeros_like(carry_ref)

    a = jnp.where(i < n_prompt_blocks, ap_ref[...], as_ref[...])
    x = x_ref[...] + jnp.dot(a, w_ref[...], preferred_element_type=F32)
    xo_ref[...] = x
    h = _rms(x, g_ref[...])
    h_ref[...] = h.astype(h_ref.dtype)
    logits = jnp.dot(h.astype(BF16), rw_ref[...].astype(BF16), preferred_element_type=F32)
    lane = lax.broadcasted_iota(jnp.int32, (tm, LANES), 1)
    neg = -jnp.inf
    lg = jnp.where(lane < N_EXPERTS, logits, neg)
    m1 = jnp.max(lg, axis=-1, keepdims=True)
    e1 = jnp.min(jnp.where(lg == m1, lane, LANES), axis=-1, keepdims=True)
    lg2 = jnp.where(lane == e1, neg, lg)
    m2 = jnp.max(lg2, axis=-1, keepdims=True)
    e2 = jnp.min(jnp.where(lg2 == m2, lane, LANES), axis=-1, keepdims=True)
    ex = jnp.exp(m2 - m1)
    g1 = 1.0 / (1.0 + ex)
    g2 = ex / (1.0 + ex)

    onehot = jnp.where((lane == e1) | (lane == e2), 1.0, 0.0)
    r = lax.broadcasted_iota(jnp.int32, (tm, tm), 0)
    c = lax.broadcasted_iota(jnp.int32, (tm, tm), 1)
    before = jnp.where(r > c, 1.0, 0.0).astype(BF16)
    seen = jnp.dot(before, onehot.astype(BF16), preferred_element_type=F32) + carry_ref[0:1, :]
    r1 = jnp.sum(jnp.where(lane == e1, seen, 0.0), axis=-1, keepdims=True)
    r2 = jnp.sum(jnp.where(lane == e2, seen, 0.0), axis=-1, keepdims=True)
    for blk in range(tm // TOKEN_BLOCK):
        cb_ref[blk] = jnp.broadcast_to(seen[blk * TOKEN_BLOCK:blk * TOKEN_BLOCK + 1], (SUBLANES, LANES))
    total = carry_ref[0:1, :] + jnp.sum(onehot, axis=0, keepdims=True)
    carry_ref[...] = jnp.broadcast_to(total, carry_ref.shape)
    cnt_ref[...] = jnp.broadcast_to(total, cnt_ref.shape)

    meta = jnp.zeros((tm, LANES), F32)
    for idx, val in ((META_E1, e1.astype(F32)), (META_E2, e2.astype(F32)), (META_G1, g1),
                     (META_G2, g2), (META_R1, r1), (META_R2, r2)):
        meta = jnp.where(lane == idx, val, meta)
    meta_ref[...] = meta


def _router(a_prompt, a_sample, w_all, x, g_all, layer, rw_pad, m, tm):
    T = x.shape[0]
    npb = a_prompt.shape[0] // tm
    assert a_prompt.shape[0] % tm == 0 and a_sample.shape[0] % tm == 0
    return pl.pallas_call(
        functools.partial(_router_kernel, n_prompt_blocks=npb),
        grid=(T // tm,),
        in_specs=[
            pl.BlockSpec((tm, D_MODEL), lambda i: (jnp.minimum(i, npb - 1), 0)),
            pl.BlockSpec((tm, D_MODEL), lambda i: (jnp.maximum(i - npb, 0), 0)),
            pl.BlockSpec((None, D_MODEL, D_MODEL), lambda i: (m, 0, 0)),
            pl.BlockSpec((tm, D_MODEL), lambda i: (i, 0)),
            pl.BlockSpec((None, 1, D_MODEL), lambda i: (layer, 0, 0)),
            pl.BlockSpec((None, D_MODEL, LANES), lambda i: (m, 0, 0)),
        ],
        out_specs=[
            pl.BlockSpec((tm, D_MODEL), lambda i: (i, 0)),
            pl.BlockSpec((tm, D_MODEL), lambda i: (i, 0)),
            pl.BlockSpec((tm, LANES), lambda i: (i, 0)),
            pl.BlockSpec((SUBLANES, LANES), lambda i: (0, 0)),
            pl.BlockSpec((tm // TOKEN_BLOCK, SUBLANES, LANES), lambda i: (i, 0, 0)),
        ],
        out_shape=[
            jax.ShapeDtypeStruct((T, D_MODEL), F32),
            jax.ShapeDtypeStruct((T, D_MODEL), BF16),
            jax.ShapeDtypeStruct((T, LANES), F32),
            jax.ShapeDtypeStruct((SUBLANES, LANES), F32),
            jax.ShapeDtypeStruct((T // TOKEN_BLOCK, SUBLANES, LANES), F32),
        ],
        scratch_shapes=[pltpu.VMEM((SUBLANES, LANES), F32)],
        compiler_params=_params("arbitrary"),
        name="moe_router",
    )(a_prompt, a_sample, w_all, x, g_all, rw_pad)


SLOT_BLOCK = 256
WINDOW_BLOCKS = 10
TOKEN_WINDOW = WINDOW_BLOCKS * TOKEN_BLOCK


def _dispatch_kernel(sb_ref, ws_ref, lo_ref, first_ref, n_ref, h_ref, dest_ref, o_ref):
    i = pl.program_id(0)

    @pl.when(i < n_ref[0])
    def _():
        slot = sb_ref[i] * SLOT_BLOCK + lax.broadcasted_iota(jnp.int32, (SLOT_BLOCK, 1), 0)
        token = ws_ref[i] + lax.broadcasted_iota(jnp.int32, (1, TOKEN_WINDOW), 1)
        dest = jnp.where(token >= lo_ref[i], dest_ref[...], -1)
        hit = (dest[0:1, :] == slot) | (dest[1:2, :] == slot)
        rows = jnp.dot(jnp.where(hit, 1.0, 0.0).astype(BF16), h_ref[...], preferred_element_type=F32)

        @pl.when(first_ref[i] == 1)
        def _():
            o_ref[...] = rows.astype(o_ref.dtype)

        @pl.when(first_ref[i] == 0)
        def _():
            o_ref[...] = (o_ref[...].astype(F32) + rows).astype(o_ref.dtype)


def _dispatch(h, dest_t, items, n_slots):
    def at(i, sb, ws, lo, fi, ni):
        return pl.multiple_of(ws[i], TOKEN_BLOCK)

    grid_spec = pltpu.PrefetchScalarGridSpec(
        num_scalar_prefetch=5,
        grid=(items[0].shape[0],),
        in_specs=[
            pl.BlockSpec((pl.Element(TOKEN_WINDOW), pl.Element(D_MODEL)), lambda *a: (at(*a), 0)),
            pl.BlockSpec((pl.Element(SUBLANES), pl.Element(TOKEN_WINDOW)), lambda *a: (0, at(*a))),
        ],
        out_specs=pl.BlockSpec((SLOT_BLOCK, D_MODEL), lambda i, sb, ws, lo, fi, ni: (sb[i], 0)),
    )
    return pl.pallas_call(
        _dispatch_kernel,
        grid_spec=grid_spec,
        out_shape=jax.ShapeDtypeStruct((n_slots, D_MODEL), BF16),
        compiler_params=_params("arbitrary"),
        name="moe_dispatch",
    )(*items, h, dest_t)


def _dispatch_items(cblk, counts, pstart, pend, block_expert, bm, n_slots, T):
    n_tb = T // TOKEN_BLOCK
    n_sb = n_slots // SLOT_BLOCK
    sb_start = jnp.arange(n_sb, dtype=jnp.int32) * SLOT_BLOCK
    e_sb = block_expert[sb_start // bm]
    r0 = sb_start - pstart[e_sb]
    c_sb = counts[e_sb]
    in_region = sb_start < pend[e_sb]
    has_tokens = r0 < c_sb
    r1 = jnp.minimum(r0 + SLOT_BLOCK, c_sb) - 1
    before = cblk[:, 0, :N_EXPERTS].astype(jnp.int32)[:, e_sb]
    tb_first = jnp.where(has_tokens, jnp.sum(before <= r0[None, :], axis=0) - 1, 0)
    tb_last = jnp.where(has_tokens, jnp.sum(before <= r1[None, :], axis=0) - 1, 0)
    n_win = jnp.where(in_region, (tb_last - tb_first) // WINDOW_BLOCKS + 1, 1)
    ends = jnp.cumsum(n_win)
    total = ends[-1]
    max_items = n_sb + N_EXPERTS * (-(-n_tb // WINDOW_BLOCKS) + 1)
    i = jnp.minimum(jnp.arange(max_items, dtype=jnp.int32), total - 1)
    sb = jnp.sum(ends[None, :] <= i[:, None], axis=1).astype(jnp.int32)
    k = i - (ends - n_win)[sb]
    lo = ((tb_first[sb] + k * WINDOW_BLOCKS) * TOKEN_BLOCK).astype(jnp.int32)
    start = jnp.minimum(lo, T - TOKEN_WINDOW)
    return sb, start, lo, (k == 0).astype(jnp.int32), total.astype(jnp.int32).reshape(1)


def _experts_kernel(be_ref, nu_ref, xs_ref, wg_ref, wu_ref, wd_ref, o_ref, acc_ref):
    b = pl.program_id(0)
    f = pl.program_id(1)
    used = b < nu_ref[0]

    @pl.when(used & (f == 0))
    def _():
        acc_ref[...] = jnp.zeros_like(acc_ref)

    @pl.when(used)
    def _():
        acc_ref[...] = _swiglu_acc(acc_ref[...], xs_ref[...], wg_ref[...], wu_ref[...], wd_ref[...])

        @pl.when(f == pl.num_programs(1) - 1)
        def _():
            o_ref[...] = acc_ref[...].astype(o_ref.dtype)

    @pl.when(jnp.logical_not(used) & (f == 0))
    def _():
        o_ref[...] = jnp.zeros_like(o_ref)


def _experts(xs, block_expert, n_used, wg, wu, wd, m, bm, tf):
    n_blocks = xs.shape[0] // bm
    nf = D_FF // tf

    def row(b, nu):
        return jnp.minimum(b, nu[0] - 1)

    def fidx(b, f, nu):
        return jnp.where(b < nu[0], f, nf - 1)

    grid_spec = pltpu.PrefetchScalarGridSpec(
        num_scalar_prefetch=2,
        grid=(n_blocks, nf),
        in_specs=[
            pl.BlockSpec((bm, D_MODEL), lambda b, f, be, nu: (row(b, nu), 0)),
            pl.BlockSpec((None, None, D_MODEL, tf), lambda b, f, be, nu: (m, be[b], 0, fidx(b, f, nu))),
            pl.BlockSpec((None, None, D_MODEL, tf), lambda b, f, be, nu: (m, be[b], 0, fidx(b, f, nu))),
            pl.BlockSpec((None, None, tf, D_MODEL), lambda b, f, be, nu: (m, be[b], fidx(b, f, nu), 0)),
        ],
        out_specs=pl.BlockSpec((bm, D_MODEL), lambda b, f, be, nu: (b, 0)),
        scratch_shapes=[pltpu.VMEM((bm, D_MODEL), F32)],
    )
    return pl.pallas_call(
        _experts_kernel,
        grid_spec=grid_spec,
        out_shape=jax.ShapeDtypeStruct((n_blocks * bm, D_MODEL), BF16),
        compiler_params=_params("arbitrary", "arbitrary"),
        name="moe_experts",
    )(block_expert, n_used, xs, wg, wu, wd)


BF16_ROWS = 16
WINDOW = TOKEN_BLOCK + BF16_ROWS


def _combine_kernel(win_ref, ps_ref, pe_ref, x_ref, meta_ref, *refs, n_prompt_blocks=None):
    ys_refs, out_refs = refs[:N_EXPERTS], refs[N_EXPERTS:]
    blk = pl.program_id(0)
    meta = meta_ref[...]
    e1, e2 = meta[:, META_E1:META_E1 + 1], meta[:, META_E2:META_E2 + 1]
    d1, d2 = meta[:, META_R1:META_R1 + 1], meta[:, META_R2:META_R2 + 1]
    for e in range(N_EXPERTS):
        start = ps_ref[e].astype(F32)
        d1 = d1 + jnp.where(e1 == e, start, 0.0)
        d2 = d2 + jnp.where(e2 == e, start, 0.0)
    d1, d2, g1, g2 = (jnp.broadcast_to(t, (TOKEN_BLOCK, WINDOW))
                      for t in (d1, d2, meta[:, META_G1:META_G1 + 1], meta[:, META_G2:META_G2 + 1]))
    lane = lax.broadcasted_iota(jnp.int32, (1, WINDOW), 1)
    acc = x_ref[...]
    for e in range(N_EXPERTS):
        slot = win_ref[blk * N_EXPERTS + e] + lane
        slot = jnp.where((slot >= ps_ref[e]) & (slot < pe_ref[e]), slot, -1).astype(F32)
        sel = jnp.where(d1 == slot, g1, 0.0) + jnp.where(d2 == slot, g2, 0.0)
        acc = acc + jnp.dot(sel.astype(BF16), ys_refs[e][...], preferred_element_type=F32)
    if n_prompt_blocks is None:
        out_refs[0][...] = acc
    else:
        g_ref, prompt_ref, sample_ref = out_refs
        y = _rms(acc, g_ref[...])

        @pl.when(blk < n_prompt_blocks)
        def _():
            prompt_ref[...] = y

        @pl.when(blk >= n_prompt_blocks)
        def _():
            sample_ref[...] = y


def _combine(x, meta, ys, windows, pstart, pend, final_g=None, n_prompt_rows=None):
    T = x.shape[0]
    nb = T // TOKEN_BLOCK

    def window(e):
        return pl.BlockSpec((pl.Element(WINDOW), pl.Element(D_MODEL)),
                            lambda i, win, ps, pe: (pl.multiple_of(win[i * N_EXPERTS + e], BF16_ROWS), 0))

    in_specs = [pl.BlockSpec((TOKEN_BLOCK, D_MODEL), lambda i, win, ps, pe: (i, 0)),
                pl.BlockSpec((TOKEN_BLOCK, LANES), lambda i, win, ps, pe: (i, 0))] \
        + [window(e) for e in range(N_EXPERTS)]
    operands = [windows, pstart, pend, x, meta] + [ys] * N_EXPERTS
    if final_g is None:
        npb = None
        out_specs = pl.BlockSpec((TOKEN_BLOCK, D_MODEL), lambda i, win, ps, pe: (i, 0))
        out_shape = jax.ShapeDtypeStruct((T, D_MODEL), F32)
    else:
        assert n_prompt_rows % TOKEN_BLOCK == 0
        npb = n_prompt_rows // TOKEN_BLOCK
        in_specs.append(pl.BlockSpec((1, D_MODEL), lambda i, win, ps, pe: (0, 0)))
        operands.append(final_g)
        out_specs = [pl.BlockSpec((TOKEN_BLOCK, D_MODEL), lambda i, win, ps, pe: (jnp.minimum(i, npb - 1), 0)),
                     pl.BlockSpec((TOKEN_BLOCK, D_MODEL), lambda i, win, ps, pe: (jnp.maximum(i - npb, 0), 0))]
        out_shape = [jax.ShapeDtypeStruct((n_prompt_rows, D_MODEL), F32),
                     jax.ShapeDtypeStruct((T - n_prompt_rows, D_MODEL), F32)]
    grid_spec = pltpu.PrefetchScalarGridSpec(
        num_scalar_prefetch=3, grid=(nb,), in_specs=in_specs, out_specs=out_specs)
    return pl.pallas_call(
        functools.partial(_combine_kernel, n_prompt_blocks=npb),
        grid_spec=grid_spec,
        out_shape=out_shape,
        compiler_params=_params("arbitrary"),
        name="moe_combine",
    )(*operands)


def _mixer_out_and_moe(o_p, o_s, w_out, x, norm_ffn_g, layer, rw_pad, wg, wu, wd, m, tm, bm, tf,
                       final_g=None, n_prompt_rows=None):
    T = x.shape[0]
    x, h, meta, cnt, cblk = _router(o_p, o_s, w_out, x, norm_ffn_g, layer, rw_pad, m, tm)
    e = meta[:, META_E1:META_E2 + 1].astype(jnp.int32)
    rank = meta[:, META_R1:META_R2 + 1].astype(jnp.int32)
    counts = cnt[0, :N_EXPERTS].astype(jnp.int32)
    padded = (counts + bm - 1) // bm * bm
    pend = jnp.cumsum(padded)
    pstart = pend - padded
    dest = pstart[e] + rank
    n_blocks = (T * 2) // bm + N_EXPERTS + 1
    block_start = jnp.arange(n_blocks, dtype=jnp.int32) * bm
    block_expert = jnp.minimum(jnp.sum(pend[None, :] <= block_start[:, None], axis=1),
                               N_EXPERTS - 1).astype(jnp.int32)
    n_used = (pend[-1] // bm).astype(jnp.int32).reshape(1)
    dest_t = jnp.full((SUBLANES, T), -1, jnp.int32).at[0:2].set(dest.T)
    items = _dispatch_items(cblk, counts, pstart, pend, block_expert, bm, n_blocks * bm, T)
    xs = _dispatch(h, dest_t, items, n_blocks * bm)
    ys = _experts(xs, block_expert, n_used, wg, wu, wd, m, bm, tf)
    run_start = pstart[None, :] + cblk[:, 0, :N_EXPERTS].astype(jnp.int32)
    windows = (run_start // BF16_ROWS * BF16_ROWS).reshape(-1).astype(jnp.int32)
    return _combine(x, meta, ys, windows, pstart.astype(jnp.int32), pend.astype(jnp.int32),
                    final_g, n_prompt_rows)


def _tril(w):
    n = w.shape[-1]
    return jnp.where(jnp.tril(jnp.ones((n, n), dtype=bool)), w, jnp.zeros((), w.dtype))


def _mix_tables(w_s, b_s, sample_len):
    reps = A_CHUNK // sample_len
    eye = jnp.eye(reps, dtype=w_s.dtype)
    w_prompt = _tril(w_s)
    w_small = _tril(w_s[:, :sample_len, :sample_len])
    w_sample = jax.vmap(lambda w: jnp.kron(eye, w))(w_small)
    b_prompt = b_s
    b_sample = jnp.tile(b_s[:, :sample_len], (1, reps))
    wmix = jnp.stack([w_prompt, w_sample])
    bias = jnp.stack([b_prompt, b_sample])[..., None]
    return wmix, jnp.broadcast_to(bias, bias.shape[:-1] + (LANES,))


def kernel(x_prompt, x_sample, state_hgrn, norm_mix_g, norm_ffn_g, final_norm_g, a_w_in, a_ln_g, a_ln_b, a_w_s, a_b_s, a_w_out, b_w_in, b_lb_logits, b_norm_g, b_w_out, ffn_w_gate, ffn_w_up, ffn_w_down, moe_router, moe_w_gate, moe_w_up, moe_w_down):
    n_p, L, d = x_prompt.shape
    n_s, l_s, _ = x_sample.shape
    assert d == D_MODEL and l_s == SAMPLE_LEN and L % A_CHUNK == 0
    T_p, T_s = n_p * L, n_s * l_s
    T = T_p + T_s
    depth = norm_mix_g.shape[0]
    assert depth % 2 == 0, "the last layer must be an HGRN2 + MoE layer"

    tm_gate = 2 * A_CHUNK
    assert T_p % tm_gate == 0 and T_s % tm_gate == 0
    tm_big = next(t for t in (1536, 768, 512, 256, 128) if T % t == 0)
    bm = 1024
    tf = 512
    lb_rows = min(L, 512)

    x = jnp.concatenate([x_prompt.reshape(T_p, d), x_sample.reshape(T_s, d)], axis=0)

    p = jax.nn.softmax(b_lb_logits.astype(F32), axis=0)
    lbs = (jnp.cumsum(p, axis=0) - p[0:1])[:, None, :]
    mix_g = norm_mix_g[:, None, :]
    ffn_g = norm_ffn_g[:, None, :]
    b_ng = b_norm_g[:, None, :]
    ln_g = a_ln_g[:, None, :]
    ln_b = a_ln_b[:, None, :]
    rw_pad = jnp.pad(moe_router, ((0, 0), (0, 0), (0, LANES - N_EXPERTS)))
    a_w_in, a_w_out, b_w_in, b_w_out, ffn_w_gate, ffn_w_up, ffn_w_down = (
        w.astype(BF16) for w in (a_w_in, a_w_out, b_w_in, b_w_out, ffn_w_gate, ffn_w_up, ffn_w_down))

    hg_prompt, v_sample = [], []
    hg_sample = state_hgrn
    for layer in range(depth):
        j = layer // 2
        if layer % 2 == 0:
            z = _norm_matmul(x, mix_g, layer, a_w_in, j, _gelu, BF16, tm_big, 1024)
            wmix, bias = _mix_tables(a_w_s[j], a_b_s[j], l_s)
            x, v = _gmlp_gate(z, x, ln_g, ln_b, wmix, bias, a_w_out, j, T_p // tm_gate, tm_gate)
            v_sample.append(v.reshape(n_s, l_s, A_HALF))
            x = _ffn_dense(x, ffn_g, layer, ffn_w_gate, ffn_w_up, ffn_w_down, j, tm_big, tf)
        else:
            proj = _norm_matmul(x, mix_g, layer, b_w_in, j, _identity, F32, tm_big, 1024)
            o_p, s_p = _hgrn_prompt(proj, lbs, b_ng, j, n_p, L, lb_rows)
            o_s, hg_sample = _hgrn_sample(proj, T_p, hg_sample, lbs, b_ng, j, n_s, 8)
            hg_prompt.append(s_p)
            last = layer == depth - 1
            x = _mixer_out_and_moe(o_p, o_s, b_w_out, x, ffn_g, layer, rw_pad, moe_w_gate, moe_w_up, moe_w_down,
                                   j, math.gcd(T_p, T_s, 512), bm, tf,
                                   final_norm_g[None, :] if last else None, T_p if last else None)

    y_prompt, y_sample = x
    return (y_prompt.reshape(n_p, L, d), y_sample.reshape(n_s, l_s, d),
            jnp.stack(hg_prompt), hg_sample, jnp.stack(v_sample))
```

```python
import functools
import math

import jax
import jax.numpy as jnp
from jax import lax
from jax.experimental import pallas as pl
from jax.experimental.pallas import tpu as pltpu

F32 = jnp.float32
BF16 = jnp.bfloat16

D_MODEL = 1024
A_CHUNK = 128
A_HALF = 3 * D_MODEL
A_GROUPS = 8
A_GROUP_DIM = A_HALF // A_GROUPS
HG_HEADS = 8
HG_DK = 128
HG_CHUNK = 64
HG_SUB = 32
FORGET_FLOOR = 1e-20
D_FF = 7 * D_MODEL // 2
N_EXPERTS = 8
NORM_EPS = 1e-6
LANES = 128
SUBLANES = 8
SAMPLE_LEN = 4
MASKED_LOG = -1e30

VMEM_LIMIT = 56 * 1024 * 1024


def _params(*sem):
    return pltpu.CompilerParams(dimension_semantics=sem, vmem_limit_bytes=VMEM_LIMIT)


def _rms(x, g):
    ms = jnp.mean(x * x, axis=-1, keepdims=True)
    return x * lax.rsqrt(ms + NORM_EPS) * g


def _gelu(y):
    return 0.5 * y * (1.0 + lax.erf(y * math.sqrt(0.5)))


def _identity(y):
    return y


def _norm_matmul_kernel(x_ref, g_ref, w_ref, o_ref, h_ref, *, act):
    @pl.when(pl.program_id(1) == 0)
    def _():
        h_ref[...] = _rms(x_ref[...], g_ref[...]).astype(BF16)

    y = jnp.dot(h_ref[...], w_ref[...].astype(BF16), preferred_element_type=F32)
    o_ref[...] = act(y).astype(o_ref.dtype)


def _norm_matmul(x, g_all, layer, w_all, w_layer, act, out_dtype, tm, tn):
    T = x.shape[0]
    N = w_all.shape[-1]
    return pl.pallas_call(
        functools.partial(_norm_matmul_kernel, act=act),
        grid=(T // tm, N // tn),
        in_specs=[
            pl.BlockSpec((tm, D_MODEL), lambda i, j: (i, 0)),
            pl.BlockSpec((None, 1, D_MODEL), lambda i, j: (layer, 0, 0)),
            pl.BlockSpec((None, D_MODEL, tn), lambda i, j: (w_layer, 0, j)),
        ],
        out_specs=pl.BlockSpec((tm, tn), lambda i, j: (i, j)),
        out_shape=jax.ShapeDtypeStruct((T, N), out_dtype),
        scratch_shapes=[pltpu.VMEM((tm, D_MODEL), BF16)],
        compiler_params=_params("parallel", "arbitrary"),
        name="norm_matmul",
    )(x, g_all, w_all)


def _gmlp_gate_kernel(z_ref, lng_ref, lnb_ref, wmix_ref, bias_ref, wout_ref, x_ref, o_ref, v_ref):
    tm = z_ref.shape[0]
    zv = z_ref[:, A_HALF:].astype(F32)
    mu = jnp.mean(zv, axis=-1, keepdims=True)
    xc = zv - mu
    rstd = lax.rsqrt(jnp.mean(xc * xc, axis=-1, keepdims=True) + NORM_EPS)
    v = xc * rstd * lng_ref[...] + lnb_ref[...]
    v_ref[...] = v
    vb = v.astype(BF16)
    groups = []
    for g in range(A_GROUPS):
        cols = slice(g * A_GROUP_DIM, (g + 1) * A_GROUP_DIM)
        wm = wmix_ref[g].astype(BF16)
        bias = jnp.concatenate([bias_ref[g]] * (A_GROUP_DIM // LANES), axis=1)
        rows = []
        for c in range(tm // A_CHUNK):
            sl = slice(c * A_CHUNK, (c + 1) * A_CHUNK)
            s = jnp.dot(wm, vb[sl, cols], preferred_element_type=F32) + bias
            rows.append((z_ref[sl, cols].astype(F32) * s).astype(BF16))
        groups.append(jnp.concatenate(rows, axis=0))
    gated = jnp.concatenate(groups, axis=1)
    o_ref[...] = x_ref[...] + jnp.dot(gated, wout_ref[...], preferred_element_type=F32)


def _gmlp_gate(z, x, ln_g, ln_b, wmix, bias, w_out, j, n_prompt_blocks, tm):
    T = x.shape[0]
    nb = T // tm

    def kind(i):
        return jnp.where(i >= n_prompt_blocks, 1, 0)

    return pl.pallas_call(
        _gmlp_gate_kernel,
        grid=(nb,),
        in_specs=[
            pl.BlockSpec((tm, 2 * A_HALF), lambda i: (i, 0)),
            pl.BlockSpec((None, 1, A_HALF), lambda i: (j, 0, 0)),
            pl.BlockSpec((None, 1, A_HALF), lambda i: (j, 0, 0)),
            pl.BlockSpec((None, A_GROUPS, A_CHUNK, A_CHUNK), lambda i: (kind(i), 0, 0, 0)),
            pl.BlockSpec((None, A_GROUPS, A_CHUNK, LANES), lambda i: (kind(i), 0, 0, 0)),
            pl.BlockSpec((None, A_HALF, D_MODEL), lambda i: (j, 0, 0)),
            pl.BlockSpec((tm, D_MODEL), lambda i: (i, 0)),
        ],
        out_specs=[
            pl.BlockSpec((tm, D_MODEL), lambda i: (i, 0)),
            pl.BlockSpec((tm, A_HALF), lambda i: (jnp.maximum(i - n_prompt_blocks, 0), 0)),
        ],
        out_shape=[
            jax.ShapeDtypeStruct((T, D_MODEL), F32),
            jax.ShapeDtypeStruct(((nb - n_prompt_blocks) * tm, A_HALF), F32),
        ],
        compiler_params=_params("arbitrary"),
        name="gmlp_gate",
    )(z, ln_g, ln_b, wmix, bias, w_out, x)


FF_SPLIT = 2


def _swiglu_acc(acc, h, wg, wu, wd):
    sub = wg.shape[1] // FF_SPLIT
    for c in range(FF_SPLIT):
        cols = slice(c * sub, (c + 1) * sub)
        a = jnp.dot(h, wg[:, cols].astype(BF16), preferred_element_type=F32)
        b = jnp.dot(h, wu[:, cols].astype(BF16), preferred_element_type=F32)
        m = (a * jax.nn.sigmoid(a) * b).astype(BF16)
        acc = acc + jnp.dot(m, wd[cols, :].astype(BF16), preferred_element_type=F32)
    return acc


def _ffn_kernel(x_ref, g_ref, wg_ref, wu_ref, wd_ref, o_ref, h_ref):
    @pl.when(pl.program_id(1) == 0)
    def _():
        x = x_ref[...]
        h_ref[...] = _rms(x, g_ref[...]).astype(BF16)
        o_ref[...] = x

    o_ref[...] = _swiglu_acc(o_ref[...], h_ref[...], wg_ref[...], wu_ref[...], wd_ref[...])


def _ffn_dense(x, g_all, layer, wg, wu, wd, m, tm, tf):
    T = x.shape[0]
    return pl.pallas_call(
        _ffn_kernel,
        grid=(T // tm, D_FF // tf),
        in_specs=[
            pl.BlockSpec((tm, D_MODEL), lambda i, f: (i, 0)),
            pl.BlockSpec((None, 1, D_MODEL), lambda i, f: (layer, 0, 0)),
            pl.BlockSpec((None, D_MODEL, tf), lambda i, f: (m, 0, f)),
            pl.BlockSpec((None, D_MODEL, tf), lambda i, f: (m, 0, f)),
            pl.BlockSpec((None, tf, D_MODEL), lambda i, f: (m, f, 0)),
        ],
        out_specs=pl.BlockSpec((tm, D_MODEL), lambda i, f: (i, 0)),
        out_shape=jax.ShapeDtypeStruct((T, D_MODEL), F32),
        scratch_shapes=[pltpu.VMEM((tm, D_MODEL), BF16)],
        compiler_params=_params("parallel", "arbitrary"),
        name="ffn_dense",
    )(x, g_all, wg, wu, wd)


_NT = (((1,), (1,)), ((), ()))
_TN = (((0,), (0,)), ((), ()))


HG_GROUP = 8
MAX_BLOCK_DECAY = 75.0


def _hgrn_gates(qpre, fpre, lb):
    q = qpre * jax.nn.sigmoid(qpre)
    sig = jax.nn.sigmoid(fpre)
    f = lb + (1.0 - lb) * sig
    logf = jnp.log(jnp.maximum(f, FORGET_FLOOR))
    k = (1.0 - lb) * (1.0 - sig)
    return q, k, logf


def _cumsum_rows(x):
    C = x.shape[0]
    if C >= HG_SUB:
        r = lax.broadcasted_iota(jnp.int32, (C, C), 0)
        c = lax.broadcasted_iota(jnp.int32, (C, C), 1)
        tri = jnp.where(r >= c, 1.0, 0.0).astype(F32)
        return jnp.dot(tri, x, preferred_element_type=F32, precision=lax.Precision.HIGHEST)
    row = lax.broadcasted_iota(jnp.int32, (C, 1), 0)
    out = jnp.zeros_like(x)
    for s in range(C):
        out = out + jnp.where(row >= s, x[s:s + 1], 0.0)
    return out


def _chunk_prep(q, k, logf):
    C = q.shape[0]
    b = _cumsum_rows(logf)
    b_last = b[C - 1:C]
    qe = (q * jnp.exp(b)).astype(BF16)
    kd = (k * jnp.exp(b_last - b)).astype(BF16)
    return b, qe, kd, jnp.exp(b_last)


def _intra_exact(q, k, v, b, sub):
    C = q.shape[0]
    row = lax.broadcasted_iota(jnp.int32, (sub, 1), 0)
    parts = []
    for blk in range(C // sub):
        lo = blk * sub
        b_i, q_i, k_i, v_i = b[lo:lo + sub], q[lo:lo + sub], k[lo:lo + sub], v[lo:lo + sub]
        if blk == 0:
            o_i = jnp.zeros((sub, HG_DK), F32)
        else:
            ref = b[lo - 1:lo]
            qs = (q_i * jnp.exp(b_i - ref)).astype(BF16)
            ks = (k[:lo] * jnp.exp(ref - b[:lo])).astype(BF16)
            a = lax.dot_general(qs, ks, _NT, preferred_element_type=F32)
            o_i = jnp.dot(a.astype(BF16), v[:lo].astype(BF16), preferred_element_type=F32)
        for s in range(sub):
            rel = jnp.where(row >= s, b_i - b_i[s:s + 1], MASKED_LOG)
            a_col = jnp.sum(q_i * k_i[s:s + 1] * jnp.exp(rel), axis=-1, keepdims=True)
            o_i = o_i + a_col * v_i[s:s + 1]
        parts.append(o_i)
    return parts[0] if len(parts) == 1 else jnp.concatenate(parts, axis=0)


def _intra_factored(q, k, v, b, sub):
    C = q.shape[0]
    vb = v.astype(BF16)
    heads = [slice(h * HG_DK, (h + 1) * HG_DK) for h in range(q.shape[1] // HG_DK)]
    scores = []
    for blk in range(C // sub):
        lo, hi = blk * sub, (blk + 1) * sub
        if blk == 0:
            qs = q[:hi] * jnp.exp(b[:hi])
            ks = k[:hi] * jnp.exp(-b[:hi])
        else:
            ref = b[lo - 1:lo]
            qs = q[lo:hi] * jnp.exp(b[lo:hi] - ref)
            ks = k[:hi] * jnp.exp(ref - b[:hi])
        qs, ks = qs.astype(BF16), ks.astype(BF16)
        r = lax.broadcasted_iota(jnp.int32, (sub, hi), 0)
        c = lax.broadcasted_iota(jnp.int32, (sub, hi), 1)
        row = []
        for cols in heads:
            a = lax.dot_general(qs[:, cols], ks[:, cols], _NT, preferred_element_type=F32)
            row.append(jnp.where(c <= r + lo, a, 0.0).astype(BF16))
        scores.append(row)
    parts = []
    for blk, row in enumerate(scores):
        hi = (blk + 1) * sub
        outs = [jnp.dot(a, vb[:hi, cols], preferred_element_type=F32) for a, cols in zip(row, heads)]
        parts.append(outs[0] if len(outs) == 1 else jnp.concatenate(outs, axis=1))
    return jnp.concatenate(parts, axis=0)


def _head_norm_gate(o, ng, gate):
    return (o * lax.rsqrt(jnp.mean(o * o, axis=-1, keepdims=True) + NORM_EPS) * ng * gate)


def _hgrn_prompt_kernel(q_ref, f_ref, i_ref, g_ref, lb_ref, ng_ref, o_ref, sfin_ref,
                        st_ref, qs_ref, ks_ref, lf_ref, qe_ref, kd_ref, eb_ref, oi_ref):
    tb = pl.program_id(2)
    rows_total = q_ref.shape[0]
    n_chunks = rows_total // HG_CHUNK

    @pl.when(tb == 0)
    def _():
        st_ref[...] = jnp.zeros_like(st_ref)

    q, k, logf = _hgrn_gates(q_ref[...], f_ref[...], lb_ref[...])
    qs_ref[...] = q
    ks_ref[...] = k
    lf_ref[...] = logf
    block_decay = jnp.sum(logf.reshape(rows_total // HG_SUB, HG_SUB, logf.shape[-1]), axis=1)
    mild = jnp.min(block_decay) >= -MAX_BLOCK_DECAY

    heads = [slice(h * HG_DK, (h + 1) * HG_DK) for h in range(HG_GROUP)]

    def exact_group(qg, kg, vg, b, sub):
        return jnp.concatenate([_intra_exact(qg[:, c], kg[:, c], vg[:, c], b[:, c], sub) for c in heads], axis=1)

    def intra_pass(intra):
        def body(ci, carry):
            rows = pl.ds(pl.multiple_of(ci * HG_CHUNK, HG_CHUNK), HG_CHUNK)
            qg, kg, vg = qs_ref[rows, :], ks_ref[rows, :], i_ref[rows, :]
            b, qe, kd, eb_last = _chunk_prep(qg, kg, lf_ref[rows, :])
            qe_ref[rows, :] = qe
            kd_ref[rows, :] = kd
            eb_ref[pl.ds(pl.multiple_of(ci * SUBLANES, SUBLANES), SUBLANES), :] = \
                jnp.broadcast_to(eb_last, (SUBLANES, eb_last.shape[1]))
            oi_ref[rows, :] = intra(qg, kg, vg, b, HG_SUB)
            return carry
        lax.fori_loop(0, n_chunks, body, 0, unroll=4)

    @pl.when(mild)
    def _():
        intra_pass(_intra_factored)

    @pl.when(jnp.logical_not(mild))
    def _():
        intra_pass(exact_group)

    ng = ng_ref[...]

    def state_body(ci, carry):
        rows = pl.ds(pl.multiple_of(ci * HG_CHUNK, HG_CHUNK), HG_CHUNK)
        st = st_ref[...]
        stb = st.astype(BF16)
        qe, kd, oi = qe_ref[rows, :], kd_ref[rows, :], oi_ref[rows, :]
        vb = i_ref[rows, :].astype(BF16)
        eb_last = eb_ref[pl.ds(pl.multiple_of(ci * SUBLANES, SUBLANES), 1), :]
        outs, adds = [], []
        for c in heads:
            o = oi[:, c] + lax.dot_general(qe[:, c], stb[:, c], _NT, preferred_element_type=F32)
            outs.append(o * lax.rsqrt(jnp.mean(o * o, axis=-1, keepdims=True) + NORM_EPS))
            adds.append(lax.dot_general(vb[:, c], kd[:, c], _TN, preferred_element_type=F32))
        st_ref[...] = st * eb_last + jnp.concatenate(adds, axis=1)
        gate = jax.nn.sigmoid(g_ref[rows, :])
        o_ref[rows, :] = (jnp.concatenate(outs, axis=1) * ng * gate).astype(o_ref.dtype)
        return carry

    lax.fori_loop(0, n_chunks, state_body, 0, unroll=4)

    @pl.when(tb == pl.num_programs(2) - 1)
    def _():
        for h, c in enumerate(heads):
            sfin_ref[h] = st_ref[:, c].T


def _hgrn_prompt(proj, lbs, ng_all, j, n_seq, L, lb_rows):
    nb = L // lb_rows
    ng_groups = HG_HEADS // HG_GROUP
    gw = HG_GROUP * HG_DK

    def field(k):
        return pl.BlockSpec((lb_rows, gw), lambda n, h, t: (n * nb + t, k * ng_groups + h))

    vec = pl.BlockSpec((None, 1, gw), lambda n, h, t: (j, 0, h))
    return pl.pallas_call(
        _hgrn_prompt_kernel,
        grid=(n_seq, ng_groups, nb),
        in_specs=[field(0), field(1), field(2), field(3), vec, vec],
        out_specs=[
            pl.BlockSpec((lb_rows, gw), lambda n, h, t: (n * nb + t, h)),
            pl.BlockSpec((None, HG_GROUP, HG_DK, HG_DK), lambda n, h, t: (n, h, 0, 0)),
        ],
        out_shape=[
            jax.ShapeDtypeStruct((n_seq * L, D_MODEL), BF16),
            jax.ShapeDtypeStruct((n_seq, HG_HEADS, HG_DK, HG_DK), F32),
        ],
        scratch_shapes=[
            pltpu.VMEM((HG_DK, gw), F32),
            pltpu.VMEM((lb_rows, gw), F32),
            pltpu.VMEM((lb_rows, gw), F32),
            pltpu.VMEM((lb_rows, gw), F32),
            pltpu.VMEM((lb_rows, gw), BF16),
            pltpu.VMEM((lb_rows, gw), BF16),
            pltpu.VMEM((lb_rows // HG_CHUNK * SUBLANES, gw), F32),
            pltpu.VMEM((lb_rows, gw), F32),
        ],
        compiler_params=_params("parallel", "parallel", "arbitrary"),
        name="hgrn_prompt",
    )(proj, proj, proj, proj, lbs, ng_all)


def _hgrn_sample_kernel(q_ref, f_ref, i_ref, g_ref, lb_ref, ng_ref, s0_ref, o_ref, sfin_ref):
    n_seq = s0_ref.shape[0]
    q, k, logf = _hgrn_gates(q_ref[...], f_ref[...], lb_ref[...])
    v = i_ref[...]
    row = lax.broadcasted_iota(jnp.int32, (SUBLANES, 1), 0)
    first_half = row < SAMPLE_LEN
    prep = []
    for seq in range(n_seq):
        rows = slice(seq // 2 * SUBLANES, (seq // 2 + 1) * SUBLANES)
        mine = first_half if seq % 2 == 0 else jnp.logical_not(first_half)
        qm, km, vm, lm = (jnp.where(mine, t[rows], 0.0) for t in (q, k, v, logf))
        b, qe, kd, eb_last = _chunk_prep(qm, km, lm)
        prep.append((qe, kd, eb_last, vm.astype(BF16), _intra_exact(qm, km, vm, b, SUBLANES)))
    states = [s0_ref[seq].T for seq in range(n_seq)]
    outs = [oi + lax.dot_general(qe, st.astype(BF16), _NT, preferred_element_type=F32)
            for (qe, _, _, _, oi), st in zip(prep, states)]
    news = [st * eb_last + lax.dot_general(vb, kd, _TN, preferred_element_type=F32)
            for (_, kd, eb_last, vb, _), st in zip(prep, states)]
    sfin_ref[...] = jnp.stack([st.T for st in news])
    tiles = [jnp.where(first_half, outs[2 * p], outs[2 * p + 1]) for p in range(n_seq // 2)]
    gate = jax.nn.sigmoid(g_ref[...])
    o_ref[...] = _head_norm_gate(jnp.concatenate(tiles, axis=0), ng_ref[...], gate).astype(o_ref.dtype)


def _hgrn_sample(proj, row0, state_all, lbs, ng_all, j, n_seq, sb):
    H = HG_HEADS
    rb = sb * SAMPLE_LEN
    assert row0 % rb == 0 and sb % 2 == 0
    r0 = row0 // rb

    def field(k):
        return pl.BlockSpec((rb, HG_DK), lambda s, h: (r0 + s, k * H + h))

    vec = pl.BlockSpec((None, 1, HG_DK), lambda s, h: (j, 0, h))
    return pl.pallas_call(
        _hgrn_sample_kernel,
        grid=(n_seq // sb, H),
        in_specs=[field(0), field(1), field(2), field(3), vec, vec,
                  pl.BlockSpec((None, sb, None, HG_DK, HG_DK), lambda s, h: (j, s, h, 0, 0))],
        out_specs=[
            pl.BlockSpec((rb, HG_DK), lambda s, h: (s, h)),
            pl.BlockSpec((None, sb, None, HG_DK, HG_DK), lambda s, h: (j, s, h, 0, 0)),
        ],
        out_shape=[
            jax.ShapeDtypeStruct((n_seq * SAMPLE_LEN, D_MODEL), BF16),
            jax.ShapeDtypeStruct(state_all.shape, state_all.dtype),
        ],
        input_output_aliases={6: 1},
        compiler_params=_params("parallel", "parallel"),
        name="hgrn_sample",
    )(proj, proj, proj, proj, lbs, ng_all, state_all)


META_E1, META_E2, META_G1, META_G2, META_R1, META_R2 = range(6)
TOKEN_BLOCK = 128


def _router_kernel(ap_ref, as_ref, w_ref, x_ref, g_ref, rw_ref, xo_ref, h_ref, meta_ref, cnt_ref, cb_ref,
                   carry_ref, *, n_prompt_blocks):
    i = pl.program_id(0)
    tm = x_ref.shape[0]

    @pl.when(i == 0)
    def _():
        carry_ref[...] = jnp.zeros_like(carry_ref)

    a = jnp.where(i < n_prompt_blocks, ap_ref[...], as_ref[...])
    x = x_ref[...] + jnp.dot(a, w_ref[...], preferred_element_type=F32)
    xo_ref[...] = x
    h = _rms(x, g_ref[...])
    h_ref[...] = h.astype(h_ref.dtype)
    logits = jnp.dot(h.astype(BF16), rw_ref[...].astype(BF16), preferred_element_type=F32)
    lane = lax.broadcasted_iota(jnp.int32, (tm, LANES), 1)
    neg = -jnp.inf
    lg = jnp.where(lane < N_EXPERTS, logits, neg)
    m1 = jnp.max(lg, axis=-1, keepdims=True)
    e1 = jnp.min(jnp.where(lg == m1, lane, LANES), axis=-1, keepdims=True)
    lg2 = jnp.where(lane == e1, neg, lg)
    m2 = jnp.max(lg2, axis=-1, keepdims=True)
    e2 = jnp.min(jnp.where(lg2 == m2, lane, LANES), axis=-1, keepdims=True)
    ex = jnp.exp(m2 - m1)
    g1 = 1.0 / (1.0 + ex)
    g2 = ex / (1.0 + ex)

    onehot = jnp.where((lane == e1) | (lane == e2), 1.0, 0.0)
    r = lax.broadcasted_iota(jnp.int32, (tm, tm), 0)
    c = lax.broadcasted_iota(jnp.int32, (tm, tm), 1)
    before = jnp.where(r > c, 1.0, 0.0).astype(BF16)
    seen = jnp.dot(before, onehot.astype(BF16), preferred_element_type=F32) + carry_ref[0:1, :]
    r1 = jnp.sum(jnp.where(lane == e1, seen, 0.0), axis=-1, keepdims=True)
    r2 = jnp.sum(jnp.where(lane == e2, seen, 0.0), axis=-1, keepdims=True)
    for blk in range(tm // TOKEN_BLOCK):
        cb_ref[blk] = jnp.broadcast_to(seen[blk * TOKEN_BLOCK:blk * TOKEN_BLOCK + 1], (SUBLANES, LANES))
    total = carry_ref[0:1, :] + jnp.sum(onehot, axis=0, keepdims=True)
    carry_ref[...] = jnp.broadcast_to(total, carry_ref.shape)
    cnt_ref[...] = jnp.broadcast_to(total, cnt_ref.shape)

    meta = jnp.zeros((tm, LANES), F32)
    for idx, val in ((META_E1, e1.astype(F32)), (META_E2, e2.astype(F32)), (META_G1, g1),
                     (META_G2, g2), (META_R1, r1), (META_R2, r2)):
        meta = jnp.where(lane == idx, val, meta)
    meta_ref[...] = meta


def _router(a_prompt, a_sample, w_all, x, g_all, layer, rw_pad, m, tm):
    T = x.shape[0]
    npb = a_prompt.shape[0] // tm
    assert a_prompt.shape[0] % tm == 0 and a_sample.shape[0] % tm == 0
    return pl.pallas_call(
        functools.partial(_router_kernel, n_prompt_blocks=npb),
        grid=(T // tm,),
        in_specs=[
            pl.BlockSpec((tm, D_MODEL), lambda i: (jnp.minimum(i, npb - 1), 0)),
            pl.BlockSpec((tm, D_MODEL), lambda i: (jnp.maximum(i - npb, 0), 0)),
            pl.BlockSpec((None, D_MODEL, D_MODEL), lambda i: (m, 0, 0)),
            pl.BlockSpec((tm, D_MODEL), lambda i: (i, 0)),
            pl.BlockSpec((None, 1, D_MODEL), lambda i: (layer, 0, 0)),
            pl.BlockSpec((None, D_MODEL, LANES), lambda i: (m, 0, 0)),
        ],
        out_specs=[
            pl.BlockSpec((tm, D_MODEL), lambda i: (i, 0)),
            pl.BlockSpec((tm, D_MODEL), lambda i: (i, 0)),
            pl.BlockSpec((tm, LANES), lambda i: (i, 0)),
            pl.BlockSpec((SUBLANES, LANES), lambda i: (0, 0)),
            pl.BlockSpec((tm // TOKEN_BLOCK, SUBLANES, LANES), lambda i: (i, 0, 0)),
        ],
        out_shape=[
            jax.ShapeDtypeStruct((T, D_MODEL), F32),
            jax.ShapeDtypeStruct((T, D_MODEL), BF16),
            jax.ShapeDtypeStruct((T, LANES), F32),
            jax.ShapeDtypeStruct((SUBLANES, LANES), F32),
            jax.ShapeDtypeStruct((T // TOKEN_BLOCK, SUBLANES, LANES), F32),
        ],
        scratch_shapes=[pltpu.VMEM((SUBLANES, LANES), F32)],
        compiler_params=_params("arbitrary"),
        name="moe_router",
    )(a_prompt, a_sample, w_all, x, g_all, rw_pad)


SLOT_BLOCK = 256
WINDOW_BLOCKS = 10
TOKEN_WINDOW = WINDOW_BLOCKS * TOKEN_BLOCK


def _dispatch_kernel(sb_ref, ws_ref, lo_ref, first_ref, n_ref, h_ref, dest_ref, o_ref):
    i = pl.program_id(0)

    @pl.when(i < n_ref[0])
    def _():
        slot = sb_ref[i] * SLOT_BLOCK + lax.broadcasted_iota(jnp.int32, (SLOT_BLOCK, 1), 0)
        token = ws_ref[i] + lax.broadcasted_iota(jnp.int32, (1, TOKEN_WINDOW), 1)
        dest = jnp.where(token >= lo_ref[i], dest_ref[...], -1)
        hit = (dest[0:1, :] == slot) | (dest[1:2, :] == slot)
        rows = jnp.dot(jnp.where(hit, 1.0, 0.0).astype(BF16), h_ref[...], preferred_element_type=F32)

        @pl.when(first_ref[i] == 1)
        def _():
            o_ref[...] = rows.astype(o_ref.dtype)

        @pl.when(first_ref[i] == 0)
        def _():
            o_ref[...] = (o_ref[...].astype(F32) + rows).astype(o_ref.dtype)


def _dispatch(h, dest_t, items, n_slots):
    def at(i, sb, ws, lo, fi, ni):
        return pl.multiple_of(ws[i], TOKEN_BLOCK)

    grid_spec = pltpu.PrefetchScalarGridSpec(
        num_scalar_prefetch=5,
        grid=(items[0].shape[0],),
        in_specs=[
            pl.BlockSpec((pl.Element(TOKEN_WINDOW), pl.Element(D_MODEL)), lambda *a: (at(*a), 0)),
            pl.BlockSpec((pl.Element(SUBLANES), pl.Element(TOKEN_WINDOW)), lambda *a: (0, at(*a))),
        ],
        out_specs=pl.BlockSpec((SLOT_BLOCK, D_MODEL), lambda i, sb, ws, lo, fi, ni: (sb[i], 0)),
    )
    return pl.pallas_call(
        _dispatch_kernel,
        grid_spec=grid_spec,
        out_shape=jax.ShapeDtypeStruct((n_slots, D_MODEL), BF16),
        compiler_params=_params("arbitrary"),
        name="moe_dispatch",
    )(*items, h, dest_t)


def _dispatch_items(cblk, counts, pstart, pend, block_expert, bm, n_slots, T):
    n_tb = T // TOKEN_BLOCK
    n_sb = n_slots // SLOT_BLOCK
    sb_start = jnp.arange(n_sb, dtype=jnp.int32) * SLOT_BLOCK
    e_sb = block_expert[sb_start // bm]
    r0 = sb_start - pstart[e_sb]
    c_sb = counts[e_sb]
    in_region = sb_start < pend[e_sb]
    has_tokens = r0 < c_sb
    r1 = jnp.minimum(r0 + SLOT_BLOCK, c_sb) - 1
    before = cblk[:, 0, :N_EXPERTS].astype(jnp.int32)[:, e_sb]
    tb_first = jnp.where(has_tokens, jnp.sum(before <= r0[None, :], axis=0) - 1, 0)
    tb_last = jnp.where(has_tokens, jnp.sum(before <= r1[None, :], axis=0) - 1, 0)
    n_win = jnp.where(in_region, (tb_last - tb_first) // WINDOW_BLOCKS + 1, 1)
    ends = jnp.cumsum(n_win)
    total = ends[-1]
    max_items = n_sb + N_EXPERTS * (-(-n_tb // WINDOW_BLOCKS) + 1)
    i = jnp.minimum(jnp.arange(max_items, dtype=jnp.int32), total - 1)
    sb = jnp.sum(ends[None, :] <= i[:, None], axis=1).astype(jnp.int32)
    k = i - (ends - n_win)[sb]
    lo = ((tb_first[sb] + k * WINDOW_BLOCKS) * TOKEN_BLOCK).astype(jnp.int32)
    start = jnp.minimum(lo, T - TOKEN_WINDOW)
    return sb, start, lo, (k == 0).astype(jnp.int32), total.astype(jnp.int32).reshape(1)


def _experts_kernel(be_ref, nu_ref, xs_ref, wg_ref, wu_ref, wd_ref, o_ref, acc_ref):
    b = pl.program_id(0)
    f = pl.program_id(1)
    used = b < nu_ref[0]

    @pl.when(used & (f == 0))
    def _():
        acc_ref[...] = jnp.zeros_like(acc_ref)

    @pl.when(used)
    def _():
        acc_ref[...] = _swiglu_acc(acc_ref[...], xs_ref[...], wg_ref[...], wu_ref[...], wd_ref[...])

        @pl.when(f == pl.num_programs(1) - 1)
        def _():
            o_ref[...] = acc_ref[...].astype(o_ref.dtype)

    @pl.when(jnp.logical_not(used) & (f == 0))
    def _():
        o_ref[...] = jnp.zeros_like(o_ref)


def _experts(xs, block_expert, n_used, wg, wu, wd, m, bm, tf):
    n_blocks = xs.shape[0] // bm
    nf = D_FF // tf

    def row(b, nu):
        return jnp.minimum(b, nu[0] - 1)

    def fidx(b, f, nu):
        return jnp.where(b < nu[0], f, nf - 1)

    grid_spec = pltpu.PrefetchScalarGridSpec(
        num_scalar_prefetch=2,
        grid=(n_blocks, nf),
        in_specs=[
            pl.BlockSpec((bm, D_MODEL), lambda b, f, be, nu: (row(b, nu), 0)),
            pl.BlockSpec((None, None, D_MODEL, tf), lambda b, f, be, nu: (m, be[b], 0, fidx(b, f, nu))),
            pl.BlockSpec((None, None, D_MODEL, tf), lambda b, f, be, nu: (m, be[b], 0, fidx(b, f, nu))),
            pl.BlockSpec((None, None, tf, D_MODEL), lambda b, f, be, nu: (m, be[b], fidx(b, f, nu), 0)),
        ],
        out_specs=pl.BlockSpec((bm, D_MODEL), lambda b, f, be, nu: (b, 0)),
        scratch_shapes=[pltpu.VMEM((bm, D_MODEL), F32)],
    )
    return pl.pallas_call(
        _experts_kernel,
        grid_spec=grid_spec,
        out_shape=jax.ShapeDtypeStruct((n_blocks * bm, D_MODEL), BF16),
        compiler_params=_params("arbitrary", "arbitrary"),
        name="moe_experts",
    )(block_expert, n_used, xs, wg, wu, wd)


BF16_ROWS = 16
WINDOW = TOKEN_BLOCK + BF16_ROWS


def _combine_kernel(win_ref, ps_ref, pe_ref, x_ref, meta_ref, *refs, n_prompt_blocks=None):
    ys_refs, out_refs = refs[:N_EXPERTS], refs[N_EXPERTS:]
    blk = pl.program_id(0)
    meta = meta_ref[...]
    e1, e2 = meta[:, META_E1:META_E1 + 1], meta[:, META_E2:META_E2 + 1]
    d1, d2 = meta[:, META_R1:META_R1 + 1], meta[:, META_R2:META_R2 + 1]
    for e in range(N_EXPERTS):
        start = ps_ref[e].astype(F32)
        d1 = d1 + jnp.where(e1 == e, start, 0.0)
        d2 = d2 + jnp.where(e2 == e, start, 0.0)
    d1, d2, g1, g2 = (jnp.broadcast_to(t, (TOKEN_BLOCK, WINDOW))
                      for t in (d1, d2, meta[:, META_G1:META_G1 + 1], meta[:, META_G2:META_G2 + 1]))
    lane = lax.broadcasted_iota(jnp.int32, (1, WINDOW), 1)
    acc = x_ref[...]
    for e in range(N_EXPERTS):
        slot = win_ref[blk * N_EXPERTS + e] + lane
        slot = jnp.where((slot >= ps_ref[e]) & (slot < pe_ref[e]), slot, -1).astype(F32)
        sel = jnp.where(d1 == slot, g1, 0.0) + jnp.where(d2 == slot, g2, 0.0)
        acc = acc + jnp.dot(sel.astype(BF16), ys_refs[e][...], preferred_element_type=F32)
    if n_prompt_blocks is None:
        out_refs[0][...] = acc
    else:
        g_ref, prompt_ref, sample_ref = out_refs
        y = _rms(acc, g_ref[...])

        @pl.when(blk < n_prompt_blocks)
        def _():
            prompt_ref[...] = y

        @pl.when(blk >= n_prompt_blocks)
        def _():
            sample_ref[...] = y


def _combine(x, meta, ys, windows, pstart, pend, final_g=None, n_prompt_rows=None):
    T = x.shape[0]
    nb = T // TOKEN_BLOCK

    def window(e):
        return pl.BlockSpec((pl.Element(WINDOW), pl.Element(D_MODEL)),
                            lambda i, win, ps, pe: (pl.multiple_of(win[i * N_EXPERTS + e], BF16_ROWS), 0))

    in_specs = [pl.BlockSpec((TOKEN_BLOCK, D_MODEL), lambda i, win, ps, pe: (i, 0)),
                pl.BlockSpec((TOKEN_BLOCK, LANES), lambda i, win, ps, pe: (i, 0))] \
        + [window(e) for e in range(N_EXPERTS)]
    operands = [windows, pstart, pend, x, meta] + [ys] * N_EXPERTS
    if final_g is None:
        npb = None
        out_specs = pl.BlockSpec((TOKEN_BLOCK, D_MODEL), lambda i, win, ps, pe: (i, 0))
        out_shape = jax.ShapeDtypeStruct((T, D_MODEL), F32)
    else:
        assert n_prompt_rows % TOKEN_BLOCK == 0
        npb = n_prompt_rows // TOKEN_BLOCK
        in_specs.append(pl.BlockSpec((1, D_MODEL), lambda i, win, ps, pe: (0, 0)))
        operands.append(final_g)
        out_specs = [pl.BlockSpec((TOKEN_BLOCK, D_MODEL), lambda i, win, ps, pe: (jnp.minimum(i, npb - 1), 0)),
                     pl.BlockSpec((TOKEN_BLOCK, D_MODEL), lambda i, win, ps, pe: (jnp.maximum(i - npb, 0), 0))]
        out_shape = [jax.ShapeDtypeStruct((n_prompt_rows, D_MODEL), F32),
                     jax.ShapeDtypeStruct((T - n_prompt_rows, D_MODEL), F32)]
    grid_spec = pltpu.PrefetchScalarGridSpec(
        num_scalar_prefetch=3, grid=(nb,), in_specs=in_specs, out_specs=out_specs)
    return pl.pallas_call(
        functools.partial(_combine_kernel, n_prompt_blocks=npb),
        grid_spec=grid_spec,
        out_shape=out_shape,
        compiler_params=_params("arbitrary"),
        name="moe_combine",
    )(*operands)


def _mixer_out_and_moe(o_p, o_s, w_out, x, norm_ffn_g, layer, rw_pad, wg, wu, wd, m, tm, bm, tf,
                       final_g=None, n_prompt_rows=None):
    T = x.shape[0]
    x, h, meta, cnt, cblk = _router(o_p, o_s, w_out, x, norm_ffn_g, layer, rw_pad, m, tm)
    e = meta[:, META_E1:META_E2 + 1].astype(jnp.int32)
    rank = meta[:, META_R1:META_R2 + 1].astype(jnp.int32)
    counts = cnt[0, :N_EXPERTS].astype(jnp.int32)
    padded = (counts + bm - 1) // bm * bm
    pend = jnp.cumsum(padded)
    pstart = pend - padded
    dest = pstart[e] + rank
    n_blocks = (T * 2) // bm + N_EXPERTS + 1
    block_start = jnp.arange(n_blocks, dtype=jnp.int32) * bm
    block_expert = jnp.minimum(jnp.sum(pend[None, :] <= block_start[:, None], axis=1),
                               N_EXPERTS - 1).astype(jnp.int32)
    n_used = (pend[-1] // bm).astype(jnp.int32).reshape(1)
    dest_t = jnp.full((SUBLANES, T), -1, jnp.int32).at[0:2].set(dest.T)
    items = _dispatch_items(cblk, counts, pstart, pend, block_expert, bm, n_blocks * bm, T)
    xs = _dispatch(h, dest_t, items, n_blocks * bm)
    ys = _experts(xs, block_expert, n_used, wg, wu, wd, m, bm, tf)
    run_start = pstart[None, :] + cblk[:, 0, :N_EXPERTS].astype(jnp.int32)
    windows = (run_start // BF16_ROWS * BF16_ROWS).reshape(-1).astype(jnp.int32)
    return _combine(x, meta, ys, windows, pstart.astype(jnp.int32), pend.astype(jnp.int32),
                    final_g, n_prompt_rows)


def _tril(w):
    n = w.shape[-1]
    return jnp.where(jnp.tril(jnp.ones((n, n), dtype=bool)), w, jnp.zeros((), w.dtype))


def _mix_tables(w_s, b_s, sample_len):
    reps = A_CHUNK // sample_len
    eye = jnp.eye(reps, dtype=w_s.dtype)
    w_prompt = _tril(w_s)
    w_small = _tril(w_s[:, :sample_len, :sample_len])
    w_sample = jax.vmap(lambda w: jnp.kron(eye, w))(w_small)
    b_prompt = b_s
    b_sample = jnp.tile(b_s[:, :sample_len], (1, reps))
    wmix = jnp.stack([w_prompt, w_sample])
    bias = jnp.stack([b_prompt, b_sample])[..., None]
    return wmix, jnp.broadcast_to(bias, bias.shape[:-1] + (LANES,))


def kernel(x_prompt, x_sample, state_hgrn, norm_mix_g, norm_ffn_g, final_norm_g, a_w_in, a_ln_g, a_ln_b, a_w_s, a_b_s, a_w_out, b_w_in, b_lb_logits, b_norm_g, b_w_out, ffn_w_gate, ffn_w_up, ffn_w_down, moe_router, moe_w_gate, moe_w_up, moe_w_down):
    n_p, L, d = x_prompt.shape
    n_s, l_s, _ = x_sample.shape
    assert d == D_MODEL and l_s == SAMPLE_LEN and L % A_CHUNK == 0
    T_p, T_s = n_p * L, n_s * l_s
    T = T_p + T_s
    depth = norm_mix_g.shape[0]
    assert depth % 2 == 0, "the last layer must be an HGRN2 + MoE layer"

    tm_gate = 2 * A_CHUNK
    assert T_p % tm_gate == 0 and T_s % tm_gate == 0
    tm_big = next(t for t in (1536, 768, 512, 256, 128) if T % t == 0)
    bm = 1024
    tf = 512
    lb_rows = min(L, 512)

    x = jnp.concatenate([x_prompt.reshape(T_p, d), x_sample.reshape(T_s, d)], axis=0)

    p = jax.nn.softmax(b_lb_logits.astype(F32), axis=0)
    lbs = (jnp.cumsum(p, axis=0) - p[0:1])[:, None, :]
    mix_g = norm_mix_g[:, None, :]
    ffn_g = norm_ffn_g[:, None, :]
    b_ng = b_norm_g[:, None, :]
    ln_g = a_ln_g[:, None, :]
    ln_b = a_ln_b[:, None, :]
    rw_pad = jnp.pad(moe_router, ((0, 0), (0, 0), (0, LANES - N_EXPERTS)))
    a_w_in, a_w_out, b_w_in, b_w_out, ffn_w_gate, ffn_w_up, ffn_w_down = (
        w.astype(BF16) for w in (a_w_in, a_w_out, b_w_in, b_w_out, ffn_w_gate, ffn_w_up, ffn_w_down))

    hg_prompt, v_sample = [], []
    hg_sample = state_hgrn
    for layer in range(depth):
        j = layer // 2
        if layer % 2 == 0:
            z = _norm_matmul(x, mix_g, layer, a_w_in, j, _gelu, BF16, tm_big, 1024)
            wmix, bias = _mix_tables(a_w_s[j], a_b_s[j], l_s)
            x, v = _gmlp_gate(z, x, ln_g, ln_b, wmix, bias, a_w_out, j, T_p // tm_gate, tm_gate)
            v_sample.append(v.reshape(n_s, l_s, A_HALF))
            x = _ffn_dense(x, ffn_g, layer, ffn_w_gate, ffn_w_up, ffn_w_down, j, tm_big, tf)
        else:
            proj = _norm_matmul(x, mix_g, layer, b_w_in, j, _identity, F32, tm_big, 1024)
            o_p, s_p = _hgrn_prompt(proj, lbs, b_ng, j, n_p, L, lb_rows)
            o_s, hg_sample = _hgrn_sample(proj, T_p, hg_sample, lbs, b_ng, j, n_s, 8)
            hg_prompt.append(s_p)
            last = layer == depth - 1
            x = _mixer_out_and_moe(o_p, o_s, b_w_out, x, ffn_g, layer, rw_pad, moe_w_gate, moe_w_up, moe_w_down,
                                   j, math.gcd(T_p, T_s, 512), bm, tf,
                                   final_norm_g[None, :] if last else None, T_p if last else None)

    y_prompt, y_sample = x
    return (y_prompt.reshape(n_p, L, d), y_sample.reshape(n_s, l_s, d),
            jnp.stack(hg_prompt), hg_sample, jnp.stack(v_sample))
```

```python
import functools
import math

import jax
import jax.numpy as jnp
from jax import lax
from jax.experimental import pallas as pl
from jax.experimental.pallas import tpu as pltpu

F32 = jnp.float32
BF16 = jnp.bfloat16

D_MODEL = 1024
A_CHUNK = 128
A_HALF = 3 * D_MODEL
A_GROUPS = 8
A_GROUP_DIM = A_HALF // A_GROUPS
HG_HEADS = 8
HG_DK = 128
HG_CHUNK = 64
HG_SUB = 32
FORGET_FLOOR = 1e-20
D_FF = 7 * D_MODEL // 2
N_EXPERTS = 8
NORM_EPS = 1e-6
LANES = 128
SUBLANES = 8
SAMPLE_LEN = 4
MASKED_LOG = -1e30

VMEM_LIMIT = 56 * 1024 * 1024


def _params(*sem):
    return pltpu.CompilerParams(dimension_semantics=sem, vmem_limit_bytes=VMEM_LIMIT)


def _rms(x, g):
    ms = jnp.mean(x * x, axis=-1, keepdims=True)
    return x * lax.rsqrt(ms + NORM_EPS) * g


def _gelu(y):
    return 0.5 * y * (1.0 + lax.erf(y * math.sqrt(0.5)))


def _identity(y):
    return y


def _norm_matmul_kernel(x_ref, g_ref, w_ref, o_ref, h_ref, *, act):
    @pl.when(pl.program_id(1) == 0)
    def _():
        h_ref[...] = _rms(x_ref[...], g_ref[...]).astype(BF16)

    y = jnp.dot(h_ref[...], w_ref[...].astype(BF16), preferred_element_type=F32)
    o_ref[...] = act(y).astype(o_ref.dtype)


def _norm_matmul(x, g_all, layer, w_all, w_layer, act, out_dtype, tm, tn):
    T = x.shape[0]
    N = w_all.shape[-1]
    return pl.pallas_call(
        functools.partial(_norm_matmul_kernel, act=act),
        grid=(T // tm, N // tn),
        in_specs=[
            pl.BlockSpec((tm, D_MODEL), lambda i, j: (i, 0)),
            pl.BlockSpec((None, 1, D_MODEL), lambda i, j: (layer, 0, 0)),
            pl.BlockSpec((None, D_MODEL, tn), lambda i, j: (w_layer, 0, j)),
        ],
        out_specs=pl.BlockSpec((tm, tn), lambda i, j: (i, j)),
        out_shape=jax.ShapeDtypeStruct((T, N), out_dtype),
        scratch_shapes=[pltpu.VMEM((tm, D_MODEL), BF16)],
        compiler_params=_params("parallel", "arbitrary"),
        name="norm_matmul",
    )(x, g_all, w_all)


def _gmlp_gate_kernel(z_ref, lng_ref, lnb_ref, wmix_ref, bias_ref, wout_ref, x_ref, o_ref, v_ref):
    tm = z_ref.shape[0]
    zv = z_ref[:, A_HALF:].astype(F32)
    mu = jnp.mean(zv, axis=-1, keepdims=True)
    xc = zv - mu
    rstd = lax.rsqrt(jnp.mean(xc * xc, axis=-1, keepdims=True) + NORM_EPS)
    v = xc * rstd * lng_ref[...] + lnb_ref[...]
    v_ref[...] = v
    vb = v.astype(BF16)
    groups = []
    for g in range(A_GROUPS):
        cols = slice(g * A_GROUP_DIM, (g + 1) * A_GROUP_DIM)
        wm = wmix_ref[g].astype(BF16)
        bias = jnp.concatenate([bias_ref[g]] * (A_GROUP_DIM // LANES), axis=1)
        rows = []
        for c in range(tm // A_CHUNK):
            sl = slice(c * A_CHUNK, (c + 1) * A_CHUNK)
            s = jnp.dot(wm, vb[sl, cols], preferred_element_type=F32) + bias
            rows.append((z_ref[sl, cols].astype(F32) * s).astype(BF16))
        groups.append(jnp.concatenate(rows, axis=0))
    gated = jnp.concatenate(groups, axis=1)
    o_ref[...] = x_ref[...] + jnp.dot(gated, wout_ref[...], preferred_element_type=F32)


def _gmlp_gate(z, x, ln_g, ln_b, wmix, bias, w_out, j, n_prompt_blocks, tm):
    T = x.shape[0]
    nb = T // tm

    def kind(i):
        return jnp.where(i >= n_prompt_blocks, 1, 0)

    return pl.pallas_call(
        _gmlp_gate_kernel,
        grid=(nb,),
        in_specs=[
            pl.BlockSpec((tm, 2 * A_HALF), lambda i: (i, 0)),
            pl.BlockSpec((None, 1, A_HALF), lambda i: (j, 0, 0)),
            pl.BlockSpec((None, 1, A_HALF), lambda i: (j, 0, 0)),
            pl.BlockSpec((None, A_GROUPS, A_CHUNK, A_CHUNK), lambda i: (kind(i), 0, 0, 0)),
            pl.BlockSpec((None, A_GROUPS, A_CHUNK, LANES), lambda i: (kind(i), 0, 0, 0)),
            pl.BlockSpec((None, A_HALF, D_MODEL), lambda i: (j, 0, 0)),
            pl.BlockSpec((tm, D_MODEL), lambda i: (i, 0)),
        ],
        out_specs=[
            pl.BlockSpec((tm, D_MODEL), lambda i: (i, 0)),
            pl.BlockSpec((tm, A_HALF), lambda i: (jnp.maximum(i - n_prompt_blocks, 0), 0)),
        ],
        out_shape=[
            jax.ShapeDtypeStruct((T, D_MODEL), F32),
            jax.ShapeDtypeStruct(((nb - n_prompt_blocks) * tm, A_HALF), F32),
        ],
        compiler_params=_params("arbitrary"),
        name="gmlp_gate",
    )(z, ln_g, ln_b, wmix, bias, w_out, x)


FF_SPLIT = 2


def _swiglu_acc(acc, h, wg, wu, wd):
    sub = wg.shape[1] // FF_SPLIT
    for c in range(FF_SPLIT):
        cols = slice(c * sub, (c + 1) * sub)
        a = jnp.dot(h, wg[:, cols].astype(BF16), preferred_element_type=F32)
        b = jnp.dot(h, wu[:, cols].astype(BF16), preferred_element_type=F32)
        m = (a * jax.nn.sigmoid(a) * b).astype(BF16)
        acc = acc + jnp.dot(m, wd[cols, :].astype(BF16), preferred_element_type=F32)
    return acc


def _ffn_kernel(x_ref, g_ref, wg_ref, wu_ref, wd_ref, o_ref, h_ref):
    @pl.when(pl.program_id(1) == 0)
    def _():
        x = x_ref[...]
        h_ref[...] = _rms(x, g_ref[...]).astype(BF16)
        o_ref[...] = x

    o_ref[...] = _swiglu_acc(o_ref[...], h_ref[...], wg_ref[...], wu_ref[...], wd_ref[...])


def _ffn_dense(x, g_all, layer, wg, wu, wd, m, tm, tf):
    T = x.shape[0]
    return pl.pallas_call(
        _ffn_kernel,
        grid=(T // tm, D_FF // tf),
        in_specs=[
            pl.BlockSpec((tm, D_MODEL), lambda i, f: (i, 0)),
            pl.BlockSpec((None, 1, D_MODEL), lambda i, f: (layer, 0, 0)),
            pl.BlockSpec((None, D_MODEL, tf), lambda i, f: (m, 0, f)),
            pl.BlockSpec((None, D_MODEL, tf), lambda i, f: (m, 0, f)),
            pl.BlockSpec((None, tf, D_MODEL), lambda i, f: (m, f, 0)),
        ],
        out_specs=pl.BlockSpec((tm, D_MODEL), lambda i, f: (i, 0)),
        out_shape=jax.ShapeDtypeStruct((T, D_MODEL), F32),
        scratch_shapes=[pltpu.VMEM((tm, D_MODEL), BF16)],
        compiler_params=_params("parallel", "arbitrary"),
        name="ffn_dense",
    )(x, g_all, wg, wu, wd)


_NT = (((1,), (1,)), ((), ()))
_TN = (((0,), (0,)), ((), ()))


HG_GROUP = 8
MAX_BLOCK_DECAY = 75.0


def _hgrn_gates(qpre, fpre, lb):
    q = qpre * jax.nn.sigmoid(qpre)
    sig = jax.nn.sigmoid(fpre)
    f = lb + (1.0 - lb) * sig
    logf = jnp.log(jnp.maximum(f, FORGET_FLOOR))
    k = (1.0 - lb) * (1.0 - sig)
    return q, k, logf


def _cumsum_rows(x):
    C = x.shape[0]
    if C >= HG_SUB:
        r = lax.broadcasted_iota(jnp.int32, (C, C), 0)
        c = lax.broadcasted_iota(jnp.int32, (C, C), 1)
        tri = jnp.where(r >= c, 1.0, 0.0).astype(F32)
        return jnp.dot(tri, x, preferred_element_type=F32, precision=lax.Precision.HIGHEST)
    row = lax.broadcasted_iota(jnp.int32, (C, 1), 0)
    out = jnp.zeros_like(x)
    for s in range(C):
        out = out + jnp.where(row >= s, x[s:s + 1], 0.0)
    return out


def _chunk_prep(q, k, logf):
    C = q.shape[0]
    b = _cumsum_rows(logf)
    b_last = b[C - 1:C]
    qe = (q * jnp.exp(b)).astype(BF16)
    kd = (k * jnp.exp(b_last - b)).astype(BF16)
    return b, qe, kd, jnp.exp(b_last)


def _intra_exact(q, k, v, b, sub):
    C = q.shape[0]
    row = lax.broadcasted_iota(jnp.int32, (sub, 1), 0)
    parts = []
    for blk in range(C // sub):
        lo = blk * sub
        b_i, q_i, k_i, v_i = b[lo:lo + sub], q[lo:lo + sub], k[lo:lo + sub], v[lo:lo + sub]
        if blk == 0:
            o_i = jnp.zeros((sub, HG_DK), F32)
        else:
            ref = b[lo - 1:lo]
            qs = (q_i * jnp.exp(b_i - ref)).astype(BF16)
            ks = (k[:lo] * jnp.exp(ref - b[:lo])).astype(BF16)
            a = lax.dot_general(qs, ks, _NT, preferred_element_type=F32)
            o_i = jnp.dot(a.astype(BF16), v[:lo].astype(BF16), preferred_element_type=F32)
        for s in range(sub):
            rel = jnp.where(row >= s, b_i - b_i[s:s + 1], MASKED_LOG)
            a_col = jnp.sum(q_i * k_i[s:s + 1] * jnp.exp(rel), axis=-1, keepdims=True)
            o_i = o_i + a_col * v_i[s:s + 1]
        parts.append(o_i)
    return parts[0] if len(parts) == 1 else jnp.concatenate(parts, axis=0)


def _intra_factored(q, k, v, b, sub):
    C = q.shape[0]
    vb = v.astype(BF16)
    heads = [slice(h * HG_DK, (h + 1) * HG_DK) for h in range(q.shape[1] // HG_DK)]
    scores = []
    for blk in range(C // sub):
        lo, hi = blk * sub, (blk + 1) * sub
        if blk == 0:
            qs = q[:hi] * jnp.exp(b[:hi])
            ks = k[:hi] * jnp.exp(-b[:hi])
        else:
            ref = b[lo - 1:lo]
            qs = q[lo:hi] * jnp.exp(b[lo:hi] - ref)
            ks = k[:hi] * jnp.exp(ref - b[:hi])
        qs, ks = qs.astype(BF16), ks.astype(BF16)
        r = lax.broadcasted_iota(jnp.int32, (sub, hi), 0)
        c = lax.broadcasted_iota(jnp.int32, (sub, hi), 1)
        row = []
        for cols in heads:
            a = lax.dot_general(qs[:, cols], ks[:, cols], _NT, preferred_element_type=F32)
            row.append(jnp.where(c <= r + lo, a, 0.0).astype(BF16))
        scores.append(row)
    parts = []
    for blk, row in enumerate(scores):
        hi = (blk + 1) * sub
        outs = [jnp.dot(a, vb[:hi, cols], preferred_element_type=F32) for a, cols in zip(row, heads)]
        parts.append(outs[0] if len(outs) == 1 else jnp.concatenate(outs, axis=1))
    return jnp.concatenate(parts, axis=0)


def _head_norm_gate(o, ng, gate):
    return (o * lax.rsqrt(jnp.mean(o * o, axis=-1, keepdims=True) + NORM_EPS) * ng * gate)


def _hgrn_prompt_kernel(q_ref, f_ref, i_ref, g_ref, lb_ref, ng_ref, o_ref, sfin_ref,
                        st_ref, qs_ref, ks_ref, lf_ref, qe_ref, kd_ref, eb_ref, oi_ref):
    tb = pl.program_id(2)
    rows_total = q_ref.shape[0]
    n_chunks = rows_total // HG_CHUNK

    @pl.when(tb == 0)
    def _():
        st_ref[...] = jnp.zeros_like(st_ref)

    q, k, logf = _hgrn_gates(q_ref[...], f_ref[...], lb_ref[...])
    qs_ref[...] = q
    ks_ref[...] = k
    lf_ref[...] = logf
    block_decay = jnp.sum(logf.reshape(rows_total // HG_SUB, HG_SUB, logf.shape[-1]), axis=1)
    mild = jnp.min(block_decay) >= -MAX_BLOCK_DECAY

    heads = [slice(h * HG_DK, (h + 1) * HG_DK) for h in range(HG_GROUP)]

    def exact_group(qg, kg, vg, b, sub):
        return jnp.concatenate([_intra_exact(qg[:, c], kg[:, c], vg[:, c], b[:, c], sub) for c in heads], axis=1)

    def intra_pass(intra):
        def body(ci, carry):
            rows = pl.ds(pl.multiple_of(ci * HG_CHUNK, HG_CHUNK), HG_CHUNK)
            qg, kg, vg = qs_ref[rows, :], ks_ref[rows, :], i_ref[rows, :]
            b, qe, kd, eb_last = _chunk_prep(qg, kg, lf_ref[rows, :])
            qe_ref[rows, :] = qe
            kd_ref[rows, :] = kd
            eb_ref[pl.ds(pl.multiple_of(ci * SUBLANES, SUBLANES), SUBLANES), :] = \
                jnp.broadcast_to(eb_last, (SUBLANES, eb_last.shape[1]))
            oi_ref[rows, :] = intra(qg, kg, vg, b, HG_SUB)
            return carry
        lax.fori_loop(0, n_chunks, body, 0, unroll=4)

    @pl.when(mild)
    def _():
        intra_pass(_intra_factored)

    @pl.when(jnp.logical_not(mild))
    def _():
        intra_pass(exact_group)

    ng = ng_ref[...]

    def state_body(ci, carry):
        rows = pl.ds(pl.multiple_of(ci * HG_CHUNK, HG_CHUNK), HG_CHUNK)
        st = st_ref[...]
        stb = st.astype(BF16)
        qe, kd, oi = qe_ref[rows, :], kd_ref[rows, :], oi_ref[rows, :]
        vb = i_ref[rows, :].astype(BF16)
        eb_last = eb_ref[pl.ds(pl.multiple_of(ci * SUBLANES, SUBLANES), 1), :]
        outs, adds = [], []
        for c in heads:
            o = oi[:, c] + lax.dot_general(qe[:, c], stb[:, c], _NT, preferred_element_type=F32)
            outs.append(o * lax.rsqrt(jnp.mean(o * o, axis=-1, keepdims=True) + NORM_EPS))
            adds.append(lax.dot_general(vb[:, c], kd[:, c], _TN, preferred_element_type=F32))
        st_ref[...] = st * eb_last + jnp.concatenate(adds, axis=1)
        gate = jax.nn.sigmoid(g_ref[rows, :])
        o_ref[rows, :] = (jnp.concatenate(outs, axis=1) * ng * gate).astype(o_ref.dtype)
        return carry

    lax.fori_loop(0, n_chunks, state_body, 0, unroll=4)

    @pl.when(tb == pl.num_programs(2) - 1)
    def _():
        for h, c in enumerate(heads):
            sfin_ref[h] = st_ref[:, c].T


def _hgrn_prompt(proj, lbs, ng_all, j, n_seq, L, lb_rows):
    nb = L // lb_rows
    ng_groups = HG_HEADS // HG_GROUP
    gw = HG_GROUP * HG_DK

    def field(k):
        return pl.BlockSpec((lb_rows, gw), lambda n, h, t: (n * nb + t, k * ng_groups + h))

    vec = pl.BlockSpec((None, 1, gw), lambda n, h, t: (j, 0, h))
    return pl.pallas_call(
        _hgrn_prompt_kernel,
        grid=(n_seq, ng_groups, nb),
        in_specs=[field(0), field(1), field(2), field(3), vec, vec],
        out_specs=[
            pl.BlockSpec((lb_rows, gw), lambda n, h, t: (n * nb + t, h)),
            pl.BlockSpec((None, HG_GROUP, HG_DK, HG_DK), lambda n, h, t: (n, h, 0, 0)),
        ],
        out_shape=[
            jax.ShapeDtypeStruct((n_seq * L, D_MODEL), BF16),
            jax.ShapeDtypeStruct((n_seq, HG_HEADS, HG_DK, HG_DK), F32),
        ],
        scratch_shapes=[
            pltpu.VMEM((HG_DK, gw), F32),
            pltpu.VMEM((lb_rows, gw), F32),
            pltpu.VMEM((lb_rows, gw), F32),
            pltpu.VMEM((lb_rows, gw), F32),
            pltpu.VMEM((lb_rows, gw), BF16),
            pltpu.VMEM((lb_rows, gw), BF16),
            pltpu.VMEM((lb_rows // HG_CHUNK * SUBLANES, gw), F32),
            pltpu.VMEM((lb_rows, gw), F32),
        ],
        compiler_params=_params("parallel", "parallel", "arbitrary"),
        name="hgrn_prompt",
    )(proj, proj, proj, proj, lbs, ng_all)


def _hgrn_sample_kernel(q_ref, f_ref, i_ref, g_ref, lb_ref, ng_ref, s0_ref, o_ref, sfin_ref):
    n_seq = s0_ref.shape[0]
    q, k, logf = _hgrn_gates(q_ref[...], f_ref[...], lb_ref[...])
    v = i_ref[...]
    row = lax.broadcasted_iota(jnp.int32, (SUBLANES, 1), 0)
    first_half = row < SAMPLE_LEN
    prep = []
    for seq in range(n_seq):
        rows = slice(seq // 2 * SUBLANES, (seq // 2 + 1) * SUBLANES)
        mine = first_half if seq % 2 == 0 else jnp.logical_not(first_half)
        qm, km, vm, lm = (jnp.where(mine, t[rows], 0.0) for t in (q, k, v, logf))
        b, qe, kd, eb_last = _chunk_prep(qm, km, lm)
        prep.append((qe, kd, eb_last, vm.astype(BF16), _intra_exact(qm, km, vm, b, SUBLANES)))
    states = [s0_ref[seq].T for seq in range(n_seq)]
    outs = [oi + lax.dot_general(qe, st.astype(BF16), _NT, preferred_element_type=F32)
            for (qe, _, _, _, oi), st in zip(prep, states)]
    news = [st * eb_last + lax.dot_general(vb, kd, _TN, preferred_element_type=F32)
            for (_, kd, eb_last, vb, _), st in zip(prep, states)]
    sfin_ref[...] = jnp.stack([st.T for st in news])
    tiles = [jnp.where(first_half, outs[2 * p], outs[2 * p + 1]) for p in range(n_seq // 2)]
    gate = jax.nn.sigmoid(g_ref[...])
    o_ref[...] = _head_norm_gate(jnp.concatenate(tiles, axis=0), ng_ref[...], gate).astype(o_ref.dtype)


def _hgrn_sample(proj, row0, state_all, lbs, ng_all, j, n_seq, sb):
    H = HG_HEADS
    rb = sb * SAMPLE_LEN
    assert row0 % rb == 0 and sb % 2 == 0
    r0 = row0 // rb

    def field(k):
        return pl.BlockSpec((rb, HG_DK), lambda s, h: (r0 + s, k * H + h))

    vec = pl.BlockSpec((None, 1, HG_DK), lambda s, h: (j, 0, h))
    return pl.pallas_call(
        _hgrn_sample_kernel,
        grid=(n_seq // sb, H),
        in_specs=[field(0), field(1), field(2), field(3), vec, vec,
                  pl.BlockSpec((None, sb, None, HG_DK, HG_DK), lambda s, h: (j, s, h, 0, 0))],
        out_specs=[
            pl.BlockSpec((rb, HG_DK), lambda s, h: (s, h)),
            pl.BlockSpec((None, sb, None, HG_DK, HG_DK), lambda s, h: (j, s, h, 0, 0)),
        ],
        out_shape=[
            jax.ShapeDtypeStruct((n_seq * SAMPLE_LEN, D_MODEL), BF16),
            jax.ShapeDtypeStruct(state_all.shape, state_all.dtype),
        ],
        input_output_aliases={6: 1},
        compiler_params=_params("parallel", "parallel"),
        name="hgrn_sample",
    )(proj, proj, proj, proj, lbs, ng_all, state_all)


META_E1, META_E2, META_G1, META_G2, META_R1, META_R2 = range(6)
TOKEN_BLOCK = 128


def _router_kernel(ap_ref, as_ref, w_ref, x_ref, g_ref, rw_ref, xo_ref, h_ref, meta_ref, cnt_ref, cb_ref,
                   carry_ref, *, n_prompt_blocks):
    i = pl.program_id(0)
    tm = x_ref.shape[0]

    @pl.when(i == 0)
    def _():
        carry_ref[...] = jnp.zeros_like(carry_ref)

    a = jnp.where(i < n_prompt_blocks, ap_ref[...], as_ref[...])
    x = x_ref[...] + jnp.dot(a, w_ref[...], preferred_element_type=F32)
    xo_ref[...] = x
    h = _rms(x, g_ref[...])
    h_ref[...] = h.astype(h_ref.dtype)
    logits = jnp.dot(h.astype(BF16), rw_ref[...].astype(BF16), preferred_element_type=F32)
    lane = lax.broadcasted_iota(jnp.int32, (tm, LANES), 1)
    neg = -jnp.inf
    lg = jnp.where(lane < N_EXPERTS, logits, neg)
    m1 = jnp.max(lg, axis=-1, keepdims=True)
    e1 = jnp.min(jnp.where(lg == m1, lane, LANES), axis=-1, keepdims=True)
    lg2 = jnp.where(lane == e1, neg, lg)
    m2 = jnp.max(lg2, axis=-1, keepdims=True)
    e2 = jnp.min(jnp.where(lg2 == m2, lane, LANES), axis=-1, keepdims=True)
    ex = jnp.exp(m2 - m1)
    g1 = 1.0 / (1.0 + ex)
    g2 = ex / (1.0 + ex)

    onehot = jnp.where((lane == e1) | (lane == e2), 1.0, 0.0)
    r = lax.broadcasted_iota(jnp.int32, (tm, tm), 0)
    c = lax.broadcasted_iota(jnp.int32, (tm, tm), 1)
    before = jnp.where(r > c, 1.0, 0.0).astype(BF16)
    seen = jnp.dot(before, onehot.astype(BF16), preferred_element_type=F32) + carry_ref[0:1, :]
    r1 = jnp.sum(jnp.where(lane == e1, seen, 0.0), axis=-1, keepdims=True)
    r2 = jnp.sum(jnp.where(lane == e2, seen, 0.0), axis=-1, keepdims=True)
    for blk in range(tm // TOKEN_BLOCK):
        cb_ref[blk] = jnp.broadcast_to(seen[blk * TOKEN_BLOCK:blk * TOKEN_BLOCK + 1], (SUBLANES, LANES))
    total = carry_ref[0:1, :] + jnp.sum(onehot, axis=0, keepdims=True)
    carry_ref[...] = jnp.broadcast_to(total, carry_ref.shape)
    cnt_ref[...] = jnp.broadcast_to(total, cnt_ref.shape)

    meta = jnp.zeros((tm, LANES), F32)
    for idx, val in ((META_E1, e1.astype(F32)), (META_E2, e2.astype(F32)), (META_G1, g1),
                     (META_G2, g2), (META_R1, r1), (META_R2, r2)):
        meta = jnp.where(lane == idx, val, meta)
    meta_ref[...] = meta


def _router(a_prompt, a_sample, w_all, x, g_all, layer, rw_pad, m, tm):
    T = x.shape[0]
    npb = a_prompt.shape[0] // tm
    assert a_prompt.shape[0] % tm == 0 and a_sample.shape[0] % tm == 0
    return pl.pallas_call(
        functools.partial(_router_kernel, n_prompt_blocks=npb),
        grid=(T // tm,),
        in_specs=[
            pl.BlockSpec((tm, D_MODEL), lambda i: (jnp.minimum(i, npb - 1), 0)),
            pl.BlockSpec((tm, D_MODEL), lambda i: (jnp.maximum(i - npb, 0), 0)),
            pl.BlockSpec((None, D_MODEL, D_MODEL), lambda i: (m, 0, 0)),
            pl.BlockSpec((tm, D_MODEL), lambda i: (i, 0)),
            pl.BlockSpec((None, 1, D_MODEL), lambda i: (layer, 0, 0)),
            pl.BlockSpec((None, D_MODEL, LANES), lambda i: (m, 0, 0)),
        ],
        out_specs=[
            pl.BlockSpec((tm, D_MODEL), lambda i: (i, 0)),
            pl.BlockSpec((tm, D_MODEL), lambda i: (i, 0)),
            pl.BlockSpec((tm, LANES), lambda i: (i, 0)),
            pl.BlockSpec((SUBLANES, LANES), lambda i: (0, 0)),
            pl.BlockSpec((tm // TOKEN_BLOCK, SUBLANES, LANES), lambda i: (i, 0, 0)),
        ],
        out_shape=[
            jax.ShapeDtypeStruct((T, D_MODEL), F32),
            jax.ShapeDtypeStruct((T, D_MODEL), BF16),
            jax.ShapeDtypeStruct((T, LANES), F32),
            jax.ShapeDtypeStruct((SUBLANES, LANES), F32),
            jax.ShapeDtypeStruct((T // TOKEN_BLOCK, SUBLANES, LANES), F32),
        ],
        scratch_shapes=[pltpu.VMEM((SUBLANES, LANES), F32)],
        compiler_params=_params("arbitrary"),
        name="moe_router",
    )(a_prompt, a_sample, w_all, x, g_all, rw_pad)


SLOT_BLOCK = 256
WINDOW_BLOCKS = 10
TOKEN_WINDOW = WINDOW_BLOCKS * TOKEN_BLOCK


def _dispatch_kernel(sb_ref, ws_ref, lo_ref, first_ref, n_ref, h_ref, dest_ref, o_ref):
    i = pl.program_id(0)

    @pl.when(i < n_ref[0])
    def _():
        slot = sb_ref[i] * SLOT_BLOCK + lax.broadcasted_iota(jnp.int32, (SLOT_BLOCK, 1), 0)
        token = ws_ref[i] + lax.broadcasted_iota(jnp.int32, (1, TOKEN_WINDOW), 1)
        dest = jnp.where(token >= lo_ref[i], dest_ref[...], -1)
        hit = (dest[0:1, :] == slot) | (dest[1:2, :] == slot)
        rows = jnp.dot(jnp.where(hit, 1.0, 0.0).astype(BF16), h_ref[...], preferred_element_type=F32)

        @pl.when(first_ref[i] == 1)
        def _():
            o_ref[...] = rows.astype(o_ref.dtype)

        @pl.when(first_ref[i] == 0)
        def _():
            o_ref[...] = (o_ref[...].astype(F32) + rows).astype(o_ref.dtype)


def _dispatch(h, dest_t, items, n_slots):
    def at(i, sb, ws, lo, fi, ni):
        return pl.multiple_of(ws[i], TOKEN_BLOCK)

    grid_spec = pltpu.PrefetchScalarGridSpec(
        num_scalar_prefetch=5,
        grid=(items[0].shape[0],),
        in_specs=[
            pl.BlockSpec((pl.Element(TOKEN_WINDOW), pl.Element(D_MODEL)), lambda *a: (at(*a), 0)),
            pl.BlockSpec((pl.Element(SUBLANES), pl.Element(TOKEN_WINDOW)), lambda *a: (0, at(*a))),
        ],
        out_specs=pl.BlockSpec((SLOT_BLOCK, D_MODEL), lambda i, sb, ws, lo, fi, ni: (sb[i], 0)),
    )
    return pl.pallas_call(
        _dispatch_kernel,
        grid_spec=grid_spec,
        out_shape=jax.ShapeDtypeStruct((n_slots, D_MODEL), BF16),
        compiler_params=_params("arbitrary"),
        name="moe_dispatch",
    )(*items, h, dest_t)


def _dispatch_items(cblk, counts, pstart, pend, block_expert, bm, n_slots, T):
    n_tb = T // TOKEN_BLOCK
    n_sb = n_slots // SLOT_BLOCK
    sb_start = jnp.arange(n_sb, dtype=jnp.int32) * SLOT_BLOCK
    e_sb = block_expert[sb_start // bm]
    r0 = sb_start - pstart[e_sb]
    c_sb = counts[e_sb]
    in_region = sb_start < pend[e_sb]
    has_tokens = r0 < c_sb
    r1 = jnp.minimum(r0 + SLOT_BLOCK, c_sb) - 1
    before = cblk[:, 0, :N_EXPERTS].astype(jnp.int32)[:, e_sb]
    tb_first = jnp.where(has_tokens, jnp.sum(before <= r0[None, :], axis=0) - 1, 0)
    tb_last = jnp.where(has_tokens, jnp.sum(before <= r1[None, :], axis=0) - 1, 0)
    n_win = jnp.where(in_region, (tb_last - tb_first) // WINDOW_BLOCKS + 1, 1)
    ends = jnp.cumsum(n_win)
    total = ends[-1]
    max_items = n_sb + N_EXPERTS * (-(-n_tb // WINDOW_BLOCKS) + 1)
    i = jnp.minimum(jnp.arange(max_items, dtype=jnp.int32), total - 1)
    sb = jnp.sum(ends[None, :] <= i[:, None], axis=1).astype(jnp.int32)
    k = i - (ends - n_win)[sb]
    lo = ((tb_first[sb] + k * WINDOW_BLOCKS) * TOKEN_BLOCK).astype(jnp.int32)
    start = jnp.minimum(lo, T - TOKEN_WINDOW)
    return sb, start, lo, (k == 0).astype(jnp.int32), total.astype(jnp.int32).reshape(1)


def _experts_kernel(be_ref, nu_ref, xs_ref, wg_ref, wu_ref, wd_ref, o_ref, acc_ref):
    b = pl.program_id(0)
    f = pl.program_id(1)
    used = b < nu_ref[0]

    @pl.when(used & (f == 0))
    def _():
        acc_ref[...] = jnp.zeros_like(acc_ref)

    @pl.when(used)
    def _():
        acc_ref[...] = _swiglu_acc(acc_ref[...], xs_ref[...], wg_ref[...], wu_ref[...], wd_ref[...])

        @pl.when(f == pl.num_programs(1) - 1)
        def _():
            o_ref[...] = acc_ref[...].astype(o_ref.dtype)

    @pl.when(jnp.logical_not(used) & (f == 0))
    def _():
        o_ref[...] = jnp.zeros_like(o_ref)


def _experts(xs, block_expert, n_used, wg, wu, wd, m, bm, tf):
    n_blocks = xs.shape[0] // bm
    nf = D_FF // tf

    def row(b, nu):
        return jnp.minimum(b, nu[0] - 1)

    def fidx(b, f, nu):
        return jnp.where(b < nu[0], f, nf - 1)

    grid_spec = pltpu.PrefetchScalarGridSpec(
        num_scalar_prefetch=2,
        grid=(n_blocks, nf),
        in_specs=[
            pl.BlockSpec((bm, D_MODEL), lambda b, f, be, nu: (row(b, nu), 0)),
            pl.BlockSpec((None, None, D_MODEL, tf), lambda b, f, be, nu: (m, be[b], 0, fidx(b, f, nu))),
            pl.BlockSpec((None, None, D_MODEL, tf), lambda b, f, be, nu: (m, be[b], 0, fidx(b, f, nu))),
            pl.BlockSpec((None, None, tf, D_MODEL), lambda b, f, be, nu: (m, be[b], fidx(b, f, nu), 0)),
        ],
        out_specs=pl.BlockSpec((bm, D_MODEL), lambda b, f, be, nu: (b, 0)),
        scratch_shapes=[pltpu.VMEM((bm, D_MODEL), F32)],
    )
    return pl.pallas_call(
        _experts_kernel,
        grid_spec=grid_spec,
        out_shape=jax.ShapeDtypeStruct((n_blocks * bm, D_MODEL), BF16),
        compiler_params=_params("arbitrary", "arbitrary"),
        name="moe_experts",
    )(block_expert, n_used, xs, wg, wu, wd)


BF16_ROWS = 16
WINDOW = TOKEN_BLOCK + BF16_ROWS


def _combine_kernel(win_ref, ps_ref, pe_ref, x_ref, meta_ref, *refs, n_prompt_blocks=None):
    ys_refs, out_refs = refs[:N_EXPERTS], refs[N_EXPERTS:]
    blk = pl.program_id(0)
    meta = meta_ref[...]
    e1, e2 = meta[:, META_E1:META_E1 + 1], meta[:, META_E2:META_E2 + 1]
    d1, d2 = meta[:, META_R1:META_R1 + 1], meta[:, META_R2:META_R2 + 1]
    for e in range(N_EXPERTS):
        start = ps_ref[e].astype(F32)
        d1 = d1 + jnp.where(e1 == e, start, 0.0)
        d2 = d2 + jnp.where(e2 == e, start, 0.0)
    d1, d2, g1, g2 = (jnp.broadcast_to(t, (TOKEN_BLOCK, WINDOW))
                      for t in (d1, d2, meta[:, META_G1:META_G1 + 1], meta[:, META_G2:META_G2 + 1]))
    lane = lax.broadcasted_iota(jnp.int32, (1, WINDOW), 1)
    acc = x_ref[...]
    for e in range(N_EXPERTS):
        slot = win_ref[blk * N_EXPERTS + e] + lane
        slot = jnp.where((slot >= ps_ref[e]) & (slot < pe_ref[e]), slot, -1).astype(F32)
        sel = jnp.where(d1 == slot, g1, 0.0) + jnp.where(d2 == slot, g2, 0.0)
        acc = acc + jnp.dot(sel.astype(BF16), ys_refs[e][...], preferred_element_type=F32)
    if n_prompt_blocks is None:
        out_refs[0][...] = acc
    else:
        g_ref, prompt_ref, sample_ref = out_refs
        y = _rms(acc, g_ref[...])

        @pl.when(blk < n_prompt_blocks)
        def _():
            prompt_ref[...] = y

        @pl.when(blk >= n_prompt_blocks)
        def _():
            sample_ref[...] = y


def _combine(x, meta, ys, windows, pstart, pend, final_g=None, n_prompt_rows=None):
    T = x.shape[0]
    nb = T // TOKEN_BLOCK

    def window(e):
        return pl.BlockSpec((pl.Element(WINDOW), pl.Element(D_MODEL)),
                            lambda i, win, ps, pe: (pl.multiple_of(win[i * N_EXPERTS + e], BF16_ROWS), 0))

    in_specs = [pl.BlockSpec((TOKEN_BLOCK, D_MODEL), lambda i, win, ps, pe: (i, 0)),
                pl.BlockSpec((TOKEN_BLOCK, LANES), lambda i, win, ps, pe: (i, 0))] \
        + [window(e) for e in range(N_EXPERTS)]
    operands = [windows, pstart, pend, x, meta] + [ys] * N_EXPERTS
    if final_g is None:
        npb = None
        out_specs = pl.BlockSpec((TOKEN_BLOCK, D_MODEL), lambda i, win, ps, pe: (i, 0))
        out_shape = jax.ShapeDtypeStruct((T, D_MODEL), F32)
    else:
        assert n_prompt_rows % TOKEN_BLOCK == 0
        npb = n_prompt_rows // TOKEN_BLOCK
        in_specs.append(pl.BlockSpec((1, D_MODEL), lambda i, win, ps, pe: (0, 0)))
        operands.append(final_g)
        out_specs = [pl.BlockSpec((TOKEN_BLOCK, D_MODEL), lambda i, win, ps, pe: (jnp.minimum(i, npb - 1), 0)),
                     pl.BlockSpec((TOKEN_BLOCK, D_MODEL), lambda i, win, ps, pe: (jnp.maximum(i - npb, 0), 0))]
        out_shape = [jax.ShapeDtypeStruct((n_prompt_rows, D_MODEL), F32),
                     jax.ShapeDtypeStruct((T - n_prompt_rows, D_MODEL), F32)]
    grid_spec = pltpu.PrefetchScalarGridSpec(
        num_scalar_prefetch=3, grid=(nb,), in_specs=in_specs, out_specs=out_specs)
    return pl.pallas_call(
        functools.partial(_combine_kernel, n_prompt_blocks=npb),
        grid_spec=grid_spec,
        out_shape=out_shape,
        compiler_params=_params("arbitrary"),
        name="moe_combine",
    )(*operands)


def _mixer_out_and_moe(o_p, o_s, w_out, x, norm_ffn_g, layer, rw_pad, wg, wu, wd, m, tm, bm, tf,
                       final_g=None, n_prompt_rows=None):
    T = x.shape[0]
    x, h, meta, cnt, cblk = _router(o_p, o_s, w_out, x, norm_ffn_g, layer, rw_pad, m, tm)
    e = meta[:, META_E1:META_E2 + 1].astype(jnp.int32)
    rank = meta[:, META_R1:META_R2 + 1].astype(jnp.int32)
    counts = cnt[0, :N_EXPERTS].astype(jnp.int32)
    padded = (counts + bm - 1) // bm * bm
    pend = jnp.cumsum(padded)
    pstart = pend - padded
    dest = pstart[e] + rank
    n_blocks = (T * 2) // bm + N_EXPERTS + 1
    block_start = jnp.arange(n_blocks, dtype=jnp.int32) * bm
    block_expert = jnp.minimum(jnp.sum(pend[None, :] <= block_start[:, None], axis=1),
                               N_EXPERTS - 1).astype(jnp.int32)
    n_used = (pend[-1] // bm).astype(jnp.int32).reshape(1)
    dest_t = jnp.full((SUBLANES, T), -1, jnp.int32).at[0:2].set(dest.T)
    items = _dispatch_items(cblk, counts, pstart, pend, block_expert, bm, n_blocks * bm, T)
    xs = _dispatch(h, dest_t, items, n_blocks * bm)
    ys = _experts(xs, block_expert, n_used, wg, wu, wd, m, bm, tf)
    run_start = pstart[None, :] + cblk[:, 0, :N_EXPERTS].astype(jnp.int32)
    windows = (run_start // BF16_ROWS * BF16_ROWS).reshape(-1).astype(jnp.int32)
    return _combine(x, meta, ys, windows, pstart.astype(jnp.int32), pend.astype(jnp.int32),
                    final_g, n_prompt_rows)


def _tril(w):
    n = w.shape[-1]
    return jnp.where(jnp.tril(jnp.ones((n, n), dtype=bool)), w, jnp.zeros((), w.dtype))


def _mix_tables(w_s, b_s, sample_len):
    reps = A_CHUNK // sample_len
    eye = jnp.eye(reps, dtype=w_s.dtype)
    w_prompt = _tril(w_s)
    w_small = _tril(w_s[:, :sample_len, :sample_len])
    w_sample = jax.vmap(lambda w: jnp.kron(eye, w))(w_small)
    b_prompt = b_s
    b_sample = jnp.tile(b_s[:, :sample_len], (1, reps))
    wmix = jnp.stack([w_prompt, w_sample])
    bias = jnp.stack([b_prompt, b_sample])[..., None]
    return wmix, jnp.broadcast_to(bias, bias.shape[:-1] + (LANES,))


def kernel(x_prompt, x_sample, state_hgrn, norm_mix_g, norm_ffn_g, final_norm_g, a_w_in, a_ln_g, a_ln_b, a_w_s, a_b_s, a_w_out, b_w_in, b_lb_logits, b_norm_g, b_w_out, ffn_w_gate, ffn_w_up, ffn_w_down, moe_router, moe_w_gate, moe_w_up, moe_w_down):
    n_p, L, d = x_prompt.shape
    n_s, l_s, _ = x_sample.shape
    assert d == D_MODEL and l_s == SAMPLE_LEN and L % A_CHUNK == 0
    T_p, T_s = n_p * L, n_s * l_s
    T = T_p + T_s
    depth = norm_mix_g.shape[0]
    assert depth % 2 == 0, "the last layer must be an HGRN2 + MoE layer"

    tm_gate = 2 * A_CHUNK
    assert T_p % tm_gate == 0 and T_s % tm_gate == 0
    tm_big = next(t for t in (1536, 768, 512, 256, 128) if T % t == 0)
    bm = 1024
    tf = 512
    lb_rows = min(L, 512)

    x = jnp.concatenate([x_prompt.reshape(T_p, d), x_sample.reshape(T_s, d)], axis=0)

    p = jax.nn.softmax(b_lb_logits.astype(F32), axis=0)
    lbs = (jnp.cumsum(p, axis=0) - p[0:1])[:, None, :]
    mix_g = norm_mix_g[:, None, :]
    ffn_g = norm_ffn_g[:, None, :]
    b_ng = b_norm_g[:, None, :]
    ln_g = a_ln_g[:, None, :]
    ln_b = a_ln_b[:, None, :]
    rw_pad = jnp.pad(moe_router, ((0, 0), (0, 0), (0, LANES - N_EXPERTS)))
    a_w_in, a_w_out, b_w_in, b_w_out, ffn_w_gate, ffn_w_up, ffn_w_down = (
        w.astype(BF16) for w in (a_w_in, a_w_out, b_w_in, b_w_out, ffn_w_gate, ffn_w_up, ffn_w_down))

    hg_prompt, v_sample = [], []
    hg_sample = state_hgrn
    for layer in range(depth):
        j = layer // 2
        if layer % 2 == 0:
            z = _norm_matmul(x, mix_g, layer, a_w_in, j, _gelu, BF16, tm_big, 1024)
            wmix, bias = _mix_tables(a_w_s[j], a_b_s[j], l_s)
            x, v = _gmlp_gate(z, x, ln_g, ln_b, wmix, bias, a_w_out, j, T_p // tm_gate, tm_gate)
            v_sample.append(v.reshape(n_s, l_s, A_HALF))
            x = _ffn_dense(x, ffn_g, layer, ffn_w_gate, ffn_w_up, ffn_w_down, j, tm_big, tf)
        else:
            proj = _norm_matmul(x, mix_g, layer, b_w_in, j, _identity, F32, tm_big, 1024)
            o_p, s_p = _hgrn_prompt(proj, lbs, b_ng, j, n_p, L, lb_rows)
            o_s, hg_sample = _hgrn_sample(proj, T_p, hg_sample, lbs, b_ng, j, n_s, 16)
            hg_prompt.append(s_p)
            last = layer == depth - 1
            x = _mixer_out_and_moe(o_p, o_s, b_w_out, x, ffn_g, layer, rw_pad, moe_w_gate, moe_w_up, moe_w_down,
                                   j, math.gcd(T_p, T_s, 512), bm, tf,
                                   final_norm_g[None, :] if last else None, T_p if last else None)

    y_prompt, y_sample = x
    return (y_prompt.reshape(n_p, L, d), y_sample.reshape(n_s, l_s, d),
            jnp.stack(hg_prompt), hg_sample, jnp.stack(v_sample))
```
